```python
import jax, jax.numpy as jnp
from jax import lax
import numpy as np

D_MODEL = 1024
BATCH = 8
SEQ = 4096
DEPTH = 1

N_META = 16
N_Q_HEADS = 16
N_KV_HEADS = 2
HEAD_DIM = 64
GROUP = N_Q_HEADS // N_KV_HEADS
ROT_DIM = HEAD_DIM // 4
ROPE_THETA = 500000.0
WINDOW = 128
BLOCK = 128
ATTN_WIDTH = N_Q_HEADS * HEAD_DIM
KV_WIDTH = N_KV_HEADS * HEAD_DIM
CONV_CH = D_MODEL
CONV_K = 31
FFN_DIM = 2816
FFN_CONV_K = 3
IN_WIDTH = ATTN_WIDTH + 2 * KV_WIDTH + 2 * CONV_CH + 2 * D_MODEL
RMS_EPS = 1e-6
LN_EPS = 1e-5
NEG_INF = -1e30

kernel_name = "hybrid_swa_sink_conformer_convffn_block"


def rms_norm(x, g):
    xf = x.astype(jnp.float32)
    y = xf * lax.rsqrt(jnp.mean(xf * xf, axis=-1, keepdims=True) + RMS_EPS)
    return (y * g.astype(jnp.float32)).astype(x.dtype)


def layer_norm(x, g, b):
    xf = x.astype(jnp.float32)
    mu = jnp.mean(xf, axis=-1, keepdims=True)
    var = jnp.mean(jnp.square(xf - mu), axis=-1, keepdims=True)
    y = (xf - mu) * lax.rsqrt(var + LN_EPS)
    return (y * g.astype(jnp.float32) + b.astype(jnp.float32)).astype(x.dtype)


def causal_dwconv(x, w, b):
    k = w.shape[0]
    y = lax.conv_general_dilated(
        x, w[:, None, :].astype(x.dtype), window_strides=(1,), padding=[(k - 1, 0)],
        dimension_numbers=("NWC", "WIO", "NWC"), feature_group_count=x.shape[-1])
    return y + b.astype(x.dtype)


def partial_rope(x, pos):
    half = ROT_DIM // 2
    inv_freq = ROPE_THETA ** (-jnp.arange(half, dtype=jnp.float32) * 2.0 / ROT_DIM)
    ang = pos.astype(jnp.float32)[:, None] * inv_freq[None, :]
    cos = jnp.cos(ang)[None, :, None, :]
    sin = jnp.sin(ang)[None, :, None, :]
    xr = x[..., :ROT_DIM].astype(jnp.float32)
    x1, x2 = xr[..., :half], xr[..., half:]
    rot = jnp.concatenate([x1 * cos - x2 * sin, x2 * cos + x1 * sin], axis=-1).astype(x.dtype)
    return jnp.concatenate([rot, x[..., ROT_DIM:]], axis=-1)


def sliding_window_sink_attention(q, k, v, sinks):
    bsz, seq_len = q.shape[0], q.shape[1]
    pad = BLOCK - N_META
    padded = seq_len + pad
    nb = padded // BLOCK
    scale = HEAD_DIM ** -0.5

    def pad_front(a):
        return jnp.pad(a, ((0, 0), (pad, 0), (0, 0), (0, 0)))

    def shift_block(a):
        return jnp.concatenate([jnp.zeros_like(a[:, :1]), a[:, :-1]], axis=1)

    qb = (pad_front(q) * scale).reshape(bsz, nb, BLOCK, N_KV_HEADS, GROUP, HEAD_DIM)
    kb = pad_front(k).reshape(bsz, nb, BLOCK, N_KV_HEADS, HEAD_DIM)
    vb = pad_front(v).reshape(bsz, nb, BLOCK, N_KV_HEADS, HEAD_DIM)
    k_meta = jnp.broadcast_to(k[:, None, :N_META], (bsz, nb, N_META, N_KV_HEADS, HEAD_DIM))
    v_meta = jnp.broadcast_to(v[:, None, :N_META], (bsz, nb, N_META, N_KV_HEADS, HEAD_DIM))
    keys = jnp.concatenate([k_meta, shift_block(kb), kb], axis=2)
    vals = jnp.concatenate([v_meta, shift_block(vb), vb], axis=2)

    tpos = (jnp.arange(padded) - pad).reshape(nb, BLOCK)
    tq = tpos[:, :, None]
    t_meta = jnp.arange(N_META)[None, None, :]
    t_loc = jnp.concatenate([tpos - BLOCK, tpos], axis=1)[:, None, :]
    meta_ok = jnp.broadcast_to(t_meta <= tq, (nb, BLOCK, N_META))
    loc_ok = (t_loc >= N_META) & (t_loc <= tq) & (tq - t_loc < WINDOW)
    mask = jnp.concatenate([meta_ok, loc_ok], axis=-1)

    s = jnp.einsum("bnqhgd,bnkhd->bnhgqk", qb, keys).astype(jnp.float32)
    s = jnp.where(mask[None, :, None, None], s, NEG_INF)
    sink = sinks.astype(jnp.float32).reshape(N_KV_HEADS, GROUP)[None, None, :, :, None, None]
    sink = jnp.broadcast_to(sink, s.shape[:-1] + (1,))
    probs = jax.nn.softmax(jnp.concatenate([s, sink], axis=-1), axis=-1)[..., :-1]
    o = jnp.einsum("bnhgqk,bnkhd->bnqhgd", probs.astype(v.dtype), vals)
    return o.reshape(bsz, padded, ATTN_WIDTH)[:, pad:]


def hybrid_mixer(h, pos, w_in, b_in, attn_sinks, w_attn_proj, conv_dw_w, conv_dw_b,
                 conv_ln_g, conv_ln_b, w_conv_proj, b_conv_proj, w_out):
    bsz, seq_len = h.shape[0], h.shape[1]
    proj = h @ w_in + b_in
    cuts = np.cumsum([ATTN_WIDTH, KV_WIDTH, KV_WIDTH, 2 * CONV_CH, D_MODEL]).tolist()
    q, k, v, glu_in, gate_a, gate_c = jnp.split(proj, cuts, axis=-1)

    q = partial_rope(q.reshape(bsz, seq_len, N_Q_HEADS, HEAD_DIM), pos)
    k = partial_rope(k.reshape(bsz, seq_len, N_KV_HEADS, HEAD_DIM), pos)
    v = v.reshape(bsz, seq_len, N_KV_HEADS, HEAD_DIM)
    attn = sliding_window_sink_attention(q, k, v, attn_sinks) @ w_attn_proj

    a, g = jnp.split(glu_in, 2, axis=-1)
    c = causal_dwconv(a * jax.nn.sigmoid(g), conv_dw_w, conv_dw_b)
    c = jax.nn.silu(layer_norm(c, conv_ln_g, conv_ln_b))
    conv = c @ w_conv_proj + b_conv_proj

    merged = jax.nn.sigmoid(gate_a) * attn + jax.nn.sigmoid(gate_c) * conv
    return merged @ w_out


def conv_ffn(h, w_up, ffn_dw_w, ffn_dw_b, w_down):
    u = causal_dwconv(h @ w_up, ffn_dw_w, ffn_dw_b)
    gate, val = jnp.split(u, 2, axis=-1)
    return (jax.nn.silu(gate) * val) @ w_down


def _fwd_setup_inputs(seed: int = 0) -> dict:
    key = jax.random.key(seed)
    ks = jax.random.split(key, 24)
    f32 = jnp.float32

    def nrm(k, shape, scale):
        return jax.random.normal(k, shape, f32) * scale

    def gain(k, shape):
        return 1.0 + 0.1 * jax.random.normal(k, shape, f32)

    L = DEPTH
    return {
        "x": nrm(ks[0], (BATCH, SEQ, D_MODEL), 1.0),
        "meta_tokens": nrm(ks[1], (N_META, D_MODEL), 1.0),
        "norm_pre_mix": gain(ks[2], (L, D_MODEL)),
        "norm_post_mix": gain(ks[3], (L, D_MODEL)),
        "w_in": nrm(ks[4], (L, D_MODEL, IN_WIDTH), D_MODEL ** -0.5),
        "b_in": nrm(ks[5], (L, IN_WIDTH), 0.02),
        "attn_sinks": nrm(ks[6], (L, N_Q_HEADS), 0.5),
        "w_attn_proj": nrm(ks[7], (L, ATTN_WIDTH, D_MODEL), ATTN_WIDTH ** -0.5),
        "conv_dw_w": nrm(ks[8], (L, CONV_K, CONV_CH), CONV_K ** -0.5),
        "conv_dw_b": nrm(ks[9], (L, CONV_CH), 0.02),
        "conv_ln_g": gain(ks[10], (L, CONV_CH)),
        "conv_ln_b": nrm(ks[11], (L, CONV_CH), 0.02),
        "w_conv_proj": nrm(ks[12], (L, CONV_CH, D_MODEL), CONV_CH ** -0.5),
        "b_conv_proj": nrm(ks[13], (L, D_MODEL), 0.02),
        "w_out": nrm(ks[14], (L, D_MODEL, D_MODEL), D_MODEL ** -0.5),
        "norm_pre_ffn": gain(ks[15], (L, D_MODEL)),
        "norm_post_ffn": gain(ks[16], (L, D_MODEL)),
        "w_up": nrm(ks[17], (L, D_MODEL, 2 * FFN_DIM), D_MODEL ** -0.5),
        "ffn_dw_w": nrm(ks[18], (L, FFN_CONV_K, 2 * FFN_DIM), FFN_CONV_K ** -0.5),
        "ffn_dw_b": nrm(ks[19], (L, 2 * FFN_DIM), 0.02),
        "w_down": nrm(ks[20], (L, FFN_DIM, D_MODEL), FFN_DIM ** -0.5),
    }


def _fwd_reference(x, meta_tokens, norm_pre_mix, norm_post_mix, w_in, b_in, attn_sinks, w_attn_proj,
              conv_dw_w, conv_dw_b, conv_ln_g, conv_ln_b, w_conv_proj, b_conv_proj, w_out,
              norm_pre_ffn, norm_post_ffn, w_up, ffn_dw_w, ffn_dw_b, w_down):
    bsz = x.shape[0]
    meta = jnp.broadcast_to(meta_tokens.astype(x.dtype)[None], (bsz, N_META, D_MODEL))
    h = jnp.concatenate([meta, x], axis=1)
    pos = jnp.arange(h.shape[1])
    for l in range(DEPTH):
        mix = hybrid_mixer(rms_norm(h, norm_pre_mix[l]), pos, w_in[l], b_in[l], attn_sinks[l],
                           w_attn_proj[l], conv_dw_w[l], conv_dw_b[l], conv_ln_g[l], conv_ln_b[l],
                           w_conv_proj[l], b_conv_proj[l], w_out[l])
        h = h + rms_norm(mix, norm_post_mix[l])
        ffn = conv_ffn(rms_norm(h, norm_pre_ffn[l]), w_up[l], ffn_dw_w[l], ffn_dw_b[l], w_down[l])
        h = h + rms_norm(ffn, norm_post_ffn[l])
    return h[:, N_META:]


import jax as _jax
import jax.numpy as _jnp

TWIN_FORMAT = 'train_step'
FWD_PARAMS = ['x', 'meta_tokens', 'norm_pre_mix', 'norm_post_mix', 'w_in', 'b_in', 'attn_sinks', 'w_attn_proj', 'conv_dw_w', 'conv_dw_b', 'conv_ln_g', 'conv_ln_b', 'w_conv_proj', 'b_conv_proj', 'w_out', 'norm_pre_ffn', 'norm_post_ffn', 'w_up', 'ffn_dw_w', 'ffn_dw_b', 'w_down']
TWIN_WEIGHTS = ['meta_tokens', 'norm_pre_mix', 'norm_post_mix', 'w_in', 'b_in', 'attn_sinks', 'w_attn_proj', 'conv_dw_w', 'conv_dw_b', 'conv_ln_g', 'conv_ln_b', 'w_conv_proj', 'b_conv_proj', 'w_out', 'norm_pre_ffn', 'norm_post_ffn', 'w_up', 'ffn_dw_w', 'ffn_dw_b', 'w_down']
TWIN_DIFF_INPUT = 'x'
TWIN_INPUTS = ['x', 'meta_tokens', 'norm_pre_mix', 'norm_post_mix', 'w_in', 'b_in', 'attn_sinks', 'w_attn_proj', 'conv_dw_w', 'conv_dw_b', 'conv_ln_g', 'conv_ln_b', 'w_conv_proj', 'b_conv_proj', 'w_out', 'norm_pre_ffn', 'norm_post_ffn', 'w_up', 'ffn_dw_w', 'ffn_dw_b', 'w_down', 'loss_target', 'm_meta_tokens', 'm_norm_pre_mix', 'm_norm_post_mix', 'm_w_in', 'm_b_in', 'm_attn_sinks', 'm_w_attn_proj', 'm_conv_dw_w', 'm_conv_dw_b', 'm_conv_ln_g', 'm_conv_ln_b', 'm_w_conv_proj', 'm_b_conv_proj', 'm_w_out', 'm_norm_pre_ffn', 'm_norm_post_ffn', 'm_w_up', 'm_ffn_dw_w', 'm_ffn_dw_b', 'm_w_down', 'v_meta_tokens', 'v_norm_pre_mix', 'v_norm_post_mix', 'v_w_in', 'v_b_in', 'v_attn_sinks', 'v_w_attn_proj', 'v_conv_dw_w', 'v_conv_dw_b', 'v_conv_ln_g', 'v_conv_ln_b', 'v_w_conv_proj', 'v_b_conv_proj', 'v_w_out', 'v_norm_pre_ffn', 'v_norm_post_ffn', 'v_w_up', 'v_ffn_dw_w', 'v_ffn_dw_b', 'v_w_down']
TWIN_OUTPUTS = ['loss', 'grad_x', 'grad_meta_tokens', 'grad_norm_pre_mix', 'grad_norm_post_mix', 'grad_w_in', 'grad_b_in', 'grad_attn_sinks', 'grad_w_attn_proj', 'grad_conv_dw_w', 'grad_conv_dw_b', 'grad_conv_ln_g', 'grad_conv_ln_b', 'grad_w_conv_proj', 'grad_b_conv_proj', 'grad_w_out', 'grad_norm_pre_ffn', 'grad_norm_post_ffn', 'grad_w_up', 'grad_ffn_dw_w', 'grad_ffn_dw_b', 'grad_w_down', 'delta_meta_tokens', 'delta_norm_pre_mix', 'delta_norm_post_mix', 'delta_w_in', 'delta_b_in', 'delta_attn_sinks', 'delta_w_attn_proj', 'delta_conv_dw_w', 'delta_conv_dw_b', 'delta_conv_ln_g', 'delta_conv_ln_b', 'delta_w_conv_proj', 'delta_b_conv_proj', 'delta_w_out', 'delta_norm_pre_ffn', 'delta_norm_post_ffn', 'delta_w_up', 'delta_ffn_dw_w', 'delta_ffn_dw_b', 'delta_w_down', 'new_m_meta_tokens', 'new_m_norm_pre_mix', 'new_m_norm_post_mix', 'new_m_w_in', 'new_m_b_in', 'new_m_attn_sinks', 'new_m_w_attn_proj', 'new_m_conv_dw_w', 'new_m_conv_dw_b', 'new_m_conv_ln_g', 'new_m_conv_ln_b', 'new_m_w_conv_proj', 'new_m_b_conv_proj', 'new_m_w_out', 'new_m_norm_pre_ffn', 'new_m_norm_post_ffn', 'new_m_w_up', 'new_m_ffn_dw_w', 'new_m_ffn_dw_b', 'new_m_w_down', 'new_v_meta_tokens', 'new_v_norm_pre_mix', 'new_v_norm_post_mix', 'new_v_w_in', 'new_v_b_in', 'new_v_attn_sinks', 'new_v_w_attn_proj', 'new_v_conv_dw_w', 'new_v_conv_dw_b', 'new_v_conv_ln_g', 'new_v_conv_ln_b', 'new_v_w_conv_proj', 'new_v_b_conv_proj', 'new_v_w_out', 'new_v_norm_pre_ffn', 'new_v_norm_post_ffn', 'new_v_w_up', 'new_v_ffn_dw_w', 'new_v_ffn_dw_b', 'new_v_w_down']
TWIN_LEAF_KINDS = {'loss': 'loss', 'grad_x': 'grad_x', 'grad_meta_tokens': 'grad_w', 'grad_norm_pre_mix': 'grad_w', 'grad_norm_post_mix': 'grad_w', 'grad_w_in': 'grad_w', 'grad_b_in': 'grad_w', 'grad_attn_sinks': 'grad_w', 'grad_w_attn_proj': 'grad_w', 'grad_conv_dw_w': 'grad_w', 'grad_conv_dw_b': 'grad_w', 'grad_conv_ln_g': 'grad_w', 'grad_conv_ln_b': 'grad_w', 'grad_w_conv_proj': 'grad_w', 'grad_b_conv_proj': 'grad_w', 'grad_w_out': 'grad_w', 'grad_norm_pre_ffn': 'grad_w', 'grad_norm_post_ffn': 'grad_w', 'grad_w_up': 'grad_w', 'grad_ffn_dw_w': 'grad_w', 'grad_ffn_dw_b': 'grad_w', 'grad_w_down': 'grad_w', 'delta_meta_tokens': 'delta_w', 'delta_norm_pre_mix': 'delta_w', 'delta_norm_post_mix': 'delta_w', 'delta_w_in': 'delta_w', 'delta_b_in': 'delta_w', 'delta_attn_sinks': 'delta_w', 'delta_w_attn_proj': 'delta_w', 'delta_conv_dw_w': 'delta_w', 'delta_conv_dw_b': 'delta_w', 'delta_conv_ln_g': 'delta_w', 'delta_conv_ln_b': 'delta_w', 'delta_w_conv_proj': 'delta_w', 'delta_b_conv_proj': 'delta_w', 'delta_w_out': 'delta_w', 'delta_norm_pre_ffn': 'delta_w', 'delta_norm_post_ffn': 'delta_w', 'delta_w_up': 'delta_w', 'delta_ffn_dw_w': 'delta_w', 'delta_ffn_dw_b': 'delta_w', 'delta_w_down': 'delta_w', 'new_m_meta_tokens': 'new_m', 'new_m_norm_pre_mix': 'new_m', 'new_m_norm_post_mix': 'new_m', 'new_m_w_in': 'new_m', 'new_m_b_in': 'new_m', 'new_m_attn_sinks': 'new_m', 'new_m_w_attn_proj': 'new_m', 'new_m_conv_dw_w': 'new_m', 'new_m_conv_dw_b': 'new_m', 'new_m_conv_ln_g': 'new_m', 'new_m_conv_ln_b': 'new_m', 'new_m_w_conv_proj': 'new_m', 'new_m_b_conv_proj': 'new_m', 'new_m_w_out': 'new_m', 'new_m_norm_pre_ffn': 'new_m', 'new_m_norm_post_ffn': 'new_m', 'new_m_w_up': 'new_m', 'new_m_ffn_dw_w': 'new_m', 'new_m_ffn_dw_b': 'new_m', 'new_m_w_down': 'new_m', 'new_v_meta_tokens': 'new_v', 'new_v_norm_pre_mix': 'new_v', 'new_v_norm_post_mix': 'new_v', 'new_v_w_in': 'new_v', 'new_v_b_in': 'new_v', 'new_v_attn_sinks': 'new_v', 'new_v_w_attn_proj': 'new_v', 'new_v_conv_dw_w': 'new_v', 'new_v_conv_dw_b': 'new_v', 'new_v_conv_ln_g': 'new_v', 'new_v_conv_ln_b': 'new_v', 'new_v_w_conv_proj': 'new_v', 'new_v_b_conv_proj': 'new_v', 'new_v_w_out': 'new_v', 'new_v_norm_pre_ffn': 'new_v', 'new_v_norm_post_ffn': 'new_v', 'new_v_w_up': 'new_v', 'new_v_ffn_dw_w': 'new_v', 'new_v_ffn_dw_b': 'new_v', 'new_v_w_down': 'new_v'}


def _forward(args):
    return _fwd_reference(*[args[k] for k in FWD_PARAMS])


def _output_shape():
    def fwd():
        inp = _fwd_setup_inputs(0)
        return _fwd_reference(*[inp[k] for k in FWD_PARAMS])
    out = _jax.eval_shape(fwd)
    return out.shape, out.dtype

N_MICROBATCH = 1
ADAM_LR = 0.001
ADAM_B1 = 0.9
ADAM_B2 = 0.999
ADAM_EPS = 1e-08
ADAM_WD = 0.01
ADAM_STEP = 10
PER_EXAMPLE_BATCH_AXIS = {'x': 0, 'loss_target': 0}
SHARED_INPUTS = []
_WEIGHT_DTYPES = {'meta_tokens': _jnp.float32, 'norm_pre_mix': _jnp.float32, 'norm_post_mix': _jnp.float32, 'w_in': _jnp.float32, 'b_in': _jnp.float32, 'attn_sinks': _jnp.float32, 'w_attn_proj': _jnp.float32, 'conv_dw_w': _jnp.float32, 'conv_dw_b': _jnp.float32, 'conv_ln_g': _jnp.float32, 'conv_ln_b': _jnp.float32, 'w_conv_proj': _jnp.float32, 'b_conv_proj': _jnp.float32, 'w_out': _jnp.float32, 'norm_pre_ffn': _jnp.float32, 'norm_post_ffn': _jnp.float32, 'w_up': _jnp.float32, 'ffn_dw_w': _jnp.float32, 'ffn_dw_b': _jnp.float32, 'w_down': _jnp.float32}
MOMENT_SCALE = {'meta_tokens': 4.571643e-02, 'norm_pre_mix': 5.825983e-01, 'norm_post_mix': 3.245680e+01, 'w_in': 2.557813e-01, 'b_in': 3.734958e+00, 'attn_sinks': 2.453826e-02, 'w_attn_proj': 2.086365e-01, 'conv_dw_w': 5.926196e-01, 'conv_dw_b': 7.439600e+00, 'conv_ln_g': 2.698075e+00, 'conv_ln_b': 4.015915e+00, 'w_conv_proj': 1.572487e+00, 'b_conv_proj': 7.933440e+00, 'w_out': 1.730813e+00, 'norm_pre_ffn': 1.338775e+00, 'norm_post_ffn': 3.230846e+01, 'w_up': 5.478584e-01, 'ffn_dw_w': 6.555079e-01, 'ffn_dw_b': 2.233401e+00, 'w_down': 1.047694e+00}


def _to_microbatches(a, axis):
    t = _jnp.moveaxis(a, axis, 0)
    t = t.reshape((N_MICROBATCH, t.shape[0] // N_MICROBATCH) + t.shape[1:])
    return _jnp.moveaxis(t, 1, axis + 1)


def setup_inputs(seed: int = 0) -> dict:
    inp = _fwd_setup_inputs(seed)
    key = _jax.random.fold_in(_jax.random.key(seed), 7919)
    shape, _ = _output_shape()
    out = dict(inp)
    out["loss_target"] = _jax.random.normal(_jax.random.fold_in(key, 0), shape, _jnp.float32)
    for i, name in enumerate(TWIN_WEIGHTS):
        w = inp[name].astype(_jnp.float32)
        if MOMENT_SCALE is None:
            s = _jnp.sqrt(_jnp.mean(_jnp.square(w)) + 1e-30)
        else:
            s = MOMENT_SCALE[name]
        km, kv = _jax.random.split(_jax.random.fold_in(key, i + 1))
        out[name] = w
        out["m_" + name] = s * _jax.random.normal(km, w.shape, _jnp.float32)
        out["v_" + name] = (s * s) * _jax.random.uniform(kv, w.shape, _jnp.float32, 0.5, 1.5)
    if N_MICROBATCH > 1:
        for name, axis in PER_EXAMPLE_BATCH_AXIS.items():
            out[name] = _to_microbatches(out[name], axis)
    return {'x': out['x'], 'meta_tokens': out['meta_tokens'], 'norm_pre_mix': out['norm_pre_mix'], 'norm_post_mix': out['norm_post_mix'], 'w_in': out['w_in'], 'b_in': out['b_in'], 'attn_sinks': out['attn_sinks'], 'w_attn_proj': out['w_attn_proj'], 'conv_dw_w': out['conv_dw_w'], 'conv_dw_b': out['conv_dw_b'], 'conv_ln_g': out['conv_ln_g'], 'conv_ln_b': out['conv_ln_b'], 'w_conv_proj': out['w_conv_proj'], 'b_conv_proj': out['b_conv_proj'], 'w_out': out['w_out'], 'norm_pre_ffn': out['norm_pre_ffn'], 'norm_post_ffn': out['norm_post_ffn'], 'w_up': out['w_up'], 'ffn_dw_w': out['ffn_dw_w'], 'ffn_dw_b': out['ffn_dw_b'], 'w_down': out['w_down'], 'loss_target': out['loss_target'], 'm_meta_tokens': out['m_meta_tokens'], 'm_norm_pre_mix': out['m_norm_pre_mix'], 'm_norm_post_mix': out['m_norm_post_mix'], 'm_w_in': out['m_w_in'], 'm_b_in': out['m_b_in'], 'm_attn_sinks': out['m_attn_sinks'], 'm_w_attn_proj': out['m_w_attn_proj'], 'm_conv_dw_w': out['m_conv_dw_w'], 'm_conv_dw_b': out['m_conv_dw_b'], 'm_conv_ln_g': out['m_conv_ln_g'], 'm_conv_ln_b': out['m_conv_ln_b'], 'm_w_conv_proj': out['m_w_conv_proj'], 'm_b_conv_proj': out['m_b_conv_proj'], 'm_w_out': out['m_w_out'], 'm_norm_pre_ffn': out['m_norm_pre_ffn'], 'm_norm_post_ffn': out['m_norm_post_ffn'], 'm_w_up': out['m_w_up'], 'm_ffn_dw_w': out['m_ffn_dw_w'], 'm_ffn_dw_b': out['m_ffn_dw_b'], 'm_w_down': out['m_w_down'], 'v_meta_tokens': out['v_meta_tokens'], 'v_norm_pre_mix': out['v_norm_pre_mix'], 'v_norm_post_mix': out['v_norm_post_mix'], 'v_w_in': out['v_w_in'], 'v_b_in': out['v_b_in'], 'v_attn_sinks': out['v_attn_sinks'], 'v_w_attn_proj': out['v_w_attn_proj'], 'v_conv_dw_w': out['v_conv_dw_w'], 'v_conv_dw_b': out['v_conv_dw_b'], 'v_conv_ln_g': out['v_conv_ln_g'], 'v_conv_ln_b': out['v_conv_ln_b'], 'v_w_conv_proj': out['v_w_conv_proj'], 'v_b_conv_proj': out['v_b_conv_proj'], 'v_w_out': out['v_w_out'], 'v_norm_pre_ffn': out['v_norm_pre_ffn'], 'v_norm_post_ffn': out['v_norm_post_ffn'], 'v_w_up': out['v_w_up'], 'v_ffn_dw_w': out['v_ffn_dw_w'], 'v_ffn_dw_b': out['v_ffn_dw_b'], 'v_w_down': out['v_w_down']}


def _loss(weights, diff, rest, loss_target):
    with _jax.named_scope("forward"):
        args = {**rest, TWIN_DIFF_INPUT: diff, **{k: w.astype(_WEIGHT_DTYPES[k]) for k, w in weights.items()}}
        y = _forward(args)
    with _jax.named_scope("loss_head"):
        err = _jnp.square(y.astype(_jnp.float32) - loss_target)
        return 0.5 * _jnp.sum(_jnp.mean(err, axis=-1)) if err.ndim else 0.5 * err


def _adamw(w, g, m, v):
    m = ADAM_B1 * m + (1.0 - ADAM_B1) * g
    v = ADAM_B2 * v + (1.0 - ADAM_B2) * _jnp.square(g)
    m_hat = m / (1.0 - ADAM_B1 ** ADAM_STEP)
    v_hat = v / (1.0 - ADAM_B2 ** ADAM_STEP)
    delta = -ADAM_LR * (m_hat / (_jnp.sqrt(v_hat) + ADAM_EPS) + ADAM_WD * w)
    return delta, m, v


def reference(x, meta_tokens, norm_pre_mix, norm_post_mix, w_in, b_in, attn_sinks, w_attn_proj, conv_dw_w, conv_dw_b, conv_ln_g, conv_ln_b, w_conv_proj, b_conv_proj, w_out, norm_pre_ffn, norm_post_ffn, w_up, ffn_dw_w, ffn_dw_b, w_down, loss_target, m_meta_tokens, m_norm_pre_mix, m_norm_post_mix, m_w_in, m_b_in, m_attn_sinks, m_w_attn_proj, m_conv_dw_w, m_conv_dw_b, m_conv_ln_g, m_conv_ln_b, m_w_conv_proj, m_b_conv_proj, m_w_out, m_norm_pre_ffn, m_norm_post_ffn, m_w_up, m_ffn_dw_w, m_ffn_dw_b, m_w_down, v_meta_tokens, v_norm_pre_mix, v_norm_post_mix, v_w_in, v_b_in, v_attn_sinks, v_w_attn_proj, v_conv_dw_w, v_conv_dw_b, v_conv_ln_g, v_conv_ln_b, v_w_conv_proj, v_b_conv_proj, v_w_out, v_norm_pre_ffn, v_norm_post_ffn, v_w_up, v_ffn_dw_w, v_ffn_dw_b, v_w_down):
    given = dict(x=x, meta_tokens=meta_tokens, norm_pre_mix=norm_pre_mix, norm_post_mix=norm_post_mix, w_in=w_in, b_in=b_in, attn_sinks=attn_sinks, w_attn_proj=w_attn_proj, conv_dw_w=conv_dw_w, conv_dw_b=conv_dw_b, conv_ln_g=conv_ln_g, conv_ln_b=conv_ln_b, w_conv_proj=w_conv_proj, b_conv_proj=b_conv_proj, w_out=w_out, norm_pre_ffn=norm_pre_ffn, norm_post_ffn=norm_post_ffn, w_up=w_up, ffn_dw_w=ffn_dw_w, ffn_dw_b=ffn_dw_b, w_down=w_down, loss_target=loss_target, m_meta_tokens=m_meta_tokens, m_norm_pre_mix=m_norm_pre_mix, m_norm_post_mix=m_norm_post_mix, m_w_in=m_w_in, m_b_in=m_b_in, m_attn_sinks=m_attn_sinks, m_w_attn_proj=m_w_attn_proj, m_conv_dw_w=m_conv_dw_w, m_conv_dw_b=m_conv_dw_b, m_conv_ln_g=m_conv_ln_g, m_conv_ln_b=m_conv_ln_b, m_w_conv_proj=m_w_conv_proj, m_b_conv_proj=m_b_conv_proj, m_w_out=m_w_out, m_norm_pre_ffn=m_norm_pre_ffn, m_norm_post_ffn=m_norm_post_ffn, m_w_up=m_w_up, m_ffn_dw_w=m_ffn_dw_w, m_ffn_dw_b=m_ffn_dw_b, m_w_down=m_w_down, v_meta_tokens=v_meta_tokens, v_norm_pre_mix=v_norm_pre_mix, v_norm_post_mix=v_norm_post_mix, v_w_in=v_w_in, v_b_in=v_b_in, v_attn_sinks=v_attn_sinks, v_w_attn_proj=v_w_attn_proj, v_conv_dw_w=v_conv_dw_w, v_conv_dw_b=v_conv_dw_b, v_conv_ln_g=v_conv_ln_g, v_conv_ln_b=v_conv_ln_b, v_w_conv_proj=v_w_conv_proj, v_b_conv_proj=v_b_conv_proj, v_w_out=v_w_out, v_norm_pre_ffn=v_norm_pre_ffn, v_norm_post_ffn=v_norm_post_ffn, v_w_up=v_w_up, v_ffn_dw_w=v_ffn_dw_w, v_ffn_dw_b=v_ffn_dw_b, v_w_down=v_w_down)
    weights = {n: given[n] for n in TWIN_WEIGHTS}
    shared = {n: given[n] for n in SHARED_INPUTS}
    per_example = {n: given[n] for n in ['x']}
    grad_fn = _jax.value_and_grad(_loss, argnums=(0, 1))

    def one_microbatch(ex, loss_target):
        ex = dict(ex)
        diff = ex.pop(TWIN_DIFF_INPUT)
        return grad_fn(weights, diff, {**shared, **ex}, loss_target)

    if N_MICROBATCH == 1:
        loss, (grad_w, grad_x) = one_microbatch(per_example, given["loss_target"])
    else:
        def body(carry, xs):
            loss_sum, grad_sum = carry
            l_k, (gw_k, gx_k) = one_microbatch(xs[0], xs[1])
            with _jax.named_scope("update"):
                return (loss_sum + l_k, _jax.tree.map(_jnp.add, grad_sum, gw_k)), gx_k

        init = (_jnp.zeros((), _jnp.float32), _jax.tree.map(_jnp.zeros_like, weights))
        (loss, grad_w), grad_x = _jax.lax.scan(body, init, (per_example, given["loss_target"]))
    with _jax.named_scope("update"):
        delta_w, new_m, new_v = {}, {}, {}
        for n in TWIN_WEIGHTS:
            delta_w[n], new_m[n], new_v[n] = _adamw(weights[n], grad_w[n], given["m_" + n], given["v_" + n])
    return (loss, grad_x, *[grad_w[n] for n in TWIN_WEIGHTS], *[delta_w[n] for n in TWIN_WEIGHTS],
            *[new_m[n] for n in TWIN_WEIGHTS], *[new_v[n] for n in TWIN_WEIGHTS])
```

```python
import functools

import jax
import jax.numpy as jnp
import numpy as np
from jax import lax
from jax.experimental import pallas as pl
from jax.experimental.pallas import tpu as pltpu

F32, BF16 = jnp.float32, jnp.bfloat16

D = 1024
NH, NKV, HD = 16, 2, 64
NMETA, BLK = 16, 128
PAD = BLK - NMETA
ROT = HD // 4
THETA = 500000.0
CONV_K = 31
FFN = 2816
FFN_K = 3
IN_W = 5376
QKV_W, GLU_W, GATE_W = 1280, 2048, 2048
RMS_EPS, LN_EPS, NEG = 1e-6, 1e-5, -1e30
LR, B1, B2, ADAM_EPS, WD, STEP = 0.001, 0.9, 0.999, 1e-08, 0.01, 10

VMEM_LIMIT = 56 * 2 ** 20
MESH = pl.DeviceIdType.MESH

NT_DIMS = (((1,), (1,)), ((), ()))
TN_DIMS = (((0,), (0,)), ((), ()))


def _params(sem, **kw):
    return pltpu.CompilerParams(dimension_semantics=sem, vmem_limit_bytes=VMEM_LIMIT, **kw)


def _tile(n, pref, mult=16):
    for t in range(min(pref, n), 0, -1):
        if n % t == 0 and t % mult == 0:
            return t
    return n


def _row(tr, w, col=0):
    return pl.BlockSpec((tr, w), lambda i: (i, col))


def _rrow(tr, w, nt, col=0):
    return pl.BlockSpec((tr, w), lambda t: (nt - 1 - t, col))


def _const(shape):
    return pl.BlockSpec(shape, lambda *_: (0,) * len(shape))


def _sds(shape, dt):
    return jax.ShapeDtypeStruct(shape, dt)


def _rms(x, g):
    r = lax.rsqrt(jnp.mean(x * x, -1, keepdims=True) + RMS_EPS)
    return x * r * g, r


def _rms_bwd(dy, x, r, g):
    gy = dy * g
    return r * gy - x * (r * r * r) * jnp.mean(x * gy, -1, keepdims=True)


def _colsum(x):
    return jnp.sum(x, axis=0, keepdims=True)


def _rope(x, c, sa, sb):
    n = x.shape[1]
    return x * c + pltpu.roll(x, n - 8, 1) * sa + pltpu.roll(x, 8, 1) * sb


def _rope_bwd(d, c, sa, sb):
    n = d.shape[1]
    return d * c + pltpu.roll(d * sa, 8, 1) + pltpu.roll(d * sb, n - 8, 1)


def _rope_tables(R):
    half = ROT // 2
    inv = THETA ** (-jnp.arange(half, dtype=F32) * 2.0 / ROT)
    pos = (jnp.arange(R) - PAD).astype(F32)
    ang = pos[:, None] * inv[None, :]
    cos, sin = jnp.cos(ang), jnp.sin(ang)
    one, zero = jnp.ones((R, HD - ROT), F32), jnp.zeros((R, HD - ROT), F32)
    z8 = jnp.zeros((R, half), F32)
    c = jnp.concatenate([cos, cos, one], 1)
    sa = jnp.concatenate([-sin, z8, zero], 1)
    sb = jnp.concatenate([z8, sin, zero], 1)
    return tuple(jnp.concatenate([t, t], 1) for t in (c, sa, sb))


IN_CHUNKS = ([(0, 512, True), (512, 1024, True), (1024, 1152, True), (1152, 1280, False)]
             + [(c, c + 512, False) for c in range(1280, IN_W, 512)])


def _in_proj(h0, g_pre, w_in, b_in, rope, tr):
    R = h0.shape[0]

    def body(h_ref, g_ref, w_ref, b_ref, c_ref, sa_ref, sb_ref, qkv_ref, glu_ref, gate_ref, n1_ref):
        n, _ = _rms(h_ref[...], g_ref[...])
        nb = n.astype(BF16)
        n1_ref[...] = nb
        for c0, c1, rot in IN_CHUNKS:
            acc = jnp.dot(nb, w_ref[:, c0:c1], preferred_element_type=F32) + b_ref[:, c0:c1]
            if rot:
                reps = (c1 - c0) // 128
                acc = _rope(acc, jnp.tile(c_ref[...], (1, reps)), jnp.tile(sa_ref[...], (1, reps)),
                            jnp.tile(sb_ref[...], (1, reps)))
            val = acc.astype(BF16)
            if c1 <= QKV_W:
                qkv_ref[:, c0:c1] = val
            elif c1 <= QKV_W + GLU_W:
                glu_ref[:, c0 - QKV_W:c1 - QKV_W] = val
            else:
                gate_ref[:, c0 - QKV_W - GLU_W:c1 - QKV_W - GLU_W] = val

    return pl.pallas_call(
        body, name="in_proj", grid=(R // tr,),
        in_specs=[_row(tr, D), _const((1, D)), _const((D, IN_W)), _const((1, IN_W)),
                  _row(tr, 128), _row(tr, 128), _row(tr, 128)],
        out_specs=[_row(tr, QKV_W), _row(tr, GLU_W), _row(tr, GATE_W), _row(tr, D)],
        out_shape=[_sds((R, QKV_W), BF16), _sds((R, GLU_W), BF16), _sds((R, GATE_W), BF16), _sds((R, D), BF16)],
        compiler_params=_params(("arbitrary",)),
    )(h0, g_pre, w_in, b_in, *rope)


def _attn_mask(n):
    qi = lax.broadcasted_iota(jnp.int32, (BLK, 3 * BLK), 0)
    kj = lax.broadcasted_iota(jnp.int32, (BLK, 3 * BLK), 1)
    tq = n * BLK + qi - PAD
    t_meta = kj - PAD
    t_loc = (n - 1) * BLK + (kj - BLK) - PAD
    meta_ok = (kj < BLK) & (t_meta >= 0) & (t_meta <= tq)
    loc_ok = (kj >= BLK) & (t_loc >= NMETA) & (t_loc <= tq) & (tq - t_loc < BLK)
    return meta_ok | loc_ok


def _dup_heads(ref0, refp, refc, low):
    a = jnp.concatenate([ref0[...], refp[...], refc[...]], 0).astype(F32)
    sw = pltpu.roll(a, HD, 1)
    return [jnp.where(low, a, sw).astype(BF16), jnp.where(low, sw, a).astype(BF16)]


def _kv_specs(nb, rev):
    def blk(col, which):
        def idx(t):
            n = nb - 1 - t if rev else t
            return ({"meta": 0, "prev": jnp.maximum(n - 1, 0), "own": n}[which], col)
        return pl.BlockSpec((BLK, BLK), idx)
    return [blk(col, w) for col in (8, 9) for w in ("meta", "prev", "own")]


def _attn_fwd(qkv, sinks):
    R = qkv.shape[0]
    nb = R // BLK

    def body(s_ref, q_ref, k0, kp, kc, v0, vp, vc, o_ref, lse_ref):
        n = pl.program_id(0)
        lane = lax.broadcasted_iota(jnp.int32, (1, BLK), 1)
        low = lane < HD
        kd, vd = _dup_heads(k0, kp, kc, low), _dup_heads(v0, vp, vc, low)
        mask = _attn_mask(n)
        lse = jnp.zeros((BLK, BLK), F32)
        zero = jnp.zeros((), BF16)
        for pair in range(NH // 2):
            qp = q_ref[:, pair * BLK:(pair + 1) * BLK]
            kv = pair // (NH // NKV // 2)
            outs = []
            for e in range(2):
                h = 2 * pair + e
                qm = jnp.where(low if e == 0 else ~low, qp, zero)
                s = lax.dot_general(qm, kd[kv], NT_DIMS, preferred_element_type=F32) * (HD ** -0.5)
                s = jnp.where(mask, s, NEG)
                sk = s_ref[h]
                m = jnp.maximum(jnp.max(s, -1, keepdims=True), sk)
                p = jnp.exp(s - m)
                l = jnp.sum(p, -1, keepdims=True) + jnp.exp(sk - m)
                outs.append(jnp.dot(p.astype(BF16), vd[kv], preferred_element_type=F32) / l)
                lse = jnp.where(lane == h, m + jnp.log(l), lse)
            o_ref[:, pair * BLK:(pair + 1) * BLK] = jnp.where(low, outs[0], outs[1]).astype(BF16)
        lse_ref[...] = lse

    return pl.pallas_call(
        body, name="attn_fwd", grid=(nb,),
        in_specs=[pl.BlockSpec(memory_space=pltpu.SMEM), pl.BlockSpec((BLK, D), lambda n: (n, 0))] + _kv_specs(nb, False),
        out_specs=[_row(BLK, D), _row(BLK, BLK)],
        out_shape=[_sds((R, D), BF16), _sds((R, BLK), F32)],
        compiler_params=_params(("arbitrary",)),
    )(sinks, qkv, *([qkv] * 6))


CONV_TCH, CONV_SUB, HALO = 256, 64, 32


def _conv_fwd(glu, w, b, tr):
    R = glu.shape[0]
    nc = D // CONV_TCH

    def body(a_ref, g_ref, w_ref, b_ref, o_ref, buf):
        i = pl.program_id(1)

        @pl.when(i == 0)
        def _():
            buf[0:HALO, :] = jnp.zeros((HALO, CONV_TCH), F32)

        @pl.when(i > 0)
        def _():
            buf[0:HALO, :] = buf[tr:tr + HALO, :]

        row = i * tr + lax.broadcasted_iota(jnp.int32, (tr, 1), 0)
        a, g = a_ref[...].astype(F32), g_ref[...].astype(F32)
        buf[HALO:HALO + tr, :] = jnp.where(row >= PAD, a * jax.nn.sigmoid(g), 0.0)
        for r0 in range(0, tr, CONV_SUB):
            acc = jnp.broadcast_to(b_ref[...], (CONV_SUB, CONV_TCH))
            for k in range(CONV_K):
                acc = acc + w_ref[k:k + 1, :] * buf[pl.ds(r0 + HALO - (CONV_K - 1) + k, CONV_SUB), :]
            o_ref[r0:r0 + CONV_SUB, :] = acc.astype(BF16)

    return pl.pallas_call(
        body, name="conv_fwd", grid=(nc, R // tr),
        in_specs=[pl.BlockSpec((tr, CONV_TCH), lambda c, i: (i, c)),
                  pl.BlockSpec((tr, CONV_TCH), lambda c, i: (i, nc + c)),
                  pl.BlockSpec((CONV_K, CONV_TCH), lambda c, i: (0, c)),
                  pl.BlockSpec((1, CONV_TCH), lambda c, i: (0, c))],
        out_specs=pl.BlockSpec((tr, CONV_TCH), lambda c, i: (i, c)),
        out_shape=_sds((R, D), BF16),
        scratch_shapes=[pltpu.VMEM((tr + HALO, CONV_TCH), F32)],
        compiler_params=_params(("arbitrary", "arbitrary")),
    )(glu, glu, w, b)


def _ln_silu(c1, lg, lb):
    mu = jnp.mean(c1, -1, keepdims=True)
    xc = c1 - mu
    rs = lax.rsqrt(jnp.mean(xc * xc, -1, keepdims=True) + LN_EPS)
    xh = xc * rs
    c2 = xh * lg + lb
    sg = jax.nn.sigmoid(c2)
    return xh, rs, c2, sg


def _mix_out(attn, c1, gates, h0, w_ap, w_cp, w_out, lg, lb, b_cp, g_post, g_ffn, tr):
    R = attn.shape[0]

    def body(at_ref, c1_ref, ga_ref, gc_ref, h0_ref, wap, wcp, wo, lg_ref, lb_ref, bcp, gp, gf,
             ao_ref, co_ref, c3_ref, mg_ref, mix_ref, h1_ref, n2_ref):
        ao = jnp.dot(at_ref[...], wap[...], preferred_element_type=F32)
        _, _, c2, sg = _ln_silu(c1_ref[...].astype(F32), lg_ref[...], lb_ref[...])
        c3 = (c2 * sg).astype(BF16)
        c3_ref[...] = c3
        co = jnp.dot(c3, wcp[...], preferred_element_type=F32) + bcp[...]
        ao_b, co_b = ao.astype(BF16), co.astype(BF16)
        ao_ref[...] = ao_b
        co_ref[...] = co_b
        merged = (jax.nn.sigmoid(ga_ref[...].astype(F32)) * ao_b.astype(F32)
                  + jax.nn.sigmoid(gc_ref[...].astype(F32)) * co_b.astype(F32)).astype(BF16)
        mg_ref[...] = merged
        mix = jnp.dot(merged, wo[...], preferred_element_type=F32).astype(BF16)
        mix_ref[...] = mix
        y, _ = _rms(mix.astype(F32), gp[...])
        h1 = h0_ref[...] + y
        h1_ref[...] = h1
        n2, _ = _rms(h1, gf[...])
        row = pl.program_id(0) * tr + lax.broadcasted_iota(jnp.int32, (tr, 1), 0)
        n2_ref[...] = jnp.where(row >= PAD, n2, 0.0).astype(BF16)

    vec = _const((1, D))
    return pl.pallas_call(
        body, name="mix_out", grid=(R // tr,),
        in_specs=[_row(tr, D), _row(tr, D), _row(tr, D, 0), _row(tr, D, 1), _row(tr, D),
                  _const((D, D)), _const((D, D)), _const((D, D)), vec, vec, vec, vec, vec],
        out_specs=[_row(tr, D)] * 7,
        out_shape=[_sds((R, D), BF16)] * 5 + [_sds((R, D), F32), _sds((R, D), BF16)],
        compiler_params=_params(("arbitrary",)),
    )(attn, c1, gates, gates, h0, w_ap, w_cp, w_out, lg, lb, b_cp, g_post, g_ffn)


FFN_CH = 256


def _shift_down(x, k, halo):
    tr = x.shape[0]
    row = lax.broadcasted_iota(jnp.int32, (tr, 1), 0)
    y = pltpu.roll(x, k, 0)
    for j in range(k):
        y = jnp.where(row == j, halo[8 - k + j:8 - k + j + 1, :], y)
    return y


def _shift_up(x, k, halo):
    tr = x.shape[0]
    row = lax.broadcasted_iota(jnp.int32, (tr, 1), 0)
    y = pltpu.roll(x, tr - k, 0)
    for j in range(k):
        y = jnp.where(row == tr - k + j, halo[j:j + 1, :], y)
    return y


def _conv3(x, halo, w, b):
    return w[2:3, :] * x + w[1:2, :] * _shift_down(x, 1, halo) + w[0:1, :] * _shift_down(x, 2, halo) + b


def _ffn_up(n2, w_up, fw, fb, tr):
    R = n2.shape[0]

    def body(n_ref, w_ref, fw_ref, fb_ref, up_ref, act_ref, carry):
        @pl.when(pl.program_id(0) == 0)
        def _():
            carry[...] = jnp.zeros_like(carry)

        nb = n_ref[...]
        for c in range(0, FFN, FFN_CH):
            us = []
            for off in (c, FFN + c):
                cs = slice(off, off + FFN_CH)
                x = jnp.dot(nb, w_ref[:, cs], preferred_element_type=F32).astype(BF16)
                up_ref[:, cs] = x
                x = x.astype(F32)
                us.append(_conv3(x, carry[:, cs], fw_ref[:, cs], fb_ref[:, cs]))
                carry[:, cs] = x[tr - 8:tr, :]
            act_ref[:, c:c + FFN_CH] = (us[0] * jax.nn.sigmoid(us[0]) * us[1]).astype(BF16)

    return pl.pallas_call(
        body, name="ffn_up", grid=(R // tr,),
        in_specs=[_row(tr, D), _const((D, 2 * FFN)), _const((FFN_K, 2 * FFN)), _const((1, 2 * FFN))],
        out_specs=[_row(tr, 2 * FFN), _row(tr, FFN)],
        out_shape=[_sds((R, 2 * FFN), BF16), _sds((R, FFN), BF16)],
        scratch_shapes=[pltpu.VMEM((8, 2 * FFN), F32)],
        compiler_params=_params(("arbitrary",)),
    )(n2, w_up, fw, fb)


def _ffn_down(act, w_down, h1, tgt, g_post, tr):
    R = act.shape[0]

    def body(a_ref, w_ref, h1_ref, t_ref, g_ref, dh2_ref, dffn_ref, loss_ref, dg_ref):
        @pl.when(pl.program_id(0) == 0)
        def _():
            loss_ref[...] = jnp.zeros_like(loss_ref)
            dg_ref[...] = jnp.zeros_like(dg_ref)

        f = jnp.dot(a_ref[...], w_ref[...], preferred_element_type=F32)
        g = g_ref[...]
        y, r = _rms(f, g)
        row = pl.program_id(0) * tr + lax.broadcasted_iota(jnp.int32, (tr, 1), 0)
        e = jnp.where(row >= BLK, h1_ref[...] + y - t_ref[...], 0.0)
        loss_ref[...] += _colsum(e * e) * (0.5 / D)
        dy = e * (1.0 / D)
        dh2_ref[...] = dy
        dffn_ref[...] = _rms_bwd(dy, f, r, g).astype(BF16)
        dg_ref[...] += _colsum(dy * f * r)

    return pl.pallas_call(
        body, name="ffn_down", grid=(R // tr,),
        in_specs=[_row(tr, FFN), _const((FFN, D)), _row(tr, D), _row(tr, D), _const((1, D))],
        out_specs=[_row(tr, D), _row(tr, D), _const((1, D)), _const((1, D))],
        out_shape=[_sds((R, D), F32), _sds((R, D), BF16), _sds((1, D), F32), _sds((1, D), F32)],
        compiler_params=_params(("arbitrary",)),
    )(act, w_down, h1, tgt, g_post)


def _ffn_bwd_act(dffn, w_down, up, fw, fb, tr):
    R = dffn.shape[0]
    nt = R // tr

    def body(d_ref, w_ref, up_ref, hal_ref, fw_ref, fb_ref, dup_ref, dfw_ref, dfb_ref, carry):
        t = pl.program_id(0)
        i = nt - 1 - t

        @pl.when(t == 0)
        def _():
            carry[...] = jnp.zeros_like(carry)
            dfw_ref[...] = jnp.zeros_like(dfw_ref)
            dfb_ref[...] = jnp.zeros_like(dfb_ref)

        dff = d_ref[...]
        row = i * tr + lax.broadcasted_iota(jnp.int32, (tr, 1), 0)
        first = i == 0
        for c in range(0, FFN, FFN_CH):
            dact = lax.dot_general(dff, w_ref[c:c + FFN_CH, :], NT_DIMS, preferred_element_type=F32)
            xs, us = [], []
            for off in (c, FFN + c):
                cs = slice(off, off + FFN_CH)
                x = up_ref[:, cs].astype(F32)
                halo = jnp.where(first, 0.0, hal_ref[:, cs].astype(F32))
                x1, x2 = _shift_down(x, 1, halo), _shift_down(x, 2, halo)
                w = fw_ref[:, cs]
                us.append(w[2:3, :] * x + w[1:2, :] * x1 + w[0:1, :] * x2 + fb_ref[:, cs])
                xs.append((x, x1, x2))
            sg = jax.nn.sigmoid(us[0])
            silu = us[0] * sg
            dus = [dact * us[1] * sg * (1.0 + us[0] * (1.0 - sg)), dact * silu]
            for (x, x1, x2), du, off in zip(xs, dus, (c, FFN + c)):
                cs = slice(off, off + FFN_CH)
                w = fw_ref[:, cs]
                nxt = carry[:, cs]
                dx = w[2:3, :] * du + w[1:2, :] * _shift_up(du, 1, nxt) + w[0:1, :] * _shift_up(du, 2, nxt)
                dup_ref[:, cs] = jnp.where(row >= PAD, dx, 0.0).astype(BF16)
                dfw_ref[0:1, cs] += _colsum(x2 * du)
                dfw_ref[1:2, cs] += _colsum(x1 * du)
                dfw_ref[2:3, cs] += _colsum(x * du)
                dfb_ref[:, cs] += _colsum(du)
                carry[:, cs] = du[0:8, :]

    halo_spec = pl.BlockSpec((8, 2 * FFN), lambda t: (jnp.maximum((nt - 1 - t) * (tr // 8) - 1, 0), 0))
    return pl.pallas_call(
        body, name="ffn_bwd_act", grid=(nt,),
        in_specs=[_rrow(tr, D, nt), _const((FFN, D)), _rrow(tr, 2 * FFN, nt), halo_spec,
                  _const((FFN_K, 2 * FFN)), _const((1, 2 * FFN))],
        out_specs=[_rrow(tr, 2 * FFN, nt), _const((8, 2 * FFN)), _const((1, 2 * FFN))],
        out_shape=[_sds((R, 2 * FFN), BF16), _sds((8, 2 * FFN), F32), _sds((1, 2 * FFN), F32)],
        scratch_shapes=[pltpu.VMEM((8, 2 * FFN), F32)],
        compiler_params=_params(("arbitrary",)),
    )(dffn, w_down, up, up, fw, fb)


def _ffn_bwd_in(dup, w_up, h1, dh2, mix, g_ffn, g_post, tr):
    R = dup.shape[0]

    def body(d_ref, w_ref, h1_ref, dh2_ref, mix_ref, gf_ref, gp_ref, dh1_ref, dmix_ref, dgf_ref, dgp_ref):
        @pl.when(pl.program_id(0) == 0)
        def _():
            dgf_ref[...] = jnp.zeros_like(dgf_ref)
            dgp_ref[...] = jnp.zeros_like(dgp_ref)

        dn2 = lax.dot_general(d_ref[...], w_ref[...], NT_DIMS, preferred_element_type=F32)
        h1 = h1_ref[...]
        _, r2 = _rms(h1, gf_ref[...])
        dh1 = dh2_ref[...] + _rms_bwd(dn2, h1, r2, gf_ref[...])
        dgf_ref[...] += _colsum(dn2 * h1 * r2)
        dh1_ref[...] = dh1
        m = mix_ref[...].astype(F32)
        _, rm = _rms(m, gp_ref[...])
        dmix_ref[...] = _rms_bwd(dh1, m, rm, gp_ref[...]).astype(BF16)
        dgp_ref[...] += _colsum(dh1 * m * rm)

    vec = _const((1, D))
    return pl.pallas_call(
        body, name="ffn_bwd_in", grid=(R // tr,),
        in_specs=[_row(tr, 2 * FFN), _const((D, 2 * FFN)), _row(tr, D), _row(tr, D), _row(tr, D), vec, vec],
        out_specs=[_row(tr, D), _row(tr, D), vec, vec],
        out_shape=[_sds((R, D), F32), _sds((R, D), BF16), _sds((1, D), F32), _sds((1, D), F32)],
        compiler_params=_params(("arbitrary",)),
    )(dup, w_up, h1, dh2, mix, g_ffn, g_post)


def _mix_bwd(dmix, ao, co, gates, c1, w_out, w_ap, w_cp, lg, lb, tr):
    R = dmix.shape[0]

    def body(dm_ref, ao_ref, co_ref, ga_ref, gc_ref, c1_ref, wo, wap, wcp, lg_ref, lb_ref,
             dao_ref, dco_ref, dgate_ref, dattn_ref, dc1_ref, dbcp_ref, dlg_ref, dlb_ref, dcb_ref):
        @pl.when(pl.program_id(0) == 0)
        def _():
            for ref in (dbcp_ref, dlg_ref, dlb_ref, dcb_ref):
                ref[...] = jnp.zeros_like(ref)

        dmg = lax.dot_general(dm_ref[...], wo[...], NT_DIMS, preferred_element_type=F32)
        sa = jax.nn.sigmoid(ga_ref[...].astype(F32))
        sc = jax.nn.sigmoid(gc_ref[...].astype(F32))
        dao = (dmg * sa).astype(BF16)
        dco = (dmg * sc).astype(BF16)
        dao_ref[...] = dao
        dco_ref[...] = dco
        dgate_ref[:, 0:D] = (dmg * ao_ref[...].astype(F32) * sa * (1.0 - sa)).astype(BF16)
        dgate_ref[:, D:2 * D] = (dmg * co_ref[...].astype(F32) * sc * (1.0 - sc)).astype(BF16)
        dbcp_ref[...] += _colsum(dco.astype(F32))
        dattn_ref[...] = lax.dot_general(dao, wap[...], NT_DIMS, preferred_element_type=F32).astype(BF16)
        dc3 = lax.dot_general(dco, wcp[...], NT_DIMS, preferred_element_type=F32)
        xh, rs, c2, sg = _ln_silu(c1_ref[...].astype(F32), lg_ref[...], lb_ref[...])
        dc2 = dc3 * sg * (1.0 + c2 * (1.0 - sg))
        dlg_ref[...] += _colsum(dc2 * xh)
        dlb_ref[...] += _colsum(dc2)
        dxh = dc2 * lg_ref[...]
        dc1 = rs * (dxh - jnp.mean(dxh, -1, keepdims=True) - xh * jnp.mean(dxh * xh, -1, keepdims=True))
        dc1_ref[...] = dc1
        dcb_ref[...] += _colsum(dc1)

    vec = _const((1, D))
    return pl.pallas_call(
        body, name="mix_bwd", grid=(R // tr,),
        in_specs=[_row(tr, D), _row(tr, D), _row(tr, D), _row(tr, D, 0), _row(tr, D, 1), _row(tr, D),
                  _const((D, D)), _const((D, D)), _const((D, D)), vec, vec],
        out_specs=[_row(tr, D), _row(tr, D), _row(tr, 2 * D), _row(tr, D), _row(tr, D), vec, vec, vec, vec],
        out_shape=[_sds((R, D), BF16), _sds((R, D), BF16), _sds((R, 2 * D), BF16), _sds((R, D), BF16),
                   _sds((R, D), F32)] + [_sds((1, D), F32)] * 4,
        compiler_params=_params(("arbitrary",)),
    )(dmix, ao, co, gates, gates, c1, w_out, w_ap, w_cp, lg, lb)


def _conv_bwd(dc1, glu, w, tr):
    R = dc1.shape[0]
    nt, nc = R // tr, D // CONV_TCH

    def body(d_ref, a_ref, g_ref, w_ref, dglu_a, dglu_g, dw_ref, buf):
        t = pl.program_id(1)
        i = nt - 1 - t

        @pl.when(t == 0)
        def _():
            buf[tr:tr + HALO, :] = jnp.zeros((HALO, CONV_TCH), F32)
            dw_ref[...] = jnp.zeros_like(dw_ref)

        @pl.when(t > 0)
        def _():
            buf[tr:tr + HALO, :] = buf[0:HALO, :]

        buf[0:tr, :] = d_ref[...]
        for r0 in range(0, tr, CONV_SUB):
            rs = slice(r0, r0 + CONV_SUB)
            row = i * tr + r0 + lax.broadcasted_iota(jnp.int32, (CONV_SUB, 1), 0)
            a, g = a_ref[rs, :].astype(F32), g_ref[rs, :].astype(F32)
            sg = jax.nn.sigmoid(g)
            glu = jnp.where(row >= PAD, a * sg, 0.0)
            acc = jnp.zeros((CONV_SUB, CONV_TCH), F32)
            for k in range(CONV_K):
                win = buf[pl.ds(r0 + CONV_K - 1 - k, CONV_SUB), :]
                acc = acc + w_ref[k:k + 1, :] * win
                dw_ref[k:k + 1, :] += _colsum(glu * win)
            dglu = jnp.where(row >= PAD, acc, 0.0)
            dglu_a[rs, :] = (dglu * sg).astype(BF16)
            dglu_g[rs, :] = (dglu * a * sg * (1.0 - sg)).astype(BF16)

    def rspec(col0):
        return pl.BlockSpec((tr, CONV_TCH), lambda c, t: (nt - 1 - t, col0 + c))

    return pl.pallas_call(
        body, name="conv_bwd", grid=(nc, nt),
        in_specs=[rspec(0), rspec(0), rspec(nc), pl.BlockSpec((CONV_K, CONV_TCH), lambda c, t: (0, c))],
        out_specs=[rspec(0), rspec(0), pl.BlockSpec((HALO, CONV_TCH), lambda c, t: (0, c))],
        out_shape=[_sds((R, D), BF16), _sds((R, D), BF16), _sds((HALO, D), F32)],
        scratch_shapes=[pltpu.VMEM((tr + HALO, CONV_TCH), F32)],
        compiler_params=_params(("arbitrary", "arbitrary")),
    )(dc1, glu, glu, w)


def _attn_bwd(qkv, o, do, lse, sinks, rope):
    R = qkv.shape[0]
    nb = R // BLK

    def body(s_ref, q_ref, k0, kp, kc, v0, vp, vc, o_ref, do_ref, lse_ref, c_ref, sa_ref, sb_ref,
             dq_ref, dk_ref, dv_ref, dsink_ref, car_k, car_v, met_k, met_v):
        t = pl.program_id(0)
        n = nb - 1 - t

        @pl.when(t == 0)
        def _():
            for ref in (car_k, car_v, met_k, met_v, dsink_ref):
                ref[...] = jnp.zeros_like(ref)

        lane = lax.broadcasted_iota(jnp.int32, (1, BLK), 1)
        low = lane < HD
        kd, vd = _dup_heads(k0, kp, kc, low), _dup_heads(v0, vp, vc, low)
        mask = _attn_mask(n)
        tabs = (c_ref[...], sa_ref[...], sb_ref[...])
        zero = jnp.zeros((), BF16)
        dk_acc = [jnp.zeros((3 * BLK, BLK), F32) for _ in range(NKV)]
        dv_acc = [jnp.zeros((3 * BLK, BLK), F32) for _ in range(NKV)]
        dsink = jnp.zeros((1, BLK), F32)
        lse_all = lse_ref[...]
        for pair in range(NH // 2):
            cs = slice(pair * BLK, (pair + 1) * BLK)
            qp, dop = q_ref[:, cs], do_ref[:, cs]
            prod = dop.astype(F32) * o_ref[:, cs].astype(F32)
            kv = pair // (NH // NKV // 2)
            dqs = []
            for e in range(2):
                h = 2 * pair + e
                sel = low if e == 0 else ~low
                qm, dom = jnp.where(sel, qp, zero), jnp.where(sel, dop, zero)
                lse_h = jnp.sum(jnp.where(lane == h, lse_all, 0.0), -1, keepdims=True)
                s = lax.dot_general(qm, kd[kv], NT_DIMS, preferred_element_type=F32) * (HD ** -0.5)
                p = jnp.where(mask, jnp.exp(s - lse_h), 0.0)
                dp = lax.dot_general(dom, vd[kv], NT_DIMS, preferred_element_type=F32)
                delta = jnp.sum(jnp.where(sel, prod, 0.0), -1, keepdims=True)
                ds = (p * (dp - delta) * (HD ** -0.5)).astype(BF16)
                dqs.append(jnp.dot(ds, kd[kv], preferred_element_type=F32))
                dk_acc[kv] = dk_acc[kv] + lax.dot_general(ds, qm, TN_DIMS, preferred_element_type=F32)
                dv_acc[kv] = dv_acc[kv] + lax.dot_general(p.astype(BF16), dom, TN_DIMS, preferred_element_type=F32)
                ps = jnp.exp(s_ref[h] - lse_h)
                dsink = dsink + jnp.where(lane == h, -jnp.sum(ps * delta), 0.0)
            dq_ref[:, cs] = _rope_bwd(jnp.where(low, dqs[0], dqs[1]), *tabs).astype(BF16)
        dsink_ref[0:1, :] += dsink

        def fold(acc):
            tot = [a + pltpu.roll(a, HD, 1) for a in acc]
            return jnp.where(low, tot[0], tot[1])

        dk_all, dv_all = fold(dk_acc), fold(dv_acc)
        met_k[...] += dk_all[0:BLK, :]
        met_v[...] += dv_all[0:BLK, :]
        last = jnp.where(n == 0, 1.0, 0.0)
        dk_n = dk_all[2 * BLK:3 * BLK, :] + car_k[...] + last * met_k[...]
        dv_n = dv_all[2 * BLK:3 * BLK, :] + car_v[...] + last * met_v[...]
        dk_ref[...] = _rope_bwd(dk_n, *tabs).astype(BF16)
        dv_ref[...] = dv_n.astype(BF16)
        car_k[...] = dk_all[BLK:2 * BLK, :]
        car_v[...] = dv_all[BLK:2 * BLK, :]

    rblk = lambda w: pl.BlockSpec((BLK, w), lambda t: (nb - 1 - t, 0))
    return pl.pallas_call(
        body, name="attn_bwd", grid=(nb,),
        in_specs=[pl.BlockSpec(memory_space=pltpu.SMEM), rblk(D)] + _kv_specs(nb, True)
                 + [rblk(D), rblk(D), rblk(BLK), rblk(BLK), rblk(BLK), rblk(BLK)],
        out_specs=[rblk(D), rblk(BLK), rblk(BLK), _const((8, BLK))],
        out_shape=[_sds((R, D), BF16), _sds((R, BLK), BF16), _sds((R, BLK), BF16), _sds((8, BLK), F32)],
        scratch_shapes=[pltpu.VMEM((BLK, BLK), F32)] * 4,
        compiler_params=_params(("arbitrary",)),
    )(sinks, qkv, *([qkv] * 6), o, do, lse, *rope)


def _in_bwd(dproj, w_in, h0, dh1, g_pre, tr):
    R = dproj.shape[0]

    def body(d_ref, w_ref, h0_ref, dh1_ref, g_ref, dh0_ref, dg_ref, db_ref):
        @pl.when(pl.program_id(0) == 0)
        def _():
            dg_ref[...] = jnp.zeros_like(dg_ref)
            db_ref[...] = jnp.zeros_like(db_ref)

        d = d_ref[...]
        dn1 = lax.dot_general(d, w_ref[...], NT_DIMS, preferred_element_type=F32)
        h0 = h0_ref[...]
        _, r = _rms(h0, g_ref[...])
        dh0_ref[...] = dh1_ref[...] + _rms_bwd(dn1, h0, r, g_ref[...])
        dg_ref[...] += _colsum(dn1 * h0 * r)
        db_ref[...] += _colsum(d.astype(F32))

    return pl.pallas_call(
        body, name="in_bwd", grid=(R // tr,),
        in_specs=[_row(tr, IN_W), _const((D, IN_W)), _row(tr, D), _row(tr, D), _const((1, D))],
        out_specs=[_row(tr, D), _const((1, D)), _const((1, IN_W))],
        out_shape=[_sds((R, D), F32), _sds((1, D), F32), _sds((1, IN_W), F32)],
        compiler_params=_params(("arbitrary",)),
    )(dproj, w_in, h0, dh1, g_pre)


def _dw(a, b, name, tn, tr):
    R, ka = a.shape
    n = b.shape[1]
    nt = R // tr

    def body(a_ref, b_ref, o_ref, acc):
        i = pl.program_id(1)

        @pl.when(i == 0)
        def _():
            acc[...] = jnp.zeros_like(acc)

        acc[...] += lax.dot_general(a_ref[...], b_ref[...], TN_DIMS, preferred_element_type=F32)

        @pl.when(i == nt - 1)
        def _():
            o_ref[...] = acc[...].astype(BF16)

    return pl.pallas_call(
        body, name=name, grid=(n // tn, nt),
        in_specs=[pl.BlockSpec((tr, ka), lambda j, i: (i, 0)), pl.BlockSpec((tr, tn), lambda j, i: (i, j))],
        out_specs=pl.BlockSpec((ka, tn), lambda j, i: (0, j)),
        out_shape=_sds((ka, n), BF16),
        scratch_shapes=[pltpu.VMEM((ka, tn), F32)],
        compiler_params=_params(("arbitrary", "arbitrary")),
    )(a, b)


SMALL = ["norm_pre_mix", "norm_post_mix", "b_in", "attn_sinks", "conv_dw_b", "conv_ln_g", "conv_ln_b",
         "b_conv_proj", "norm_pre_ffn", "norm_post_ffn", "ffn_dw_b"]


def local_step(x, tgt, W):
    S = x.shape[0]
    R = S + BLK
    tr = _tile(R, 384)
    trw = _tile(R, 1056)
    rope = _rope_tables(R)
    h0 = jnp.concatenate([jnp.zeros((PAD, D), F32), W["meta_tokens"], x], 0)
    tgt_p = jnp.concatenate([jnp.zeros((BLK, D), F32), tgt], 0)

    qkv, glu, gates, n1 = _in_proj(h0, W["norm_pre_mix"], W["w_in"], W["b_in"], rope, tr)
    sinks = W["attn_sinks"].reshape(NH)
    attn, lse = _attn_fwd(qkv, sinks)
    c1 = _conv_fwd(glu, W["conv_dw_w"], W["conv_dw_b"], tr)
    ao, co, c3, merged, mix, h1, n2 = _mix_out(
        attn, c1, gates, h0, W["w_attn_proj"], W["w_conv_proj"], W["w_out"], W["conv_ln_g"], W["conv_ln_b"],
        W["b_conv_proj"], W["norm_post_mix"], W["norm_pre_ffn"], tr)
    up, act = _ffn_up(n2, W["w_up"], W["ffn_dw_w"], W["ffn_dw_b"], tr)
    dh2, dffn, loss_cols, dg_post_ffn = _ffn_down(act, W["w_down"], h1, tgt_p, W["norm_post_ffn"], tr)

    dup, dfw, dfb = _ffn_bwd_act(dffn, W["w_down"], up, W["ffn_dw_w"], W["ffn_dw_b"], tr)
    dh1, dmix, dg_pre_ffn, dg_post_mix = _ffn_bwd_in(dup, W["w_up"], h1, dh2, mix, W["norm_pre_ffn"],
                                                      W["norm_post_mix"], tr)
    dao, dco, dgates, dattn, dc1, db_cp, dlg, dlb, dcb = _mix_bwd(
        dmix, ao, co, gates, c1, W["w_out"], W["w_attn_proj"], W["w_conv_proj"], W["conv_ln_g"], W["conv_ln_b"], tr)
    dglu_a, dglu_g, dcw = _conv_bwd(dc1, glu, W["conv_dw_w"], tr)
    dq, dk, dv, dsink = _attn_bwd(qkv, attn, dattn, lse, sinks, rope)
    dproj = jnp.concatenate([dq, dk, dv, dglu_a, dglu_g, dgates], 1)
    dh0, dg_pre_mix, db_in = _in_bwd(dproj, W["w_in"], h0, dh1, W["norm_pre_mix"], tr)

    grads = {
        "w_in": _dw(n1, dproj, "dw_in", 768, trw),
        "w_attn_proj": _dw(attn, dao, "dw_attn_proj", D, trw),
        "w_conv_proj": _dw(c3, dco, "dw_conv_proj", D, trw),
        "w_out": _dw(merged, dmix, "dw_out", D, trw),
        "w_up": _dw(n2, dup, "dw_up", 512, trw),
        "w_down": _dw(act, dffn, "dw_down", 512, trw),
        "conv_dw_w": dcw[0:CONV_K],
        "ffn_dw_w": dfw[0:FFN_K],
        "meta_tokens": dh0[PAD:BLK],
        "norm_pre_mix": dg_pre_mix, "norm_post_mix": dg_post_mix, "b_in": db_in,
        "attn_sinks": dsink[0:1, 0:NH], "conv_dw_b": dcb, "conv_ln_g": dlg, "conv_ln_b": dlb,
        "b_conv_proj": db_cp, "norm_pre_ffn": dg_pre_ffn, "norm_post_ffn": dg_post_ffn, "ffn_dw_b": dfb,
    }
    return loss_cols, dh0[BLK:], grads


BIG = [("w_in", (D, IN_W), 1, False), ("w_attn_proj", (D, D), 0, False), ("w_conv_proj", (D, D), 0, False),
       ("w_out", (D, D), 0, False), ("w_up", (D, 2 * FFN), 1, False), ("w_down", (FFN, D), 0, False),
       ("conv_dw_w", (CONV_K, D), 1, True), ("ffn_dw_w", (FFN_K, 2 * FFN), 1, True), ("meta_tokens", (NMETA, D), 1, True)]
N_CHIPS = 4
PACK_W = 1024
QROWS = 4288
HROWS = QROWS // 2
SUM_TR = 1072


def _qshape(shape, axis):
    return tuple(s // N_CHIPS if a == axis else s for a, s in enumerate(shape))


def _plane_rows(shape, axis):
    n = int(np.prod(_qshape(shape, axis)))
    return -(-(-(-n // PACK_W)) // 8) * 8


def _to_planes(a, rows):
    lead = a.shape[:-1]
    a = jnp.pad(a, [(0, 0)] * len(lead) + [(0, rows * PACK_W - a.shape[-1])])
    return a.reshape(lead + (rows, PACK_W))


def _hi_lo(a):
    hi = a.astype(BF16)
    return hi, (a.astype(F32) - hi.astype(F32)).astype(BF16)


def _pack(parts):
    planes = []
    for name, shape, axis, split in BIG:
        rows = _plane_rows(shape, axis)
        a = parts[name]
        lead = a.shape[:a.ndim - 2]
        flat = a.reshape(lead + (-1,))
        for piece in (_hi_lo(flat) if split else (flat.astype(BF16),)):
            planes.append(_to_planes(piece, rows))
    used = sum(p.shape[-2] for p in planes)
    lead = planes[0].shape[:-2]
    planes.append(jnp.zeros(lead + (QROWS - used, PACK_W), BF16))
    return jnp.concatenate(planes, axis=-2)


def _unpack(packed, dtype_of):
    out, r0 = {}, 0
    lead = packed.shape[:-2]
    for name, shape, axis, split in BIG:
        rows = _plane_rows(shape, axis)
        qs = _qshape(shape, axis)
        n = int(np.prod(qs))
        pieces = []
        for _ in range(2 if split else 1):
            seg = packed[..., r0:r0 + rows, :].reshape(lead + (rows * PACK_W,))[..., :n]
            pieces.append(seg.reshape(lead + qs))
            r0 += rows
        out[name] = (pieces[0].astype(F32) + pieces[1].astype(F32)) if split else pieces[0].astype(dtype_of(name))
    return out


def _split_quarters(full):
    out = {}
    for name, shape, axis, _ in BIG:
        a = full[name]
        qs = _qshape(shape, axis)
        if axis == 0:
            out[name] = a.reshape((N_CHIPS,) + qs)
        else:
            out[name] = a.reshape(shape[0], N_CHIPS, qs[1]).transpose(1, 0, 2)
    return out


def _join_quarters(parts):
    out = {}
    for name, shape, axis, _ in BIG:
        a = parts[name]
        out[name] = a.reshape(shape) if axis == 0 else a.transpose(1, 0, 2).reshape(shape)
    return out


ANY = pl.BlockSpec(memory_space=pl.ANY)


def _place():
    x, y, c = lax.axis_index("x"), lax.axis_index("y"), lax.axis_index("c")
    chips = [(1 - x, y), (x, 1 - y), (1 - x, 1 - y)]
    return x, y, c, chips


def _rcopy(src, dst, ssem, rsem, to):
    return pltpu.make_async_remote_copy(src_ref=src, dst_ref=dst, send_sem=ssem, recv_sem=rsem,
                                        device_id=to, device_id_type=MESH)


def _gather_quarters(pq):
    def body(p_ref, out_ref, ssem, rsem, lsem):
        x, y, c, chips = _place()
        q = 2 * x + y
        sib = (x, y, 1 - c)
        mine, other = pl.ds(c * HROWS, HROWS), pl.ds((1 - c) * HROWS, HROWS)
        local = pltpu.make_async_copy(p_ref, out_ref.at[q], lsem)
        local.start()
        first = [_rcopy(p_ref.at[mine], out_ref.at[q, mine], ssem.at[j], rsem.at[j], (cx, cy, c))
                 for j, (cx, cy) in enumerate(chips)]
        for cp in first:
            cp.start()
        passed = []
        for j, (cx, cy) in enumerate(chips):
            landed = out_ref.at[2 * cx + cy, mine]
            _rcopy(p_ref.at[mine], landed, ssem.at[j], rsem.at[j], (cx, cy, c)).wait_recv()
            cp = _rcopy(landed, landed, ssem.at[3 + j], rsem.at[3 + j], sib)
            cp.start()
            passed.append(cp)
        for j, (cx, cy) in enumerate(chips):
            theirs = out_ref.at[2 * cx + cy, other]
            _rcopy(theirs, theirs, ssem.at[3 + j], rsem.at[3 + j], sib).wait_recv()
        for cp in first + passed:
            cp.wait_send()
        local.wait()

    return pl.pallas_call(
        body, name="gather_quarters", in_specs=[ANY], out_specs=ANY,
        out_shape=_sds((N_CHIPS, QROWS, PACK_W), BF16),
        scratch_shapes=[pltpu.SemaphoreType.DMA((6,)), pltpu.SemaphoreType.DMA((6,)), pltpu.SemaphoreType.DMA],
    )(pq)


def _sibling_swap(p):
    def body(p_ref, recv_ref, ssem, rsem):
        x, y, c, _ = _place()
        cp = _rcopy(p_ref.at[:, pl.ds((1 - c) * HROWS, HROWS)], recv_ref, ssem, rsem, (x, y, 1 - c))
        cp.start()
        cp.wait()

    return pl.pallas_call(
        body, name="sibling_swap", in_specs=[ANY], out_specs=ANY,
        out_shape=_sds((N_CHIPS, HROWS, PACK_W), BF16),
        scratch_shapes=[pltpu.SemaphoreType.DMA, pltpu.SemaphoreType.DMA],
    )(p)


def _sum_pair(p, recv, c):
    nt = HROWS // SUM_TR

    def body(c_ref, a_ref, b_ref, o_ref):
        o_ref[...] = (a_ref[...].astype(F32) + b_ref[...].astype(F32)).astype(BF16)

    grid_spec = pltpu.PrefetchScalarGridSpec(
        num_scalar_prefetch=1, grid=(N_CHIPS, nt),
        in_specs=[pl.BlockSpec((None, SUM_TR, PACK_W), lambda q, i, c_ref: (q, c_ref[0] * nt + i, 0)),
                  pl.BlockSpec((None, SUM_TR, PACK_W), lambda q, i, c_ref: (q, i, 0))],
        out_specs=pl.BlockSpec((None, SUM_TR, PACK_W), lambda q, i, c_ref: (q, i, 0)))
    return pl.pallas_call(
        body, name="sum_pair", grid_spec=grid_spec, out_shape=_sds((N_CHIPS, HROWS, PACK_W), BF16),
        compiler_params=_params(("arbitrary", "arbitrary")),
    )(c, p, recv)


def _chip_swap(cp_sum):
    def body(s_ref, recv_ref, ssem, rsem):
        x, y, c, chips = _place()
        copies = [_rcopy(s_ref.at[2 * cx + cy], recv_ref.at[j], ssem.at[j], rsem.at[j], (cx, cy, c))
                  for j, (cx, cy) in enumerate(chips)]
        for cp in copies:
            cp.start()
        for cp in copies:
            cp.wait()

    return pl.pallas_call(
        body, name="chip_swap", in_specs=[ANY], out_specs=ANY,
        out_shape=_sds((N_CHIPS - 1, HROWS, PACK_W), BF16),
        scratch_shapes=[pltpu.SemaphoreType.DMA((3,)), pltpu.SemaphoreType.DMA((3,))],
    )(cp_sum)


def _sum_chips(cp_sum, recv, q):
    nt = HROWS // SUM_TR

    def body(q_ref, a_ref, r0_ref, r1_ref, r2_ref, o_ref):
        acc = a_ref[...].astype(F32)
        for ref in (r0_ref, r1_ref, r2_ref):
            acc = acc + ref[...].astype(F32)
        o_ref[...] = acc

    def from_chip(j):
        return pl.BlockSpec((None, SUM_TR, PACK_W), lambda i, q_ref: (j, i, 0))

    grid_spec = pltpu.PrefetchScalarGridSpec(
        num_scalar_prefetch=1, grid=(nt,),
        in_specs=[pl.BlockSpec((None, SUM_TR, PACK_W), lambda i, q_ref: (q_ref[0], i, 0)),
                  from_chip(0), from_chip(1), from_chip(2)],
        out_specs=pl.BlockSpec((SUM_TR, PACK_W), lambda i, q_ref: (i, 0)))
    return pl.pallas_call(
        body, name="sum_chips", grid_spec=grid_spec, out_shape=_sds((HROWS, PACK_W), F32),
        compiler_params=_params(("arbitrary",)),
    )(q, cp_sum, recv, recv, recv)


def _sibling_share(g_half):
    def body(g_ref, out_ref, ssem, rsem, lsem):
        x, y, c, _ = _place()
        mine = out_ref.at[pl.ds(c * HROWS, HROWS)]
        local = pltpu.make_async_copy(g_ref, mine, lsem)
        local.start()
        cp = _rcopy(g_ref, mine, ssem, rsem, (x, y, 1 - c))
        cp.start()
        cp.wait_send()
        theirs = out_ref.at[pl.ds((1 - c) * HROWS, HROWS)]
        _rcopy(g_ref, theirs, ssem, rsem, (x, y, 1 - c)).wait_recv()
        local.wait()

    return pl.pallas_call(
        body, name="sibling_share", in_specs=[ANY], out_specs=ANY,
        out_shape=_sds((QROWS, PACK_W), F32),
        scratch_shapes=[pltpu.SemaphoreType.DMA, pltpu.SemaphoreType.DMA, pltpu.SemaphoreType.DMA],
    )(g_half)


N_DEV = 8
SMALL_ROWS = 24


def _small_allreduce(sm):
    def body(s_ref, o_ref, buf, ssem, rsem):
        x, y, c, _ = _place()
        me = 4 * x + 2 * y + c
        buf[me] = s_ref[...]
        copies = []
        for d in range(1, N_DEV):
            dx, dy, dc = d >> 2, (d >> 1) & 1, d & 1
            to = (x ^ dx, y ^ dy, c ^ dc)
            cp = _rcopy(s_ref, buf.at[me], ssem.at[d - 1], rsem.at[d - 1], to)
            cp.start()
            copies.append(cp)
        for d in range(1, N_DEV):
            src = me ^ d
            _rcopy(s_ref, buf.at[src], ssem.at[d - 1], rsem.at[d - 1], (x, y, c)).wait_recv()
        for cp in copies:
            cp.wait_send()
        acc = buf[0]
        for k in range(1, N_DEV):
            acc = acc + buf[k]
        o_ref[...] = acc

    vm = pl.BlockSpec(memory_space=pltpu.VMEM)
    return pl.pallas_call(
        body, name="small_allreduce", in_specs=[vm], out_specs=vm,
        out_shape=_sds((SMALL_ROWS, PACK_W), F32),
        scratch_shapes=[pltpu.VMEM((N_DEV, SMALL_ROWS, PACK_W), F32),
                        pltpu.SemaphoreType.DMA((N_DEV - 1,)), pltpu.SemaphoreType.DMA((N_DEV - 1,))],
    )(sm)


SMALL_PLAN = [("norm_pre_mix", D), ("norm_post_mix", D), ("b_in", IN_W), ("attn_sinks", NH), ("conv_dw_b", D),
              ("conv_ln_g", D), ("conv_ln_b", D), ("b_conv_proj", D), ("norm_pre_ffn", D), ("norm_post_ffn", D),
              ("ffn_dw_b", 2 * FFN), ("loss", D)]


def _pack_small(parts):
    rows = [_to_planes(parts[name].reshape(-1), -(-n // PACK_W)) for name, n in SMALL_PLAN]
    used = sum(r.shape[0] for r in rows)
    return jnp.concatenate(rows + [jnp.zeros((SMALL_ROWS - used, PACK_W), F32)], 0)


def _unpack_small(packed):
    out, r0 = {}, 0
    for name, n in SMALL_PLAN:
        rows = -(-n // PACK_W)
        out[name] = packed[r0:r0 + rows].reshape(-1)[:n].reshape(1, n)
        r0 += rows
    return out


def _adamw(w, g, m, v, name):
    rows, cols = w.shape
    tr = _tile(rows, 256, 8) if rows % 8 == 0 else rows

    def body(w_ref, g_ref, m_ref, v_ref, d_ref, nm_ref, nv_ref):
        g = g_ref[...]
        m = B1 * m_ref[...] + (1.0 - B1) * g
        v = B2 * v_ref[...] + (1.0 - B2) * (g * g)
        nm_ref[...] = m
        nv_ref[...] = v
        m_hat = m / (1.0 - B1 ** STEP)
        v_hat = v / (1.0 - B2 ** STEP)
        d_ref[...] = -LR * (m_hat / (jnp.sqrt(v_hat) + ADAM_EPS) + WD * w_ref[...])

    spec = pl.BlockSpec((tr, cols), lambda i: (i, 0))
    return pl.pallas_call(
        body, name=name, grid=(rows // tr,), in_specs=[spec] * 4, out_specs=[spec] * 3,
        out_shape=[_sds((rows, cols), F32)] * 3, compiler_params=_params(("arbitrary",)),
    )(w, g, m, v)


NAMES = ["meta_tokens", "norm_pre_mix", "norm_post_mix", "w_in", "b_in", "attn_sinks", "w_attn_proj", "conv_dw_w",
         "conv_dw_b", "conv_ln_g", "conv_ln_b", "w_conv_proj", "b_conv_proj", "w_out", "norm_pre_ffn", "norm_post_ffn",
         "w_up", "ffn_dw_w", "ffn_dw_b", "w_down"]
MATMUL = ("w_in", "w_attn_proj", "w_conv_proj", "w_out", "w_up", "w_down")


def _two_d(a):
    return a.reshape(a.shape[-2:])


def kernel(x, meta_tokens, norm_pre_mix, norm_post_mix, w_in, b_in, attn_sinks, w_attn_proj, conv_dw_w, conv_dw_b, conv_ln_g, conv_ln_b, w_conv_proj, b_conv_proj, w_out, norm_pre_ffn, norm_post_ffn, w_up, ffn_dw_w, ffn_dw_b, w_down, loss_target, m_meta_tokens, m_norm_pre_mix, m_norm_post_mix, m_w_in, m_b_in, m_attn_sinks, m_w_attn_proj, m_conv_dw_w, m_conv_dw_b, m_conv_ln_g, m_conv_ln_b, m_w_conv_proj, m_b_conv_proj, m_w_out, m_norm_pre_ffn, m_norm_post_ffn, m_w_up, m_ffn_dw_w, m_ffn_dw_b, m_w_down, v_meta_tokens, v_norm_pre_mix, v_norm_post_mix, v_w_in, v_b_in, v_attn_sinks, v_w_attn_proj, v_conv_dw_w, v_conv_dw_b, v_conv_ln_g, v_conv_ln_b, v_w_conv_proj, v_b_conv_proj, v_w_out, v_norm_pre_ffn, v_norm_post_ffn, v_w_up, v_ffn_dw_w, v_ffn_dw_b, v_w_down):
    args = locals()
    w = {n: args[n] for n in NAMES}
    m = {n: args["m_" + n] for n in NAMES}
    v = {n: args["v_" + n] for n in NAMES}
    big = [name for name, _, _, _ in BIG]
    cx, cy, cc = lax.axis_index("x"), lax.axis_index("y"), lax.axis_index("c")

    gathered = _gather_quarters(_pack({n: _two_d(w[n]) for n in big}))
    full = _join_quarters(_unpack(gathered, lambda n: BF16))
    W = dict(full)
    for n in SMALL:
        W[n] = _two_d(w[n])

    loss_cols, grad_x, grads = local_step(x[0], loss_target[0], W)

    partial = _pack(_split_quarters({n: grads[n] for n in big}))
    pair_sum = _sum_pair(partial, _sibling_swap(partial), cc.reshape(1).astype(jnp.int32))
    own_half = _sum_chips(pair_sum, _chip_swap(pair_sum), (2 * cx + cy).reshape(1).astype(jnp.int32))
    g_big = _unpack(_sibling_share(own_half), lambda n: F32)
    small = dict(grads)
    small["loss"] = loss_cols
    g_small = _unpack_small(_small_allreduce(_pack_small(small)))
    loss = jnp.sum(g_small["loss"])

    g, delta, new_m, new_v = {}, {}, {}, {}
    for n in big:
        shape = w[n].shape
        g[n] = g_big[n].reshape(shape)
        d, nm, nv = _adamw(_two_d(w[n]), _two_d(g[n]), _two_d(m[n]), _two_d(v[n]), "adamw_" + n)
        delta[n], new_m[n], new_v[n] = d.reshape(shape), nm.reshape(shape), nv.reshape(shape)
    no_loss = jnp.zeros((1, D), F32)
    packs = [_pack_small(dict({n: d[n] for n in SMALL}, loss=no_loss)) for d in (w, dict(g_small), m, v)]
    ud, um, uv = (_unpack_small(a) for a in _adamw(*packs, "adamw_small"))
    for n in SMALL:
        g[n], delta[n], new_m[n], new_v[n] = g_small[n], ud[n], um[n], uv[n]

    return (loss, grad_x[None], *[g[n] for n in NAMES], *[delta[n] for n in NAMES],
            *[new_m[n] for n in NAMES], *[new_v[n] for n in NAMES])
```

```python
import jax
import jax.numpy as jnp
from jax import lax
from jax.experimental import pallas as pl
from jax.experimental.pallas import tpu as pltpu

F32, BF16 = jnp.float32, jnp.bfloat16

D = 1024
NH, NKV, HD = 16, 2, 64
NMETA, BLK = 16, 128
PAD = BLK - NMETA
ROT = HD // 4
THETA = 500000.0
CONV_K = 31
FFN = 2816
FFN_K = 3
IN_W = 5376
QKV_W, GLU_W, GATE_W = 1280, 2048, 2048
RMS_EPS, LN_EPS, NEG = 1e-6, 1e-5, -1e30
LR, B1, B2, ADAM_EPS, WD, STEP = 0.001, 0.9, 0.999, 1e-08, 0.01, 10

VMEM_LIMIT = 56 * 2 ** 20
MESH = pl.DeviceIdType.MESH

NT_DIMS = (((1,), (1,)), ((), ()))
TN_DIMS = (((0,), (0,)), ((), ()))


def _params(sem, **kw):
    return pltpu.CompilerParams(dimension_semantics=sem, vmem_limit_bytes=VMEM_LIMIT, **kw)


def _tile(n, pref, mult=16):
    for t in range(min(pref, n), 0, -1):
        if n % t == 0 and t % mult == 0:
            return t
    return n


def _row(tr, w, col=0):
    return pl.BlockSpec((tr, w), lambda i: (i, col))


def _rrow(tr, w, nt, col=0):
    return pl.BlockSpec((tr, w), lambda t: (nt - 1 - t, col))


def _const(shape):
    return pl.BlockSpec(shape, lambda *_: (0,) * len(shape))


def _sds(shape, dt):
    return jax.ShapeDtypeStruct(shape, dt)


def _rms(x, g):
    r = lax.rsqrt(jnp.mean(x * x, -1, keepdims=True) + RMS_EPS)
    return x * r * g, r


def _rms_bwd(dy, x, r, g):
    gy = dy * g
    return r * gy - x * (r * r * r) * jnp.mean(x * gy, -1, keepdims=True)


def _colsum(x):
    return jnp.sum(x, axis=0, keepdims=True)


def _rope(x, c, sa, sb):
    n = x.shape[1]
    return x * c + pltpu.roll(x, n - 8, 1) * sa + pltpu.roll(x, 8, 1) * sb


def _rope_bwd(d, c, sa, sb):
    n = d.shape[1]
    return d * c + pltpu.roll(d * sa, 8, 1) + pltpu.roll(d * sb, n - 8, 1)


def _rope_tables(R):
    half = ROT // 2
    lane = jnp.arange(2 * HD) % HD
    inv = THETA ** (-(lane % half).astype(F32) * 2.0 / ROT)
    pos = (jnp.arange(R) - PAD).astype(F32)
    ang = pos[:, None] * inv[None, :]
    cos, sin = jnp.cos(ang), jnp.sin(ang)
    c = jnp.where(lane < ROT, cos, 1.0)
    sa = jnp.where(lane < half, -sin, 0.0)
    sb = jnp.where((lane >= half) & (lane < ROT), sin, 0.0)
    return c, sa, sb


IN_CHUNKS = ([(0, 512, True), (512, 1024, True), (1024, 1152, True), (1152, 1280, False)]
             + [(c, c + 512, False) for c in range(1280, IN_W, 512)])


def _in_proj(h0, g_pre, w_in, b_in, rope, tr):
    R = h0.shape[0]

    def body(h_ref, g_ref, w_ref, b_ref, c_ref, sa_ref, sb_ref, qkv_ref, glu_ref, gate_ref, n1_ref):
        n, _ = _rms(h_ref[...], g_ref[...])
        nb = n.astype(BF16)
        n1_ref[...] = nb
        for c0, c1, rot in IN_CHUNKS:
            acc = jnp.dot(nb, w_ref[:, c0:c1], preferred_element_type=F32) + b_ref[:, c0:c1]
            if rot:
                reps = (c1 - c0) // 128
                acc = _rope(acc, jnp.tile(c_ref[...], (1, reps)), jnp.tile(sa_ref[...], (1, reps)),
                            jnp.tile(sb_ref[...], (1, reps)))
            val = acc.astype(BF16)
            if c1 <= QKV_W:
                qkv_ref[:, c0:c1] = val
            elif c1 <= QKV_W + GLU_W:
                glu_ref[:, c0 - QKV_W:c1 - QKV_W] = val
            else:
                gate_ref[:, c0 - QKV_W - GLU_W:c1 - QKV_W - GLU_W] = val

    return pl.pallas_call(
        body, name="in_proj", grid=(R // tr,),
        in_specs=[_row(tr, D), _const((1, D)), _const((D, IN_W)), _const((1, IN_W)),
                  _row(tr, 128), _row(tr, 128), _row(tr, 128)],
        out_specs=[_row(tr, QKV_W), _row(tr, GLU_W), _row(tr, GATE_W), _row(tr, D)],
        out_shape=[_sds((R, QKV_W), BF16), _sds((R, GLU_W), BF16), _sds((R, GATE_W), BF16), _sds((R, D), BF16)],
        compiler_params=_params(("arbitrary",)),
    )(h0, g_pre, w_in, b_in, *rope)


def _attn_mask(n):
    qi = lax.broadcasted_iota(jnp.int32, (BLK, 3 * BLK), 0)
    kj = lax.broadcasted_iota(jnp.int32, (BLK, 3 * BLK), 1)
    tq = n * BLK + qi - PAD
    t_meta = kj - PAD
    t_loc = (n - 1) * BLK + (kj - BLK) - PAD
    meta_ok = (kj < BLK) & (t_meta >= 0) & (t_meta <= tq)
    loc_ok = (kj >= BLK) & (t_loc >= NMETA) & (t_loc <= tq) & (tq - t_loc < BLK)
    return meta_ok | loc_ok


def _dup_heads(ref0, refp, refc, low):
    a = jnp.concatenate([ref0[...], refp[...], refc[...]], 0).astype(F32)
    sw = pltpu.roll(a, HD, 1)
    return [jnp.where(low, a, sw).astype(BF16), jnp.where(low, sw, a).astype(BF16)]


def _kv_specs(nb, rev):
    def blk(col, which):
        def idx(t):
            n = nb - 1 - t if rev else t
            return ({"meta": 0, "prev": jnp.maximum(n - 1, 0), "own": n}[which], col)
        return pl.BlockSpec((BLK, BLK), idx)
    return [blk(col, w) for col in (8, 9) for w in ("meta", "prev", "own")]


def _attn_fwd(qkv, sinks):
    R = qkv.shape[0]
    nb = R // BLK

    def body(s_ref, q_ref, k0, kp, kc, v0, vp, vc, o_ref, lse_ref):
        n = pl.program_id(0)
        lane = lax.broadcasted_iota(jnp.int32, (1, BLK), 1)
        low = lane < HD
        kd, vd = _dup_heads(k0, kp, kc, low), _dup_heads(v0, vp, vc, low)
        mask = _attn_mask(n)
        lse = jnp.zeros((BLK, BLK), F32)
        zero = jnp.zeros((), BF16)
        for pair in range(NH // 2):
            qp = q_ref[:, pair * BLK:(pair + 1) * BLK]
            kv = pair // (NH // NKV // 2)
            outs = []
            for e in range(2):
                h = 2 * pair + e
                qm = jnp.where(low if e == 0 else ~low, qp, zero)
                s = lax.dot_general(qm, kd[kv], NT_DIMS, preferred_element_type=F32) * (HD ** -0.5)
                s = jnp.where(mask, s, NEG)
                sk = s_ref[h]
                m = jnp.maximum(jnp.max(s, -1, keepdims=True), sk)
                p = jnp.exp(s - m)
                l = jnp.sum(p, -1, keepdims=True) + jnp.exp(sk - m)
                outs.append(jnp.dot(p.astype(BF16), vd[kv], preferred_element_type=F32) / l)
                lse = jnp.where(lane == h, m + jnp.log(l), lse)
            o_ref[:, pair * BLK:(pair + 1) * BLK] = jnp.where(low, outs[0], outs[1]).astype(BF16)
        lse_ref[...] = lse

    return pl.pallas_call(
        body, name="attn_fwd", grid=(nb,),
        in_specs=[pl.BlockSpec(memory_space=pltpu.SMEM), pl.BlockSpec((BLK, D), lambda n: (n, 0))] + _kv_specs(nb, False),
        out_specs=[_row(BLK, D), _row(BLK, BLK)],
        out_shape=[_sds((R, D), BF16), _sds((R, BLK), F32)],
        compiler_params=_params(("arbitrary",)),
    )(sinks, qkv, *([qkv] * 6))


CONV_TCH, CONV_SUB, HALO = 256, 64, 32


def _conv_fwd(glu, w, b, tr):
    R = glu.shape[0]
    nc = D // CONV_TCH

    def body(a_ref, g_ref, w_ref, b_ref, o_ref, buf):
        i = pl.program_id(1)

        @pl.when(i == 0)
        def _():
            buf[0:HALO, :] = jnp.zeros((HALO, CONV_TCH), F32)

        @pl.when(i > 0)
        def _():
            buf[0:HALO, :] = buf[tr:tr + HALO, :]

        row = i * tr + lax.broadcasted_iota(jnp.int32, (tr, 1), 0)
        a, g = a_ref[...].astype(F32), g_ref[...].astype(F32)
        buf[HALO:HALO + tr, :] = jnp.where(row >= PAD, a * jax.nn.sigmoid(g), 0.0)
        for r0 in range(0, tr, CONV_SUB):
            acc = jnp.broadcast_to(b_ref[...], (CONV_SUB, CONV_TCH))
            for k in range(CONV_K):
                acc = acc + w_ref[k:k + 1, :] * buf[pl.ds(r0 + HALO - (CONV_K - 1) + k, CONV_SUB), :]
            o_ref[r0:r0 + CONV_SUB, :] = acc.astype(BF16)

    return pl.pallas_call(
        body, name="conv_fwd", grid=(nc, R // tr),
        in_specs=[pl.BlockSpec((tr, CONV_TCH), lambda c, i: (i, c)),
                  pl.BlockSpec((tr, CONV_TCH), lambda c, i: (i, nc + c)),
                  pl.BlockSpec((CONV_K, CONV_TCH), lambda c, i: (0, c)),
                  pl.BlockSpec((1, CONV_TCH), lambda c, i: (0, c))],
        out_specs=pl.BlockSpec((tr, CONV_TCH), lambda c, i: (i, c)),
        out_shape=_sds((R, D), BF16),
        scratch_shapes=[pltpu.VMEM((tr + HALO, CONV_TCH), F32)],
        compiler_params=_params(("arbitrary", "arbitrary")),
    )(glu, glu, w, b)


def _ln_silu(c1, lg, lb):
    mu = jnp.mean(c1, -1, keepdims=True)
    xc = c1 - mu
    rs = lax.rsqrt(jnp.mean(xc * xc, -1, keepdims=True) + LN_EPS)
    xh = xc * rs
    c2 = xh * lg + lb
    sg = jax.nn.sigmoid(c2)
    return xh, rs, c2, sg


def _mix_out(attn, c1, gates, h0, w_ap, w_cp, w_out, lg, lb, b_cp, g_post, g_ffn, tr):
    R = attn.shape[0]

    def body(at_ref, c1_ref, ga_ref, gc_ref, h0_ref, wap, wcp, wo, lg_ref, lb_ref, bcp, gp, gf,
             ao_ref, co_ref, c3_ref, mg_ref, mix_ref, h1_ref, n2_ref):
        ao = jnp.dot(at_ref[...], wap[...], preferred_element_type=F32)
        _, _, c2, sg = _ln_silu(c1_ref[...].astype(F32), lg_ref[...], lb_ref[...])
        c3 = (c2 * sg).astype(BF16)
        c3_ref[...] = c3
        co = jnp.dot(c3, wcp[...], preferred_element_type=F32) + bcp[...]
        ao_b, co_b = ao.astype(BF16), co.astype(BF16)
        ao_ref[...] = ao_b
        co_ref[...] = co_b
        merged = (jax.nn.sigmoid(ga_ref[...].astype(F32)) * ao_b.astype(F32)
                  + jax.nn.sigmoid(gc_ref[...].astype(F32)) * co_b.astype(F32)).astype(BF16)
        mg_ref[...] = merged
        mix = jnp.dot(merged, wo[...], preferred_element_type=F32).astype(BF16)
        mix_ref[...] = mix
        y, _ = _rms(mix.astype(F32), gp[...])
        h1 = h0_ref[...] + y
        h1_ref[...] = h1
        n2, _ = _rms(h1, gf[...])
        row = pl.program_id(0) * tr + lax.broadcasted_iota(jnp.int32, (tr, 1), 0)
        n2_ref[...] = jnp.where(row >= PAD, n2, 0.0).astype(BF16)

    vec = _const((1, D))
    return pl.pallas_call(
        body, name="mix_out", grid=(R // tr,),
        in_specs=[_row(tr, D), _row(tr, D), _row(tr, D, 0), _row(tr, D, 1), _row(tr, D),
                  _const((D, D)), _const((D, D)), _const((D, D)), vec, vec, vec, vec, vec],
        out_specs=[_row(tr, D)] * 7,
        out_shape=[_sds((R, D), BF16)] * 5 + [_sds((R, D), F32), _sds((R, D), BF16)],
        compiler_params=_params(("arbitrary",)),
    )(attn, c1, gates, gates, h0, w_ap, w_cp, w_out, lg, lb, b_cp, g_post, g_ffn)


FFN_CH = 256
N_CHIPS = 4
UPQ = 2 * FFN // N_CHIPS
UP_CHUNKS = [(q, c0, min(c0 + 512, UPQ)) for q in range(N_CHIPS // 2) for c0 in range(0, UPQ, 512)]


def _shift_down(x, k, halo):
    tr = x.shape[0]
    row = lax.broadcasted_iota(jnp.int32, (tr, 1), 0)
    y = pltpu.roll(x, k, 0)
    for j in range(k):
        y = jnp.where(row == j, halo[8 - k + j:8 - k + j + 1, :], y)
    return y


def _shift_up(x, k, halo):
    tr = x.shape[0]
    row = lax.broadcasted_iota(jnp.int32, (tr, 1), 0)
    y = pltpu.roll(x, tr - k, 0)
    for j in range(k):
        y = jnp.where(row == tr - k + j, halo[j:j + 1, :], y)
    return y


def _conv3(x, halo, w, b):
    return w[2:3, :] * x + w[1:2, :] * _shift_down(x, 1, halo) + w[0:1, :] * _shift_down(x, 2, halo) + b


def _ffn_up(n2, w_up, fw, fb, tr):
    R = n2.shape[0]

    def body(n_ref, w_ref, fw_ref, fb_ref, up_ref, act_ref, carry):
        @pl.when(pl.program_id(0) == 0)
        def _():
            carry[...] = jnp.zeros_like(carry)

        nb = n_ref[...]
        for q, c0, c1 in UP_CHUNKS:
            us = []
            for qq in (q, q + N_CHIPS // 2):
                cs = slice(qq * UPQ + c0, qq * UPQ + c1)
                x = jnp.dot(nb, w_ref[qq, :, c0:c1], preferred_element_type=F32).astype(BF16)
                up_ref[:, cs] = x
                x = x.astype(F32)
                us.append(_conv3(x, carry[:, cs], fw_ref[:, cs], fb_ref[:, cs]))
                carry[:, cs] = x[tr - 8:tr, :]
            act_ref[:, q * UPQ + c0:q * UPQ + c1] = (us[0] * jax.nn.sigmoid(us[0]) * us[1]).astype(BF16)

    return pl.pallas_call(
        body, name="ffn_up", grid=(R // tr,),
        in_specs=[_row(tr, D), _const((N_CHIPS, D, UPQ)), _const((FFN_K, 2 * FFN)), _const((1, 2 * FFN))],
        out_specs=[_row(tr, 2 * FFN), _row(tr, FFN)],
        out_shape=[_sds((R, 2 * FFN), BF16), _sds((R, FFN), BF16)],
        scratch_shapes=[pltpu.VMEM((8, 2 * FFN), F32)],
        compiler_params=_params(("arbitrary",)),
    )(n2, w_up, fw, fb)


def _ffn_down(act, w_down, h1, tgt, g_post, tr):
    R = act.shape[0]

    def body(a_ref, w_ref, h1_ref, t_ref, g_ref, dh2_ref, dffn_ref, loss_ref, dg_ref):
        @pl.when(pl.program_id(0) == 0)
        def _():
            loss_ref[...] = jnp.zeros_like(loss_ref)
            dg_ref[...] = jnp.zeros_like(dg_ref)

        f = jnp.dot(a_ref[...], w_ref[...], preferred_element_type=F32)
        g = g_ref[...]
        y, r = _rms(f, g)
        row = pl.program_id(0) * tr + lax.broadcasted_iota(jnp.int32, (tr, 1), 0)
        e = jnp.where(row >= BLK, h1_ref[...] + y - t_ref[...], 0.0)
        loss_ref[...] += _colsum(e * e) * (0.5 / D)
        dy = e * (1.0 / D)
        dh2_ref[...] = dy
        dffn_ref[...] = _rms_bwd(dy, f, r, g).astype(BF16)
        dg_ref[...] += _colsum(dy * f * r)

    return pl.pallas_call(
        body, name="ffn_down", grid=(R // tr,),
        in_specs=[_row(tr, FFN), _const((FFN, D)), _row(tr, D), _row(tr, D), _const((1, D))],
        out_specs=[_row(tr, D), _row(tr, D), _const((1, D)), _const((1, D))],
        out_shape=[_sds((R, D), F32), _sds((R, D), BF16), _sds((1, D), F32), _sds((1, D), F32)],
        compiler_params=_params(("arbitrary",)),
    )(act, w_down, h1, tgt, g_post)


def _ffn_bwd_act(dffn, w_down, up, fw, fb, tr):
    R = dffn.shape[0]
    nt = R // tr

    def body(d_ref, w_ref, up_ref, hal_ref, fw_ref, fb_ref, dup_ref, dfw_ref, dfb_ref, carry):
        t = pl.program_id(0)
        i = nt - 1 - t

        @pl.when(t == 0)
        def _():
            carry[...] = jnp.zeros_like(carry)
            dfw_ref[...] = jnp.zeros_like(dfw_ref)
            dfb_ref[...] = jnp.zeros_like(dfb_ref)

        dff = d_ref[...]
        row = i * tr + lax.broadcasted_iota(jnp.int32, (tr, 1), 0)
        first = i == 0
        for c in range(0, FFN, FFN_CH):
            dact = lax.dot_general(dff, w_ref[c:c + FFN_CH, :], NT_DIMS, preferred_element_type=F32)
            xs, us = [], []
            for off in (c, FFN + c):
                cs = slice(off, off + FFN_CH)
                x = up_ref[:, cs].astype(F32)
                halo = jnp.where(first, 0.0, hal_ref[:, cs].astype(F32))
                x1, x2 = _shift_down(x, 1, halo), _shift_down(x, 2, halo)
                w = fw_ref[:, cs]
                us.append(w[2:3, :] * x + w[1:2, :] * x1 + w[0:1, :] * x2 + fb_ref[:, cs])
                xs.append((x, x1, x2))
            sg = jax.nn.sigmoid(us[0])
            silu = us[0] * sg
            dus = [dact * us[1] * sg * (1.0 + us[0] * (1.0 - sg)), dact * silu]
            for (x, x1, x2), du, off in zip(xs, dus, (c, FFN + c)):
                cs = slice(off, off + FFN_CH)
                w = fw_ref[:, cs]
                nxt = carry[:, cs]
                dx = w[2:3, :] * du + w[1:2, :] * _shift_up(du, 1, nxt) + w[0:1, :] * _shift_up(du, 2, nxt)
                dup_ref[:, cs] = jnp.where(row >= PAD, dx, 0.0).astype(BF16)
                dfw_ref[0:1, cs] += _colsum(x2 * du)
                dfw_ref[1:2, cs] += _colsum(x1 * du)
                dfw_ref[2:3, cs] += _colsum(x * du)
                dfb_ref[:, cs] += _colsum(du)
                carry[:, cs] = du[0:8, :]

    halo_spec = pl.BlockSpec((8, 2 * FFN), lambda t: (jnp.maximum((nt - 1 - t) * (tr // 8) - 1, 0), 0))
    return pl.pallas_call(
        body, name="ffn_bwd_act", grid=(nt,),
        in_specs=[_rrow(tr, D, nt), _const((FFN, D)), _rrow(tr, 2 * FFN, nt), halo_spec,
                  _const((FFN_K, 2 * FFN)), _const((1, 2 * FFN))],
        out_specs=[_rrow(tr, 2 * FFN, nt), _const((8, 2 * FFN)), _const((1, 2 * FFN))],
        out_shape=[_sds((R, 2 * FFN), BF16), _sds((8, 2 * FFN), F32), _sds((1, 2 * FFN), F32)],
        scratch_shapes=[pltpu.VMEM((8, 2 * FFN), F32)],
        compiler_params=_params(("arbitrary",)),
    )(dffn, w_down, up, up, fw, fb)


def _ffn_bwd_in(dup, w_up, h1, dh2, mix, g_ffn, g_post, tr):
    R = dup.shape[0]

    def body(d_ref, w_ref, h1_ref, dh2_ref, mix_ref, gf_ref, gp_ref, dh1_ref, dmix_ref, dgf_ref, dgp_ref):
        @pl.when(pl.program_id(0) == 0)
        def _():
            dgf_ref[...] = jnp.zeros_like(dgf_ref)
            dgp_ref[...] = jnp.zeros_like(dgp_ref)

        dn2 = sum(lax.dot_general(d_ref[:, q * UPQ:(q + 1) * UPQ], w_ref[q], NT_DIMS, preferred_element_type=F32)
                  for q in range(N_CHIPS))
        h1 = h1_ref[...]
        _, r2 = _rms(h1, gf_ref[...])
        dh1 = dh2_ref[...] + _rms_bwd(dn2, h1, r2, gf_ref[...])
        dgf_ref[...] += _colsum(dn2 * h1 * r2)
        dh1_ref[...] = dh1
        m = mix_ref[...].astype(F32)
        _, rm = _rms(m, gp_ref[...])
        dmix_ref[...] = _rms_bwd(dh1, m, rm, gp_ref[...]).astype(BF16)
        dgp_ref[...] += _colsum(dh1 * m * rm)

    vec = _const((1, D))
    return pl.pallas_call(
        body, name="ffn_bwd_in", grid=(R // tr,),
        in_specs=[_row(tr, 2 * FFN), _const((N_CHIPS, D, UPQ)), _row(tr, D), _row(tr, D), _row(tr, D), vec, vec],
        out_specs=[_row(tr, D), _row(tr, D), vec, vec],
        out_shape=[_sds((R, D), F32), _sds((R, D), BF16), _sds((1, D), F32), _sds((1, D), F32)],
        compiler_params=_params(("arbitrary",)),
    )(dup, w_up, h1, dh2, mix, g_ffn, g_post)


def _mix_bwd(dmix, ao, co, gates, c1, w_out, w_ap, w_cp, lg, lb, tr):
    R = dmix.shape[0]

    def body(dm_ref, ao_ref, co_ref, ga_ref, gc_ref, c1_ref, wo, wap, wcp, lg_ref, lb_ref,
             dao_ref, dco_ref, dgate_ref, dattn_ref, dc1_ref, dbcp_ref, dlg_ref, dlb_ref, dcb_ref):
        @pl.when(pl.program_id(0) == 0)
        def _():
            for ref in (dbcp_ref, dlg_ref, dlb_ref, dcb_ref):
                ref[...] = jnp.zeros_like(ref)

        dmg = lax.dot_general(dm_ref[...], wo[...], NT_DIMS, preferred_element_type=F32)
        sa = jax.nn.sigmoid(ga_ref[...].astype(F32))
        sc = jax.nn.sigmoid(gc_ref[...].astype(F32))
        dao = (dmg * sa).astype(BF16)
        dco = (dmg * sc).astype(BF16)
        dao_ref[...] = dao
        dco_ref[...] = dco
        dgate_ref[:, 0:D] = (dmg * ao_ref[...].astype(F32) * sa * (1.0 - sa)).astype(BF16)
        dgate_ref[:, D:2 * D] = (dmg * co_ref[...].astype(F32) * sc * (1.0 - sc)).astype(BF16)
        dbcp_ref[...] += _colsum(dco.astype(F32))
        dattn_ref[...] = lax.dot_general(dao, wap[...], NT_DIMS, preferred_element_type=F32).astype(BF16)
        dc3 = lax.dot_general(dco, wcp[...], NT_DIMS, preferred_element_type=F32)
        xh, rs, c2, sg = _ln_silu(c1_ref[...].astype(F32), lg_ref[...], lb_ref[...])
        dc2 = dc3 * sg * (1.0 + c2 * (1.0 - sg))
        dlg_ref[...] += _colsum(dc2 * xh)
        dlb_ref[...] += _colsum(dc2)
        dxh = dc2 * lg_ref[...]
        dc1 = rs * (dxh - jnp.mean(dxh, -1, keepdims=True) - xh * jnp.mean(dxh * xh, -1, keepdims=True))
        dc1_ref[...] = dc1
        dcb_ref[...] += _colsum(dc1)

    vec = _const((1, D))
    return pl.pallas_call(
        body, name="mix_bwd", grid=(R // tr,),
        in_specs=[_row(tr, D), _row(tr, D), _row(tr, D), _row(tr, D, 0), _row(tr, D, 1), _row(tr, D),
                  _const((D, D)), _const((D, D)), _const((D, D)), vec, vec],
        out_specs=[_row(tr, D), _row(tr, D), _row(tr, 2 * D), _row(tr, D), _row(tr, D), vec, vec, vec, vec],
        out_shape=[_sds((R, D), BF16), _sds((R, D), BF16), _sds((R, 2 * D), BF16), _sds((R, D), BF16),
                   _sds((R, D), F32)] + [_sds((1, D), F32)] * 4,
        compiler_params=_params(("arbitrary",)),
    )(dmix, ao, co, gates, gates, c1, w_out, w_ap, w_cp, lg, lb)


def _conv_bwd(dc1, glu, w, tr):
    R = dc1.shape[0]
    nt, nc = R // tr, D // CONV_TCH

    def body(d_ref, a_ref, g_ref, w_ref, dglu_a, dglu_g, dw_ref, buf):
        t = pl.program_id(1)
        i = nt - 1 - t

        @pl.when(t == 0)
        def _():
            buf[tr:tr + HALO, :] = jnp.zeros((HALO, CONV_TCH), F32)
            dw_ref[...] = jnp.zeros_like(dw_ref)

        @pl.when(t > 0)
        def _():
            buf[tr:tr + HALO, :] = buf[0:HALO, :]

        buf[0:tr, :] = d_ref[...]
        for r0 in range(0, tr, CONV_SUB):
            rs = slice(r0, r0 + CONV_SUB)
            row = i * tr + r0 + lax.broadcasted_iota(jnp.int32, (CONV_SUB, 1), 0)
            a, g = a_ref[rs, :].astype(F32), g_ref[rs, :].astype(F32)
            sg = jax.nn.sigmoid(g)
            glu = jnp.where(row >= PAD, a * sg, 0.0)
            acc = jnp.zeros((CONV_SUB, CONV_TCH), F32)
            for k in range(CONV_K):
                win = buf[pl.ds(r0 + CONV_K - 1 - k, CONV_SUB), :]
                acc = acc + w_ref[k:k + 1, :] * win
                dw_ref[k:k + 1, :] += _colsum(glu * win)
            dglu = jnp.where(row >= PAD, acc, 0.0)
            dglu_a[rs, :] = (dglu * sg).astype(BF16)
            dglu_g[rs, :] = (dglu * a * sg * (1.0 - sg)).astype(BF16)

    def rspec(col0):
        return pl.BlockSpec((tr, CONV_TCH), lambda c, t: (nt - 1 - t, col0 + c))

    return pl.pallas_call(
        body, name="conv_bwd", grid=(nc, nt),
        in_specs=[rspec(0), rspec(0), rspec(nc), pl.BlockSpec((CONV_K, CONV_TCH), lambda c, t: (0, c))],
        out_specs=[rspec(0), rspec(0), pl.BlockSpec((HALO, CONV_TCH), lambda c, t: (0, c))],
        out_shape=[_sds((R, D), BF16), _sds((R, D), BF16), _sds((HALO, D), F32)],
        scratch_shapes=[pltpu.VMEM((tr + HALO, CONV_TCH), F32)],
        compiler_params=_params(("arbitrary", "arbitrary")),
    )(dc1, glu, glu, w)


def _attn_bwd(qkv, o, do, lse, sinks, rope):
    R = qkv.shape[0]
    nb = R // BLK

    def body(s_ref, q_ref, k0, kp, kc, v0, vp, vc, o_ref, do_ref, lse_ref, c_ref, sa_ref, sb_ref,
             dq_ref, dk_ref, dv_ref, dsink_ref, car_k, car_v, met_k, met_v):
        t = pl.program_id(0)
        n = nb - 1 - t

        @pl.when(t == 0)
        def _():
            for ref in (car_k, car_v, met_k, met_v, dsink_ref):
                ref[...] = jnp.zeros_like(ref)

        lane = lax.broadcasted_iota(jnp.int32, (1, BLK), 1)
        low = lane < HD
        kd, vd = _dup_heads(k0, kp, kc, low), _dup_heads(v0, vp, vc, low)
        mask = _attn_mask(n)
        tabs = (c_ref[...], sa_ref[...], sb_ref[...])
        zero = jnp.zeros((), BF16)
        dk_acc = [jnp.zeros((3 * BLK, BLK), F32) for _ in range(NKV)]
        dv_acc = [jnp.zeros((3 * BLK, BLK), F32) for _ in range(NKV)]
        dsink = jnp.zeros((1, BLK), F32)
        lse_all = lse_ref[...]
        for pair in range(NH // 2):
            cs = slice(pair * BLK, (pair + 1) * BLK)
            qp, dop = q_ref[:, cs], do_ref[:, cs]
            prod = dop.astype(F32) * o_ref[:, cs].astype(F32)
            kv = pair // (NH // NKV // 2)
            dqs = []
            for e in range(2):
                h = 2 * pair + e
                sel = low if e == 0 else ~low
                qm, dom = jnp.where(sel, qp, zero), jnp.where(sel, dop, zero)
                lse_h = jnp.sum(jnp.where(lane == h, lse_all, 0.0), -1, keepdims=True)
                s = lax.dot_general(qm, kd[kv], NT_DIMS, preferred_element_type=F32) * (HD ** -0.5)
                p = jnp.where(mask, jnp.exp(s - lse_h), 0.0)
                dp = lax.dot_general(dom, vd[kv], NT_DIMS, preferred_element_type=F32)
                delta = jnp.sum(jnp.where(sel, prod, 0.0), -1, keepdims=True)
                ds = (p * (dp - delta) * (HD ** -0.5)).astype(BF16)
                dqs.append(jnp.dot(ds, kd[kv], preferred_element_type=F32))
                dk_acc[kv] = dk_acc[kv] + lax.dot_general(ds, qm, TN_DIMS, preferred_element_type=F32)
                dv_acc[kv] = dv_acc[kv] + lax.dot_general(p.astype(BF16), dom, TN_DIMS, preferred_element_type=F32)
                ps = jnp.exp(s_ref[h] - lse_h)
                dsink = dsink + jnp.where(lane == h, -jnp.sum(ps * delta), 0.0)
            dq_ref[:, cs] = _rope_bwd(jnp.where(low, dqs[0], dqs[1]), *tabs).astype(BF16)
        dsink_ref[0:1, :] += dsink

        def fold(acc):
            tot = [a + pltpu.roll(a, HD, 1) for a in acc]
            return jnp.where(low, tot[0], tot[1])

        dk_all, dv_all = fold(dk_acc), fold(dv_acc)
        met_k[...] += dk_all[0:BLK, :]
        met_v[...] += dv_all[0:BLK, :]
        last = jnp.where(n == 0, 1.0, 0.0)
        dk_n = dk_all[2 * BLK:3 * BLK, :] + car_k[...] + last * met_k[...]
        dv_n = dv_all[2 * BLK:3 * BLK, :] + car_v[...] + last * met_v[...]
        dk_ref[...] = _rope_bwd(dk_n, *tabs).astype(BF16)
        dv_ref[...] = dv_n.astype(BF16)
        car_k[...] = dk_all[BLK:2 * BLK, :]
        car_v[...] = dv_all[BLK:2 * BLK, :]

    rblk = lambda w: pl.BlockSpec((BLK, w), lambda t: (nb - 1 - t, 0))
    return pl.pallas_call(
        body, name="attn_bwd", grid=(nb,),
        in_specs=[pl.BlockSpec(memory_space=pltpu.SMEM), rblk(D)] + _kv_specs(nb, True)
                 + [rblk(D), rblk(D), rblk(BLK), rblk(BLK), rblk(BLK), rblk(BLK)],
        out_specs=[rblk(D), rblk(BLK), rblk(BLK), _const((8, BLK))],
        out_shape=[_sds((R, D), BF16), _sds((R, BLK), BF16), _sds((R, BLK), BF16), _sds((8, BLK), F32)],
        scratch_shapes=[pltpu.VMEM((BLK, BLK), F32)] * 4,
        compiler_params=_params(("arbitrary",)),
    )(sinks, qkv, *([qkv] * 6), o, do, lse, *rope)


def _in_bwd(dproj, w_in, h0, dh1, g_pre, tr):
    R = dproj.shape[0]

    def body(d_ref, w_ref, h0_ref, dh1_ref, g_ref, dh0_ref, dg_ref, db_ref):
        @pl.when(pl.program_id(0) == 0)
        def _():
            dg_ref[...] = jnp.zeros_like(dg_ref)
            db_ref[...] = jnp.zeros_like(db_ref)

        d = d_ref[...]
        dn1 = lax.dot_general(d, w_ref[...], NT_DIMS, preferred_element_type=F32)
        h0 = h0_ref[...]
        _, r = _rms(h0, g_ref[...])
        dh0_ref[...] = dh1_ref[...] + _rms_bwd(dn1, h0, r, g_ref[...])
        dg_ref[...] += _colsum(dn1 * h0 * r)
        db_ref[...] += _colsum(d.astype(F32))

    return pl.pallas_call(
        body, name="in_bwd", grid=(R // tr,),
        in_specs=[_row(tr, IN_W), _const((D, IN_W)), _row(tr, D), _row(tr, D), _const((1, D))],
        out_specs=[_row(tr, D), _const((1, D)), _const((1, IN_W))],
        out_shape=[_sds((R, D), F32), _sds((1, D), F32), _sds((1, IN_W), F32)],
        compiler_params=_params(("arbitrary",)),
    )(dproj, w_in, h0, dh1, g_pre)


def _dw(a, b, name, tn, tr, by_chip=False):
    R, ka = a.shape
    n = b.shape[1]
    nt = R // tr
    if by_chip:
        out_spec = pl.BlockSpec((None, ka, tn), lambda j, i: (j, 0, 0))
        out_shape = _sds((n // tn, ka, tn), BF16)
    else:
        out_spec = pl.BlockSpec((ka, tn), lambda j, i: (0, j))
        out_shape = _sds((ka, n), BF16)

    def body(a_ref, b_ref, o_ref, acc):
        i = pl.program_id(1)

        @pl.when(i == 0)
        def _():
            acc[...] = jnp.zeros_like(acc)

        acc[...] += lax.dot_general(a_ref[...], b_ref[...], TN_DIMS, preferred_element_type=F32)

        @pl.when(i == nt - 1)
        def _():
            o_ref[...] = acc[...].astype(BF16)

    return pl.pallas_call(
        body, name=name, grid=(n // tn, nt),
        in_specs=[pl.BlockSpec((tr, ka), lambda j, i: (i, 0)), pl.BlockSpec((tr, tn), lambda j, i: (i, j))],
        out_specs=out_spec, out_shape=out_shape,
        scratch_shapes=[pltpu.VMEM((ka, tn), F32)],
        compiler_params=_params(("arbitrary", "arbitrary")),
    )(a, b)


SMALL = ["norm_pre_mix", "norm_post_mix", "b_in", "attn_sinks", "conv_dw_b", "conv_ln_g", "conv_ln_b",
         "b_conv_proj", "norm_pre_ffn", "norm_post_ffn", "ffn_dw_b"]


def local_step(x, tgt, W):
    S = x.shape[0]
    R = S + BLK
    tr = _tile(R, 384)
    trw = _tile(R, 1056)
    rope = _rope_tables(R)
    h0 = jnp.concatenate([jnp.zeros((PAD, D), F32), W["meta_tokens"], x], 0)
    tgt_p = jnp.concatenate([jnp.zeros((BLK, D), F32), tgt], 0)

    qkv, glu, gates, n1 = _in_proj(h0, W["norm_pre_mix"], W["w_in"], W["b_in"], rope, tr)
    sinks = W["attn_sinks"].reshape(NH)
    attn, lse = _attn_fwd(qkv, sinks)
    c1 = _conv_fwd(glu, W["conv_dw_w"], W["conv_dw_b"], tr)
    ao, co, c3, merged, mix, h1, n2 = _mix_out(
        attn, c1, gates, h0, W["w_attn_proj"], W["w_conv_proj"], W["w_out"], W["conv_ln_g"], W["conv_ln_b"],
        W["b_conv_proj"], W["norm_post_mix"], W["norm_pre_ffn"], tr)
    up, act = _ffn_up(n2, W["w_up"], W["ffn_dw_w"], W["ffn_dw_b"], tr)
    dh2, dffn, loss_cols, dg_post_ffn = _ffn_down(act, W["w_down"], h1, tgt_p, W["norm_post_ffn"], tr)

    dup, dfw, dfb = _ffn_bwd_act(dffn, W["w_down"], up, W["ffn_dw_w"], W["ffn_dw_b"], tr)
    dh1, dmix, dg_pre_ffn, dg_post_mix = _ffn_bwd_in(dup, W["w_up"], h1, dh2, mix, W["norm_pre_ffn"],
                                                      W["norm_post_mix"], tr)
    dao, dco, dgates, dattn, dc1, db_cp, dlg, dlb, dcb = _mix_bwd(
        dmix, ao, co, gates, c1, W["w_out"], W["w_attn_proj"], W["w_conv_proj"], W["conv_ln_g"], W["conv_ln_b"], tr)
    dglu_a, dglu_g, dcw = _conv_bwd(dc1, glu, W["conv_dw_w"], tr)
    dq, dk, dv, dsink = _attn_bwd(qkv, attn, dattn, lse, sinks, rope)
    dproj = jnp.concatenate([dq, dk, dv, dglu_a, dglu_g, dgates], 1)
    dh0, dg_pre_mix, db_in = _in_bwd(dproj, W["w_in"], h0, dh1, W["norm_pre_mix"], tr)

    grads = {
        "w_in": _dw(n1, dproj, "dw_in", 768, trw),
        "w_attn_proj": _dw(attn, dao, "dw_attn_proj", D, trw),
        "w_conv_proj": _dw(c3, dco, "dw_conv_proj", D, trw),
        "w_out": _dw(merged, dmix, "dw_out", D, trw),
        "w_up": _dw(n2, dup, "dw_up", UPQ, trw, by_chip=True),
        "w_down": _dw(act, dffn, "dw_down", 512, trw),
        "conv_dw_w": dcw[0:CONV_K],
        "ffn_dw_w": dfw[0:FFN_K],
        "meta_tokens": dh0[PAD:BLK],
        "norm_pre_mix": dg_pre_mix, "norm_post_mix": dg_post_mix, "b_in": db_in,
        "attn_sinks": dsink[0:1, 0:NH], "conv_dw_b": dcb, "conv_ln_g": dlg, "conv_ln_b": dlb,
        "b_conv_proj": db_cp, "norm_pre_ffn": dg_pre_ffn, "norm_post_ffn": dg_post_ffn, "ffn_dw_b": dfb,
    }
    return loss_cols, dh0[BLK:], grads


INQ = IN_W // N_CHIPS
DQ = D // N_CHIPS
TINY_ROWS, TINY_CONV, TINY_FFN, TINY_META = 64, 0, 32, 40
SHARES = [("w_in", D, INQ, BF16), ("w_attn_proj", DQ, D, BF16), ("w_conv_proj", DQ, D, BF16), ("w_out", DQ, D, BF16),
          ("w_up", D, UPQ, BF16), ("w_down", FFN // N_CHIPS, D, BF16), ("tiny", TINY_ROWS, UPQ, F32)]
N_SH = len(SHARES)
TINY_PARTS = [("conv_dw_w", TINY_CONV, CONV_K, DQ), ("ffn_dw_w", TINY_FFN, FFN_K, UPQ), ("meta_tokens", TINY_META, NMETA, DQ)]


def _tiny_pack(parts):
    lead = parts["conv_dw_w"].shape[:-2]
    out = jnp.zeros(lead + (TINY_ROWS, UPQ), F32)
    for name, r0, k, cols in TINY_PARTS:
        out = out.at[..., r0:r0 + k, 0:cols].set(parts[name].astype(F32))
    return out


def _tiny_unpack(tiny):
    return {name: tiny[..., r0:r0 + k, 0:cols] for name, r0, k, cols in TINY_PARTS}


def _cols_by_chip(a):
    rows, n = a.shape
    return a.reshape(rows, N_CHIPS, n // N_CHIPS).transpose(1, 0, 2)


def _cols_joined(a):
    _, rows, cols = a.shape
    return a.transpose(1, 0, 2).reshape(rows, N_CHIPS * cols)


def _to_planes(a, rows):
    return jnp.pad(a, [(0, rows * D - a.shape[-1])]).reshape(rows, D)


ANY = pl.BlockSpec(memory_space=pl.ANY)


def _place():
    x, y, c = lax.axis_index("x"), lax.axis_index("y"), lax.axis_index("c")
    chips = [(1 - x, y), (x, 1 - y), (1 - x, 1 - y)]
    return x, y, c, chips


def _rcopy(src, dst, ssem, rsem, to):
    return pltpu.make_async_remote_copy(src_ref=src, dst_ref=dst, send_sem=ssem, recv_sem=rsem,
                                        device_id=to, device_id_type=MESH)


def _halves(ref_or_rows, c):
    half = ref_or_rows // 2
    return pl.ds(c * half, half), pl.ds((1 - c) * half, half)


def _gather_shares(shares):
    def body(*refs):
        ins, outs, (ssem, rsem) = refs[:N_SH], refs[N_SH:2 * N_SH], refs[2 * N_SH:]
        x, y, c, chips = _place()
        q = 2 * x + y
        sib = (x, y, 1 - c)
        sent = []
        for a, (_, rows, _, _) in enumerate(SHARES):
            mine, _ = _halves(rows, c)
            for j, (cx, cy) in enumerate(chips):
                k = 3 * a + j
                cp = _rcopy(ins[a].at[mine], outs[a].at[q, mine], ssem.at[k], rsem.at[k], (cx, cy, c))
                cp.start()
                sent.append(cp)
        for a, (_, rows, _, _) in enumerate(SHARES):
            mine, _ = _halves(rows, c)
            for j, (cx, cy) in enumerate(chips):
                k = 3 * a + j
                landed = outs[a].at[2 * cx + cy, mine]
                _rcopy(ins[a].at[mine], landed, ssem.at[k], rsem.at[k], (cx, cy, c)).wait_recv()
                cp = _rcopy(landed, landed, ssem.at[3 * N_SH + k], rsem.at[3 * N_SH + k], sib)
                cp.start()
                sent.append(cp)
        for a, (_, rows, _, _) in enumerate(SHARES):
            _, other = _halves(rows, c)
            for j, (cx, cy) in enumerate(chips):
                k = 3 * N_SH + 3 * a + j
                theirs = outs[a].at[2 * cx + cy, other]
                _rcopy(theirs, theirs, ssem.at[k], rsem.at[k], sib).wait_recv()
        for cp in sent:
            cp.wait_send()

    return pl.pallas_call(
        body, name="gather_shares", in_specs=[ANY] * N_SH, out_specs=[ANY] * N_SH,
        out_shape=[_sds((N_CHIPS, rows, cols), dt) for _, rows, cols, dt in SHARES],
        scratch_shapes=[pltpu.SemaphoreType.DMA((6 * N_SH,)), pltpu.SemaphoreType.DMA((6 * N_SH,))],
    )(*shares)


def _sibling_swap(parts):
    def body(*refs):
        ins, outs, (ssem, rsem) = refs[:N_SH], refs[N_SH:2 * N_SH], refs[2 * N_SH:]
        x, y, c, _ = _place()
        copies = []
        for a, (_, rows, _, _) in enumerate(SHARES):
            _, other = _halves(rows, c)
            cp = _rcopy(ins[a].at[:, other], outs[a], ssem.at[a], rsem.at[a], (x, y, 1 - c))
            cp.start()
            copies.append(cp)
        for cp in copies:
            cp.wait()

    return pl.pallas_call(
        body, name="sibling_swap", in_specs=[ANY] * N_SH, out_specs=[ANY] * N_SH,
        out_shape=[_sds((N_CHIPS, rows // 2, cols), dt) for _, rows, cols, dt in SHARES],
        scratch_shapes=[pltpu.SemaphoreType.DMA((N_SH,)), pltpu.SemaphoreType.DMA((N_SH,))],
    )(*parts)


def _sum_pair(parts, recvs, c):
    steps = 2

    def body(c_ref, *refs):
        for a in range(N_SH):
            dt = SHARES[a][3]
            refs[2 * N_SH + a][...] = (refs[a][...].astype(F32) + refs[N_SH + a][...].astype(F32)).astype(dt)

    own, got, out = [], [], []
    for _, rows, cols, _ in SHARES:
        blk = rows // 2 // steps
        own.append(pl.BlockSpec((None, None, blk, cols), lambda q, i, c_ref: (q, c_ref[0], i, 0)))
        got.append(pl.BlockSpec((None, blk, cols), lambda q, i, c_ref: (q, i, 0)))
        out.append(pl.BlockSpec((None, blk, cols), lambda q, i, c_ref: (q, i, 0)))
    grid_spec = pltpu.PrefetchScalarGridSpec(num_scalar_prefetch=1, grid=(N_CHIPS, steps),
                                             in_specs=own + got, out_specs=out)
    views = [p.reshape(N_CHIPS, 2, rows // 2, cols) for p, (_, rows, cols, _) in zip(parts, SHARES)]
    return pl.pallas_call(
        body, name="sum_pair", grid_spec=grid_spec,
        out_shape=[_sds((N_CHIPS, rows // 2, cols), dt) for _, rows, cols, dt in SHARES],
        compiler_params=_params(("arbitrary", "arbitrary")),
    )(c, *views, *recvs)


def _chip_swap(sums):
    def body(*refs):
        ins, outs, (ssem, rsem) = refs[:N_SH], refs[N_SH:2 * N_SH], refs[2 * N_SH:]
        x, y, c, chips = _place()
        copies = []
        for a in range(N_SH):
            for j, (cx, cy) in enumerate(chips):
                k = 3 * a + j
                cp = _rcopy(ins[a].at[2 * cx + cy], outs[a].at[j], ssem.at[k], rsem.at[k], (cx, cy, c))
                cp.start()
                copies.append(cp)
        for cp in copies:
            cp.wait()

    return pl.pallas_call(
        body, name="chip_swap", in_specs=[ANY] * N_SH, out_specs=[ANY] * N_SH,
        out_shape=[_sds((N_CHIPS - 1, rows // 2, cols), dt) for _, rows, cols, dt in SHARES],
        scratch_shapes=[pltpu.SemaphoreType.DMA((3 * N_SH,)), pltpu.SemaphoreType.DMA((3 * N_SH,))],
    )(*sums)


def _sum_chips(sums, recvs, qc):
    steps = 2

    def body(qc_ref, *refs):
        for a in range(N_SH):
            acc = refs[a][...].astype(F32)
            for j in range(1, N_CHIPS):
                acc = acc + refs[j * N_SH + a][...].astype(F32)
            refs[N_CHIPS * N_SH + a][...] = acc

    own, got, out = [], [[], [], []], []
    for _, rows, cols, _ in SHARES:
        blk = rows // 2 // steps
        own.append(pl.BlockSpec((None, blk, cols), lambda i, qc_ref: (qc_ref[0], i, 0)))
        for j in range(N_CHIPS - 1):
            got[j].append(pl.BlockSpec((None, blk, cols), lambda i, qc_ref, j=j: (j, i, 0)))
        out.append(pl.BlockSpec((None, blk, cols), lambda i, qc_ref: (qc_ref[1], i, 0)))
    grid_spec = pltpu.PrefetchScalarGridSpec(num_scalar_prefetch=1, grid=(steps,),
                                             in_specs=own + got[0] + got[1] + got[2], out_specs=out)
    return pl.pallas_call(
        body, name="sum_chips", grid_spec=grid_spec,
        out_shape=[_sds((2, rows // 2, cols), F32) for _, rows, cols, _ in SHARES],
        compiler_params=_params(("arbitrary",)),
    )(qc, *sums, *recvs, *recvs, *recvs)


def _sibling_share(halves):
    def body(*refs):
        outs, (ssem, rsem) = refs[N_SH:2 * N_SH], refs[2 * N_SH:]
        x, y, c, _ = _place()
        copies = []
        for a in range(N_SH):
            cp = _rcopy(outs[a].at[c], outs[a].at[c], ssem.at[a], rsem.at[a], (x, y, 1 - c))
            cp.start()
            copies.append(cp)
        for a in range(N_SH):
            theirs = outs[a].at[1 - c]
            _rcopy(theirs, theirs, ssem.at[a], rsem.at[a], (x, y, 1 - c)).wait_recv()
        for cp in copies:
            cp.wait_send()

    return pl.pallas_call(
        body, name="sibling_share", in_specs=[ANY] * N_SH, out_specs=[ANY] * N_SH,
        out_shape=[_sds((2, rows // 2, cols), F32) for _, rows, cols, _ in SHARES],
        input_output_aliases={a: a for a in range(N_SH)},
        scratch_shapes=[pltpu.SemaphoreType.DMA((N_SH,)), pltpu.SemaphoreType.DMA((N_SH,))],
    )(*halves)


N_DEV = 8
SMALL_ROWS = 24


def _small_allreduce(sm):
    def body(s_ref, o_ref, buf, ssem, rsem):
        x, y, c, _ = _place()
        me = 4 * x + 2 * y + c
        buf[me] = s_ref[...]
        copies = []
        for d in range(1, N_DEV):
            dx, dy, dc = d >> 2, (d >> 1) & 1, d & 1
            to = (x ^ dx, y ^ dy, c ^ dc)
            cp = _rcopy(s_ref, buf.at[me], ssem.at[d - 1], rsem.at[d - 1], to)
            cp.start()
            copies.append(cp)
        for d in range(1, N_DEV):
            src = me ^ d
            _rcopy(s_ref, buf.at[src], ssem.at[d - 1], rsem.at[d - 1], (x, y, c)).wait_recv()
        for cp in copies:
            cp.wait_send()
        acc = buf[0]
        for k in range(1, N_DEV):
            acc = acc + buf[k]
        o_ref[...] = acc

    vm = pl.BlockSpec(memory_space=pltpu.VMEM)
    return pl.pallas_call(
        body, name="small_allreduce", in_specs=[vm], out_specs=vm,
        out_shape=_sds((SMALL_ROWS, D), F32),
        scratch_shapes=[pltpu.VMEM((N_DEV, SMALL_ROWS, D), F32),
                        pltpu.SemaphoreType.DMA((N_DEV - 1,)), pltpu.SemaphoreType.DMA((N_DEV - 1,))],
    )(sm)


SMALL_PLAN = [("norm_pre_mix", D), ("norm_post_mix", D), ("b_in", IN_W), ("attn_sinks", NH), ("conv_dw_b", D),
              ("conv_ln_g", D), ("conv_ln_b", D), ("b_conv_proj", D), ("norm_pre_ffn", D), ("norm_post_ffn", D),
              ("ffn_dw_b", 2 * FFN), ("loss", D)]


def _pack_small(parts):
    rows = [_to_planes(parts[name].reshape(-1), -(-n // D)) for name, n in SMALL_PLAN]
    used = sum(r.shape[0] for r in rows)
    return jnp.concatenate(rows + [jnp.zeros((SMALL_ROWS - used, D), F32)], 0)


def _unpack_small(packed):
    out, r0 = {}, 0
    for name, n in SMALL_PLAN:
        rows = -(-n // D)
        out[name] = packed[r0:r0 + rows].reshape(-1)[:n].reshape(1, n)
        r0 += rows
    return out


def _adamw(w, g, m, v, name):
    rows, cols = w.shape
    tr = _tile(rows, 256, 8) if rows % 8 == 0 else rows

    def body(w_ref, g_ref, m_ref, v_ref, d_ref, nm_ref, nv_ref):
        g = g_ref[...]
        m = B1 * m_ref[...] + (1.0 - B1) * g
        v = B2 * v_ref[...] + (1.0 - B2) * (g * g)
        nm_ref[...] = m
        nv_ref[...] = v
        m_hat = m / (1.0 - B1 ** STEP)
        v_hat = v / (1.0 - B2 ** STEP)
        d_ref[...] = -LR * (m_hat / (jnp.sqrt(v_hat) + ADAM_EPS) + WD * w_ref[...])

    spec = pl.BlockSpec((tr, cols), lambda i: (i, 0))
    return pl.pallas_call(
        body, name=name, grid=(rows // tr,), in_specs=[spec] * 4, out_specs=[spec] * 3,
        out_shape=[_sds((rows, cols), F32)] * 3, compiler_params=_params(("arbitrary",)),
    )(w, g, m, v)


NAMES = ["meta_tokens", "norm_pre_mix", "norm_post_mix", "w_in", "b_in", "attn_sinks", "w_attn_proj", "conv_dw_w",
         "conv_dw_b", "conv_ln_g", "conv_ln_b", "w_conv_proj", "b_conv_proj", "w_out", "norm_pre_ffn", "norm_post_ffn",
         "w_up", "ffn_dw_w", "ffn_dw_b", "w_down"]
MATMUL = ("w_in", "w_attn_proj", "w_conv_proj", "w_out", "w_up", "w_down")


def _two_d(a):
    return a.reshape(a.shape[-2:])


def kernel(x, meta_tokens, norm_pre_mix, norm_post_mix, w_in, b_in, attn_sinks, w_attn_proj, conv_dw_w, conv_dw_b, conv_ln_g, conv_ln_b, w_conv_proj, b_conv_proj, w_out, norm_pre_ffn, norm_post_ffn, w_up, ffn_dw_w, ffn_dw_b, w_down, loss_target, m_meta_tokens, m_norm_pre_mix, m_norm_post_mix, m_w_in, m_b_in, m_attn_sinks, m_w_attn_proj, m_conv_dw_w, m_conv_dw_b, m_conv_ln_g, m_conv_ln_b, m_w_conv_proj, m_b_conv_proj, m_w_out, m_norm_pre_ffn, m_norm_post_ffn, m_w_up, m_ffn_dw_w, m_ffn_dw_b, m_w_down, v_meta_tokens, v_norm_pre_mix, v_norm_post_mix, v_w_in, v_b_in, v_attn_sinks, v_w_attn_proj, v_conv_dw_w, v_conv_dw_b, v_conv_ln_g, v_conv_ln_b, v_w_conv_proj, v_b_conv_proj, v_w_out, v_norm_pre_ffn, v_norm_post_ffn, v_w_up, v_ffn_dw_w, v_ffn_dw_b, v_w_down):
    args = locals()
    w = {n: args[n] for n in NAMES}
    m = {n: args["m_" + n] for n in NAMES}
    v = {n: args["v_" + n] for n in NAMES}
    tiny_names = [name for name, _, _, _ in TINY_PARTS]
    big = list(MATMUL) + tiny_names
    cx, cy, cc = lax.axis_index("x"), lax.axis_index("y"), lax.axis_index("c")
    chip = 2 * cx + cy

    own = {n: _two_d(w[n]).astype(BF16) for n in MATMUL}
    own["tiny"] = _tiny_pack({n: _two_d(w[n]) for n in tiny_names})
    gathered = _gather_shares([own[n] for n, _, _, _ in SHARES])
    full = {n: lax.dynamic_update_slice(got, own[n][None], (chip, 0, 0)) for (n, _, _, _), got in zip(SHARES, gathered)}
    W = {n: _two_d(w[n]) for n in SMALL}
    W["w_in"] = _cols_joined(full["w_in"])
    W["w_up"] = full["w_up"]
    for n in ("w_attn_proj", "w_conv_proj", "w_out", "w_down"):
        W[n] = full[n].reshape(-1, D)
    for n, a in _tiny_unpack(full["tiny"]).items():
        W[n] = _cols_joined(a)

    loss_cols, grad_x, grads = local_step(x[0], loss_target[0], W)

    part = {n: grads[n].reshape(N_CHIPS, -1, D) for n in ("w_attn_proj", "w_conv_proj", "w_out", "w_down")}
    part["w_in"] = _cols_by_chip(grads["w_in"])
    part["w_up"] = grads["w_up"]
    part["tiny"] = _tiny_pack({n: _cols_by_chip(grads[n]) for n in tiny_names})
    parts = [part[n] for n, _, _, _ in SHARES]
    sums = _sum_pair(parts, _sibling_swap(parts), cc.reshape(1))
    halves = _sum_chips(sums, _chip_swap(sums), jnp.stack([chip, cc]))
    reduced = {n: a.reshape(rows, cols) for (n, rows, cols, _), a in zip(SHARES, _sibling_share(halves))}
    g_big = {n: reduced[n] for n in MATMUL}
    g_big.update(_tiny_unpack(reduced["tiny"]))
    small = dict(grads)
    small["loss"] = loss_cols
    g_small = _unpack_small(_small_allreduce(_pack_small(small)))
    loss = jnp.sum(g_small["loss"])

    g, delta, new_m, new_v = {}, {}, {}, {}
    for n in big:
        shape = w[n].shape
        g[n] = g_big[n].reshape(shape)
        d, nm, nv = _adamw(_two_d(w[n]), _two_d(g[n]), _two_d(m[n]), _two_d(v[n]), "adamw_" + n)
        delta[n], new_m[n], new_v[n] = d.reshape(shape), nm.reshape(shape), nv.reshape(shape)
    no_loss = jnp.zeros((1, D), F32)
    packs = [_pack_small(dict({n: d[n] for n in SMALL}, loss=no_loss)) for d in (w, dict(g_small), m, v)]
    ud, um, uv = (_unpack_small(a) for a in _adamw(*packs, "adamw_small"))
    for n in SMALL:
        g[n], delta[n], new_m[n], new_v[n] = g_small[n], ud[n], um[n], uv[n]

    return (loss, grad_x[None], *[g[n] for n in NAMES], *[delta[n] for n in NAMES],
            *[new_m[n] for n in NAMES], *[new_v[n] for n in NAMES])
```

```python
import jax
import jax.numpy as jnp
from jax import lax
from jax.experimental import pallas as pl
from jax.experimental.pallas import tpu as pltpu

F32, BF16 = jnp.float32, jnp.bfloat16

D = 1024
NH, NKV, HD = 16, 2, 64
NMETA, BLK = 16, 128
PAD = BLK - NMETA
ROT = HD // 4
THETA = 500000.0
CONV_K = 31
FFN = 2816
FFN_K = 3
IN_W = 5376
QKV_W, GLU_W, GATE_W = 1280, 2048, 2048
RMS_EPS, LN_EPS, NEG = 1e-6, 1e-5, -1e30
LR, B1, B2, ADAM_EPS, WD, STEP = 0.001, 0.9, 0.999, 1e-08, 0.01, 10

VMEM_LIMIT = 56 * 2 ** 20
MESH = pl.DeviceIdType.MESH

NT_DIMS = (((1,), (1,)), ((), ()))
TN_DIMS = (((0,), (0,)), ((), ()))


def _params(sem, **kw):
    return pltpu.CompilerParams(dimension_semantics=sem, vmem_limit_bytes=VMEM_LIMIT, **kw)


def _tile(n, pref, mult=16):
    for t in range(min(pref, n), 0, -1):
        if n % t == 0 and t % mult == 0:
            return t
    return n


def _row(tr, w, col=0):
    return pl.BlockSpec((tr, w), lambda i: (i, col))


def _rrow(tr, w, nt, col=0):
    return pl.BlockSpec((tr, w), lambda t: (nt - 1 - t, col))


def _const(shape):
    return pl.BlockSpec(shape, lambda *_: (0,) * len(shape))


def _sds(shape, dt):
    return jax.ShapeDtypeStruct(shape, dt)


class _Hook:
    def __init__(self, operands, out_shape, n_sem, start, finish, mid=None):
        self.operands, self.out_shape, self.n_sem = list(operands), list(out_shape), n_sem
        self.start, self.mid, self.finish = start, mid, finish

    def scratch(self):
        return [pltpu.SemaphoreType.DMA((self.n_sem,)), pltpu.SemaphoreType.DMA((self.n_sem,))]


def _call(body, hook, steps, *, name, grid, in_specs, out_specs, out_shape, operands, semantics, scratch_shapes=()):
    in_specs, out_specs, out_shape = list(in_specs), list(out_specs), list(out_shape)
    if hook is None:
        return pl.pallas_call(body, name=name, grid=grid, in_specs=in_specs, out_specs=out_specs, out_shape=out_shape,
                              scratch_shapes=list(scratch_shapes), compiler_params=_params(semantics))(*operands)
    n_in, n_out, n_hi, n_ho = len(in_specs), len(out_specs), len(hook.operands), len(hook.out_shape)

    def wrapped(*refs):
        ins, hi = refs[:n_in], refs[n_in:n_in + n_hi]
        o0 = n_in + n_hi
        outs, ho = refs[o0:o0 + n_out], refs[o0 + n_out:o0 + n_out + n_ho]
        scratch, (ssem, rsem) = refs[o0 + n_out + n_ho:len(refs) - 2], refs[len(refs) - 2:]
        first, middle, last = steps()

        @pl.when(first)
        def _():
            hook.start(hi, ho, ssem, rsem)

        body(*ins, *outs, *scratch)
        if hook.mid is not None:
            @pl.when(middle)
            def _():
                hook.mid(hi, ho, ssem, rsem)

        @pl.when(last)
        def _():
            hook.finish(hi, ho, ssem, rsem)

    any_spec = pl.BlockSpec(memory_space=pl.ANY)
    return pl.pallas_call(
        wrapped, name=name, grid=grid, in_specs=in_specs + [any_spec] * n_hi, out_specs=out_specs + [any_spec] * n_ho,
        out_shape=out_shape + hook.out_shape, scratch_shapes=list(scratch_shapes) + hook.scratch(),
        compiler_params=_params(semantics))(*operands, *hook.operands)


def _alone(hook, name):
    n_hi = len(hook.operands)

    def body(*refs):
        hi, ho, (ssem, rsem) = refs[:n_hi], refs[n_hi:len(refs) - 2], refs[len(refs) - 2:]
        hook.start(hi, ho, ssem, rsem)
        if hook.mid is not None:
            hook.mid(hi, ho, ssem, rsem)
        hook.finish(hi, ho, ssem, rsem)

    any_spec = pl.BlockSpec(memory_space=pl.ANY)
    return pl.pallas_call(body, name=name, in_specs=[any_spec] * n_hi, out_specs=[any_spec] * len(hook.out_shape),
                          out_shape=hook.out_shape, scratch_shapes=hook.scratch())(*hook.operands)


def _steps_1d(n, mid):
    def steps():
        i = pl.program_id(0)
        return i == 0, i == min(mid, n - 1), i == n - 1
    return steps


def _rms(x, g):
    r = lax.rsqrt(jnp.mean(x * x, -1, keepdims=True) + RMS_EPS)
    return x * r * g, r


def _rms_bwd(dy, x, r, g):
    gy = dy * g
    return r * gy - x * (r * r * r) * jnp.mean(x * gy, -1, keepdims=True)


def _colsum(x):
    return jnp.sum(x, axis=0, keepdims=True)


def _rope(x, c, sa, sb):
    n = x.shape[1]
    return x * c + pltpu.roll(x, n - 8, 1) * sa + pltpu.roll(x, 8, 1) * sb


def _rope_bwd(d, c, sa, sb):
    n = d.shape[1]
    return d * c + pltpu.roll(d * sa, 8, 1) + pltpu.roll(d * sb, n - 8, 1)


def _rope_tables(R):
    half = ROT // 2
    lane = jnp.arange(2 * HD) % HD
    inv = THETA ** (-(lane % half).astype(F32) * 2.0 / ROT)
    pos = (jnp.arange(R) - PAD).astype(F32)
    ang = pos[:, None] * inv[None, :]
    cos, sin = jnp.cos(ang), jnp.sin(ang)
    c = jnp.where(lane < ROT, cos, 1.0)
    sa = jnp.where(lane < half, -sin, 0.0)
    sb = jnp.where((lane >= half) & (lane < ROT), sin, 0.0)
    return c, sa, sb


IN_CHUNKS = ([(0, 512, True), (512, 1024, True), (1024, 1152, True), (1152, 1280, False)]
             + [(c, c + 512, False) for c in range(1280, IN_W, 512)])


def _in_proj(h0, g_pre, w_in, b_in, rope, tr, hook=None):
    R = h0.shape[0]
    nt = R // tr

    def body(h_ref, g_ref, w_ref, b_ref, c_ref, sa_ref, sb_ref, qkv_ref, glu_ref, gate_ref, n1_ref):
        n, _ = _rms(h_ref[...], g_ref[...])
        nb = n.astype(BF16)
        n1_ref[...] = nb
        for c0, c1, rot in IN_CHUNKS:
            acc = jnp.dot(nb, w_ref[:, c0:c1], preferred_element_type=F32) + b_ref[:, c0:c1]
            if rot:
                reps = (c1 - c0) // 128
                acc = _rope(acc, jnp.tile(c_ref[...], (1, reps)), jnp.tile(sa_ref[...], (1, reps)),
                            jnp.tile(sb_ref[...], (1, reps)))
            val = acc.astype(BF16)
            if c1 <= QKV_W:
                qkv_ref[:, c0:c1] = val
            elif c1 <= QKV_W + GLU_W:
                glu_ref[:, c0 - QKV_W:c1 - QKV_W] = val
            else:
                gate_ref[:, c0 - QKV_W - GLU_W:c1 - QKV_W - GLU_W] = val

    return _call(
        body, hook, _steps_1d(nt, (3 * nt) // 4), name="in_proj", grid=(nt,),
        in_specs=[_row(tr, D), _const((1, D)), _const((D, IN_W)), _const((1, IN_W)),
                  _row(tr, 128), _row(tr, 128), _row(tr, 128)],
        out_specs=[_row(tr, QKV_W), _row(tr, GLU_W), _row(tr, GATE_W), _row(tr, D)],
        out_shape=[_sds((R, QKV_W), BF16), _sds((R, GLU_W), BF16), _sds((R, GATE_W), BF16), _sds((R, D), BF16)],
        operands=(h0, g_pre, w_in, b_in, *rope), semantics=("arbitrary",))


def _attn_mask(n):
    qi = lax.broadcasted_iota(jnp.int32, (BLK, 3 * BLK), 0)
    kj = lax.broadcasted_iota(jnp.int32, (BLK, 3 * BLK), 1)
    tq = n * BLK + qi - PAD
    t_meta = kj - PAD
    t_loc = (n - 1) * BLK + (kj - BLK) - PAD
    meta_ok = (kj < BLK) & (t_meta >= 0) & (t_meta <= tq)
    loc_ok = (kj >= BLK) & (t_loc >= NMETA) & (t_loc <= tq) & (tq - t_loc < BLK)
    return meta_ok | loc_ok


def _dup_heads(ref0, refp, refc, low):
    a = jnp.concatenate([ref0[...], refp[...], refc[...]], 0).astype(F32)
    sw = pltpu.roll(a, HD, 1)
    return [jnp.where(low, a, sw).astype(BF16), jnp.where(low, sw, a).astype(BF16)]


def _kv_specs(nb, rev):
    def blk(col, which):
        def idx(t):
            n = nb - 1 - t if rev else t
            return ({"meta": 0, "prev": jnp.maximum(n - 1, 0), "own": n}[which], col)
        return pl.BlockSpec((BLK, BLK), idx)
    return [blk(col, w) for col in (8, 9) for w in ("meta", "prev", "own")]


def _attn_fwd(qkv, sinks, hook=None):
    R = qkv.shape[0]
    nb = R // BLK

    def body(s_ref, q_ref, k0, kp, kc, v0, vp, vc, o_ref, lse_ref):
        n = pl.program_id(0)
        lane = lax.broadcasted_iota(jnp.int32, (1, BLK), 1)
        low = lane < HD
        kd, vd = _dup_heads(k0, kp, kc, low), _dup_heads(v0, vp, vc, low)
        mask = _attn_mask(n)
        lse = jnp.zeros((BLK, BLK), F32)
        zero = jnp.zeros((), BF16)
        for pair in range(NH // 2):
            qp = q_ref[:, pair * BLK:(pair + 1) * BLK]
            kv = pair // (NH // NKV // 2)
            outs = []
            for e in range(2):
                h = 2 * pair + e
                qm = jnp.where(low if e == 0 else ~low, qp, zero)
                s = lax.dot_general(qm, kd[kv], NT_DIMS, preferred_element_type=F32) * (HD ** -0.5)
                s = jnp.where(mask, s, NEG)
                sk = s_ref[h]
                m = jnp.maximum(jnp.max(s, -1, keepdims=True), sk)
                p = jnp.exp(s - m)
                l = jnp.sum(p, -1, keepdims=True) + jnp.exp(sk - m)
                outs.append(jnp.dot(p.astype(BF16), vd[kv], preferred_element_type=F32) / l)
                lse = jnp.where(lane == h, m + jnp.log(l), lse)
            o_ref[:, pair * BLK:(pair + 1) * BLK] = jnp.where(low, outs[0], outs[1]).astype(BF16)
        lse_ref[...] = lse

    return _call(
        body, hook, _steps_1d(nb, (3 * nb) // 4), name="attn_fwd", grid=(nb,),
        in_specs=[pl.BlockSpec(memory_space=pltpu.SMEM), pl.BlockSpec((BLK, D), lambda n: (n, 0))] + _kv_specs(nb, False),
        out_specs=[_row(BLK, D), _row(BLK, BLK)],
        out_shape=[_sds((R, D), BF16), _sds((R, BLK), F32)],
        operands=(sinks, qkv, *([qkv] * 6)), semantics=("arbitrary",))


CONV_TCH, CONV_SUB, HALO = 256, 64, 32


def _conv_fwd(glu, w, b, tr):
    R = glu.shape[0]
    nc = D // CONV_TCH

    def body(a_ref, g_ref, w_ref, b_ref, o_ref, buf):
        i = pl.program_id(1)

        @pl.when(i == 0)
        def _():
            buf[0:HALO, :] = jnp.zeros((HALO, CONV_TCH), F32)

        @pl.when(i > 0)
        def _():
            buf[0:HALO, :] = buf[tr:tr + HALO, :]

        row = i * tr + lax.broadcasted_iota(jnp.int32, (tr, 1), 0)
        a, g = a_ref[...].astype(F32), g_ref[...].astype(F32)
        buf[HALO:HALO + tr, :] = jnp.where(row >= PAD, a * jax.nn.sigmoid(g), 0.0)
        for r0 in range(0, tr, CONV_SUB):
            acc = jnp.broadcast_to(b_ref[...], (CONV_SUB, CONV_TCH))
            for k in range(CONV_K):
                acc = acc + w_ref[k:k + 1, :] * buf[pl.ds(r0 + HALO - (CONV_K - 1) + k, CONV_SUB), :]
            o_ref[r0:r0 + CONV_SUB, :] = acc.astype(BF16)

    return pl.pallas_call(
        body, name="conv_fwd", grid=(nc, R // tr),
        in_specs=[pl.BlockSpec((tr, CONV_TCH), lambda c, i: (i, c)),
                  pl.BlockSpec((tr, CONV_TCH), lambda c, i: (i, nc + c)),
                  pl.BlockSpec((CONV_K, CONV_TCH), lambda c, i: (0, c)),
                  pl.BlockSpec((1, CONV_TCH), lambda c, i: (0, c))],
        out_specs=pl.BlockSpec((tr, CONV_TCH), lambda c, i: (i, c)),
        out_shape=_sds((R, D), BF16),
        scratch_shapes=[pltpu.VMEM((tr + HALO, CONV_TCH), F32)],
        compiler_params=_params(("arbitrary", "arbitrary")),
    )(glu, glu, w, b)


def _ln_silu(c1, lg, lb):
    mu = jnp.mean(c1, -1, keepdims=True)
    xc = c1 - mu
    rs = lax.rsqrt(jnp.mean(xc * xc, -1, keepdims=True) + LN_EPS)
    xh = xc * rs
    c2 = xh * lg + lb
    sg = jax.nn.sigmoid(c2)
    return xh, rs, c2, sg


def _mix_out(attn, c1, gates, h0, w_ap, w_cp, w_out, lg, lb, b_cp, g_post, g_ffn, tr):
    R = attn.shape[0]

    def body(at_ref, c1_ref, ga_ref, gc_ref, h0_ref, wap, wcp, wo, lg_ref, lb_ref, bcp, gp, gf,
             ao_ref, co_ref, c3_ref, mg_ref, mix_ref, h1_ref, n2_ref):
        ao = jnp.dot(at_ref[...], wap[...], preferred_element_type=F32)
        _, _, c2, sg = _ln_silu(c1_ref[...].astype(F32), lg_ref[...], lb_ref[...])
        c3 = (c2 * sg).astype(BF16)
        c3_ref[...] = c3
        co = jnp.dot(c3, wcp[...], preferred_element_type=F32) + bcp[...]
        ao_b, co_b = ao.astype(BF16), co.astype(BF16)
        ao_ref[...] = ao_b
        co_ref[...] = co_b
        merged = (jax.nn.sigmoid(ga_ref[...].astype(F32)) * ao_b.astype(F32)
                  + jax.nn.sigmoid(gc_ref[...].astype(F32)) * co_b.astype(F32)).astype(BF16)
        mg_ref[...] = merged
        mix = jnp.dot(merged, wo[...], preferred_element_type=F32).astype(BF16)
        mix_ref[...] = mix
        y, _ = _rms(mix.astype(F32), gp[...])
        h1 = h0_ref[...] + y
        h1_ref[...] = h1
        n2, _ = _rms(h1, gf[...])
        row = pl.program_id(0) * tr + lax.broadcasted_iota(jnp.int32, (tr, 1), 0)
        n2_ref[...] = jnp.where(row >= PAD, n2, 0.0).astype(BF16)

    vec = _const((1, D))
    return pl.pallas_call(
        body, name="mix_out", grid=(R // tr,),
        in_specs=[_row(tr, D), _row(tr, D), _row(tr, D, 0), _row(tr, D, 1), _row(tr, D),
                  _const((D, D)), _const((D, D)), _const((D, D)), vec, vec, vec, vec, vec],
        out_specs=[_row(tr, D)] * 7,
        out_shape=[_sds((R, D), BF16)] * 5 + [_sds((R, D), F32), _sds((R, D), BF16)],
        compiler_params=_params(("arbitrary",)),
    )(attn, c1, gates, gates, h0, w_ap, w_cp, w_out, lg, lb, b_cp, g_post, g_ffn)


FFN_CH = 256
N_CHIPS = 4
UPQ = 2 * FFN // N_CHIPS
UP_CHUNKS = [(q, c0, min(c0 + 512, UPQ)) for q in range(N_CHIPS // 2) for c0 in range(0, UPQ, 512)]


def _shift_down(x, k, halo):
    tr = x.shape[0]
    row = lax.broadcasted_iota(jnp.int32, (tr, 1), 0)
    y = pltpu.roll(x, k, 0)
    for j in range(k):
        y = jnp.where(row == j, halo[8 - k + j:8 - k + j + 1, :], y)
    return y


def _shift_up(x, k, halo):
    tr = x.shape[0]
    row = lax.broadcasted_iota(jnp.int32, (tr, 1), 0)
    y = pltpu.roll(x, tr - k, 0)
    for j in range(k):
        y = jnp.where(row == tr - k + j, halo[j:j + 1, :], y)
    return y


def _conv3(x, halo, w, b):
    return w[2:3, :] * x + w[1:2, :] * _shift_down(x, 1, halo) + w[0:1, :] * _shift_down(x, 2, halo) + b


def _ffn_up(n2, w_up, fw, fb, tr):
    R = n2.shape[0]

    def body(n_ref, w_ref, fw_ref, fb_ref, up_ref, act_ref, carry):
        @pl.when(pl.program_id(0) == 0)
        def _():
            carry[...] = jnp.zeros_like(carry)

        nb = n_ref[...]
        for q, c0, c1 in UP_CHUNKS:
            us = []
            for qq in (q, q + N_CHIPS // 2):
                cs = slice(qq * UPQ + c0, qq * UPQ + c1)
                x = jnp.dot(nb, w_ref[qq, :, c0:c1], preferred_element_type=F32).astype(BF16)
                up_ref[:, cs] = x
                x = x.astype(F32)
                us.append(_conv3(x, carry[:, cs], fw_ref[:, cs], fb_ref[:, cs]))
                carry[:, cs] = x[tr - 8:tr, :]
            act_ref[:, q * UPQ + c0:q * UPQ + c1] = (us[0] * jax.nn.sigmoid(us[0]) * us[1]).astype(BF16)

    return pl.pallas_call(
        body, name="ffn_up", grid=(R // tr,),
        in_specs=[_row(tr, D), _const((N_CHIPS, D, UPQ)), _const((FFN_K, 2 * FFN)), _const((1, 2 * FFN))],
        out_specs=[_row(tr, 2 * FFN), _row(tr, FFN)],
        out_shape=[_sds((R, 2 * FFN), BF16), _sds((R, FFN), BF16)],
        scratch_shapes=[pltpu.VMEM((8, 2 * FFN), F32)],
        compiler_params=_params(("arbitrary",)),
    )(n2, w_up, fw, fb)


def _ffn_down(act, w_down, h1, tgt, g_post, tr):
    R = act.shape[0]

    def body(a_ref, w_ref, h1_ref, t_ref, g_ref, dh2_ref, dffn_ref, loss_ref, dg_ref):
        @pl.when(pl.program_id(0) == 0)
        def _():
            loss_ref[...] = jnp.zeros_like(loss_ref)
            dg_ref[...] = jnp.zeros_like(dg_ref)

        f = jnp.dot(a_ref[...], w_ref[...], preferred_element_type=F32)
        g = g_ref[...]
        y, r = _rms(f, g)
        row = pl.program_id(0) * tr + lax.broadcasted_iota(jnp.int32, (tr, 1), 0)
        e = jnp.where(row >= BLK, h1_ref[...] + y - t_ref[...], 0.0)
        loss_ref[...] += _colsum(e * e) * (0.5 / D)
        dy = e * (1.0 / D)
        dh2_ref[...] = dy
        dffn_ref[...] = _rms_bwd(dy, f, r, g).astype(BF16)
        dg_ref[...] += _colsum(dy * f * r)

    return pl.pallas_call(
        body, name="ffn_down", grid=(R // tr,),
        in_specs=[_row(tr, FFN), _const((FFN, D)), _row(tr, D), _row(tr, D), _const((1, D))],
        out_specs=[_row(tr, D), _row(tr, D), _const((1, D)), _const((1, D))],
        out_shape=[_sds((R, D), F32), _sds((R, D), BF16), _sds((1, D), F32), _sds((1, D), F32)],
        compiler_params=_params(("arbitrary",)),
    )(act, w_down, h1, tgt, g_post)


def _ffn_bwd_act(dffn, w_down, up, fw, fb, tr):
    R = dffn.shape[0]
    nt = R // tr

    def body(d_ref, w_ref, up_ref, hal_ref, fw_ref, fb_ref, dup_ref, dfw_ref, dfb_ref, carry):
        t = pl.program_id(0)
        i = nt - 1 - t

        @pl.when(t == 0)
        def _():
            carry[...] = jnp.zeros_like(carry)
            dfw_ref[...] = jnp.zeros_like(dfw_ref)
            dfb_ref[...] = jnp.zeros_like(dfb_ref)

        dff = d_ref[...]
        row = i * tr + lax.broadcasted_iota(jnp.int32, (tr, 1), 0)
        first = i == 0
        for c in range(0, FFN, FFN_CH):
            dact = lax.dot_general(dff, w_ref[c:c + FFN_CH, :], NT_DIMS, preferred_element_type=F32)
            xs, us = [], []
            for off in (c, FFN + c):
                cs = slice(off, off + FFN_CH)
                x = up_ref[:, cs].astype(F32)
                halo = jnp.where(first, 0.0, hal_ref[:, cs].astype(F32))
                x1, x2 = _shift_down(x, 1, halo), _shift_down(x, 2, halo)
                w = fw_ref[:, cs]
                us.append(w[2:3, :] * x + w[1:2, :] * x1 + w[0:1, :] * x2 + fb_ref[:, cs])
                xs.append((x, x1, x2))
            sg = jax.nn.sigmoid(us[0])
            silu = us[0] * sg
            dus = [dact * us[1] * sg * (1.0 + us[0] * (1.0 - sg)), dact * silu]
            for (x, x1, x2), du, off in zip(xs, dus, (c, FFN + c)):
                cs = slice(off, off + FFN_CH)
                w = fw_ref[:, cs]
                nxt = carry[:, cs]
                dx = w[2:3, :] * du + w[1:2, :] * _shift_up(du, 1, nxt) + w[0:1, :] * _shift_up(du, 2, nxt)
                dup_ref[:, cs] = jnp.where(row >= PAD, dx, 0.0).astype(BF16)
                dfw_ref[0:1, cs] += _colsum(x2 * du)
                dfw_ref[1:2, cs] += _colsum(x1 * du)
                dfw_ref[2:3, cs] += _colsum(x * du)
                dfb_ref[:, cs] += _colsum(du)
                carry[:, cs] = du[0:8, :]

    halo_spec = pl.BlockSpec((8, 2 * FFN), lambda t: (jnp.maximum((nt - 1 - t) * (tr // 8) - 1, 0), 0))
    return pl.pallas_call(
        body, name="ffn_bwd_act", grid=(nt,),
        in_specs=[_rrow(tr, D, nt), _const((FFN, D)), _rrow(tr, 2 * FFN, nt), halo_spec,
                  _const((FFN_K, 2 * FFN)), _const((1, 2 * FFN))],
        out_specs=[_rrow(tr, 2 * FFN, nt), _const((8, 2 * FFN)), _const((1, 2 * FFN))],
        out_shape=[_sds((R, 2 * FFN), BF16), _sds((8, 2 * FFN), F32), _sds((1, 2 * FFN), F32)],
        scratch_shapes=[pltpu.VMEM((8, 2 * FFN), F32)],
        compiler_params=_params(("arbitrary",)),
    )(dffn, w_down, up, up, fw, fb)


def _ffn_bwd_in(dup, w_up, h1, dh2, mix, g_ffn, g_post, tr):
    R = dup.shape[0]

    def body(d_ref, w_ref, h1_ref, dh2_ref, mix_ref, gf_ref, gp_ref, dh1_ref, dmix_ref, dgf_ref, dgp_ref):
        @pl.when(pl.program_id(0) == 0)
        def _():
            dgf_ref[...] = jnp.zeros_like(dgf_ref)
            dgp_ref[...] = jnp.zeros_like(dgp_ref)

        dn2 = sum(lax.dot_general(d_ref[:, q * UPQ:(q + 1) * UPQ], w_ref[q], NT_DIMS, preferred_element_type=F32)
                  for q in range(N_CHIPS))
        h1 = h1_ref[...]
        _, r2 = _rms(h1, gf_ref[...])
        dh1 = dh2_ref[...] + _rms_bwd(dn2, h1, r2, gf_ref[...])
        dgf_ref[...] += _colsum(dn2 * h1 * r2)
        dh1_ref[...] = dh1
        m = mix_ref[...].astype(F32)
        _, rm = _rms(m, gp_ref[...])
        dmix_ref[...] = _rms_bwd(dh1, m, rm, gp_ref[...]).astype(BF16)
        dgp_ref[...] += _colsum(dh1 * m * rm)

    vec = _const((1, D))
    return pl.pallas_call(
        body, name="ffn_bwd_in", grid=(R // tr,),
        in_specs=[_row(tr, 2 * FFN), _const((N_CHIPS, D, UPQ)), _row(tr, D), _row(tr, D), _row(tr, D), vec, vec],
        out_specs=[_row(tr, D), _row(tr, D), vec, vec],
        out_shape=[_sds((R, D), F32), _sds((R, D), BF16), _sds((1, D), F32), _sds((1, D), F32)],
        compiler_params=_params(("arbitrary",)),
    )(dup, w_up, h1, dh2, mix, g_ffn, g_post)


def _mix_bwd(dmix, ao, co, gates, c1, w_out, w_ap, w_cp, lg, lb, tr):
    R = dmix.shape[0]

    def body(dm_ref, ao_ref, co_ref, ga_ref, gc_ref, c1_ref, wo, wap, wcp, lg_ref, lb_ref,
             dao_ref, dco_ref, dgate_ref, dattn_ref, dc1_ref, dbcp_ref, dlg_ref, dlb_ref, dcb_ref):
        @pl.when(pl.program_id(0) == 0)
        def _():
            for ref in (dbcp_ref, dlg_ref, dlb_ref, dcb_ref):
                ref[...] = jnp.zeros_like(ref)

        dmg = lax.dot_general(dm_ref[...], wo[...], NT_DIMS, preferred_element_type=F32)
        sa = jax.nn.sigmoid(ga_ref[...].astype(F32))
        sc = jax.nn.sigmoid(gc_ref[...].astype(F32))
        dao = (dmg * sa).astype(BF16)
        dco = (dmg * sc).astype(BF16)
        dao_ref[...] = dao
        dco_ref[...] = dco
        dgate_ref[:, 0:D] = (dmg * ao_ref[...].astype(F32) * sa * (1.0 - sa)).astype(BF16)
        dgate_ref[:, D:2 * D] = (dmg * co_ref[...].astype(F32) * sc * (1.0 - sc)).astype(BF16)
        dbcp_ref[...] += _colsum(dco.astype(F32))
        dattn_ref[...] = lax.dot_general(dao, wap[...], NT_DIMS, preferred_element_type=F32).astype(BF16)
        dc3 = lax.dot_general(dco, wcp[...], NT_DIMS, preferred_element_type=F32)
        xh, rs, c2, sg = _ln_silu(c1_ref[...].astype(F32), lg_ref[...], lb_ref[...])
        dc2 = dc3 * sg * (1.0 + c2 * (1.0 - sg))
        dlg_ref[...] += _colsum(dc2 * xh)
        dlb_ref[...] += _colsum(dc2)
        dxh = dc2 * lg_ref[...]
        dc1 = rs * (dxh - jnp.mean(dxh, -1, keepdims=True) - xh * jnp.mean(dxh * xh, -1, keepdims=True))
        dc1_ref[...] = dc1
        dcb_ref[...] += _colsum(dc1)

    vec = _const((1, D))
    return pl.pallas_call(
        body, name="mix_bwd", grid=(R // tr,),
        in_specs=[_row(tr, D), _row(tr, D), _row(tr, D), _row(tr, D, 0), _row(tr, D, 1), _row(tr, D),
                  _const((D, D)), _const((D, D)), _const((D, D)), vec, vec],
        out_specs=[_row(tr, D), _row(tr, D), _row(tr, 2 * D), _row(tr, D), _row(tr, D), vec, vec, vec, vec],
        out_shape=[_sds((R, D), BF16), _sds((R, D), BF16), _sds((R, 2 * D), BF16), _sds((R, D), BF16),
                   _sds((R, D), F32)] + [_sds((1, D), F32)] * 4,
        compiler_params=_params(("arbitrary",)),
    )(dmix, ao, co, gates, gates, c1, w_out, w_ap, w_cp, lg, lb)


def _conv_bwd(dc1, glu, w, tr, hook=None):
    R = dc1.shape[0]
    nt, nc = R // tr, D // CONV_TCH

    def body(d_ref, a_ref, g_ref, w_ref, dglu_a, dglu_g, dw_ref, buf):
        t = pl.program_id(1)
        i = nt - 1 - t

        @pl.when(t == 0)
        def _():
            buf[tr:tr + HALO, :] = jnp.zeros((HALO, CONV_TCH), F32)
            dw_ref[...] = jnp.zeros_like(dw_ref)

        @pl.when(t > 0)
        def _():
            buf[tr:tr + HALO, :] = buf[0:HALO, :]

        buf[0:tr, :] = d_ref[...]
        for r0 in range(0, tr, CONV_SUB):
            rs = slice(r0, r0 + CONV_SUB)
            row = i * tr + r0 + lax.broadcasted_iota(jnp.int32, (CONV_SUB, 1), 0)
            a, g = a_ref[rs, :].astype(F32), g_ref[rs, :].astype(F32)
            sg = jax.nn.sigmoid(g)
            glu = jnp.where(row >= PAD, a * sg, 0.0)
            acc = jnp.zeros((CONV_SUB, CONV_TCH), F32)
            for k in range(CONV_K):
                win = buf[pl.ds(r0 + CONV_K - 1 - k, CONV_SUB), :]
                acc = acc + w_ref[k:k + 1, :] * win
                dw_ref[k:k + 1, :] += _colsum(glu * win)
            dglu = jnp.where(row >= PAD, acc, 0.0)
            dglu_a[rs, :] = (dglu * sg).astype(BF16)
            dglu_g[rs, :] = (dglu * a * sg * (1.0 - sg)).astype(BF16)

    def rspec(col0):
        return pl.BlockSpec((tr, CONV_TCH), lambda c, t: (nt - 1 - t, col0 + c))

    def steps():
        c, t = pl.program_id(0), pl.program_id(1)
        return (c == 0) & (t == 0), False, (c == nc - 1) & (t == nt - 1)

    return _call(
        body, hook, steps, name="conv_bwd", grid=(nc, nt),
        in_specs=[rspec(0), rspec(0), rspec(nc), pl.BlockSpec((CONV_K, CONV_TCH), lambda c, t: (0, c))],
        out_specs=[rspec(0), rspec(0), pl.BlockSpec((HALO, CONV_TCH), lambda c, t: (0, c))],
        out_shape=[_sds((R, D), BF16), _sds((R, D), BF16), _sds((HALO, D), F32)],
        scratch_shapes=[pltpu.VMEM((tr + HALO, CONV_TCH), F32)],
        operands=(dc1, glu, glu, w), semantics=("arbitrary", "arbitrary"))


def _attn_bwd(qkv, o, do, lse, sinks, rope):
    R = qkv.shape[0]
    nb = R // BLK

    def body(s_ref, q_ref, k0, kp, kc, v0, vp, vc, o_ref, do_ref, lse_ref, c_ref, sa_ref, sb_ref,
             dq_ref, dk_ref, dv_ref, dsink_ref, car_k, car_v, met_k, met_v):
        t = pl.program_id(0)
        n = nb - 1 - t

        @pl.when(t == 0)
        def _():
            for ref in (car_k, car_v, met_k, met_v, dsink_ref):
                ref[...] = jnp.zeros_like(ref)

        lane = lax.broadcasted_iota(jnp.int32, (1, BLK), 1)
        low = lane < HD
        kd, vd = _dup_heads(k0, kp, kc, low), _dup_heads(v0, vp, vc, low)
        mask = _attn_mask(n)
        tabs = (c_ref[...], sa_ref[...], sb_ref[...])
        zero = jnp.zeros((), BF16)
        dk_acc = [jnp.zeros((3 * BLK, BLK), F32) for _ in range(NKV)]
        dv_acc = [jnp.zeros((3 * BLK, BLK), F32) for _ in range(NKV)]
        dsink = jnp.zeros((1, BLK), F32)
        lse_all = lse_ref[...]
        for pair in range(NH // 2):
            cs = slice(pair * BLK, (pair + 1) * BLK)
            qp, dop = q_ref[:, cs], do_ref[:, cs]
            prod = dop.astype(F32) * o_ref[:, cs].astype(F32)
            kv = pair // (NH // NKV // 2)
            dqs = []
            for e in range(2):
                h = 2 * pair + e
                sel = low if e == 0 else ~low
                qm, dom = jnp.where(sel, qp, zero), jnp.where(sel, dop, zero)
                lse_h = jnp.sum(jnp.where(lane == h, lse_all, 0.0), -1, keepdims=True)
                s = lax.dot_general(qm, kd[kv], NT_DIMS, preferred_element_type=F32) * (HD ** -0.5)
                p = jnp.where(mask, jnp.exp(s - lse_h), 0.0)
                dp = lax.dot_general(dom, vd[kv], NT_DIMS, preferred_element_type=F32)
                delta = jnp.sum(jnp.where(sel, prod, 0.0), -1, keepdims=True)
                ds = (p * (dp - delta) * (HD ** -0.5)).astype(BF16)
                dqs.append(jnp.dot(ds, kd[kv], preferred_element_type=F32))
                dk_acc[kv] = dk_acc[kv] + lax.dot_general(ds, qm, TN_DIMS, preferred_element_type=F32)
                dv_acc[kv] = dv_acc[kv] + lax.dot_general(p.astype(BF16), dom, TN_DIMS, preferred_element_type=F32)
                ps = jnp.exp(s_ref[h] - lse_h)
                dsink = dsink + jnp.where(lane == h, -jnp.sum(ps * delta), 0.0)
            dq_ref[:, cs] = _rope_bwd(jnp.where(low, dqs[0], dqs[1]), *tabs).astype(BF16)
        dsink_ref[0:1, :] += dsink

        def fold(acc):
            tot = [a + pltpu.roll(a, HD, 1) for a in acc]
            return jnp.where(low, tot[0], tot[1])

        dk_all, dv_all = fold(dk_acc), fold(dv_acc)
        met_k[...] += dk_all[0:BLK, :]
        met_v[...] += dv_all[0:BLK, :]
        last = jnp.where(n == 0, 1.0, 0.0)
        dk_n = dk_all[2 * BLK:3 * BLK, :] + car_k[...] + last * met_k[...]
        dv_n = dv_all[2 * BLK:3 * BLK, :] + car_v[...] + last * met_v[...]
        dk_ref[...] = _rope_bwd(dk_n, *tabs).astype(BF16)
        dv_ref[...] = dv_n.astype(BF16)
        car_k[...] = dk_all[BLK:2 * BLK, :]
        car_v[...] = dv_all[BLK:2 * BLK, :]

    rblk = lambda w: pl.BlockSpec((BLK, w), lambda t: (nb - 1 - t, 0))
    return pl.pallas_call(
        body, name="attn_bwd", grid=(nb,),
        in_specs=[pl.BlockSpec(memory_space=pltpu.SMEM), rblk(D)] + _kv_specs(nb, True)
                 + [rblk(D), rblk(D), rblk(BLK), rblk(BLK), rblk(BLK), rblk(BLK)],
        out_specs=[rblk(D), rblk(BLK), rblk(BLK), _const((8, BLK))],
        out_shape=[_sds((R, D), BF16), _sds((R, BLK), BF16), _sds((R, BLK), BF16), _sds((8, BLK), F32)],
        scratch_shapes=[pltpu.VMEM((BLK, BLK), F32)] * 4,
        compiler_params=_params(("arbitrary",)),
    )(sinks, qkv, *([qkv] * 6), o, do, lse, *rope)


def _in_bwd(dproj, w_in, h0, dh1, g_pre, tr):
    R = dproj.shape[0]

    def body(d_ref, w_ref, h0_ref, dh1_ref, g_ref, dh0_ref, dg_ref, db_ref):
        @pl.when(pl.program_id(0) == 0)
        def _():
            dg_ref[...] = jnp.zeros_like(dg_ref)
            db_ref[...] = jnp.zeros_like(db_ref)

        d = d_ref[...]
        dn1 = lax.dot_general(d, w_ref[...], NT_DIMS, preferred_element_type=F32)
        h0 = h0_ref[...]
        _, r = _rms(h0, g_ref[...])
        dh0_ref[...] = dh1_ref[...] + _rms_bwd(dn1, h0, r, g_ref[...])
        dg_ref[...] += _colsum(dn1 * h0 * r)
        db_ref[...] += _colsum(d.astype(F32))

    return pl.pallas_call(
        body, name="in_bwd", grid=(R // tr,),
        in_specs=[_row(tr, IN_W), _const((D, IN_W)), _row(tr, D), _row(tr, D), _const((1, D))],
        out_specs=[_row(tr, D), _const((1, D)), _const((1, IN_W))],
        out_shape=[_sds((R, D), F32), _sds((1, D), F32), _sds((1, IN_W), F32)],
        compiler_params=_params(("arbitrary",)),
    )(dproj, w_in, h0, dh1, g_pre)


def _dw(a, b, name, tn, tr, by_chip=False):
    R, ka = a.shape
    n = b.shape[1]
    nt = R // tr
    if by_chip:
        out_spec = pl.BlockSpec((None, ka, tn), lambda j, i: (j, 0, 0))
        out_shape = _sds((n // tn, ka, tn), BF16)
    else:
        out_spec = pl.BlockSpec((ka, tn), lambda j, i: (0, j))
        out_shape = _sds((ka, n), BF16)

    def body(a_ref, b_ref, o_ref, acc):
        i = pl.program_id(1)

        @pl.when(i == 0)
        def _():
            acc[...] = jnp.zeros_like(acc)

        acc[...] += lax.dot_general(a_ref[...], b_ref[...], TN_DIMS, preferred_element_type=F32)

        @pl.when(i == nt - 1)
        def _():
            o_ref[...] = acc[...].astype(BF16)

    return pl.pallas_call(
        body, name=name, grid=(n // tn, nt),
        in_specs=[pl.BlockSpec((tr, ka), lambda j, i: (i, 0)), pl.BlockSpec((tr, tn), lambda j, i: (i, j))],
        out_specs=out_spec, out_shape=out_shape,
        scratch_shapes=[pltpu.VMEM((ka, tn), F32)],
        compiler_params=_params(("arbitrary", "arbitrary")),
    )(a, b)


SMALL = ["norm_pre_mix", "norm_post_mix", "b_in", "attn_sinks", "conv_dw_b", "conv_ln_g", "conv_ln_b",
         "b_conv_proj", "norm_pre_ffn", "norm_post_ffn", "ffn_dw_b"]


def local_step(x, tgt, W, dist=None):
    W = dict(W)
    S = x.shape[0]
    R = S + BLK
    tr = _tile(R, 384)
    trw = _tile(R, 1056)
    rope = _rope_tables(R)
    h0 = jnp.concatenate([jnp.zeros((PAD, D), F32), W["meta_tokens"], x], 0)
    tgt_p = jnp.concatenate([jnp.zeros((BLK, D), F32), tgt], 0)

    qkv, glu, gates, n1, *got = _in_proj(h0, W["norm_pre_mix"], W["w_in"], W["b_in"], rope, tr,
                                         dist and dist.gather_hook(BRANCH_SHARES))
    if dist:
        W.update(dist.weights(BRANCH_SHARES, got))
    sinks = W["attn_sinks"].reshape(NH)
    attn, lse, *got = _attn_fwd(qkv, sinks, dist and dist.gather_hook(FFN_SHARES))
    if dist:
        W.update(dist.weights(FFN_SHARES, got))
    c1 = _conv_fwd(glu, W["conv_dw_w"], W["conv_dw_b"], tr)
    ao, co, c3, merged, mix, h1, n2 = _mix_out(
        attn, c1, gates, h0, W["w_attn_proj"], W["w_conv_proj"], W["w_out"], W["conv_ln_g"], W["conv_ln_b"],
        W["b_conv_proj"], W["norm_post_mix"], W["norm_pre_ffn"], tr)
    up, act = _ffn_up(n2, W["w_up"], W["ffn_dw_w"], W["ffn_dw_b"], tr)
    dh2, dffn, loss_cols, dg_post_ffn = _ffn_down(act, W["w_down"], h1, tgt_p, W["norm_post_ffn"], tr)

    dw_down = _dw(act, dffn, "dw_down", 512, trw)
    dup, dfw, dfb = _ffn_bwd_act(dffn, W["w_down"], up, W["ffn_dw_w"], W["ffn_dw_b"], tr)
    dw_up = _dw(n2, dup, "dw_up", UPQ, trw, by_chip=True)
    ffn_sums = dist and dist.pair_sums(FFN_SHARES, [dw_up, dw_down.reshape(N_CHIPS, -1, D)], "ffn")
    dh1, dmix, dg_pre_ffn, dg_post_mix = _ffn_bwd_in(dup, W["w_up"], h1, dh2, mix, W["norm_pre_ffn"],
                                                      W["norm_post_mix"], tr)
    dao, dco, dgates, dattn, dc1, db_cp, dlg, dlb, dcb = _mix_bwd(
        dmix, ao, co, gates, c1, W["w_out"], W["w_attn_proj"], W["w_conv_proj"], W["conv_ln_g"], W["conv_ln_b"], tr)
    dglu_a, dglu_g, dcw, *got = _conv_bwd(dc1, glu, W["conv_dw_w"], tr, dist and dist.chip_hook(FFN_SHARES, ffn_sums))
    if dist:
        dist.finish(FFN_SHARES, ffn_sums, got, "ffn")
    dq, dk, dv, dsink = _attn_bwd(qkv, attn, dattn, lse, sinks, rope)
    dproj = jnp.concatenate([dq, dk, dv, dglu_a, dglu_g, dgates], 1)
    dh0, dg_pre_mix, db_in = _in_bwd(dproj, W["w_in"], h0, dh1, W["norm_pre_mix"], tr)

    grads = {
        "w_in": _dw(n1, dproj, "dw_in", 768, trw),
        "w_attn_proj": _dw(attn, dao, "dw_attn_proj", D, trw),
        "w_conv_proj": _dw(c3, dco, "dw_conv_proj", D, trw),
        "w_out": _dw(merged, dmix, "dw_out", D, trw),
        "w_up": dw_up,
        "w_down": dw_down,
        "conv_dw_w": dcw[0:CONV_K],
        "ffn_dw_w": dfw[0:FFN_K],
        "meta_tokens": dh0[PAD:BLK],
        "norm_pre_mix": dg_pre_mix, "norm_post_mix": dg_post_mix, "b_in": db_in,
        "attn_sinks": dsink[0:1, 0:NH], "conv_dw_b": dcb, "conv_ln_g": dlg, "conv_ln_b": dlb,
        "b_conv_proj": db_cp, "norm_pre_ffn": dg_pre_ffn, "norm_post_ffn": dg_post_ffn, "ffn_dw_b": dfb,
    }
    return loss_cols, dh0[BLK:], grads


INQ = IN_W // N_CHIPS
DQ = D // N_CHIPS
TINY_ROWS, TINY_CONV, TINY_FFN, TINY_META = 64, 0, 32, 40
SHARES = [("w_in", D, INQ, BF16), ("w_attn_proj", DQ, D, BF16), ("w_conv_proj", DQ, D, BF16), ("w_out", DQ, D, BF16),
          ("w_up", D, UPQ, BF16), ("w_down", FFN // N_CHIPS, D, BF16), ("tiny", TINY_ROWS, UPQ, F32)]
N_SH = len(SHARES)
TINY_PARTS = [("conv_dw_w", TINY_CONV, CONV_K, DQ), ("ffn_dw_w", TINY_FFN, FFN_K, UPQ), ("meta_tokens", TINY_META, NMETA, DQ)]


def _tiny_pack(parts):
    lead = parts["conv_dw_w"].shape[:-2]
    out = jnp.zeros(lead + (TINY_ROWS, UPQ), F32)
    for name, r0, k, cols in TINY_PARTS:
        out = out.at[..., r0:r0 + k, 0:cols].set(parts[name].astype(F32))
    return out


def _tiny_unpack(tiny):
    return {name: tiny[..., r0:r0 + k, 0:cols] for name, r0, k, cols in TINY_PARTS}


def _cols_by_chip(a):
    rows, n = a.shape
    return a.reshape(rows, N_CHIPS, n // N_CHIPS).transpose(1, 0, 2)


def _cols_joined(a):
    _, rows, cols = a.shape
    return a.transpose(1, 0, 2).reshape(rows, N_CHIPS * cols)


def _to_planes(a, rows):
    return jnp.pad(a, [(0, rows * D - a.shape[-1])]).reshape(rows, D)


ANY = pl.BlockSpec(memory_space=pl.ANY)


def _place():
    x, y, c = lax.axis_index("x"), lax.axis_index("y"), lax.axis_index("c")
    chips = [(1 - x, y), (x, 1 - y), (1 - x, 1 - y)]
    return x, y, c, chips


def _rcopy(src, dst, ssem, rsem, to):
    return pltpu.make_async_remote_copy(src_ref=src, dst_ref=dst, send_sem=ssem, recv_sem=rsem,
                                        device_id=to, device_id_type=MESH)


def _halves(ref_or_rows, c):
    half = ref_or_rows // 2
    return pl.ds(c * half, half), pl.ds((1 - c) * half, half)


FIRST_SHARES, BRANCH_SHARES, FFN_SHARES = [0, 6], [1, 2, 3], [4, 5]
MIXER_SHARES = FIRST_SHARES + BRANCH_SHARES


def _gather_hook(own, idx):
    n = len(idx)

    def copies(kind, ins, outs, ssem, rsem):
        x, y, c, chips = _place()
        q = 2 * x + y
        sib = (x, y, 1 - c)
        out = []
        for i, a in enumerate(idx):
            mine, other = _halves(SHARES[a][1], c)
            for j, (cx, cy) in enumerate(chips):
                k, to = 3 * i + j, (cx, cy, c)
                landed, theirs = outs[i].at[2 * cx + cy, mine], outs[i].at[2 * cx + cy, other]
                if kind == "send":
                    out.append(_rcopy(ins[i].at[mine], outs[i].at[q, mine], ssem.at[k], rsem.at[k], to))
                elif kind == "landing":
                    out.append(_rcopy(ins[i].at[mine], landed, ssem.at[k], rsem.at[k], to))
                elif kind == "pass":
                    out.append(_rcopy(landed, landed, ssem.at[3 * n + k], rsem.at[3 * n + k], sib))
                else:
                    out.append(_rcopy(theirs, theirs, ssem.at[3 * n + k], rsem.at[3 * n + k], sib))
        return out

    def start(*refs):
        for cp in copies("send", *refs):
            cp.start()

    def mid(*refs):
        for landed, cp in zip(copies("landing", *refs), copies("pass", *refs)):
            landed.wait_recv()
            cp.start()

    def finish(*refs):
        for cp in copies("arrival", *refs):
            cp.wait_recv()
        for cp in copies("send", *refs) + copies("pass", *refs):
            cp.wait_send()

    shapes = [_sds((N_CHIPS,) + SHARES[a][1:3], SHARES[a][3]) for a in idx]
    return _Hook(own, shapes, 6 * n, start, finish, mid)


def _chip_hook(sums, idx):
    def copies(ins, outs, ssem, rsem):
        x, y, c, chips = _place()
        return [_rcopy(ins[i].at[2 * cx + cy], outs[i].at[j], ssem.at[3 * i + j], rsem.at[3 * i + j], (cx, cy, c))
                for i in range(len(idx)) for j, (cx, cy) in enumerate(chips)]

    def start(*refs):
        for cp in copies(*refs):
            cp.start()

    def finish(*refs):
        for cp in copies(*refs):
            cp.wait()

    shapes = [_sds((N_CHIPS - 1, SHARES[a][1] // 2, SHARES[a][2]), SHARES[a][3]) for a in idx]
    return _Hook(sums, shapes, 3 * len(idx), start, finish)


def _sibling_swap(parts, idx, tag):
    def copies(ins, outs, ssem, rsem):
        x, y, c, _ = _place()
        return [_rcopy(ins[i].at[:, _halves(SHARES[a][1], c)[1]], outs[i], ssem.at[i], rsem.at[i], (x, y, 1 - c))
                for i, a in enumerate(idx)]

    def start(*refs):
        for cp in copies(*refs):
            cp.start()

    def finish(*refs):
        for cp in copies(*refs):
            cp.wait()

    shapes = [_sds((N_CHIPS, SHARES[a][1] // 2, SHARES[a][2]), SHARES[a][3]) for a in idx]
    return _alone(_Hook(parts, shapes, len(idx), start, finish), "sibling_swap_" + tag)


def _sum_pair(parts, recvs, c, idx, tag):
    steps, n = 2, len(idx)

    def body(c_ref, *refs):
        for i, a in enumerate(idx):
            refs[2 * n + i][...] = (refs[i][...].astype(F32) + refs[n + i][...].astype(F32)).astype(SHARES[a][3])

    own, got, out, views, shapes = [], [], [], [], []
    for p, a in zip(parts, idx):
        _, rows, cols, dt = SHARES[a]
        blk = rows // 2 // steps
        own.append(pl.BlockSpec((None, None, blk, cols), lambda q, i, c_ref: (q, c_ref[0], i, 0)))
        got.append(pl.BlockSpec((None, blk, cols), lambda q, i, c_ref: (q, i, 0)))
        out.append(pl.BlockSpec((None, blk, cols), lambda q, i, c_ref: (q, i, 0)))
        views.append(p.reshape(N_CHIPS, 2, rows // 2, cols))
        shapes.append(_sds((N_CHIPS, rows // 2, cols), dt))
    grid_spec = pltpu.PrefetchScalarGridSpec(num_scalar_prefetch=1, grid=(N_CHIPS, steps),
                                             in_specs=own + got, out_specs=out)
    return pl.pallas_call(body, name="sum_pair_" + tag, grid_spec=grid_spec, out_shape=shapes,
                          compiler_params=_params(("arbitrary", "arbitrary")))(c, *views, *recvs)


def _sum_chips(sums, recvs, qc, idx, tag):
    steps, n = 2, len(idx)

    def body(qc_ref, *refs):
        for i in range(n):
            acc = refs[i][...].astype(F32)
            for j in range(1, N_CHIPS):
                acc = acc + refs[j * n + i][...].astype(F32)
            refs[N_CHIPS * n + i][...] = acc

    own, got, out, shapes = [], [[], [], []], [], []
    for a in idx:
        _, rows, cols, _ = SHARES[a]
        blk = rows // 2 // steps
        own.append(pl.BlockSpec((None, blk, cols), lambda i, qc_ref: (qc_ref[0], i, 0)))
        for j in range(N_CHIPS - 1):
            got[j].append(pl.BlockSpec((None, blk, cols), lambda i, qc_ref, j=j: (j, i, 0)))
        out.append(pl.BlockSpec((None, blk, cols), lambda i, qc_ref: (qc_ref[1], i, 0)))
        shapes.append(_sds((2, rows // 2, cols), F32))
    grid_spec = pltpu.PrefetchScalarGridSpec(num_scalar_prefetch=1, grid=(steps,),
                                             in_specs=own + got[0] + got[1] + got[2], out_specs=out)
    return pl.pallas_call(body, name="sum_chips_" + tag, grid_spec=grid_spec, out_shape=shapes,
                          compiler_params=_params(("arbitrary",)))(qc, *sums, *recvs, *recvs, *recvs)


def _sibling_share(halves, idx, tag):
    n = len(idx)

    def body(*refs):
        outs, (ssem, rsem) = refs[n:2 * n], refs[2 * n:]
        x, y, c, _ = _place()
        copies = []
        for i in range(n):
            cp = _rcopy(outs[i].at[c], outs[i].at[c], ssem.at[i], rsem.at[i], (x, y, 1 - c))
            cp.start()
            copies.append(cp)
        for i in range(n):
            theirs = outs[i].at[1 - c]
            _rcopy(theirs, theirs, ssem.at[i], rsem.at[i], (x, y, 1 - c)).wait_recv()
        for cp in copies:
            cp.wait_send()

    return pl.pallas_call(
        body, name="sibling_share_" + tag, in_specs=[ANY] * n, out_specs=[ANY] * n,
        out_shape=[_sds((2, SHARES[a][1] // 2, SHARES[a][2]), F32) for a in idx],
        input_output_aliases={i: i for i in range(n)},
        scratch_shapes=[pltpu.SemaphoreType.DMA((n,)), pltpu.SemaphoreType.DMA((n,))],
    )(*halves)


class _Dist:
    def __init__(self, own):
        self.own = own
        self.core = lax.axis_index("c")
        self.chip = 2 * lax.axis_index("x") + lax.axis_index("y")
        self.reduced = {}

    def gather_hook(self, idx):
        return _gather_hook([self.own[a] for a in idx], idx)

    def weights(self, idx, gathered):
        out = {}
        for a, got in zip(idx, gathered):
            name = SHARES[a][0]
            full = lax.dynamic_update_slice(got, self.own[a][None], (self.chip, 0, 0))
            if name == "w_in":
                out[name] = _cols_joined(full)
            elif name == "w_up":
                out[name] = full
            elif name == "tiny":
                out.update({n: _cols_joined(t) for n, t in _tiny_unpack(full).items()})
            else:
                out[name] = full.reshape(-1, D)
        return out

    def pair_sums(self, idx, parts, tag):
        return _sum_pair(parts, _sibling_swap(parts, idx, tag), self.core.reshape(1), idx, tag)

    def chip_hook(self, idx, sums):
        return _chip_hook(sums, idx)

    def finish(self, idx, sums, recvs, tag):
        halves = _sum_chips(sums, recvs, jnp.stack([self.chip, self.core]), idx, tag)
        for a, full in zip(idx, _sibling_share(halves, idx, tag)):
            self.reduced[SHARES[a][0]] = full.reshape(SHARES[a][1:3])


N_DEV = 8
SMALL_ROWS = 24


def _small_allreduce(sm):
    def body(s_ref, o_ref, buf, ssem, rsem):
        x, y, c, _ = _place()
        me = 4 * x + 2 * y + c
        buf[me] = s_ref[...]
        copies = []
        for d in range(1, N_DEV):
            dx, dy, dc = d >> 2, (d >> 1) & 1, d & 1
            to = (x ^ dx, y ^ dy, c ^ dc)
            cp = _rcopy(s_ref, buf.at[me], ssem.at[d - 1], rsem.at[d - 1], to)
            cp.start()
            copies.append(cp)
        for d in range(1, N_DEV):
            src = me ^ d
            _rcopy(s_ref, buf.at[src], ssem.at[d - 1], rsem.at[d - 1], (x, y, c)).wait_recv()
        for cp in copies:
            cp.wait_send()
        acc = buf[0]
        for k in range(1, N_DEV):
            acc = acc + buf[k]
        o_ref[...] = acc

    vm = pl.BlockSpec(memory_space=pltpu.VMEM)
    return pl.pallas_call(
        body, name="small_allreduce", in_specs=[vm], out_specs=vm,
        out_shape=_sds((SMALL_ROWS, D), F32),
        scratch_shapes=[pltpu.VMEM((N_DEV, SMALL_ROWS, D), F32),
                        pltpu.SemaphoreType.DMA((N_DEV - 1,)), pltpu.SemaphoreType.DMA((N_DEV - 1,))],
    )(sm)


SMALL_PLAN = [("norm_pre_mix", D), ("norm_post_mix", D), ("b_in", IN_W), ("attn_sinks", NH), ("conv_dw_b", D),
              ("conv_ln_g", D), ("conv_ln_b", D), ("b_conv_proj", D), ("norm_pre_ffn", D), ("norm_post_ffn", D),
              ("ffn_dw_b", 2 * FFN), ("loss", D)]


def _pack_small(parts):
    rows = [_to_planes(parts[name].reshape(-1), -(-n // D)) for name, n in SMALL_PLAN]
    used = sum(r.shape[0] for r in rows)
    return jnp.concatenate(rows + [jnp.zeros((SMALL_ROWS - used, D), F32)], 0)


def _unpack_small(packed):
    out, r0 = {}, 0
    for name, n in SMALL_PLAN:
        rows = -(-n // D)
        out[name] = packed[r0:r0 + rows].reshape(-1)[:n].reshape(1, n)
        r0 += rows
    return out


def _adamw(w, g, m, v, name):
    rows, cols = w.shape
    tr = _tile(rows, 256, 8) if rows % 8 == 0 else rows

    def body(w_ref, g_ref, m_ref, v_ref, d_ref, nm_ref, nv_ref):
        g = g_ref[...]
        m = B1 * m_ref[...] + (1.0 - B1) * g
        v = B2 * v_ref[...] + (1.0 - B2) * (g * g)
        nm_ref[...] = m
        nv_ref[...] = v
        m_hat = m / (1.0 - B1 ** STEP)
        v_hat = v / (1.0 - B2 ** STEP)
        d_ref[...] = -LR * (m_hat / (jnp.sqrt(v_hat) + ADAM_EPS) + WD * w_ref[...])

    spec = pl.BlockSpec((tr, cols), lambda i: (i, 0))
    return pl.pallas_call(
        body, name=name, grid=(rows // tr,), in_specs=[spec] * 4, out_specs=[spec] * 3,
        out_shape=[_sds((rows, cols), F32)] * 3, compiler_params=_params(("arbitrary",)),
    )(w, g, m, v)


NAMES = ["meta_tokens", "norm_pre_mix", "norm_post_mix", "w_in", "b_in", "attn_sinks", "w_attn_proj", "conv_dw_w",
         "conv_dw_b", "conv_ln_g", "conv_ln_b", "w_conv_proj", "b_conv_proj", "w_out", "norm_pre_ffn", "norm_post_ffn",
         "w_up", "ffn_dw_w", "ffn_dw_b", "w_down"]
MATMUL = ("w_in", "w_attn_proj", "w_conv_proj", "w_out", "w_up", "w_down")


def _two_d(a):
    return a.reshape(a.shape[-2:])


def kernel(x, meta_tokens, norm_pre_mix, norm_post_mix, w_in, b_in, attn_sinks, w_attn_proj, conv_dw_w, conv_dw_b, conv_ln_g, conv_ln_b, w_conv_proj, b_conv_proj, w_out, norm_pre_ffn, norm_post_ffn, w_up, ffn_dw_w, ffn_dw_b, w_down, loss_target, m_meta_tokens, m_norm_pre_mix, m_norm_post_mix, m_w_in, m_b_in, m_attn_sinks, m_w_attn_proj, m_conv_dw_w, m_conv_dw_b, m_conv_ln_g, m_conv_ln_b, m_w_conv_proj, m_b_conv_proj, m_w_out, m_norm_pre_ffn, m_norm_post_ffn, m_w_up, m_ffn_dw_w, m_ffn_dw_b, m_w_down, v_meta_tokens, v_norm_pre_mix, v_norm_post_mix, v_w_in, v_b_in, v_attn_sinks, v_w_attn_proj, v_conv_dw_w, v_conv_dw_b, v_conv_ln_g, v_conv_ln_b, v_w_conv_proj, v_b_conv_proj, v_w_out, v_norm_pre_ffn, v_norm_post_ffn, v_w_up, v_ffn_dw_w, v_ffn_dw_b, v_w_down):
    args = locals()
    w = {n: args[n] for n in NAMES}
    m = {n: args["m_" + n] for n in NAMES}
    v = {n: args["v_" + n] for n in NAMES}
    tiny_names = [name for name, _, _, _ in TINY_PARTS]
    big = list(MATMUL) + tiny_names

    own = {n: _two_d(w[n]).astype(BF16) for n in MATMUL}
    own["tiny"] = _tiny_pack({n: _two_d(w[n]) for n in tiny_names})
    dist = _Dist([own[n] for n, _, _, _ in SHARES])
    W = {n: _two_d(w[n]) for n in SMALL}
    W.update(dist.weights(FIRST_SHARES, _alone(dist.gather_hook(FIRST_SHARES), "gather_first")))

    loss_cols, grad_x, grads = local_step(x[0], loss_target[0], W, dist)

    part = {n: grads[n].reshape(N_CHIPS, -1, D) for n in ("w_attn_proj", "w_conv_proj", "w_out")}
    part["w_in"] = _cols_by_chip(grads["w_in"])
    part["tiny"] = _tiny_pack({n: _cols_by_chip(grads[n]) for n in tiny_names})
    sums = dist.pair_sums(MIXER_SHARES, [part[SHARES[a][0]] for a in MIXER_SHARES], "mixer")
    dist.finish(MIXER_SHARES, sums, _alone(dist.chip_hook(MIXER_SHARES, sums), "chip_swap_mixer"), "mixer")
    g_big = {n: dist.reduced[n] for n in MATMUL}
    g_big.update(_tiny_unpack(dist.reduced["tiny"]))
    small = dict(grads)
    small["loss"] = loss_cols
    g_small = _unpack_small(_small_allreduce(_pack_small(small)))
    loss = jnp.sum(g_small["loss"])

    g, delta, new_m, new_v = {}, {}, {}, {}
    for n in big:
        shape = w[n].shape
        g[n] = g_big[n].reshape(shape)
        d, nm, nv = _adamw(_two_d(w[n]), _two_d(g[n]), _two_d(m[n]), _two_d(v[n]), "adamw_" + n)
        delta[n], new_m[n], new_v[n] = d.reshape(shape), nm.reshape(shape), nv.reshape(shape)
    no_loss = jnp.zeros((1, D), F32)
    packs = [_pack_small(dict({n: d[n] for n in SMALL}, loss=no_loss)) for d in (w, dict(g_small), m, v)]
    ud, um, uv = (_unpack_small(a) for a in _adamw(*packs, "adamw_small"))
    for n in SMALL:
        g[n], delta[n], new_m[n], new_v[n] = g_small[n], ud[n], um[n], uv[n]

    return (loss, grad_x[None], *[g[n] for n in NAMES], *[delta[n] for n in NAMES],
            *[new_m[n] for n in NAMES], *[new_v[n] for n in NAMES])
```

```python
import jax
import jax.numpy as jnp
from jax import lax
from jax.experimental import pallas as pl
from jax.experimental.pallas import tpu as pltpu

F32, BF16 = jnp.float32, jnp.bfloat16

D = 1024
NH, NKV, HD = 16, 2, 64
NMETA, BLK = 16, 128
PAD = BLK - NMETA
ROT = HD // 4
THETA = 500000.0
CONV_K = 31
FFN = 2816
FFN_K = 3
IN_W = 5376
QKV_W, GLU_W, GATE_W = 1280, 2048, 2048
RMS_EPS, LN_EPS, NEG = 1e-6, 1e-5, -1e30
LR, B1, B2, ADAM_EPS, WD, STEP = 0.001, 0.9, 0.999, 1e-08, 0.01, 10

VMEM_LIMIT = 56 * 2 ** 20
MESH = pl.DeviceIdType.MESH

NT_DIMS = (((1,), (1,)), ((), ()))
TN_DIMS = (((0,), (0,)), ((), ()))


def _params(sem, **kw):
    return pltpu.CompilerParams(dimension_semantics=sem, vmem_limit_bytes=VMEM_LIMIT, **kw)


def _tile(n, pref, mult=16):
    for t in range(min(pref, n), 0, -1):
        if n % t == 0 and t % mult == 0:
            return t
    return n


def _row(tr, w, col=0):
    return pl.BlockSpec((tr, w), lambda i: (i, col))


def _rrow(tr, w, nt, col=0):
    return pl.BlockSpec((tr, w), lambda t: (nt - 1 - t, col))


def _const(shape):
    return pl.BlockSpec(shape, lambda *_: (0,) * len(shape))


def _sds(shape, dt):
    return jax.ShapeDtypeStruct(shape, dt)


class _Hook:
    def __init__(self, operands, out_shape, n_sem, start, finish, mid=None):
        self.operands, self.out_shape, self.n_sem = list(operands), list(out_shape), n_sem
        self.start, self.mid, self.finish = start, mid, finish

    def scratch(self):
        return [pltpu.SemaphoreType.DMA((self.n_sem,)), pltpu.SemaphoreType.DMA((self.n_sem,))]


def _call(body, hook, steps, *, name, grid, in_specs, out_specs, out_shape, operands, semantics, scratch_shapes=()):
    in_specs, out_specs, out_shape = list(in_specs), list(out_specs), list(out_shape)
    if hook is None:
        return pl.pallas_call(body, name=name, grid=grid, in_specs=in_specs, out_specs=out_specs, out_shape=out_shape,
                              scratch_shapes=list(scratch_shapes), compiler_params=_params(semantics))(*operands)
    n_in, n_out, n_hi, n_ho = len(in_specs), len(out_specs), len(hook.operands), len(hook.out_shape)

    def wrapped(*refs):
        ins, hi = refs[:n_in], refs[n_in:n_in + n_hi]
        o0 = n_in + n_hi
        outs, ho = refs[o0:o0 + n_out], refs[o0 + n_out:o0 + n_out + n_ho]
        scratch, (ssem, rsem) = refs[o0 + n_out + n_ho:len(refs) - 2], refs[len(refs) - 2:]
        first, middle, last = steps()

        @pl.when(first)
        def _():
            hook.start(hi, ho, ssem, rsem)

        body(*ins, *outs, *scratch)
        if hook.mid is not None:
            @pl.when(middle)
            def _():
                hook.mid(hi, ho, ssem, rsem)

        @pl.when(last)
        def _():
            hook.finish(hi, ho, ssem, rsem)

    any_spec = pl.BlockSpec(memory_space=pl.ANY)
    return pl.pallas_call(
        wrapped, name=name, grid=grid, in_specs=in_specs + [any_spec] * n_hi, out_specs=out_specs + [any_spec] * n_ho,
        out_shape=out_shape + hook.out_shape, scratch_shapes=list(scratch_shapes) + hook.scratch(),
        compiler_params=_params(semantics))(*operands, *hook.operands)


def _alone(hook, name):
    n_hi = len(hook.operands)

    def body(*refs):
        hi, ho, (ssem, rsem) = refs[:n_hi], refs[n_hi:len(refs) - 2], refs[len(refs) - 2:]
        hook.start(hi, ho, ssem, rsem)
        if hook.mid is not None:
            hook.mid(hi, ho, ssem, rsem)
        hook.finish(hi, ho, ssem, rsem)

    any_spec = pl.BlockSpec(memory_space=pl.ANY)
    return pl.pallas_call(body, name=name, in_specs=[any_spec] * n_hi, out_specs=[any_spec] * len(hook.out_shape),
                          out_shape=hook.out_shape, scratch_shapes=hook.scratch())(*hook.operands)


def _steps_1d(n, mid):
    def steps():
        i = pl.program_id(0)
        return i == 0, i == min(mid, n - 1), i == n - 1
    return steps


def _rms(x, g):
    r = lax.rsqrt(jnp.mean(x * x, -1, keepdims=True) + RMS_EPS)
    return x * r * g, r


def _rms_bwd(dy, x, r, g):
    gy = dy * g
    return r * gy - x * (r * r * r) * jnp.mean(x * gy, -1, keepdims=True)


def _colsum(x):
    return jnp.sum(x, axis=0, keepdims=True)


def _rope(x, c, sa, sb):
    n = x.shape[1]
    return x * c + pltpu.roll(x, n - 8, 1) * sa + pltpu.roll(x, 8, 1) * sb


def _rope_bwd(d, c, sa, sb):
    n = d.shape[1]
    return d * c + pltpu.roll(d * sa, 8, 1) + pltpu.roll(d * sb, n - 8, 1)


def _rope_tables(R):
    half = ROT // 2
    lane = jnp.arange(2 * HD) % HD
    inv = THETA ** (-(lane % half).astype(F32) * 2.0 / ROT)
    pos = (jnp.arange(R) - PAD).astype(F32)
    ang = pos[:, None] * inv[None, :]
    cos, sin = jnp.cos(ang), jnp.sin(ang)
    c = jnp.where(lane < ROT, cos, 1.0)
    sa = jnp.where(lane < half, -sin, 0.0)
    sb = jnp.where((lane >= half) & (lane < ROT), sin, 0.0)
    return c, sa, sb


IN_CHUNKS = ([(0, 512, True), (512, 1024, True), (1024, 1152, True), (1152, 1280, False)]
             + [(c, c + 512, False) for c in range(1280, IN_W, 512)])


def _in_proj(h0, g_pre, w_in, b_in, rope, tr, hook=None):
    R = h0.shape[0]
    nt = R // tr

    def body(h_ref, g_ref, w_ref, b_ref, c_ref, sa_ref, sb_ref, qkv_ref, glu_ref, gate_ref, n1_ref):
        n, _ = _rms(h_ref[...], g_ref[...])
        nb = n.astype(BF16)
        n1_ref[...] = nb
        for c0, c1, rot in IN_CHUNKS:
            acc = lax.dot_general(nb, w_ref[c0:c1, :], NT_DIMS, preferred_element_type=F32) + b_ref[:, c0:c1]
            if rot:
                reps = (c1 - c0) // 128
                acc = _rope(acc, jnp.tile(c_ref[...], (1, reps)), jnp.tile(sa_ref[...], (1, reps)),
                            jnp.tile(sb_ref[...], (1, reps)))
            val = acc.astype(BF16)
            if c1 <= QKV_W:
                qkv_ref[:, c0:c1] = val
            elif c1 <= QKV_W + GLU_W:
                glu_ref[:, c0 - QKV_W:c1 - QKV_W] = val
            else:
                gate_ref[:, c0 - QKV_W - GLU_W:c1 - QKV_W - GLU_W] = val

    return _call(
        body, hook, _steps_1d(nt, (3 * nt) // 4), name="in_proj", grid=(nt,),
        in_specs=[_row(tr, D), _const((1, D)), _const((IN_W, D)), _const((1, IN_W)),
                  _row(tr, 128), _row(tr, 128), _row(tr, 128)],
        out_specs=[_row(tr, QKV_W), _row(tr, GLU_W), _row(tr, GATE_W), _row(tr, D)],
        out_shape=[_sds((R, QKV_W), BF16), _sds((R, GLU_W), BF16), _sds((R, GATE_W), BF16), _sds((R, D), BF16)],
        operands=(h0, g_pre, w_in, b_in, *rope), semantics=("arbitrary",))


def _attn_mask(n):
    qi = lax.broadcasted_iota(jnp.int32, (BLK, 3 * BLK), 0)
    kj = lax.broadcasted_iota(jnp.int32, (BLK, 3 * BLK), 1)
    tq = n * BLK + qi - PAD
    t_meta = kj - PAD
    t_loc = (n - 1) * BLK + (kj - BLK) - PAD
    meta_ok = (kj < BLK) & (t_meta >= 0) & (t_meta <= tq)
    loc_ok = (kj >= BLK) & (t_loc >= NMETA) & (t_loc <= tq) & (tq - t_loc < BLK)
    return meta_ok | loc_ok


def _dup_heads(ref0, refp, refc, low):
    a = jnp.concatenate([ref0[...], refp[...], refc[...]], 0).astype(F32)
    sw = pltpu.roll(a, HD, 1)
    return [jnp.where(low, a, sw).astype(BF16), jnp.where(low, sw, a).astype(BF16)]


def _kv_specs(nb, rev):
    def blk(col, which):
        def idx(t):
            n = nb - 1 - t if rev else t
            return ({"meta": 0, "prev": jnp.maximum(n - 1, 0), "own": n}[which], col)
        return pl.BlockSpec((BLK, BLK), idx)
    return [blk(col, w) for col in (8, 9) for w in ("meta", "prev", "own")]


def _attn_fwd(qkv, sinks, hook=None):
    R = qkv.shape[0]
    nb = R // BLK

    def body(s_ref, q_ref, k0, kp, kc, v0, vp, vc, o_ref, lse_ref):
        n = pl.program_id(0)
        lane = lax.broadcasted_iota(jnp.int32, (1, BLK), 1)
        low = lane < HD
        kd, vd = _dup_heads(k0, kp, kc, low), _dup_heads(v0, vp, vc, low)
        mask = _attn_mask(n)
        lse = jnp.zeros((BLK, BLK), F32)
        zero = jnp.zeros((), BF16)
        for pair in range(NH // 2):
            qp = q_ref[:, pair * BLK:(pair + 1) * BLK]
            kv = pair // (NH // NKV // 2)
            outs = []
            for e in range(2):
                h = 2 * pair + e
                qm = jnp.where(low if e == 0 else ~low, qp, zero)
                s = lax.dot_general(qm, kd[kv], NT_DIMS, preferred_element_type=F32) * (HD ** -0.5)
                s = jnp.where(mask, s, NEG)
                sk = s_ref[h]
                m = jnp.maximum(jnp.max(s, -1, keepdims=True), sk)
                p = jnp.exp(s - m)
                l = jnp.sum(p, -1, keepdims=True) + jnp.exp(sk - m)
                outs.append(jnp.dot(p.astype(BF16), vd[kv], preferred_element_type=F32) / l)
                lse = jnp.where(lane == h, m + jnp.log(l), lse)
            o_ref[:, pair * BLK:(pair + 1) * BLK] = jnp.where(low, outs[0], outs[1]).astype(BF16)
        lse_ref[...] = lse

    return _call(
        body, hook, _steps_1d(nb, (3 * nb) // 4), name="attn_fwd", grid=(nb,),
        in_specs=[pl.BlockSpec(memory_space=pltpu.SMEM), pl.BlockSpec((BLK, D), lambda n: (n, 0))] + _kv_specs(nb, False),
        out_specs=[_row(BLK, D), _row(BLK, BLK)],
        out_shape=[_sds((R, D), BF16), _sds((R, BLK), F32)],
        operands=(sinks, qkv, *([qkv] * 6)), semantics=("arbitrary",))


CONV_TCH, CONV_SUB, HALO = 256, 64, 32


def _conv_fwd(glu, w, b, tr):
    R = glu.shape[0]
    nc = D // CONV_TCH

    def body(a_ref, g_ref, w_ref, b_ref, o_ref, buf):
        i = pl.program_id(1)

        @pl.when(i == 0)
        def _():
            buf[0:HALO, :] = jnp.zeros((HALO, CONV_TCH), F32)

        @pl.when(i > 0)
        def _():
            buf[0:HALO, :] = buf[tr:tr + HALO, :]

        row = i * tr + lax.broadcasted_iota(jnp.int32, (tr, 1), 0)
        a, g = a_ref[...].astype(F32), g_ref[...].astype(F32)
        buf[HALO:HALO + tr, :] = jnp.where(row >= PAD, a * jax.nn.sigmoid(g), 0.0)
        for r0 in range(0, tr, CONV_SUB):
            acc = jnp.broadcast_to(b_ref[...], (CONV_SUB, CONV_TCH))
            for k in range(CONV_K):
                acc = acc + w_ref[k:k + 1, :] * buf[pl.ds(r0 + HALO - (CONV_K - 1) + k, CONV_SUB), :]
            o_ref[r0:r0 + CONV_SUB, :] = acc.astype(BF16)

    return pl.pallas_call(
        body, name="conv_fwd", grid=(nc, R // tr),
        in_specs=[pl.BlockSpec((tr, CONV_TCH), lambda c, i: (i, c)),
                  pl.BlockSpec((tr, CONV_TCH), lambda c, i: (i, nc + c)),
                  pl.BlockSpec((None, HALO, CONV_TCH), lambda c, i: (c, 0, 0)),
                  pl.BlockSpec((1, CONV_TCH), lambda c, i: (0, c))],
        out_specs=pl.BlockSpec((tr, CONV_TCH), lambda c, i: (i, c)),
        out_shape=_sds((R, D), BF16),
        scratch_shapes=[pltpu.VMEM((tr + HALO, CONV_TCH), F32)],
        compiler_params=_params(("arbitrary", "arbitrary")),
    )(glu, glu, w, b)


def _ln_silu(c1, lg, lb):
    mu = jnp.mean(c1, -1, keepdims=True)
    xc = c1 - mu
    rs = lax.rsqrt(jnp.mean(xc * xc, -1, keepdims=True) + LN_EPS)
    xh = xc * rs
    c2 = xh * lg + lb
    sg = jax.nn.sigmoid(c2)
    return xh, rs, c2, sg


def _mix_out(attn, c1, gates, h0, w_ap, w_cp, w_out, lg, lb, b_cp, g_post, g_ffn, tr):
    R = attn.shape[0]

    def body(at_ref, c1_ref, ga_ref, gc_ref, h0_ref, wap, wcp, wo, lg_ref, lb_ref, bcp, gp, gf,
             ao_ref, co_ref, c3_ref, mg_ref, mix_ref, h1_ref, n2_ref):
        ao = jnp.dot(at_ref[...], wap[...], preferred_element_type=F32)
        _, _, c2, sg = _ln_silu(c1_ref[...].astype(F32), lg_ref[...], lb_ref[...])
        c3 = (c2 * sg).astype(BF16)
        c3_ref[...] = c3
        co = jnp.dot(c3, wcp[...], preferred_element_type=F32) + bcp[...]
        ao_b, co_b = ao.astype(BF16), co.astype(BF16)
        ao_ref[...] = ao_b
        co_ref[...] = co_b
        merged = (jax.nn.sigmoid(ga_ref[...].astype(F32)) * ao_b.astype(F32)
                  + jax.nn.sigmoid(gc_ref[...].astype(F32)) * co_b.astype(F32)).astype(BF16)
        mg_ref[...] = merged
        mix = jnp.dot(merged, wo[...], preferred_element_type=F32).astype(BF16)
        mix_ref[...] = mix
        y, _ = _rms(mix.astype(F32), gp[...])
        h1 = h0_ref[...] + y
        h1_ref[...] = h1
        n2, _ = _rms(h1, gf[...])
        row = pl.program_id(0) * tr + lax.broadcasted_iota(jnp.int32, (tr, 1), 0)
        n2_ref[...] = jnp.where(row >= PAD, n2, 0.0).astype(BF16)

    vec = _const((1, D))
    return pl.pallas_call(
        body, name="mix_out", grid=(R // tr,),
        in_specs=[_row(tr, D), _row(tr, D), _row(tr, D, 0), _row(tr, D, 1), _row(tr, D),
                  _const((D, D)), _const((D, D)), _const((D, D)), vec, vec, vec, vec, vec],
        out_specs=[_row(tr, D)] * 7,
        out_shape=[_sds((R, D), BF16)] * 5 + [_sds((R, D), F32), _sds((R, D), BF16)],
        compiler_params=_params(("arbitrary",)),
    )(attn, c1, gates, gates, h0, w_ap, w_cp, w_out, lg, lb, b_cp, g_post, g_ffn)


FFN_CH = 256
N_CHIPS = 4
UPQ = 2 * FFN // N_CHIPS
UP_CHUNKS = [(q, c0, min(c0 + 512, UPQ)) for q in range(N_CHIPS // 2) for c0 in range(0, UPQ, 512)]
TINY_ROWS, TINY_CONV, TINY_FFN, TINY_META = 64, 0, 32, 40


def _shift_down(x, k, halo):
    tr = x.shape[0]
    row = lax.broadcasted_iota(jnp.int32, (tr, 1), 0)
    y = pltpu.roll(x, k, 0)
    for j in range(k):
        y = jnp.where(row == j, halo[8 - k + j:8 - k + j + 1, :], y)
    return y


def _shift_up(x, k, halo):
    tr = x.shape[0]
    row = lax.broadcasted_iota(jnp.int32, (tr, 1), 0)
    y = pltpu.roll(x, tr - k, 0)
    for j in range(k):
        y = jnp.where(row == tr - k + j, halo[j:j + 1, :], y)
    return y


def _conv3(x, halo, w, b):
    return w[2:3, :] * x + w[1:2, :] * _shift_down(x, 1, halo) + w[0:1, :] * _shift_down(x, 2, halo) + b


def _ffn_up(n2, w_up, fw, fb, tr):
    R = n2.shape[0]

    def body(n_ref, w_ref, fw_ref, fb_ref, up_ref, act_ref, carry):
        @pl.when(pl.program_id(0) == 0)
        def _():
            carry[...] = jnp.zeros_like(carry)

        nb = n_ref[...]
        for q, c0, c1 in UP_CHUNKS:
            us = []
            for qq in (q, q + N_CHIPS // 2):
                cs = slice(qq * UPQ + c0, qq * UPQ + c1)
                x = jnp.dot(nb, w_ref[qq, :, c0:c1], preferred_element_type=F32).astype(BF16)
                up_ref[:, cs] = x
                x = x.astype(F32)
                us.append(_conv3(x, carry[:, cs], fw_ref[qq, TINY_FFN:TINY_FFN + 8, c0:c1], fb_ref[:, cs]))
                carry[:, cs] = x[tr - 8:tr, :]
            act_ref[:, q * UPQ + c0:q * UPQ + c1] = (us[0] * jax.nn.sigmoid(us[0]) * us[1]).astype(BF16)

    return pl.pallas_call(
        body, name="ffn_up", grid=(R // tr,),
        in_specs=[_row(tr, D), _const((N_CHIPS, D, UPQ)), _const((N_CHIPS, TINY_ROWS, UPQ)), _const((1, 2 * FFN))],
        out_specs=[_row(tr, 2 * FFN), _row(tr, FFN)],
        out_shape=[_sds((R, 2 * FFN), BF16), _sds((R, FFN), BF16)],
        scratch_shapes=[pltpu.VMEM((8, 2 * FFN), F32)],
        compiler_params=_params(("arbitrary",)),
    )(n2, w_up, fw, fb)


def _ffn_down(act, w_down, h1, tgt, g_post, tr):
    R = act.shape[0]
    m = tr // BLK

    def body(a_ref, w_ref, h1_ref, g_ref, *rest):
        t_refs, (dh2_ref, dffn_ref, loss_ref, dg_ref) = rest[:m], rest[m:]

        @pl.when(pl.program_id(0) == 0)
        def _():
            loss_ref[...] = jnp.zeros_like(loss_ref)
            dg_ref[...] = jnp.zeros_like(dg_ref)

        f = jnp.dot(a_ref[...], w_ref[...], preferred_element_type=F32)
        g = g_ref[...]
        y, r = _rms(f, g)
        row = pl.program_id(0) * tr + lax.broadcasted_iota(jnp.int32, (tr, 1), 0)
        tgt_rows = jnp.concatenate([t[...] for t in t_refs], 0)
        e = jnp.where(row >= BLK, h1_ref[...] + y - tgt_rows, 0.0)
        loss_ref[...] += _colsum(e * e) * (0.5 / D)
        dy = e * (1.0 / D)
        dh2_ref[...] = dy
        dffn_ref[...] = _rms_bwd(dy, f, r, g).astype(BF16)
        dg_ref[...] += _colsum(dy * f * r)

    return pl.pallas_call(
        body, name="ffn_down", grid=(R // tr,),
        in_specs=[_row(tr, FFN), _const((FFN, D)), _row(tr, D), _const((1, D))]
                 + [pl.BlockSpec((BLK, D), lambda i, k=k: (jnp.maximum(m * i - 1 + k, 0), 0)) for k in range(m)],
        out_specs=[_row(tr, D), _row(tr, D), _const((1, D)), _const((1, D))],
        out_shape=[_sds((R, D), F32), _sds((R, D), BF16), _sds((1, D), F32), _sds((1, D), F32)],
        compiler_params=_params(("arbitrary",)),
    )(act, w_down, h1, g_post, *([tgt] * m))


def _ffn_bwd_act(dffn, w_down, up, fw, fb, tr):
    R = dffn.shape[0]
    nt = R // tr

    def body(d_ref, w_ref, up_ref, hal_ref, fw_ref, fb_ref, dup_ref, dfw_ref, dfb_ref, carry):
        t = pl.program_id(0)
        i = nt - 1 - t

        @pl.when(t == 0)
        def _():
            carry[...] = jnp.zeros_like(carry)
            dfw_ref[...] = jnp.zeros_like(dfw_ref)
            dfb_ref[...] = jnp.zeros_like(dfb_ref)

        dff = d_ref[...]
        row = i * tr + lax.broadcasted_iota(jnp.int32, (tr, 1), 0)
        first = i == 0
        for q, c0, c1 in UP_CHUNKS:
            dact = lax.dot_general(dff, w_ref[q * UPQ + c0:q * UPQ + c1, :], NT_DIMS, preferred_element_type=F32)
            chips = (q, q + N_CHIPS // 2)
            xs, us = [], []
            for qq in chips:
                cs = slice(qq * UPQ + c0, qq * UPQ + c1)
                x = up_ref[:, cs].astype(F32)
                halo = jnp.where(first, 0.0, hal_ref[:, cs].astype(F32))
                x1, x2 = _shift_down(x, 1, halo), _shift_down(x, 2, halo)
                w = fw_ref[qq, TINY_FFN:TINY_FFN + 8, c0:c1]
                us.append(w[2:3, :] * x + w[1:2, :] * x1 + w[0:1, :] * x2 + fb_ref[:, cs])
                xs.append((x, x1, x2))
            sg = jax.nn.sigmoid(us[0])
            silu = us[0] * sg
            dus = [dact * us[1] * sg * (1.0 + us[0] * (1.0 - sg)), dact * silu]
            for (x, x1, x2), du, qq in zip(xs, dus, chips):
                cs = slice(qq * UPQ + c0, qq * UPQ + c1)
                w = fw_ref[qq, TINY_FFN:TINY_FFN + 8, c0:c1]
                nxt = carry[:, cs]
                dx = w[2:3, :] * du + w[1:2, :] * _shift_up(du, 1, nxt) + w[0:1, :] * _shift_up(du, 2, nxt)
                dup_ref[:, cs] = jnp.where(row >= PAD, dx, 0.0).astype(BF16)
                dfw_ref[qq, 0:1, c0:c1] += _colsum(x2 * du)
                dfw_ref[qq, 1:2, c0:c1] += _colsum(x1 * du)
                dfw_ref[qq, 2:3, c0:c1] += _colsum(x * du)
                dfb_ref[:, cs] += _colsum(du)
                carry[:, cs] = du[0:8, :]

    halo_spec = pl.BlockSpec((8, 2 * FFN), lambda t: (jnp.maximum((nt - 1 - t) * (tr // 8) - 1, 0), 0))
    return pl.pallas_call(
        body, name="ffn_bwd_act", grid=(nt,),
        in_specs=[_rrow(tr, D, nt), _const((FFN, D)), _rrow(tr, 2 * FFN, nt), halo_spec,
                  _const((N_CHIPS, TINY_ROWS, UPQ)), _const((1, 2 * FFN))],
        out_specs=[_rrow(tr, 2 * FFN, nt), _const((N_CHIPS, 8, UPQ)), _const((1, 2 * FFN))],
        out_shape=[_sds((R, 2 * FFN), BF16), _sds((N_CHIPS, 8, UPQ), F32), _sds((1, 2 * FFN), F32)],
        scratch_shapes=[pltpu.VMEM((8, 2 * FFN), F32)],
        compiler_params=_params(("arbitrary",)),
    )(dffn, w_down, up, up, fw, fb)


def _ffn_bwd_in(dup, w_up, h1, dh2, mix, g_ffn, g_post, tr):
    R = dup.shape[0]

    def body(d_ref, w_ref, h1_ref, dh2_ref, mix_ref, gf_ref, gp_ref, dh1_ref, dmix_ref, dgf_ref, dgp_ref):
        @pl.when(pl.program_id(0) == 0)
        def _():
            dgf_ref[...] = jnp.zeros_like(dgf_ref)
            dgp_ref[...] = jnp.zeros_like(dgp_ref)

        dn2 = sum(lax.dot_general(d_ref[:, q * UPQ:(q + 1) * UPQ], w_ref[q], NT_DIMS, preferred_element_type=F32)
                  for q in range(N_CHIPS))
        h1 = h1_ref[...]
        _, r2 = _rms(h1, gf_ref[...])
        dh1 = dh2_ref[...] + _rms_bwd(dn2, h1, r2, gf_ref[...])
        dgf_ref[...] += _colsum(dn2 * h1 * r2)
        dh1_ref[...] = dh1
        m = mix_ref[...].astype(F32)
        _, rm = _rms(m, gp_ref[...])
        dmix_ref[...] = _rms_bwd(dh1, m, rm, gp_ref[...]).astype(BF16)
        dgp_ref[...] += _colsum(dh1 * m * rm)

    vec = _const((1, D))
    return pl.pallas_call(
        body, name="ffn_bwd_in", grid=(R // tr,),
        in_specs=[_row(tr, 2 * FFN), _const((N_CHIPS, D, UPQ)), _row(tr, D), _row(tr, D), _row(tr, D), vec, vec],
        out_specs=[_row(tr, D), _row(tr, D), vec, vec],
        out_shape=[_sds((R, D), F32), _sds((R, D), BF16), _sds((1, D), F32), _sds((1, D), F32)],
        compiler_params=_params(("arbitrary",)),
    )(dup, w_up, h1, dh2, mix, g_ffn, g_post)


def _mix_bwd(dmix, ao, co, gates, c1, w_out, w_ap, w_cp, lg, lb, tr):
    R = dmix.shape[0]

    def body(dm_ref, ao_ref, co_ref, ga_ref, gc_ref, c1_ref, wo, wap, wcp, lg_ref, lb_ref,
             dao_ref, dco_ref, dgate_ref, dattn_ref, dc1_ref, dbcp_ref, dlg_ref, dlb_ref, dcb_ref):
        @pl.when(pl.program_id(0) == 0)
        def _():
            for ref in (dbcp_ref, dlg_ref, dlb_ref, dcb_ref):
                ref[...] = jnp.zeros_like(ref)

        dmg = lax.dot_general(dm_ref[...], wo[...], NT_DIMS, preferred_element_type=F32)
        sa = jax.nn.sigmoid(ga_ref[...].astype(F32))
        sc = jax.nn.sigmoid(gc_ref[...].astype(F32))
        dao = (dmg * sa).astype(BF16)
        dco = (dmg * sc).astype(BF16)
        dao_ref[...] = dao
        dco_ref[...] = dco
        dgate_ref[:, 0:D] = (dmg * ao_ref[...].astype(F32) * sa * (1.0 - sa)).astype(BF16)
        dgate_ref[:, D:2 * D] = (dmg * co_ref[...].astype(F32) * sc * (1.0 - sc)).astype(BF16)
        dbcp_ref[...] += _colsum(dco.astype(F32))
        dattn_ref[...] = lax.dot_general(dao, wap[...], NT_DIMS, preferred_element_type=F32).astype(BF16)
        dc3 = lax.dot_general(dco, wcp[...], NT_DIMS, preferred_element_type=F32)
        xh, rs, c2, sg = _ln_silu(c1_ref[...].astype(F32), lg_ref[...], lb_ref[...])
        dc2 = dc3 * sg * (1.0 + c2 * (1.0 - sg))
        dlg_ref[...] += _colsum(dc2 * xh)
        dlb_ref[...] += _colsum(dc2)
        dxh = dc2 * lg_ref[...]
        dc1 = rs * (dxh - jnp.mean(dxh, -1, keepdims=True) - xh * jnp.mean(dxh * xh, -1, keepdims=True))
        dc1_ref[...] = dc1
        dcb_ref[...] += _colsum(dc1)

    vec = _const((1, D))
    return pl.pallas_call(
        body, name="mix_bwd", grid=(R // tr,),
        in_specs=[_row(tr, D), _row(tr, D), _row(tr, D), _row(tr, D, 0), _row(tr, D, 1), _row(tr, D),
                  _const((D, D)), _const((D, D)), _const((D, D)), vec, vec],
        out_specs=[_row(tr, D), _row(tr, D), _row(tr, 2 * D), _row(tr, D), _row(tr, D), vec, vec, vec, vec],
        out_shape=[_sds((R, D), BF16), _sds((R, D), BF16), _sds((R, 2 * D), BF16), _sds((R, D), BF16),
                   _sds((R, D), F32)] + [_sds((1, D), F32)] * 4,
        compiler_params=_params(("arbitrary",)),
    )(dmix, ao, co, gates, gates, c1, w_out, w_ap, w_cp, lg, lb)


def _conv_bwd(dc1, glu, w, tr, hook=None):
    R = dc1.shape[0]
    nt, nc = R // tr, D // CONV_TCH

    def body(d_ref, a_ref, g_ref, w_ref, dglu_a, dglu_g, dw_ref, buf):
        t = pl.program_id(1)
        i = nt - 1 - t

        @pl.when(t == 0)
        def _():
            buf[tr:tr + HALO, :] = jnp.zeros((HALO, CONV_TCH), F32)
            dw_ref[...] = jnp.zeros_like(dw_ref)

        @pl.when(t > 0)
        def _():
            buf[tr:tr + HALO, :] = buf[0:HALO, :]

        buf[0:tr, :] = d_ref[...]
        for r0 in range(0, tr, CONV_SUB):
            rs = slice(r0, r0 + CONV_SUB)
            row = i * tr + r0 + lax.broadcasted_iota(jnp.int32, (CONV_SUB, 1), 0)
            a, g = a_ref[rs, :].astype(F32), g_ref[rs, :].astype(F32)
            sg = jax.nn.sigmoid(g)
            glu = jnp.where(row >= PAD, a * sg, 0.0)
            acc = jnp.zeros((CONV_SUB, CONV_TCH), F32)
            for k in range(CONV_K):
                win = buf[pl.ds(r0 + CONV_K - 1 - k, CONV_SUB), :]
                acc = acc + w_ref[k:k + 1, :] * win
                dw_ref[k:k + 1, :] += _colsum(glu * win)
            dglu = jnp.where(row >= PAD, acc, 0.0)
            dglu_a[rs, :] = (dglu * sg).astype(BF16)
            dglu_g[rs, :] = (dglu * a * sg * (1.0 - sg)).astype(BF16)

    def rspec(col0):
        return pl.BlockSpec((tr, CONV_TCH), lambda c, t: (nt - 1 - t, col0 + c))

    def steps():
        c, t = pl.program_id(0), pl.program_id(1)
        return (c == 0) & (t == 0), False, (c == nc - 1) & (t == nt - 1)

    return _call(
        body, hook, steps, name="conv_bwd", grid=(nc, nt),
        in_specs=[rspec(0), rspec(0), rspec(nc), pl.BlockSpec((None, HALO, CONV_TCH), lambda c, t: (c, 0, 0))],
        out_specs=[rspec(0), rspec(0), pl.BlockSpec((None, HALO, CONV_TCH), lambda c, t: (c, 0, 0))],
        out_shape=[_sds((R, D), BF16), _sds((R, D), BF16), _sds((N_CHIPS, HALO, CONV_TCH), F32)],
        scratch_shapes=[pltpu.VMEM((tr + HALO, CONV_TCH), F32)],
        operands=(dc1, glu, glu, w), semantics=("arbitrary", "arbitrary"))


def _attn_bwd(qkv, o, do, lse, sinks, rope, hook=None):
    R = qkv.shape[0]
    nb = R // BLK

    def body(s_ref, q_ref, k0, kp, kc, v0, vp, vc, o_ref, do_ref, lse_ref, c_ref, sa_ref, sb_ref,
             dq_ref, dk_ref, dv_ref, dsink_ref, car_k, car_v, met_k, met_v):
        t = pl.program_id(0)
        n = nb - 1 - t

        @pl.when(t == 0)
        def _():
            for ref in (car_k, car_v, met_k, met_v, dsink_ref):
                ref[...] = jnp.zeros_like(ref)

        lane = lax.broadcasted_iota(jnp.int32, (1, BLK), 1)
        low = lane < HD
        kd, vd = _dup_heads(k0, kp, kc, low), _dup_heads(v0, vp, vc, low)
        mask = _attn_mask(n)
        tabs = (c_ref[...], sa_ref[...], sb_ref[...])
        zero = jnp.zeros((), BF16)
        dk_acc = [jnp.zeros((3 * BLK, BLK), F32) for _ in range(NKV)]
        dv_acc = [jnp.zeros((3 * BLK, BLK), F32) for _ in range(NKV)]
        dsink = jnp.zeros((1, BLK), F32)
        lse_all = lse_ref[...]
        for pair in range(NH // 2):
            cs = slice(pair * BLK, (pair + 1) * BLK)
            qp, dop = q_ref[:, cs], do_ref[:, cs]
            prod = dop.astype(F32) * o_ref[:, cs].astype(F32)
            kv = pair // (NH // NKV // 2)
            dqs = []
            for e in range(2):
                h = 2 * pair + e
                sel = low if e == 0 else ~low
                qm, dom = jnp.where(sel, qp, zero), jnp.where(sel, dop, zero)
                lse_h = jnp.sum(jnp.where(lane == h, lse_all, 0.0), -1, keepdims=True)
                s = lax.dot_general(qm, kd[kv], NT_DIMS, preferred_element_type=F32) * (HD ** -0.5)
                p = jnp.where(mask, jnp.exp(s - lse_h), 0.0)
                dp = lax.dot_general(dom, vd[kv], NT_DIMS, preferred_element_type=F32)
                delta = jnp.sum(jnp.where(sel, prod, 0.0), -1, keepdims=True)
                ds = (p * (dp - delta) * (HD ** -0.5)).astype(BF16)
                dqs.append(jnp.dot(ds, kd[kv], preferred_element_type=F32))
                dk_acc[kv] = dk_acc[kv] + lax.dot_general(ds, qm, TN_DIMS, preferred_element_type=F32)
                dv_acc[kv] = dv_acc[kv] + lax.dot_general(p.astype(BF16), dom, TN_DIMS, preferred_element_type=F32)
                ps = jnp.exp(s_ref[h] - lse_h)
                dsink = dsink + jnp.where(lane == h, -jnp.sum(ps * delta), 0.0)
            dq_ref[:, cs] = _rope_bwd(jnp.where(low, dqs[0], dqs[1]), *tabs).astype(BF16)
        dsink_ref[0:1, :] += dsink

        def fold(acc):
            tot = [a + pltpu.roll(a, HD, 1) for a in acc]
            return jnp.where(low, tot[0], tot[1])

        dk_all, dv_all = fold(dk_acc), fold(dv_acc)
        met_k[...] += dk_all[0:BLK, :]
        met_v[...] += dv_all[0:BLK, :]
        last = jnp.where(n == 0, 1.0, 0.0)
        dk_n = dk_all[2 * BLK:3 * BLK, :] + car_k[...] + last * met_k[...]
        dv_n = dv_all[2 * BLK:3 * BLK, :] + car_v[...] + last * met_v[...]
        dk_ref[...] = _rope_bwd(dk_n, *tabs).astype(BF16)
        dv_ref[...] = dv_n.astype(BF16)
        car_k[...] = dk_all[BLK:2 * BLK, :]
        car_v[...] = dv_all[BLK:2 * BLK, :]

    rblk = lambda w: pl.BlockSpec((BLK, w), lambda t: (nb - 1 - t, 0))
    return _call(
        body, hook, _steps_1d(nb, nb), name="attn_bwd", grid=(nb,),
        in_specs=[pl.BlockSpec(memory_space=pltpu.SMEM), rblk(D)] + _kv_specs(nb, True)
                 + [rblk(D), rblk(D), rblk(BLK), rblk(BLK), rblk(BLK), rblk(BLK)],
        out_specs=[rblk(D), rblk(BLK), rblk(BLK), _const((8, BLK))],
        out_shape=[_sds((R, D), BF16), _sds((R, BLK), BF16), _sds((R, BLK), BF16), _sds((8, BLK), F32)],
        scratch_shapes=[pltpu.VMEM((BLK, BLK), F32)] * 4,
        operands=(sinks, qkv, *([qkv] * 6), o, do, lse, *rope), semantics=("arbitrary",))


def _in_bwd(dproj, w_in, h0, dh1, g_pre, tr, hook=None):
    R = dproj.shape[0]

    def body(d_ref, w_ref, h0_ref, dh1_ref, g_ref, dh0_ref, dg_ref, db_ref):
        @pl.when(pl.program_id(0) == 0)
        def _():
            dg_ref[...] = jnp.zeros_like(dg_ref)
            db_ref[...] = jnp.zeros_like(db_ref)

        d = d_ref[...]
        dn1 = jnp.dot(d, w_ref[...], preferred_element_type=F32)
        h0 = h0_ref[...]
        _, r = _rms(h0, g_ref[...])
        dh0_ref[...] = dh1_ref[...] + _rms_bwd(dn1, h0, r, g_ref[...])
        dg_ref[...] += _colsum(dn1 * h0 * r)
        db_ref[...] += _colsum(d.astype(F32))

    return _call(
        body, hook, _steps_1d(R // tr, R // tr), name="in_bwd", grid=(R // tr,),
        in_specs=[_row(tr, IN_W), _const((IN_W, D)), _row(tr, D), _row(tr, D), _const((1, D))],
        out_specs=[_row(tr, D), _const((1, D)), _const((1, IN_W))],
        out_shape=[_sds((R, D), F32), _sds((1, D), F32), _sds((1, IN_W), F32)],
        operands=(dproj, w_in, h0, dh1, g_pre), semantics=("arbitrary",))


def _dw(a, b, name, tn, tr, by_chip=False, ta=None):
    R, ka = a.shape
    n = b.shape[1]
    nt = R // tr
    ta = ta or ka
    if by_chip:
        out_spec = pl.BlockSpec((None, ta, tn), lambda k, j, i: (j, k, 0))
        out_shape = _sds((n // tn, ka, tn), BF16)
    else:
        out_spec = pl.BlockSpec((ta, tn), lambda k, j, i: (k, j))
        out_shape = _sds((ka, n), BF16)

    def body(a_ref, b_ref, o_ref, acc):
        i = pl.program_id(2)

        @pl.when(i == 0)
        def _():
            acc[...] = jnp.zeros_like(acc)

        acc[...] += lax.dot_general(a_ref[...], b_ref[...], TN_DIMS, preferred_element_type=F32)

        @pl.when(i == nt - 1)
        def _():
            o_ref[...] = acc[...].astype(BF16)

    return pl.pallas_call(
        body, name=name, grid=(ka // ta, n // tn, nt),
        in_specs=[pl.BlockSpec((tr, ta), lambda k, j, i: (i, k)), pl.BlockSpec((tr, tn), lambda k, j, i: (i, j))],
        out_specs=out_spec, out_shape=out_shape,
        scratch_shapes=[pltpu.VMEM((ta, tn), F32)],
        compiler_params=_params(("arbitrary", "arbitrary", "arbitrary")),
    )(a, b)


SMALL = ["norm_pre_mix", "norm_post_mix", "b_in", "attn_sinks", "conv_dw_b", "conv_ln_g", "conv_ln_b",
         "b_conv_proj", "norm_pre_ffn", "norm_post_ffn", "ffn_dw_b"]


def local_step(x, tgt, W, dist=None):
    W = dict(W)
    S = x.shape[0]
    R = S + BLK
    tr = _tile(R, 384, BLK)
    trw = _tile(R, 1056)
    rope = _rope_tables(R)
    meta = _cols_joined(W["tiny"][:, TINY_META:TINY_META + NMETA, 0:DQ])
    h0 = jnp.concatenate([jnp.zeros((PAD, D), F32), meta, x], 0)

    qkv, glu, gates, n1, *got = _in_proj(h0, W["norm_pre_mix"], W["w_in"], W["b_in"], rope, tr,
                                         dist and dist.gather_hook(BRANCH_SHARES))
    if dist:
        W.update(dist.weights(BRANCH_SHARES, got))
    sinks = W["attn_sinks"].reshape(NH)
    attn, lse, *got = _attn_fwd(qkv, sinks, dist and dist.gather_hook(FFN_SHARES))
    if dist:
        W.update(dist.weights(FFN_SHARES, got))
    c1 = _conv_fwd(glu, W["tiny"], W["conv_dw_b"], tr)
    ao, co, c3, merged, mix, h1, n2 = _mix_out(
        attn, c1, gates, h0, W["w_attn_proj"], W["w_conv_proj"], W["w_out"], W["conv_ln_g"], W["conv_ln_b"],
        W["b_conv_proj"], W["norm_post_mix"], W["norm_pre_ffn"], tr)
    up, act = _ffn_up(n2, W["w_up"], W["tiny"], W["ffn_dw_b"], tr)
    dh2, dffn, loss_cols, dg_post_ffn = _ffn_down(act, W["w_down"], h1, tgt, W["norm_post_ffn"], tr)

    dw_down = _dw(act, dffn, "dw_down", 512, trw)
    dup, dfw, dfb = _ffn_bwd_act(dffn, W["w_down"], up, W["tiny"], W["ffn_dw_b"], tr)
    dw_up = _dw(n2, dup, "dw_up", UPQ, trw, by_chip=True)
    ffn_sums = dist and dist.pair_sums(FFN_SHARES, [dw_up, dw_down.reshape(N_CHIPS, -1, D)], "ffn")
    dh1, dmix, dg_pre_ffn, dg_post_mix = _ffn_bwd_in(dup, W["w_up"], h1, dh2, mix, W["norm_pre_ffn"],
                                                      W["norm_post_mix"], tr)
    dao, dco, dgates, dattn, dc1, db_cp, dlg, dlb, dcb = _mix_bwd(
        dmix, ao, co, gates, c1, W["w_out"], W["w_attn_proj"], W["w_conv_proj"], W["conv_ln_g"], W["conv_ln_b"], tr)
    dglu_a, dglu_g, dcw, *ffn_got = _conv_bwd(dc1, glu, W["tiny"], tr, dist and dist.chip_hook(FFN_SHARES, ffn_sums))
    branch = [_dw(attn, dao, "dw_attn_proj", D, trw), _dw(c3, dco, "dw_conv_proj", D, trw),
              _dw(merged, dmix, "dw_out", D, trw),
              _tiny_pack({"conv_dw_w": dcw, "ffn_dw_w": dfw, "meta_tokens": jnp.zeros((N_CHIPS, NMETA, DQ), F32)})]
    branch_sums = dist and dist.pair_sums(BRANCH_TINY_SHARES, [a.reshape(N_CHIPS, -1, a.shape[-1]) for a in branch], "branch")
    dq, dk, dv, dsink, *got = _attn_bwd(qkv, attn, dattn, lse, sinks, rope,
                                        dist and dist.chip_hook(BRANCH_TINY_SHARES, branch_sums))
    if dist:
        dist.finish(FFN_SHARES + BRANCH_TINY_SHARES, ffn_sums + branch_sums, ffn_got + got, "ffn_branch")
    dproj = jnp.concatenate([dq, dk, dv, dglu_a, dglu_g, dgates], 1)
    dw_in = _dw(dproj, n1, "dw_in", D, trw, ta=768)
    in_sums = dist and dist.pair_sums(IN_SHARES, [dw_in.reshape(N_CHIPS, -1, D)], "in")
    dh0, dg_pre_mix, db_in, *got = _in_bwd(dproj, W["w_in"], h0, dh1, W["norm_pre_mix"], tr,
                                           dist and dist.chip_hook(IN_SHARES, in_sums))
    if dist:
        dist.finish(IN_SHARES, in_sums, got, "in")

    grads = {
        "w_in": dw_in, "w_attn_proj": branch[0], "w_conv_proj": branch[1], "w_out": branch[2],
        "w_up": dw_up,
        "w_down": dw_down,
        "tiny": branch[3],
        "meta_tokens": dh0[PAD:BLK],
        "norm_pre_mix": dg_pre_mix, "norm_post_mix": dg_post_mix, "b_in": db_in,
        "attn_sinks": dsink[0:1, 0:NH], "conv_dw_b": dcb, "conv_ln_g": dlg, "conv_ln_b": dlb,
        "b_conv_proj": db_cp, "norm_pre_ffn": dg_pre_ffn, "norm_post_ffn": dg_post_ffn, "ffn_dw_b": dfb,
    }
    return loss_cols, dh0[BLK:], grads


INQ = IN_W // N_CHIPS
DQ = D // N_CHIPS
SHARES = [("w_in", INQ, D, BF16), ("w_attn_proj", DQ, D, BF16), ("w_conv_proj", DQ, D, BF16), ("w_out", DQ, D, BF16),
          ("w_up", D, UPQ, BF16), ("w_down", FFN // N_CHIPS, D, BF16), ("tiny", TINY_ROWS, UPQ, F32)]
TINY_PARTS = [("conv_dw_w", TINY_CONV, CONV_K, TINY_FFN - TINY_CONV, DQ), ("ffn_dw_w", TINY_FFN, FFN_K, TINY_META - TINY_FFN, UPQ),
              ("meta_tokens", TINY_META, NMETA, NMETA, DQ)]


def _tiny_pack(parts):
    rows = []
    for name, _, _, reserved, _ in TINY_PARTS:
        a = parts[name].astype(F32)
        pad = [(0, 0)] * (a.ndim - 2) + [(0, reserved - a.shape[-2]), (0, UPQ - a.shape[-1])]
        rows.append(jnp.pad(a, pad))
    used = sum(r.shape[-2] for r in rows)
    rows.append(jnp.zeros(rows[0].shape[:-2] + (TINY_ROWS - used, UPQ), F32))
    return jnp.concatenate(rows, axis=-2)


def _tiny_unpack(tiny):
    return {name: tiny[..., r0:r0 + k, 0:cols] for name, r0, k, _, cols in TINY_PARTS}


def _cols_by_chip(a):
    rows, n = a.shape
    return a.reshape(rows, N_CHIPS, n // N_CHIPS).transpose(1, 0, 2)


def _cols_joined(a):
    _, rows, cols = a.shape
    return a.transpose(1, 0, 2).reshape(rows, N_CHIPS * cols)


def _to_planes(a, rows):
    return jnp.pad(a, [(0, rows * D - a.shape[-1])]).reshape(rows, D)


ANY = pl.BlockSpec(memory_space=pl.ANY)


def _place():
    x, y, c = lax.axis_index("x"), lax.axis_index("y"), lax.axis_index("c")
    chips = [(1 - x, y), (x, 1 - y), (1 - x, 1 - y)]
    return x, y, c, chips


def _rcopy(src, dst, ssem, rsem, to):
    return pltpu.make_async_remote_copy(src_ref=src, dst_ref=dst, send_sem=ssem, recv_sem=rsem,
                                        device_id=to, device_id_type=MESH)


def _halves(ref_or_rows, c):
    half = ref_or_rows // 2
    return pl.ds(c * half, half), pl.ds((1 - c) * half, half)


FIRST_SHARES, BRANCH_SHARES, FFN_SHARES = [0, 6], [1, 2, 3], [4, 5]
IN_SHARES, BRANCH_TINY_SHARES = [0], [1, 2, 3, 6]


def _gather_hook(own, idx):
    n = len(idx)

    def copies(kind, ins, outs, ssem, rsem):
        x, y, c, chips = _place()
        q = 2 * x + y
        sib = (x, y, 1 - c)
        out = []
        for i, a in enumerate(idx):
            mine, other = _halves(SHARES[a][1], c)
            for j, (cx, cy) in enumerate(chips):
                k, to = 3 * i + j, (cx, cy, c)
                landed, theirs = outs[i].at[2 * cx + cy, mine], outs[i].at[2 * cx + cy, other]
                if kind == "send":
                    out.append(_rcopy(ins[i].at[mine], outs[i].at[q, mine], ssem.at[k], rsem.at[k], to))
                elif kind == "landing":
                    out.append(_rcopy(ins[i].at[mine], landed, ssem.at[k], rsem.at[k], to))
                elif kind == "pass":
                    out.append(_rcopy(landed, landed, ssem.at[3 * n + k], rsem.at[3 * n + k], sib))
                else:
                    out.append(_rcopy(theirs, theirs, ssem.at[3 * n + k], rsem.at[3 * n + k], sib))
        return out

    def own_copies(ins, outs, ssem, rsem):
        x, y, c, _ = _place()
        q = 2 * x + y
        return [_rcopy(ins[i], outs[i].at[q], ssem.at[6 * n + i], rsem.at[6 * n + i], (x, y, 1 - c)) for i in range(n)]

    def start(*refs):
        for cp in copies("send", *refs) + own_copies(*refs):
            cp.start()

    def mid(*refs):
        for landed, cp in zip(copies("landing", *refs), copies("pass", *refs)):
            landed.wait_recv()
            cp.start()

    def finish(*refs):
        for cp in copies("arrival", *refs):
            cp.wait_recv()
        for cp in copies("send", *refs) + copies("pass", *refs):
            cp.wait_send()
        for cp in own_copies(*refs):
            cp.wait()

    shapes = [_sds((N_CHIPS,) + SHARES[a][1:3], SHARES[a][3]) for a in idx]
    return _Hook(own, shapes, 7 * n, start, finish, mid)


def _chip_hook(sums, idx):
    def copies(ins, outs, ssem, rsem):
        x, y, c, chips = _place()
        return [_rcopy(ins[i].at[2 * cx + cy], outs[i].at[j], ssem.at[3 * i + j], rsem.at[3 * i + j], (cx, cy, c))
                for i in range(len(idx)) for j, (cx, cy) in enumerate(chips)]

    def start(*refs):
        for cp in copies(*refs):
            cp.start()

    def finish(*refs):
        for cp in copies(*refs):
            cp.wait()

    shapes = [_sds((N_CHIPS - 1, SHARES[a][1] // 2, SHARES[a][2]), SHARES[a][3]) for a in idx]
    return _Hook(sums, shapes, 3 * len(idx), start, finish)


def _sibling_swap(parts, idx, tag):
    def copies(ins, outs, ssem, rsem):
        x, y, c, _ = _place()
        return [_rcopy(ins[i].at[:, _halves(SHARES[a][1], c)[1]], outs[i], ssem.at[i], rsem.at[i], (x, y, 1 - c))
                for i, a in enumerate(idx)]

    def start(*refs):
        for cp in copies(*refs):
            cp.start()

    def finish(*refs):
        for cp in copies(*refs):
            cp.wait()

    shapes = [_sds((N_CHIPS, SHARES[a][1] // 2, SHARES[a][2]), SHARES[a][3]) for a in idx]
    return _alone(_Hook(parts, shapes, len(idx), start, finish), "sibling_swap_" + tag)


def _sum_pair(parts, recvs, c, idx, tag):
    steps, n = 2, len(idx)

    def body(c_ref, *refs):
        for i, a in enumerate(idx):
            refs[2 * n + i][...] = (refs[i][...].astype(F32) + refs[n + i][...].astype(F32)).astype(SHARES[a][3])

    own, got, out, views, shapes = [], [], [], [], []
    for p, a in zip(parts, idx):
        _, rows, cols, dt = SHARES[a]
        blk = rows // 2 // steps
        own.append(pl.BlockSpec((None, None, blk, cols), lambda q, i, c_ref: (q, c_ref[0], i, 0)))
        got.append(pl.BlockSpec((None, blk, cols), lambda q, i, c_ref: (q, i, 0)))
        out.append(pl.BlockSpec((None, blk, cols), lambda q, i, c_ref: (q, i, 0)))
        views.append(p.reshape(N_CHIPS, 2, rows // 2, cols))
        shapes.append(_sds((N_CHIPS, rows // 2, cols), dt))
    grid_spec = pltpu.PrefetchScalarGridSpec(num_scalar_prefetch=1, grid=(N_CHIPS, steps),
                                             in_specs=own + got, out_specs=out)
    return pl.pallas_call(body, name="sum_pair_" + tag, grid_spec=grid_spec, out_shape=shapes,
                          compiler_params=_params(("arbitrary", "arbitrary")))(c, *views, *recvs)


def _sum_chips(sums, recvs, qc, idx, tag):
    steps, n = 2, len(idx)

    def body(qc_ref, *refs):
        for i in range(n):
            acc = refs[i][...].astype(F32)
            for j in range(1, N_CHIPS):
                acc = acc + refs[j * n + i][...].astype(F32)
            refs[N_CHIPS * n + i][...] = acc

    own, got, out, shapes = [], [[], [], []], [], []
    for a in idx:
        _, rows, cols, _ = SHARES[a]
        blk = rows // 2 // steps
        own.append(pl.BlockSpec((None, blk, cols), lambda i, qc_ref: (qc_ref[0], i, 0)))
        for j in range(N_CHIPS - 1):
            got[j].append(pl.BlockSpec((None, blk, cols), lambda i, qc_ref, j=j: (j, i, 0)))
        out.append(pl.BlockSpec((None, blk, cols), lambda i, qc_ref: (qc_ref[1], i, 0)))
        shapes.append(_sds((2, rows // 2, cols), F32))
    grid_spec = pltpu.PrefetchScalarGridSpec(num_scalar_prefetch=1, grid=(steps,),
                                             in_specs=own + got[0] + got[1] + got[2], out_specs=out)
    return pl.pallas_call(body, name="sum_chips_" + tag, grid_spec=grid_spec, out_shape=shapes,
                          compiler_params=_params(("arbitrary",)))(qc, *sums, *recvs, *recvs, *recvs)


def _sibling_share(halves, idx, tag):
    n = len(idx)

    def body(*refs):
        outs, (ssem, rsem) = refs[n:2 * n], refs[2 * n:]
        x, y, c, _ = _place()
        copies = []
        for i in range(n):
            cp = _rcopy(outs[i].at[c], outs[i].at[c], ssem.at[i], rsem.at[i], (x, y, 1 - c))
            cp.start()
            copies.append(cp)
        for i in range(n):
            theirs = outs[i].at[1 - c]
            _rcopy(theirs, theirs, ssem.at[i], rsem.at[i], (x, y, 1 - c)).wait_recv()
        for cp in copies:
            cp.wait_send()

    return pl.pallas_call(
        body, name="sibling_share_" + tag, in_specs=[ANY] * n, out_specs=[ANY] * n,
        out_shape=[_sds((2, SHARES[a][1] // 2, SHARES[a][2]), F32) for a in idx],
        input_output_aliases={i: i for i in range(n)},
        scratch_shapes=[pltpu.SemaphoreType.DMA((n,)), pltpu.SemaphoreType.DMA((n,))],
    )(*halves)


class _Dist:
    def __init__(self, own):
        self.own = own
        self.core = lax.axis_index("c")
        self.chip = 2 * lax.axis_index("x") + lax.axis_index("y")
        self.reduced = {}

    def gather_hook(self, idx):
        return _gather_hook([self.own[a] for a in idx], idx)

    def weights(self, idx, gathered):
        out = {}
        for a, full in zip(idx, gathered):
            name = SHARES[a][0]
            out[name] = full if name in ("w_up", "tiny") else full.reshape(-1, D)
        return out

    def pair_sums(self, idx, parts, tag):
        return _sum_pair(parts, _sibling_swap(parts, idx, tag), self.core.reshape(1), idx, tag)

    def chip_hook(self, idx, sums):
        return _chip_hook(sums, idx)

    def finish(self, idx, sums, recvs, tag):
        halves = _sum_chips(sums, recvs, jnp.stack([self.chip, self.core]), idx, tag)
        for a, full in zip(idx, _sibling_share(halves, idx, tag)):
            self.reduced[SHARES[a][0]] = full.reshape(SHARES[a][1:3])


N_DEV = 8
SMALL_ROWS = 40


def _small_allreduce(sm):
    def body(s_ref, o_ref, buf, ssem, rsem):
        x, y, c, _ = _place()
        me = 4 * x + 2 * y + c
        buf[me] = s_ref[...]
        copies = []
        for d in range(1, N_DEV):
            dx, dy, dc = d >> 2, (d >> 1) & 1, d & 1
            to = (x ^ dx, y ^ dy, c ^ dc)
            cp = _rcopy(s_ref, buf.at[me], ssem.at[d - 1], rsem.at[d - 1], to)
            cp.start()
            copies.append(cp)
        for d in range(1, N_DEV):
            src = me ^ d
            _rcopy(s_ref, buf.at[src], ssem.at[d - 1], rsem.at[d - 1], (x, y, c)).wait_recv()
        for cp in copies:
            cp.wait_send()
        acc = buf[0]
        for k in range(1, N_DEV):
            acc = acc + buf[k]
        o_ref[...] = acc

    vm = pl.BlockSpec(memory_space=pltpu.VMEM)
    return pl.pallas_call(
        body, name="small_allreduce", in_specs=[vm], out_specs=vm,
        out_shape=_sds((SMALL_ROWS, D), F32),
        scratch_shapes=[pltpu.VMEM((N_DEV, SMALL_ROWS, D), F32),
                        pltpu.SemaphoreType.DMA((N_DEV - 1,)), pltpu.SemaphoreType.DMA((N_DEV - 1,))],
    )(sm)


SMALL_PLAN = [("norm_pre_mix", D), ("norm_post_mix", D), ("b_in", IN_W), ("attn_sinks", NH), ("conv_dw_b", D),
              ("conv_ln_g", D), ("conv_ln_b", D), ("b_conv_proj", D), ("norm_pre_ffn", D), ("norm_post_ffn", D),
              ("ffn_dw_b", 2 * FFN), ("loss", D), ("meta_tokens", NMETA * D)]


def _pack_small(parts):
    rows = [_to_planes(parts[name].reshape(-1), -(-n // D)) for name, n in SMALL_PLAN]
    used = sum(r.shape[0] for r in rows)
    return jnp.concatenate(rows + [jnp.zeros((SMALL_ROWS - used, D), F32)], 0)


def _unpack_small(packed):
    out, r0 = {}, 0
    for name, n in SMALL_PLAN:
        rows = -(-n // D)
        out[name] = packed[r0:r0 + rows].reshape(-1)[:n].reshape(1, n)
        r0 += rows
    return out


def _adamw(w, g, m, v, name):
    rows, cols = w.shape
    tr = _tile(rows, 256, 8) if rows % 8 == 0 else rows

    def body(w_ref, g_ref, m_ref, v_ref, d_ref, nm_ref, nv_ref):
        g = g_ref[...]
        m = B1 * m_ref[...] + (1.0 - B1) * g
        v = B2 * v_ref[...] + (1.0 - B2) * (g * g)
        nm_ref[...] = m
        nv_ref[...] = v
        m_hat = m / (1.0 - B1 ** STEP)
        v_hat = v / (1.0 - B2 ** STEP)
        d_ref[...] = -LR * (m_hat / (jnp.sqrt(v_hat) + ADAM_EPS) + WD * w_ref[...])

    spec = pl.BlockSpec((tr, cols), lambda i: (i, 0))
    return pl.pallas_call(
        body, name=name, grid=(rows // tr,), in_specs=[spec] * 4, out_specs=[spec] * 3,
        out_shape=[_sds((rows, cols), F32)] * 3, compiler_params=_params(("arbitrary",)),
    )(w, g, m, v)


NAMES = ["meta_tokens", "norm_pre_mix", "norm_post_mix", "w_in", "b_in", "attn_sinks", "w_attn_proj", "conv_dw_w",
         "conv_dw_b", "conv_ln_g", "conv_ln_b", "w_conv_proj", "b_conv_proj", "w_out", "norm_pre_ffn", "norm_post_ffn",
         "w_up", "ffn_dw_w", "ffn_dw_b", "w_down"]
MATMUL = ("w_in", "w_attn_proj", "w_conv_proj", "w_out", "w_up", "w_down")


def _two_d(a):
    return a.reshape(a.shape[-2:])


def kernel(x, meta_tokens, norm_pre_mix, norm_post_mix, w_in, b_in, attn_sinks, w_attn_proj, conv_dw_w, conv_dw_b, conv_ln_g, conv_ln_b, w_conv_proj, b_conv_proj, w_out, norm_pre_ffn, norm_post_ffn, w_up, ffn_dw_w, ffn_dw_b, w_down, loss_target, m_meta_tokens, m_norm_pre_mix, m_norm_post_mix, m_w_in, m_b_in, m_attn_sinks, m_w_attn_proj, m_conv_dw_w, m_conv_dw_b, m_conv_ln_g, m_conv_ln_b, m_w_conv_proj, m_b_conv_proj, m_w_out, m_norm_pre_ffn, m_norm_post_ffn, m_w_up, m_ffn_dw_w, m_ffn_dw_b, m_w_down, v_meta_tokens, v_norm_pre_mix, v_norm_post_mix, v_w_in, v_b_in, v_attn_sinks, v_w_attn_proj, v_conv_dw_w, v_conv_dw_b, v_conv_ln_g, v_conv_ln_b, v_w_conv_proj, v_b_conv_proj, v_w_out, v_norm_pre_ffn, v_norm_post_ffn, v_w_up, v_ffn_dw_w, v_ffn_dw_b, v_w_down):
    args = locals()
    w = {n: args[n] for n in NAMES}
    m = {n: args["m_" + n] for n in NAMES}
    v = {n: args["v_" + n] for n in NAMES}
    tiny_names = [part[0] for part in TINY_PARTS]
    big = list(MATMUL) + tiny_names

    def shard_2d(a, name):
        return _two_d(a).T if name == "w_in" else _two_d(a)

    own = {n: shard_2d(w[n], n).astype(BF16) for n in MATMUL}
    own["tiny"] = _tiny_pack({n: _two_d(w[n]) for n in tiny_names})
    dist = _Dist([own[n] for n, _, _, _ in SHARES])
    W = {n: _two_d(w[n]) for n in SMALL}
    W.update(dist.weights(FIRST_SHARES, _alone(dist.gather_hook(FIRST_SHARES), "gather_first")))

    loss_cols, grad_x, grads = local_step(x[0], loss_target[0], W, dist)

    small = dict(grads)
    small["loss"] = loss_cols
    g_small = _unpack_small(_small_allreduce(_pack_small(small)))
    loss = jnp.sum(g_small["loss"])
    g_big = {n: dist.reduced[n] for n in MATMUL}
    g_big.update(_tiny_unpack(dist.reduced["tiny"]))
    g_big["meta_tokens"] = lax.dynamic_slice(g_small["meta_tokens"].reshape(NMETA, D), (0, dist.chip * DQ), (NMETA, DQ))

    g, delta, new_m, new_v = {}, {}, {}, {}
    for n in big:
        shape = w[n].shape
        back = (lambda a: a.T.reshape(shape)) if n == "w_in" else (lambda a: a.reshape(shape))
        outs = _adamw(shard_2d(w[n], n), g_big[n], shard_2d(m[n], n), shard_2d(v[n], n), "adamw_" + n)
        g[n], delta[n], new_m[n], new_v[n] = (back(a) for a in (g_big[n], *outs))
    unused = dict(loss=jnp.zeros((1, D), F32), meta_tokens=jnp.zeros((NMETA, D), F32))
    packs = [_pack_small(dict({n: d[n] for n in SMALL}, **unused)) for d in (w, g_small, m, v)]
    ud, um, uv = (_unpack_small(a) for a in _adamw(*packs, "adamw_small"))
    for n in SMALL:
        g[n], delta[n], new_m[n], new_v[n] = g_small[n], ud[n], um[n], uv[n]

    return (loss, grad_x[None], *[g[n] for n in NAMES], *[delta[n] for n in NAMES],
            *[new_m[n] for n in NAMES], *[new_v[n] for n in NAMES])
```

```python
import jax
import jax.numpy as jnp
from jax import lax
from jax.experimental import pallas as pl
from jax.experimental.pallas import tpu as pltpu

F32, BF16 = jnp.float32, jnp.bfloat16

D = 1024
NH, NKV, HD = 16, 2, 64
GH = NH // NKV
NMETA, BLK = 16, 128
PAD = BLK - NMETA
ROT = HD // 4
THETA = 500000.0
CONV_K = 31
FFN = 2816
FFN_K = 3
IN_W = 5376
QKV_W, GLU_W, GATE_W = 1280, 2048, 2048
RMS_EPS, LN_EPS, NEG = 1e-6, 1e-5, -1e30
LR, B1, B2, ADAM_EPS, WD, STEP = 0.001, 0.9, 0.999, 1e-08, 0.01, 10

VMEM_LIMIT = 56 * 2 ** 20
MESH = pl.DeviceIdType.MESH

NT_DIMS = (((1,), (1,)), ((), ()))
TN_DIMS = (((0,), (0,)), ((), ()))


def _params(sem, **kw):
    return pltpu.CompilerParams(dimension_semantics=sem, vmem_limit_bytes=VMEM_LIMIT, **kw)


def _tile(n, pref, mult=16):
    for t in range(min(pref, n), 0, -1):
        if n % t == 0 and t % mult == 0:
            return t
    return n


def _row(tr, w, col=0):
    return pl.BlockSpec((tr, w), lambda i: (i, col))


def _rrow(tr, w, nt, col=0):
    return pl.BlockSpec((tr, w), lambda t: (nt - 1 - t, col))


def _const(shape):
    return pl.BlockSpec(shape, lambda *_: (0,) * len(shape))


def _sds(shape, dt):
    return jax.ShapeDtypeStruct(shape, dt)


class _Hook:
    def __init__(self, operands, out_shape, n_sem, start, finish, mid=None):
        self.operands, self.out_shape, self.n_sem = list(operands), list(out_shape), n_sem
        self.start, self.mid, self.finish = start, mid, finish

    def scratch(self):
        return [pltpu.SemaphoreType.DMA((self.n_sem,)), pltpu.SemaphoreType.DMA((self.n_sem,))]


def _call(body, hook, steps, *, name, grid, in_specs, out_specs, out_shape, operands, semantics, scratch_shapes=()):
    in_specs, out_specs, out_shape = list(in_specs), list(out_specs), list(out_shape)
    if hook is None:
        return pl.pallas_call(body, name=name, grid=grid, in_specs=in_specs, out_specs=out_specs, out_shape=out_shape,
                              scratch_shapes=list(scratch_shapes), compiler_params=_params(semantics))(*operands)
    n_in, n_out, n_hi, n_ho = len(in_specs), len(out_specs), len(hook.operands), len(hook.out_shape)

    def wrapped(*refs):
        ins, hi = refs[:n_in], refs[n_in:n_in + n_hi]
        o0 = n_in + n_hi
        outs, ho = refs[o0:o0 + n_out], refs[o0 + n_out:o0 + n_out + n_ho]
        scratch, (ssem, rsem) = refs[o0 + n_out + n_ho:len(refs) - 2], refs[len(refs) - 2:]
        first, middle, last = steps()

        @pl.when(first)
        def _():
            hook.start(hi, ho, ssem, rsem)

        body(*ins, *outs, *scratch)
        if hook.mid is not None:
            @pl.when(middle)
            def _():
                hook.mid(hi, ho, ssem, rsem)

        @pl.when(last)
        def _():
            hook.finish(hi, ho, ssem, rsem)

    any_spec = pl.BlockSpec(memory_space=pl.ANY)
    return pl.pallas_call(
        wrapped, name=name, grid=grid, in_specs=in_specs + [any_spec] * n_hi, out_specs=out_specs + [any_spec] * n_ho,
        out_shape=out_shape + hook.out_shape, scratch_shapes=list(scratch_shapes) + hook.scratch(),
        compiler_params=_params(semantics))(*operands, *hook.operands)


def _alone(hook, name):
    n_hi = len(hook.operands)

    def body(*refs):
        hi, ho, (ssem, rsem) = refs[:n_hi], refs[n_hi:len(refs) - 2], refs[len(refs) - 2:]
        hook.start(hi, ho, ssem, rsem)
        if hook.mid is not None:
            hook.mid(hi, ho, ssem, rsem)
        hook.finish(hi, ho, ssem, rsem)

    any_spec = pl.BlockSpec(memory_space=pl.ANY)
    return pl.pallas_call(body, name=name, in_specs=[any_spec] * n_hi, out_specs=[any_spec] * len(hook.out_shape),
                          out_shape=hook.out_shape, scratch_shapes=hook.scratch())(*hook.operands)


def _steps_1d(n, mid):
    def steps():
        i = pl.program_id(0)
        return i == 0, i == min(mid, n - 1), i == n - 1
    return steps


def _rms(x, g):
    r = lax.rsqrt(jnp.mean(x * x, -1, keepdims=True) + RMS_EPS)
    return x * r * g, r


def _rms_bwd(dy, x, r, g):
    gy = dy * g
    return r * gy - x * (r * r * r) * jnp.mean(x * gy, -1, keepdims=True)


def _colsum(x):
    return jnp.sum(x, axis=0, keepdims=True)


def _rope(x, c, sa, sb):
    n = x.shape[1]
    return x * c + pltpu.roll(x, n - 8, 1) * sa + pltpu.roll(x, 8, 1) * sb


def _rope_bwd(d, c, sa, sb):
    n = d.shape[1]
    return d * c + pltpu.roll(d * sa, 8, 1) + pltpu.roll(d * sb, n - 8, 1)


def _rope_tables(R):
    half = ROT // 2
    lane = jnp.arange(2 * HD) % HD
    inv = THETA ** (-(lane % half).astype(F32) * 2.0 / ROT)
    pos = (jnp.arange(R) - PAD).astype(F32)
    ang = pos[:, None] * inv[None, :]
    cos, sin = jnp.cos(ang), jnp.sin(ang)
    c = jnp.where(lane < ROT, cos, 1.0)
    sa = jnp.where(lane < half, -sin, 0.0)
    sb = jnp.where((lane >= half) & (lane < ROT), sin, 0.0)
    return c, sa, sb


IN_CHUNKS = ([(0, 512, True), (512, 1024, True), (1024, 1152, True), (1152, 1280, False)]
             + [(c, c + 512, False) for c in range(1280, IN_W, 512)])


def _in_proj(h0, g_pre, w_in, b_in, rope, tr, hook=None):
    R = h0.shape[0]
    nt = R // tr

    def body(h_ref, g_ref, w_ref, b_ref, c_ref, sa_ref, sb_ref, qkv_ref, glu_ref, gate_ref, n1_ref):
        n, _ = _rms(h_ref[...], g_ref[...])
        nb = n.astype(BF16)
        n1_ref[...] = nb
        for c0, c1, rot in IN_CHUNKS:
            acc = lax.dot_general(nb, w_ref[c0:c1, :], NT_DIMS, preferred_element_type=F32) + b_ref[:, c0:c1]
            if rot:
                reps = (c1 - c0) // 128
                acc = _rope(acc, jnp.tile(c_ref[...], (1, reps)), jnp.tile(sa_ref[...], (1, reps)),
                            jnp.tile(sb_ref[...], (1, reps)))
            val = acc.astype(BF16)
            if c1 <= QKV_W:
                qkv_ref[:, c0:c1] = val
            elif c1 <= QKV_W + GLU_W:
                glu_ref[:, c0 - QKV_W:c1 - QKV_W] = val
            else:
                gate_ref[:, c0 - QKV_W - GLU_W:c1 - QKV_W - GLU_W] = val

    return _call(
        body, hook, _steps_1d(nt, (3 * nt) // 4), name="in_proj", grid=(nt,),
        in_specs=[_row(tr, D), _const((1, D)), _const((IN_W, D)), _const((1, IN_W)),
                  _row(tr, 128), _row(tr, 128), _row(tr, 128)],
        out_specs=[_row(tr, QKV_W), _row(tr, GLU_W), _row(tr, GATE_W), _row(tr, D)],
        out_shape=[_sds((R, QKV_W), BF16), _sds((R, GLU_W), BF16), _sds((R, GATE_W), BF16), _sds((R, D), BF16)],
        operands=(h0, g_pre, w_in, b_in, *rope), semantics=("arbitrary",))


def _attn_mask(n, keys_first=False):
    shape = (3 * BLK, BLK) if keys_first else (BLK, 3 * BLK)
    qi = lax.broadcasted_iota(jnp.int32, shape, 1 if keys_first else 0)
    kj = lax.broadcasted_iota(jnp.int32, shape, 0 if keys_first else 1)
    tq = n * BLK + qi - PAD
    t_meta = kj - PAD
    t_loc = (n - 1) * BLK + (kj - BLK) - PAD
    meta_ok = (kj < BLK) & (t_meta >= 0) & (t_meta <= tq)
    loc_ok = (kj >= BLK) & (t_loc >= NMETA) & (t_loc <= tq) & (tq - t_loc < BLK)
    return meta_ok | loc_ok


def _dup_heads(ref0, refp, refc, low, transposed=False):
    a = jnp.concatenate([ref0[...], refp[...], refc[...]], 0).astype(F32)
    sw = pltpu.roll(a, HD, 1)
    heads = [jnp.where(low, a, sw), jnp.where(low, sw, a)]
    return [(h.T if transposed else h).astype(BF16) for h in heads]


def _kv_specs(nb, rev):
    def blk(col, which):
        def idx(t):
            n = nb - 1 - t if rev else t
            return ({"meta": 0, "prev": jnp.maximum(n - 1, 0), "own": n}[which], col)
        return pl.BlockSpec((BLK, BLK), idx)
    return [blk(col, w) for col in (8, 9) for w in ("meta", "prev", "own")]


def _attn_fwd(qkv, sinks, hook=None):
    R = qkv.shape[0]
    nb = R // BLK

    def body(s_ref, q_ref, k0, kp, kc, v0, vp, vc, o_ref, lse_ref):
        n = pl.program_id(0)
        lane = lax.broadcasted_iota(jnp.int32, (1, BLK), 1)
        low = lane < HD
        kd, vd_t = _dup_heads(k0, kp, kc, low), _dup_heads(v0, vp, vc, low, transposed=True)
        mask = _attn_mask(n, keys_first=True)
        zero = jnp.zeros((), BF16)
        lses = []
        for g in range(NKV):
            tiles = []
            for pair in range(GH // 2 * g, GH // 2 * (g + 1)):
                qp = q_ref[:, pair * BLK:(pair + 1) * BLK] * jnp.asarray(HD ** -0.5, BF16)
                tiles += [jnp.where(low, qp, zero), jnp.where(low, zero, qp)]
            st = lax.dot_general(kd[g], jnp.concatenate(tiles, 0), NT_DIMS, preferred_element_type=F32)
            ps, inv = [], []
            for j in range(GH):
                s = jnp.where(mask, st[:, j * BLK:(j + 1) * BLK], NEG)
                sk = s_ref[GH * g + j]
                m = jnp.maximum(jnp.max(s, 0, keepdims=True), sk)
                p = jnp.exp(s - m)
                l = jnp.sum(p, 0, keepdims=True) + jnp.exp(sk - m)
                ps.append(p.astype(BF16))
                inv.append(1.0 / l)
                lses.append(m + jnp.log(l))
            ot = jnp.dot(vd_t[g], jnp.concatenate(ps, 1), preferred_element_type=F32)
            for j in range(GH // 2):
                pair = GH // 2 * g + j
                o = [(ot[:, h * BLK:(h + 1) * BLK] * inv[h]).T for h in (2 * j, 2 * j + 1)]
                o_ref[:, pair * BLK:(pair + 1) * BLK] = jnp.where(low, o[0], o[1]).astype(BF16)
        lse_ref[...] = jnp.concatenate(lses, 0)

    return _call(
        body, hook, _steps_1d(nb, (3 * nb) // 4), name="attn_fwd", grid=(nb,),
        in_specs=[pl.BlockSpec(memory_space=pltpu.SMEM), pl.BlockSpec((BLK, D), lambda n: (n, 0))] + _kv_specs(nb, False),
        out_specs=[_row(BLK, D), _row(NH, BLK)],
        out_shape=[_sds((R, D), BF16), _sds((nb * NH, BLK), F32)],
        operands=(sinks, qkv, *([qkv] * 6)), semantics=("arbitrary",))


CONV_TCH, CONV_SUB, HALO = 256, 64, 32


def _tap_windows(buf, r0, offset_of):
    span = CONV_SUB + HALO
    x = buf[pl.ds(r0, span), :]
    by_phase = {}
    for k in range(CONV_K):
        by_phase.setdefault(offset_of(k) % 8, []).append(k)
    for phase, taps in sorted(by_phase.items()):
        y = x if phase == 0 else pltpu.roll(x, span - phase, 0)
        for k in taps:
            d = offset_of(k) - phase
            yield k, y[d:d + CONV_SUB, :]


def _conv_fwd(glu, w, b, tr):
    R = glu.shape[0]
    nc = D // CONV_TCH

    def body(a_ref, g_ref, w_ref, b_ref, o_ref, buf):
        i = pl.program_id(1)

        @pl.when(i == 0)
        def _():
            buf[0:HALO, :] = jnp.zeros((HALO, CONV_TCH), F32)

        @pl.when(i > 0)
        def _():
            buf[0:HALO, :] = buf[tr:tr + HALO, :]

        row = i * tr + lax.broadcasted_iota(jnp.int32, (tr, 1), 0)
        a, g = a_ref[...].astype(F32), g_ref[...].astype(F32)
        buf[HALO:HALO + tr, :] = jnp.where(row >= PAD, a * jax.nn.sigmoid(g), 0.0)
        for r0 in range(0, tr, CONV_SUB):
            acc = jnp.broadcast_to(b_ref[...], (CONV_SUB, CONV_TCH))
            for k, win in _tap_windows(buf, r0, lambda k: HALO - (CONV_K - 1) + k):
                acc = acc + w_ref[k:k + 1, :] * win
            o_ref[r0:r0 + CONV_SUB, :] = acc.astype(BF16)

    return pl.pallas_call(
        body, name="conv_fwd", grid=(nc, R // tr),
        in_specs=[pl.BlockSpec((tr, CONV_TCH), lambda c, i: (i, c)),
                  pl.BlockSpec((tr, CONV_TCH), lambda c, i: (i, nc + c)),
                  pl.BlockSpec((None, HALO, CONV_TCH), lambda c, i: (c, 0, 0)),
                  pl.BlockSpec((1, CONV_TCH), lambda c, i: (0, c))],
        out_specs=pl.BlockSpec((tr, CONV_TCH), lambda c, i: (i, c)),
        out_shape=_sds((R, D), BF16),
        scratch_shapes=[pltpu.VMEM((tr + HALO, CONV_TCH), F32)],
        compiler_params=_params(("arbitrary", "arbitrary")),
    )(glu, glu, w, b)


def _ln_silu(c1, lg, lb):
    mu = jnp.mean(c1, -1, keepdims=True)
    xc = c1 - mu
    rs = lax.rsqrt(jnp.mean(xc * xc, -1, keepdims=True) + LN_EPS)
    xh = xc * rs
    c2 = xh * lg + lb
    sg = jax.nn.sigmoid(c2)
    return xh, rs, c2, sg


def _mix_out(attn, c1, gates, h0, w_ap, w_cp, w_out, lg, lb, b_cp, g_post, g_ffn, tr):
    R = attn.shape[0]

    def body(at_ref, c1_ref, ga_ref, gc_ref, h0_ref, wap, wcp, wo, lg_ref, lb_ref, bcp, gp, gf,
             ao_ref, co_ref, c3_ref, mg_ref, mix_ref, h1_ref, n2_ref):
        ao = jnp.dot(at_ref[...], wap[...], preferred_element_type=F32)
        _, _, c2, sg = _ln_silu(c1_ref[...].astype(F32), lg_ref[...], lb_ref[...])
        c3 = (c2 * sg).astype(BF16)
        c3_ref[...] = c3
        co = jnp.dot(c3, wcp[...], preferred_element_type=F32) + bcp[...]
        ao_b, co_b = ao.astype(BF16), co.astype(BF16)
        ao_ref[...] = ao_b
        co_ref[...] = co_b
        merged = (jax.nn.sigmoid(ga_ref[...].astype(F32)) * ao_b.astype(F32)
                  + jax.nn.sigmoid(gc_ref[...].astype(F32)) * co_b.astype(F32)).astype(BF16)
        mg_ref[...] = merged
        mix = jnp.dot(merged, wo[...], preferred_element_type=F32).astype(BF16)
        mix_ref[...] = mix
        y, _ = _rms(mix.astype(F32), gp[...])
        h1 = h0_ref[...] + y
        h1_ref[...] = h1
        n2, _ = _rms(h1, gf[...])
        row = pl.program_id(0) * tr + lax.broadcasted_iota(jnp.int32, (tr, 1), 0)
        n2_ref[...] = jnp.where(row >= PAD, n2, 0.0).astype(BF16)

    vec = _const((1, D))
    return pl.pallas_call(
        body, name="mix_out", grid=(R // tr,),
        in_specs=[_row(tr, D), _row(tr, D), _row(tr, D, 0), _row(tr, D, 1), _row(tr, D),
                  _const((D, D)), _const((D, D)), _const((D, D)), vec, vec, vec, vec, vec],
        out_specs=[_row(tr, D)] * 7,
        out_shape=[_sds((R, D), BF16)] * 5 + [_sds((R, D), F32), _sds((R, D), BF16)],
        compiler_params=_params(("arbitrary",)),
    )(attn, c1, gates, gates, h0, w_ap, w_cp, w_out, lg, lb, b_cp, g_post, g_ffn)


FFN_CH = 256
N_CHIPS = 4
UPQ = 2 * FFN // N_CHIPS
UP_CHUNKS = [(q, c0, min(c0 + 512, UPQ)) for q in range(N_CHIPS // 2) for c0 in range(0, UPQ, 512)]
TINY_ROWS, TINY_CONV, TINY_FFN, TINY_META = 64, 0, 32, 40


def _shift_down(x, k, halo):
    tr = x.shape[0]
    row = lax.broadcasted_iota(jnp.int32, (tr, 1), 0)
    y = pltpu.roll(x, k, 0)
    for j in range(k):
        y = jnp.where(row == j, halo[8 - k + j:8 - k + j + 1, :], y)
    return y


def _shift_up(x, k, halo):
    tr = x.shape[0]
    row = lax.broadcasted_iota(jnp.int32, (tr, 1), 0)
    y = pltpu.roll(x, tr - k, 0)
    for j in range(k):
        y = jnp.where(row == tr - k + j, halo[j:j + 1, :], y)
    return y


def _conv3(x, halo, w, b):
    return w[2:3, :] * x + w[1:2, :] * _shift_down(x, 1, halo) + w[0:1, :] * _shift_down(x, 2, halo) + b


def _ffn_up(n2, w_up, fw, fb, tr):
    R = n2.shape[0]

    def body(n_ref, w_ref, fw_ref, fb_ref, up_ref, act_ref, carry):
        @pl.when(pl.program_id(0) == 0)
        def _():
            carry[...] = jnp.zeros_like(carry)

        nb = n_ref[...]
        for q, c0, c1 in UP_CHUNKS:
            us = []
            for qq in (q, q + N_CHIPS // 2):
                cs = slice(qq * UPQ + c0, qq * UPQ + c1)
                x = jnp.dot(nb, w_ref[qq, :, c0:c1], preferred_element_type=F32).astype(BF16)
                up_ref[:, cs] = x
                x = x.astype(F32)
                us.append(_conv3(x, carry[:, cs], fw_ref[qq, TINY_FFN:TINY_FFN + 8, c0:c1], fb_ref[:, cs]))
                carry[:, cs] = x[tr - 8:tr, :]
            act_ref[:, q * UPQ + c0:q * UPQ + c1] = (us[0] * jax.nn.sigmoid(us[0]) * us[1]).astype(BF16)

    return pl.pallas_call(
        body, name="ffn_up", grid=(R // tr,),
        in_specs=[_row(tr, D), _const((N_CHIPS, D, UPQ)), _const((N_CHIPS, TINY_ROWS, UPQ)), _const((1, 2 * FFN))],
        out_specs=[_row(tr, 2 * FFN), _row(tr, FFN)],
        out_shape=[_sds((R, 2 * FFN), BF16), _sds((R, FFN), BF16)],
        scratch_shapes=[pltpu.VMEM((8, 2 * FFN), F32)],
        compiler_params=_params(("arbitrary",)),
    )(n2, w_up, fw, fb)


def _ffn_down(act, w_down, h1, tgt, g_post, tr):
    R = act.shape[0]
    m = tr // BLK

    def body(a_ref, w_ref, h1_ref, g_ref, *rest):
        t_refs, (dh2_ref, dffn_ref, loss_ref, dg_ref) = rest[:m], rest[m:]

        @pl.when(pl.program_id(0) == 0)
        def _():
            loss_ref[...] = jnp.zeros_like(loss_ref)
            dg_ref[...] = jnp.zeros_like(dg_ref)

        f = jnp.dot(a_ref[...], w_ref[...], preferred_element_type=F32)
        g = g_ref[...]
        y, r = _rms(f, g)
        row = pl.program_id(0) * tr + lax.broadcasted_iota(jnp.int32, (tr, 1), 0)
        tgt_rows = jnp.concatenate([t[...] for t in t_refs], 0)
        e = jnp.where(row >= BLK, h1_ref[...] + y - tgt_rows, 0.0)
        loss_ref[...] += _colsum(e * e) * (0.5 / D)
        dy = e * (1.0 / D)
        dh2_ref[...] = dy
        dffn_ref[...] = _rms_bwd(dy, f, r, g).astype(BF16)
        dg_ref[...] += _colsum(dy * f * r)

    return pl.pallas_call(
        body, name="ffn_down", grid=(R // tr,),
        in_specs=[_row(tr, FFN), _const((FFN, D)), _row(tr, D), _const((1, D))]
                 + [pl.BlockSpec((BLK, D), lambda i, k=k: (jnp.maximum(m * i - 1 + k, 0), 0)) for k in range(m)],
        out_specs=[_row(tr, D), _row(tr, D), _const((1, D)), _const((1, D))],
        out_shape=[_sds((R, D), F32), _sds((R, D), BF16), _sds((1, D), F32), _sds((1, D), F32)],
        compiler_params=_params(("arbitrary",)),
    )(act, w_down, h1, g_post, *([tgt] * m))


def _ffn_bwd_act(dffn, w_down, up, fw, fb, tr):
    R = dffn.shape[0]
    nt = R // tr

    def body(d_ref, w_ref, up_ref, hal_ref, fw_ref, fb_ref, dup_ref, dfw_ref, dfb_ref, carry):
        t = pl.program_id(0)
        i = nt - 1 - t

        @pl.when(t == 0)
        def _():
            carry[...] = jnp.zeros_like(carry)
            dfw_ref[...] = jnp.zeros_like(dfw_ref)
            dfb_ref[...] = jnp.zeros_like(dfb_ref)

        dff = d_ref[...]
        row = i * tr + lax.broadcasted_iota(jnp.int32, (tr, 1), 0)
        first = i == 0
        for q, c0, c1 in UP_CHUNKS:
            dact = lax.dot_general(dff, w_ref[q * UPQ + c0:q * UPQ + c1, :], NT_DIMS, preferred_element_type=F32)
            chips = (q, q + N_CHIPS // 2)
            xs, us = [], []
            for qq in chips:
                cs = slice(qq * UPQ + c0, qq * UPQ + c1)
                x = up_ref[:, cs].astype(F32)
                halo = jnp.where(first, 0.0, hal_ref[:, cs].astype(F32))
                x1, x2 = _shift_down(x, 1, halo), _shift_down(x, 2, halo)
                w = fw_ref[qq, TINY_FFN:TINY_FFN + 8, c0:c1]
                us.append(w[2:3, :] * x + w[1:2, :] * x1 + w[0:1, :] * x2 + fb_ref[:, cs])
                xs.append((x, x1, x2))
            sg = jax.nn.sigmoid(us[0])
            silu = us[0] * sg
            dus = [dact * us[1] * sg * (1.0 + us[0] * (1.0 - sg)), dact * silu]
            for (x, x1, x2), du, qq in zip(xs, dus, chips):
                cs = slice(qq * UPQ + c0, qq * UPQ + c1)
                w = fw_ref[qq, TINY_FFN:TINY_FFN + 8, c0:c1]
                nxt = carry[:, cs]
                dx = w[2:3, :] * du + w[1:2, :] * _shift_up(du, 1, nxt) + w[0:1, :] * _shift_up(du, 2, nxt)
                dup_ref[:, cs] = jnp.where(row >= PAD, dx, 0.0).astype(BF16)
                dfw_ref[qq, 0:1, c0:c1] += _colsum(x2 * du)
                dfw_ref[qq, 1:2, c0:c1] += _colsum(x1 * du)
                dfw_ref[qq, 2:3, c0:c1] += _colsum(x * du)
                dfb_ref[:, cs] += _colsum(du)
                carry[:, cs] = du[0:8, :]

    halo_spec = pl.BlockSpec((8, 2 * FFN), lambda t: (jnp.maximum((nt - 1 - t) * (tr // 8) - 1, 0), 0))
    return pl.pallas_call(
        body, name="ffn_bwd_act", grid=(nt,),
        in_specs=[_rrow(tr, D, nt), _const((FFN, D)), _rrow(tr, 2 * FFN, nt), halo_spec,
                  _const((N_CHIPS, TINY_ROWS, UPQ)), _const((1, 2 * FFN))],
        out_specs=[_rrow(tr, 2 * FFN, nt), _const((N_CHIPS, 8, UPQ)), _const((1, 2 * FFN))],
        out_shape=[_sds((R, 2 * FFN), BF16), _sds((N_CHIPS, 8, UPQ), F32), _sds((1, 2 * FFN), F32)],
        scratch_shapes=[pltpu.VMEM((8, 2 * FFN), F32)],
        compiler_params=_params(("arbitrary",)),
    )(dffn, w_down, up, up, fw, fb)


def _ffn_bwd_in(dup, w_up, h1, dh2, mix, g_ffn, g_post, tr):
    R = dup.shape[0]

    def body(d_ref, w_ref, h1_ref, dh2_ref, mix_ref, gf_ref, gp_ref, dh1_ref, dmix_ref, dgf_ref, dgp_ref):
        @pl.when(pl.program_id(0) == 0)
        def _():
            dgf_ref[...] = jnp.zeros_like(dgf_ref)
            dgp_ref[...] = jnp.zeros_like(dgp_ref)

        dn2 = sum(lax.dot_general(d_ref[:, q * UPQ:(q + 1) * UPQ], w_ref[q], NT_DIMS, preferred_element_type=F32)
                  for q in range(N_CHIPS))
        h1 = h1_ref[...]
        _, r2 = _rms(h1, gf_ref[...])
        dh1 = dh2_ref[...] + _rms_bwd(dn2, h1, r2, gf_ref[...])
        dgf_ref[...] += _colsum(dn2 * h1 * r2)
        dh1_ref[...] = dh1
        m = mix_ref[...].astype(F32)
        _, rm = _rms(m, gp_ref[...])
        dmix_ref[...] = _rms_bwd(dh1, m, rm, gp_ref[...]).astype(BF16)
        dgp_ref[...] += _colsum(dh1 * m * rm)

    vec = _const((1, D))
    return pl.pallas_call(
        body, name="ffn_bwd_in", grid=(R // tr,),
        in_specs=[_row(tr, 2 * FFN), _const((N_CHIPS, D, UPQ)), _row(tr, D), _row(tr, D), _row(tr, D), vec, vec],
        out_specs=[_row(tr, D), _row(tr, D), vec, vec],
        out_shape=[_sds((R, D), F32), _sds((R, D), BF16), _sds((1, D), F32), _sds((1, D), F32)],
        compiler_params=_params(("arbitrary",)),
    )(dup, w_up, h1, dh2, mix, g_ffn, g_post)


def _mix_bwd(dmix, ao, co, gates, c1, w_out, w_ap, w_cp, lg, lb, tr):
    R = dmix.shape[0]

    def body(dm_ref, ao_ref, co_ref, ga_ref, gc_ref, c1_ref, wo, wap, wcp, lg_ref, lb_ref,
             dao_ref, dco_ref, dgate_ref, dattn_ref, dc1_ref, dbcp_ref, dlg_ref, dlb_ref, dcb_ref):
        @pl.when(pl.program_id(0) == 0)
        def _():
            for ref in (dbcp_ref, dlg_ref, dlb_ref, dcb_ref):
                ref[...] = jnp.zeros_like(ref)

        dmg = lax.dot_general(dm_ref[...], wo[...], NT_DIMS, preferred_element_type=F32)
        sa = jax.nn.sigmoid(ga_ref[...].astype(F32))
        sc = jax.nn.sigmoid(gc_ref[...].astype(F32))
        dao = (dmg * sa).astype(BF16)
        dco = (dmg * sc).astype(BF16)
        dao_ref[...] = dao
        dco_ref[...] = dco
        dgate_ref[:, 0:D] = (dmg * ao_ref[...].astype(F32) * sa * (1.0 - sa)).astype(BF16)
        dgate_ref[:, D:2 * D] = (dmg * co_ref[...].astype(F32) * sc * (1.0 - sc)).astype(BF16)
        dbcp_ref[...] += _colsum(dco.astype(F32))
        dattn_ref[...] = lax.dot_general(dao, wap[...], NT_DIMS, preferred_element_type=F32).astype(BF16)
        dc3 = lax.dot_general(dco, wcp[...], NT_DIMS, preferred_element_type=F32)
        xh, rs, c2, sg = _ln_silu(c1_ref[...].astype(F32), lg_ref[...], lb_ref[...])
        dc2 = dc3 * sg * (1.0 + c2 * (1.0 - sg))
        dlg_ref[...] += _colsum(dc2 * xh)
        dlb_ref[...] += _colsum(dc2)
        dxh = dc2 * lg_ref[...]
        dc1 = rs * (dxh - jnp.mean(dxh, -1, keepdims=True) - xh * jnp.mean(dxh * xh, -1, keepdims=True))
        dc1_ref[...] = dc1
        dcb_ref[...] += _colsum(dc1)

    vec = _const((1, D))
    return pl.pallas_call(
        body, name="mix_bwd", grid=(R // tr,),
        in_specs=[_row(tr, D), _row(tr, D), _row(tr, D), _row(tr, D, 0), _row(tr, D, 1), _row(tr, D),
                  _const((D, D)), _const((D, D)), _const((D, D)), vec, vec],
        out_specs=[_row(tr, D), _row(tr, D), _row(tr, 2 * D), _row(tr, D), _row(tr, D), vec, vec, vec, vec],
        out_shape=[_sds((R, D), BF16), _sds((R, D), BF16), _sds((R, 2 * D), BF16), _sds((R, D), BF16),
                   _sds((R, D), F32)] + [_sds((1, D), F32)] * 4,
        compiler_params=_params(("arbitrary",)),
    )(dmix, ao, co, gates, gates, c1, w_out, w_ap, w_cp, lg, lb)


def _conv_bwd(dc1, glu, w, tr, hook=None):
    R = dc1.shape[0]
    nt, nc = R // tr, D // CONV_TCH

    def body(d_ref, a_ref, g_ref, w_ref, dglu_a, dglu_g, dw_ref, buf, dw_acc):
        t = pl.program_id(1)
        i = nt - 1 - t

        @pl.when(t == 0)
        def _():
            buf[tr:tr + HALO, :] = jnp.zeros((HALO, CONV_TCH), F32)
            dw_acc[...] = jnp.zeros_like(dw_acc)

        @pl.when(t > 0)
        def _():
            buf[tr:tr + HALO, :] = buf[0:HALO, :]

        buf[0:tr, :] = d_ref[...]
        for r0 in range(0, tr, CONV_SUB):
            rs = slice(r0, r0 + CONV_SUB)
            row = i * tr + r0 + lax.broadcasted_iota(jnp.int32, (CONV_SUB, 1), 0)
            a, g = a_ref[rs, :].astype(F32), g_ref[rs, :].astype(F32)
            sg = jax.nn.sigmoid(g)
            glu = jnp.where(row >= PAD, a * sg, 0.0)
            acc = jnp.zeros((CONV_SUB, CONV_TCH), F32)
            for k, win in _tap_windows(buf, r0, lambda k: CONV_K - 1 - k):
                acc = acc + w_ref[k:k + 1, :] * win
                dw_acc[k] += jnp.sum((glu * win).reshape(CONV_SUB // 8, 8, CONV_TCH), axis=0)
            dglu = jnp.where(row >= PAD, acc, 0.0)
            dglu_a[rs, :] = (dglu * sg).astype(BF16)
            dglu_g[rs, :] = (dglu * a * sg * (1.0 - sg)).astype(BF16)

        @pl.when(t == nt - 1)
        def _():
            dw_ref[...] = jnp.sum(dw_acc[...], axis=1)

    def rspec(col0):
        return pl.BlockSpec((tr, CONV_TCH), lambda c, t: (nt - 1 - t, col0 + c))

    def steps():
        c, t = pl.program_id(0), pl.program_id(1)
        return (c == 0) & (t == 0), False, (c == nc - 1) & (t == nt - 1)

    return _call(
        body, hook, steps, name="conv_bwd", grid=(nc, nt),
        in_specs=[rspec(0), rspec(0), rspec(nc), pl.BlockSpec((None, HALO, CONV_TCH), lambda c, t: (c, 0, 0))],
        out_specs=[rspec(0), rspec(0), pl.BlockSpec((None, HALO, CONV_TCH), lambda c, t: (c, 0, 0))],
        out_shape=[_sds((R, D), BF16), _sds((R, D), BF16), _sds((N_CHIPS, HALO, CONV_TCH), F32)],
        scratch_shapes=[pltpu.VMEM((tr + HALO, CONV_TCH), F32), pltpu.VMEM((HALO, 8, CONV_TCH), F32)],
        operands=(dc1, glu, glu, w), semantics=("arbitrary", "arbitrary"))


def _attn_bwd(qkv, do, lse, sinks, rope, hook=None):
    R = qkv.shape[0]
    nb = R // BLK

    def body(s_ref, q_ref, k0, kp, kc, v0, vp, vc, do_ref, lse_ref, c_ref, sa_ref, sb_ref,
             dqkv_ref, dsink_ref, car_k, car_v, met_k, met_v):
        t = pl.program_id(0)
        n = nb - 1 - t

        @pl.when(t == 0)
        def _():
            for ref in (car_k, car_v, met_k, met_v, dsink_ref):
                ref[...] = jnp.zeros_like(ref)

        lane = lax.broadcasted_iota(jnp.int32, (1, BLK), 1)
        low = lane < HD
        kd, vd = _dup_heads(k0, kp, kc, low), _dup_heads(v0, vp, vc, low)
        kd_t = _dup_heads(k0, kp, kc, low, transposed=True)
        mask = _attn_mask(n, keys_first=True)
        tabs = (c_ref[...], sa_ref[...], sb_ref[...])
        zero = jnp.zeros((), BF16)
        dk_acc, dv_acc = [], []
        dsink = jnp.zeros((1, BLK), F32)
        for g in range(NKV):
            q_tiles, do_tiles = [], []
            for pair in range(GH // 2 * g, GH // 2 * (g + 1)):
                cs = slice(pair * BLK, (pair + 1) * BLK)
                qp, dop = q_ref[:, cs] * jnp.asarray(HD ** -0.5, BF16), do_ref[:, cs]
                q_tiles += [jnp.where(low, qp, zero), jnp.where(low, zero, qp)]
                do_tiles += [jnp.where(low, dop, zero), jnp.where(low, zero, dop)]
            qs, dos = jnp.concatenate(q_tiles, 0), jnp.concatenate(do_tiles, 0)
            st = lax.dot_general(kd[g], qs, NT_DIMS, preferred_element_type=F32)
            dpt = lax.dot_general(vd[g], dos, NT_DIMS, preferred_element_type=F32)
            ps, dss = [], []
            for j in range(GH):
                h = GH * g + j
                cs = slice(j * BLK, (j + 1) * BLK)
                lse_h = lse_ref[h:h + 1, :]
                p = jnp.where(mask, jnp.exp(st[:, cs] - lse_h), 0.0)
                dp = dpt[:, cs]
                delta = jnp.sum(p * dp, 0, keepdims=True)
                ps.append(p.astype(BF16))
                dss.append((p * (dp - delta)).astype(BF16))
                dsink = dsink + jnp.where(lane == h, -jnp.sum(jnp.exp(s_ref[h] - lse_h) * delta), 0.0)
            ds_t, p_t = jnp.concatenate(dss, 1), jnp.concatenate(ps, 1)
            dk_acc.append(jnp.dot(ds_t, qs, preferred_element_type=F32))
            dv_acc.append(jnp.dot(p_t, dos, preferred_element_type=F32))
            dq_t = jnp.dot(kd_t[g], ds_t, preferred_element_type=F32) * (HD ** -0.5)
            for j in range(GH // 2):
                pair = GH // 2 * g + j
                dq = [dq_t[:, h * BLK:(h + 1) * BLK].T for h in (2 * j, 2 * j + 1)]
                dqkv_ref[:, pair * BLK:(pair + 1) * BLK] = _rope_bwd(jnp.where(low, dq[0], dq[1]), *tabs).astype(BF16)
        dsink_ref[0:1, :] += dsink

        def fold(acc):
            tot = [a + pltpu.roll(a, HD, 1) for a in acc]
            return jnp.where(low, tot[0], tot[1])

        dk_all, dv_all = fold(dk_acc), fold(dv_acc)
        met_k[...] += dk_all[0:BLK, :]
        met_v[...] += dv_all[0:BLK, :]
        last = jnp.where(n == 0, 1.0, 0.0)
        dk_n = dk_all[2 * BLK:3 * BLK, :] + car_k[...] + last * met_k[...]
        dv_n = dv_all[2 * BLK:3 * BLK, :] + car_v[...] + last * met_v[...]
        dqkv_ref[:, D:D + BLK] = _rope_bwd(dk_n, *tabs).astype(BF16)
        dqkv_ref[:, D + BLK:QKV_W] = dv_n.astype(BF16)
        car_k[...] = dk_all[BLK:2 * BLK, :]
        car_v[...] = dv_all[BLK:2 * BLK, :]

    rblk = lambda w: pl.BlockSpec((BLK, w), lambda t: (nb - 1 - t, 0))
    return _call(
        body, hook, _steps_1d(nb, nb), name="attn_bwd", grid=(nb,),
        in_specs=[pl.BlockSpec(memory_space=pltpu.SMEM), rblk(D)] + _kv_specs(nb, True)
                 + [rblk(D), pl.BlockSpec((NH, BLK), lambda t: (nb - 1 - t, 0)), rblk(BLK), rblk(BLK), rblk(BLK)],
        out_specs=[rblk(QKV_W), _const((8, BLK))],
        out_shape=[_sds((R, QKV_W), BF16), _sds((8, BLK), F32)],
        scratch_shapes=[pltpu.VMEM((BLK, BLK), F32)] * 4,
        operands=(sinks, qkv, *([qkv] * 6), do, lse, *rope), semantics=("arbitrary",))


def _in_bwd(dproj, w_in, h0, dh1, g_pre, tr, hook=None):
    R = h0.shape[0]
    n = len(dproj)
    widths = [p.shape[1] for p in dproj]
    starts = [sum(widths[:j]) for j in range(n)]

    def body(*refs):
        d_refs, (w_ref, h0_ref, dh1_ref, g_ref, dh0_ref, dg_ref, db_ref) = refs[:n], refs[n:]

        @pl.when(pl.program_id(0) == 0)
        def _():
            dg_ref[...] = jnp.zeros_like(dg_ref)
            db_ref[...] = jnp.zeros_like(db_ref)

        dn1 = jnp.zeros((tr, D), F32)
        for d_ref, c0, wd in zip(d_refs, starts, widths):
            d = d_ref[...]
            dn1 = dn1 + jnp.dot(d, w_ref[c0:c0 + wd, :], preferred_element_type=F32)
            db_ref[:, c0:c0 + wd] += _colsum(d.astype(F32))
        h0 = h0_ref[...]
        _, r = _rms(h0, g_ref[...])
        dh0_ref[...] = dh1_ref[...] + _rms_bwd(dn1, h0, r, g_ref[...])
        dg_ref[...] += _colsum(dn1 * h0 * r)

    return _call(
        body, hook, _steps_1d(R // tr, R // tr), name="in_bwd", grid=(R // tr,),
        in_specs=[_row(tr, wd) for wd in widths] + [_const((IN_W, D)), _row(tr, D), _row(tr, D), _const((1, D))],
        out_specs=[_row(tr, D), _const((1, D)), _const((1, IN_W))],
        out_shape=[_sds((R, D), F32), _sds((1, D), F32), _sds((1, IN_W), F32)],
        operands=(*dproj, w_in, h0, dh1, g_pre), semantics=("arbitrary",))


def _dw(a, b, name, tn, tr, by_chip=False, ta=None):
    R, ka = a.shape
    n = b.shape[1]
    nt = R // tr
    ta = ta or ka
    if by_chip:
        out_spec = pl.BlockSpec((None, ta, tn), lambda k, j, i: (j, k, 0))
        out_shape = _sds((n // tn, ka, tn), BF16)
    else:
        out_spec = pl.BlockSpec((ta, tn), lambda k, j, i: (k, j))
        out_shape = _sds((ka, n), BF16)

    def body(a_ref, b_ref, o_ref, acc):
        i = pl.program_id(2)

        @pl.when(i == 0)
        def _():
            acc[...] = jnp.zeros_like(acc)

        acc[...] += lax.dot_general(a_ref[...], b_ref[...], TN_DIMS, preferred_element_type=F32)

        @pl.when(i == nt - 1)
        def _():
            o_ref[...] = acc[...].astype(BF16)

    return pl.pallas_call(
        body, name=name, grid=(ka // ta, n // tn, nt),
        in_specs=[pl.BlockSpec((tr, ta), lambda k, j, i: (i, k)), pl.BlockSpec((tr, tn), lambda k, j, i: (i, j))],
        out_specs=out_spec, out_shape=out_shape,
        scratch_shapes=[pltpu.VMEM((ta, tn), F32)],
        compiler_params=_params(("arbitrary", "arbitrary", "arbitrary")),
    )(a, b)


SMALL = ["norm_pre_mix", "norm_post_mix", "b_in", "attn_sinks", "conv_dw_b", "conv_ln_g", "conv_ln_b",
         "b_conv_proj", "norm_pre_ffn", "norm_post_ffn", "ffn_dw_b"]


def local_step(x, tgt, W, dist=None):
    W = dict(W)
    S = x.shape[0]
    R = S + BLK
    tr = _tile(R, 384, BLK)
    trw = _tile(R, 1056)
    rope = _rope_tables(R)
    meta = _cols_joined(W["tiny"][:, TINY_META:TINY_META + NMETA, 0:DQ])
    h0 = jnp.concatenate([jnp.zeros((PAD, D), F32), meta, x], 0)

    qkv, glu, gates, n1, *got = _in_proj(h0, W["norm_pre_mix"], W["w_in"], W["b_in"], rope, tr,
                                         dist and dist.gather_hook(BRANCH_SHARES))
    if dist:
        W.update(dist.weights(BRANCH_SHARES, got))
    sinks = W["attn_sinks"].reshape(NH)
    attn, lse, *got = _attn_fwd(qkv, sinks, dist and dist.gather_hook(FFN_SHARES))
    if dist:
        W.update(dist.weights(FFN_SHARES, got))
    c1 = _conv_fwd(glu, W["tiny"], W["conv_dw_b"], tr)
    ao, co, c3, merged, mix, h1, n2 = _mix_out(
        attn, c1, gates, h0, W["w_attn_proj"], W["w_conv_proj"], W["w_out"], W["conv_ln_g"], W["conv_ln_b"],
        W["b_conv_proj"], W["norm_post_mix"], W["norm_pre_ffn"], tr)
    up, act = _ffn_up(n2, W["w_up"], W["tiny"], W["ffn_dw_b"], tr)
    dh2, dffn, loss_cols, dg_post_ffn = _ffn_down(act, W["w_down"], h1, tgt, W["norm_post_ffn"], tr)

    dw_down = _dw(act, dffn, "dw_down", 512, trw)
    dup, dfw, dfb = _ffn_bwd_act(dffn, W["w_down"], up, W["tiny"], W["ffn_dw_b"], tr)
    dw_up = _dw(n2, dup, "dw_up", UPQ, trw, by_chip=True)
    ffn_sums = dist and dist.pair_sums(FFN_SHARES, [dw_up, dw_down.reshape(N_CHIPS, -1, D)], "ffn")
    dh1, dmix, dg_pre_ffn, dg_post_mix = _ffn_bwd_in(dup, W["w_up"], h1, dh2, mix, W["norm_pre_ffn"],
                                                      W["norm_post_mix"], tr)
    dao, dco, dgates, dattn, dc1, db_cp, dlg, dlb, dcb = _mix_bwd(
        dmix, ao, co, gates, c1, W["w_out"], W["w_attn_proj"], W["w_conv_proj"], W["conv_ln_g"], W["conv_ln_b"], tr)
    dglu_a, dglu_g, dcw, *ffn_got = _conv_bwd(dc1, glu, W["tiny"], tr, dist and dist.chip_hook(FFN_SHARES, ffn_sums))
    branch = [_dw(attn, dao, "dw_attn_proj", D, trw), _dw(c3, dco, "dw_conv_proj", D, trw),
              _dw(merged, dmix, "dw_out", D, trw),
              _tiny_pack({"conv_dw_w": dcw, "ffn_dw_w": dfw, "meta_tokens": jnp.zeros((N_CHIPS, NMETA, DQ), F32)})]
    branch_sums = dist and dist.pair_sums(BRANCH_TINY_SHARES, [a.reshape(N_CHIPS, -1, a.shape[-1]) for a in branch], "branch")
    dqkv, dsink, *got = _attn_bwd(qkv, dattn, lse, sinks, rope,
                                  dist and dist.chip_hook(BRANCH_TINY_SHARES, branch_sums))
    if dist:
        dist.finish(FFN_SHARES + BRANCH_TINY_SHARES, ffn_sums + branch_sums, ffn_got + got, "ffn_branch")
    dproj = [dqkv, dglu_a, dglu_g, dgates]
    dw_in = jnp.concatenate([_dw(p, n1, "dw_in_%d" % j, D, trw, ta=_tile(p.shape[1], D, BLK))
                             for j, p in enumerate(dproj)], 0)
    in_sums = dist and dist.pair_sums(IN_SHARES, [dw_in.reshape(N_CHIPS, -1, D)], "in")
    dh0, dg_pre_mix, db_in, *got = _in_bwd(dproj, W["w_in"], h0, dh1, W["norm_pre_mix"], tr,
                                           dist and dist.chip_hook(IN_SHARES, in_sums))
    if dist:
        dist.finish(IN_SHARES, in_sums, got, "in")

    grads = {
        "w_in": dw_in, "w_attn_proj": branch[0], "w_conv_proj": branch[1], "w_out": branch[2],
        "w_up": dw_up,
        "w_down": dw_down,
        "tiny": branch[3],
        "meta_tokens": dh0[PAD:BLK],
        "norm_pre_mix": dg_pre_mix, "norm_post_mix": dg_post_mix, "b_in": db_in,
        "attn_sinks": dsink[0:1, 0:NH], "conv_dw_b": dcb, "conv_ln_g": dlg, "conv_ln_b": dlb,
        "b_conv_proj": db_cp, "norm_pre_ffn": dg_pre_ffn, "norm_post_ffn": dg_post_ffn, "ffn_dw_b": dfb,
    }
    return loss_cols, dh0[BLK:], grads


INQ = IN_W // N_CHIPS
DQ = D // N_CHIPS
SHARES = [("w_in", INQ, D, BF16), ("w_attn_proj", DQ, D, BF16), ("w_conv_proj", DQ, D, BF16), ("w_out", DQ, D, BF16),
          ("w_up", D, UPQ, BF16), ("w_down", FFN // N_CHIPS, D, BF16), ("tiny", TINY_ROWS, UPQ, F32)]
TINY_PARTS = [("conv_dw_w", TINY_CONV, CONV_K, TINY_FFN - TINY_CONV, DQ), ("ffn_dw_w", TINY_FFN, FFN_K, TINY_META - TINY_FFN, UPQ),
              ("meta_tokens", TINY_META, NMETA, NMETA, DQ)]


def _tiny_pack(parts):
    rows = []
    for name, _, _, reserved, _ in TINY_PARTS:
        a = parts[name].astype(F32)
        pad = [(0, 0)] * (a.ndim - 2) + [(0, reserved - a.shape[-2]), (0, UPQ - a.shape[-1])]
        rows.append(jnp.pad(a, pad))
    used = sum(r.shape[-2] for r in rows)
    rows.append(jnp.zeros(rows[0].shape[:-2] + (TINY_ROWS - used, UPQ), F32))
    return jnp.concatenate(rows, axis=-2)


def _tiny_unpack(tiny):
    return {name: tiny[..., r0:r0 + k, 0:cols] for name, r0, k, _, cols in TINY_PARTS}


def _cols_by_chip(a):
    rows, n = a.shape
    return a.reshape(rows, N_CHIPS, n // N_CHIPS).transpose(1, 0, 2)


def _cols_joined(a):
    _, rows, cols = a.shape
    return a.transpose(1, 0, 2).reshape(rows, N_CHIPS * cols)


def _to_planes(a, rows):
    return jnp.pad(a, [(0, rows * D - a.shape[-1])]).reshape(rows, D)


ANY = pl.BlockSpec(memory_space=pl.ANY)


def _place():
    x, y, c = lax.axis_index("x"), lax.axis_index("y"), lax.axis_index("c")
    chips = [(1 - x, y), (x, 1 - y), (1 - x, 1 - y)]
    return x, y, c, chips


def _rcopy(src, dst, ssem, rsem, to):
    return pltpu.make_async_remote_copy(src_ref=src, dst_ref=dst, send_sem=ssem, recv_sem=rsem,
                                        device_id=to, device_id_type=MESH)


def _halves(ref_or_rows, c):
    half = ref_or_rows // 2
    return pl.ds(c * half, half), pl.ds((1 - c) * half, half)


FIRST_SHARES, BRANCH_SHARES, FFN_SHARES = [0, 6], [1, 2, 3], [4, 5]
IN_SHARES, BRANCH_TINY_SHARES = [0], [1, 2, 3, 6]


def _gather_hook(own, idx):
    n = len(idx)

    def copies(kind, ins, outs, ssem, rsem):
        x, y, c, chips = _place()
        q = 2 * x + y
        sib = (x, y, 1 - c)
        out = []
        for i, a in enumerate(idx):
            mine, other = _halves(SHARES[a][1], c)
            for j, (cx, cy) in enumerate(chips):
                k, to = 3 * i + j, (cx, cy, c)
                landed, theirs = outs[i].at[2 * cx + cy, mine], outs[i].at[2 * cx + cy, other]
                if kind == "send":
                    out.append(_rcopy(ins[i].at[mine], outs[i].at[q, mine], ssem.at[k], rsem.at[k], to))
                elif kind == "landing":
                    out.append(_rcopy(ins[i].at[mine], landed, ssem.at[k], rsem.at[k], to))
                elif kind == "pass":
                    out.append(_rcopy(landed, landed, ssem.at[3 * n + k], rsem.at[3 * n + k], sib))
                else:
                    out.append(_rcopy(theirs, theirs, ssem.at[3 * n + k], rsem.at[3 * n + k], sib))
        return out

    def own_copies(ins, outs, ssem, rsem):
        x, y, c, _ = _place()
        q = 2 * x + y
        return [_rcopy(ins[i], outs[i].at[q], ssem.at[6 * n + i], rsem.at[6 * n + i], (x, y, 1 - c)) for i in range(n)]

    def start(*refs):
        for cp in copies("send", *refs) + own_copies(*refs):
            cp.start()

    def mid(*refs):
        for landed, cp in zip(copies("landing", *refs), copies("pass", *refs)):
            landed.wait_recv()
            cp.start()

    def finish(*refs):
        for cp in copies("arrival", *refs):
            cp.wait_recv()
        for cp in copies("send", *refs) + copies("pass", *refs):
            cp.wait_send()
        for cp in own_copies(*refs):
            cp.wait()

    shapes = [_sds((N_CHIPS,) + SHARES[a][1:3], SHARES[a][3]) for a in idx]
    return _Hook(own, shapes, 7 * n, start, finish, mid)


def _chip_hook(sums, idx):
    def copies(ins, outs, ssem, rsem):
        x, y, c, chips = _place()
        return [_rcopy(ins[i].at[2 * cx + cy], outs[i].at[j], ssem.at[3 * i + j], rsem.at[3 * i + j], (cx, cy, c))
                for i in range(len(idx)) for j, (cx, cy) in enumerate(chips)]

    def start(*refs):
        for cp in copies(*refs):
            cp.start()

    def finish(*refs):
        for cp in copies(*refs):
            cp.wait()

    shapes = [_sds((N_CHIPS - 1, SHARES[a][1] // 2, SHARES[a][2]), SHARES[a][3]) for a in idx]
    return _Hook(sums, shapes, 3 * len(idx), start, finish)


def _sibling_swap(parts, idx, tag):
    def copies(ins, outs, ssem, rsem):
        x, y, c, _ = _place()
        return [_rcopy(ins[i].at[:, _halves(SHARES[a][1], c)[1]], outs[i], ssem.at[i], rsem.at[i], (x, y, 1 - c))
                for i, a in enumerate(idx)]

    def start(*refs):
        for cp in copies(*refs):
            cp.start()

    def finish(*refs):
        for cp in copies(*refs):
            cp.wait()

    shapes = [_sds((N_CHIPS, SHARES[a][1] // 2, SHARES[a][2]), SHARES[a][3]) for a in idx]
    return _alone(_Hook(parts, shapes, len(idx), start, finish), "sibling_swap_" + tag)


def _sum_pair(parts, recvs, c, idx, tag):
    steps, n = 2, len(idx)

    def body(c_ref, *refs):
        for i, a in enumerate(idx):
            refs[2 * n + i][...] = (refs[i][...].astype(F32) + refs[n + i][...].astype(F32)).astype(SHARES[a][3])

    own, got, out, views, shapes = [], [], [], [], []
    for p, a in zip(parts, idx):
        _, rows, cols, dt = SHARES[a]
        blk = rows // 2 // steps
        own.append(pl.BlockSpec((None, None, blk, cols), lambda q, i, c_ref: (q, c_ref[0], i, 0)))
        got.append(pl.BlockSpec((None, blk, cols), lambda q, i, c_ref: (q, i, 0)))
        out.append(pl.BlockSpec((None, blk, cols), lambda q, i, c_ref: (q, i, 0)))
        views.append(p.reshape(N_CHIPS, 2, rows // 2, cols))
        shapes.append(_sds((N_CHIPS, rows // 2, cols), dt))
    grid_spec = pltpu.PrefetchScalarGridSpec(num_scalar_prefetch=1, grid=(N_CHIPS, steps),
                                             in_specs=own + got, out_specs=out)
    return pl.pallas_call(body, name="sum_pair_" + tag, grid_spec=grid_spec, out_shape=shapes,
                          compiler_params=_params(("arbitrary", "arbitrary")))(c, *views, *recvs)


def _sum_chips(sums, recvs, qc, idx, tag):
    steps, n = 2, len(idx)

    def body(qc_ref, *refs):
        for i in range(n):
            acc = refs[i][...].astype(F32)
            for j in range(1, N_CHIPS):
                acc = acc + refs[j * n + i][...].astype(F32)
            refs[N_CHIPS * n + i][...] = acc

    own, got, out, shapes = [], [[], [], []], [], []
    for a in idx:
        _, rows, cols, _ = SHARES[a]
        blk = rows // 2 // steps
        own.append(pl.BlockSpec((None, blk, cols), lambda i, qc_ref: (qc_ref[0], i, 0)))
        for j in range(N_CHIPS - 1):
            got[j].append(pl.BlockSpec((None, blk, cols), lambda i, qc_ref, j=j: (j, i, 0)))
        out.append(pl.BlockSpec((None, blk, cols), lambda i, qc_ref: (qc_ref[1], i, 0)))
        shapes.append(_sds((2, rows // 2, cols), F32))
    grid_spec = pltpu.PrefetchScalarGridSpec(num_scalar_prefetch=1, grid=(steps,),
                                             in_specs=own + got[0] + got[1] + got[2], out_specs=out)
    return pl.pallas_call(body, name="sum_chips_" + tag, grid_spec=grid_spec, out_shape=shapes,
                          compiler_params=_params(("arbitrary",)))(qc, *sums, *recvs, *recvs, *recvs)


def _sibling_share(halves, idx, tag):
    n = len(idx)

    def body(*refs):
        outs, (ssem, rsem) = refs[n:2 * n], refs[2 * n:]
        x, y, c, _ = _place()
        copies = []
        for i in range(n):
            cp = _rcopy(outs[i].at[c], outs[i].at[c], ssem.at[i], rsem.at[i], (x, y, 1 - c))
            cp.start()
            copies.append(cp)
        for i in range(n):
            theirs = outs[i].at[1 - c]
            _rcopy(theirs, theirs, ssem.at[i], rsem.at[i], (x, y, 1 - c)).wait_recv()
        for cp in copies:
            cp.wait_send()

    return pl.pallas_call(
        body, name="sibling_share_" + tag, in_specs=[ANY] * n, out_specs=[ANY] * n,
        out_shape=[_sds((2, SHARES[a][1] // 2, SHARES[a][2]), F32) for a in idx],
        input_output_aliases={i: i for i in range(n)},
        scratch_shapes=[pltpu.SemaphoreType.DMA((n,)), pltpu.SemaphoreType.DMA((n,))],
    )(*halves)


class _Dist:
    def __init__(self, own):
        self.own = own
        self.core = lax.axis_index("c")
        self.chip = 2 * lax.axis_index("x") + lax.axis_index("y")
        self.reduced = {}

    def gather_hook(self, idx):
        return _gather_hook([self.own[a] for a in idx], idx)

    def weights(self, idx, gathered):
        out = {}
        for a, full in zip(idx, gathered):
            name = SHARES[a][0]
            out[name] = full if name in ("w_up", "tiny") else full.reshape(-1, D)
        return out

    def pair_sums(self, idx, parts, tag):
        return _sum_pair(parts, _sibling_swap(parts, idx, tag), self.core.reshape(1), idx, tag)

    def chip_hook(self, idx, sums):
        return _chip_hook(sums, idx)

    def finish(self, idx, sums, recvs, tag):
        halves = _sum_chips(sums, recvs, jnp.stack([self.chip, self.core]), idx, tag)
        for a, full in zip(idx, _sibling_share(halves, idx, tag)):
            self.reduced[SHARES[a][0]] = full.reshape(SHARES[a][1:3])


N_DEV = 8
SMALL_ROWS = 40


def _small_allreduce(sm):
    def body(s_ref, o_ref, buf, ssem, rsem):
        x, y, c, _ = _place()
        me = 4 * x + 2 * y + c
        buf[me] = s_ref[...]
        copies = []
        for d in range(1, N_DEV):
            dx, dy, dc = d >> 2, (d >> 1) & 1, d & 1
            to = (x ^ dx, y ^ dy, c ^ dc)
            cp = _rcopy(s_ref, buf.at[me], ssem.at[d - 1], rsem.at[d - 1], to)
            cp.start()
            copies.append(cp)
        for d in range(1, N_DEV):
            src = me ^ d
            _rcopy(s_ref, buf.at[src], ssem.at[d - 1], rsem.at[d - 1], (x, y, c)).wait_recv()
        for cp in copies:
            cp.wait_send()
        acc = buf[0]
        for k in range(1, N_DEV):
            acc = acc + buf[k]
        o_ref[...] = acc

    vm = pl.BlockSpec(memory_space=pltpu.VMEM)
    return pl.pallas_call(
        body, name="small_allreduce", in_specs=[vm], out_specs=vm,
        out_shape=_sds((SMALL_ROWS, D), F32),
        scratch_shapes=[pltpu.VMEM((N_DEV, SMALL_ROWS, D), F32),
                        pltpu.SemaphoreType.DMA((N_DEV - 1,)), pltpu.SemaphoreType.DMA((N_DEV - 1,))],
    )(sm)


SMALL_PLAN = [("norm_pre_mix", D), ("norm_post_mix", D), ("b_in", IN_W), ("attn_sinks", NH), ("conv_dw_b", D),
              ("conv_ln_g", D), ("conv_ln_b", D), ("b_conv_proj", D), ("norm_pre_ffn", D), ("norm_post_ffn", D),
              ("ffn_dw_b", 2 * FFN), ("loss", D), ("meta_tokens", NMETA * D)]


def _pack_small(parts):
    rows = [_to_planes(parts[name].reshape(-1), -(-n // D)) for name, n in SMALL_PLAN]
    used = sum(r.shape[0] for r in rows)
    return jnp.concatenate(rows + [jnp.zeros((SMALL_ROWS - used, D), F32)], 0)


def _unpack_small(packed):
    out, r0 = {}, 0
    for name, n in SMALL_PLAN:
        rows = -(-n // D)
        out[name] = packed[r0:r0 + rows].reshape(-1)[:n].reshape(1, n)
        r0 += rows
    return out


def _adamw_update(w_ref, g_ref, m_ref, v_ref, d_ref, nm_ref, nv_ref):
    g = g_ref[...]
    m = B1 * m_ref[...] + (1.0 - B1) * g
    v = B2 * v_ref[...] + (1.0 - B2) * (g * g)
    nm_ref[...] = m
    nv_ref[...] = v
    m_hat = m / (1.0 - B1 ** STEP)
    v_hat = v / (1.0 - B2 ** STEP)
    d_ref[...] = -LR * (m_hat / (jnp.sqrt(v_hat) + ADAM_EPS) + WD * w_ref[...])


def _adamw_vectors(ws, gs, ms, vs):
    n = len(ws)

    def body(*refs):
        for j in range(n):
            _adamw_update(*[refs[k * n + j] for k in range(7)])

    vm = pl.BlockSpec(memory_space=pltpu.VMEM)
    outs = pl.pallas_call(body, name="adamw_vectors", in_specs=[vm] * (4 * n), out_specs=[vm] * (3 * n),
                          out_shape=[_sds(w.shape, F32) for w in ws] * 3)(*ws, *gs, *ms, *vs)
    return outs[:n], outs[n:2 * n], outs[2 * n:]


def _adamw(w, g, m, v, name):
    rows, cols = w.shape
    tr = _tile(rows, 256, 8) if rows % 8 == 0 else rows

    def body(*refs):
        _adamw_update(*refs)

    spec = pl.BlockSpec((tr, cols), lambda i: (i, 0))
    return pl.pallas_call(
        body, name=name, grid=(rows // tr,), in_specs=[spec] * 4, out_specs=[spec] * 3,
        out_shape=[_sds((rows, cols), F32)] * 3, compiler_params=_params(("arbitrary",)),
    )(w, g, m, v)


NAMES = ["meta_tokens", "norm_pre_mix", "norm_post_mix", "w_in", "b_in", "attn_sinks", "w_attn_proj", "conv_dw_w",
         "conv_dw_b", "conv_ln_g", "conv_ln_b", "w_conv_proj", "b_conv_proj", "w_out", "norm_pre_ffn", "norm_post_ffn",
         "w_up", "ffn_dw_w", "ffn_dw_b", "w_down"]
MATMUL = ("w_in", "w_attn_proj", "w_conv_proj", "w_out", "w_up", "w_down")


def _two_d(a):
    return a.reshape(a.shape[-2:])


def kernel(x, meta_tokens, norm_pre_mix, norm_post_mix, w_in, b_in, attn_sinks, w_attn_proj, conv_dw_w, conv_dw_b, conv_ln_g, conv_ln_b, w_conv_proj, b_conv_proj, w_out, norm_pre_ffn, norm_post_ffn, w_up, ffn_dw_w, ffn_dw_b, w_down, loss_target, m_meta_tokens, m_norm_pre_mix, m_norm_post_mix, m_w_in, m_b_in, m_attn_sinks, m_w_attn_proj, m_conv_dw_w, m_conv_dw_b, m_conv_ln_g, m_conv_ln_b, m_w_conv_proj, m_b_conv_proj, m_w_out, m_norm_pre_ffn, m_norm_post_ffn, m_w_up, m_ffn_dw_w, m_ffn_dw_b, m_w_down, v_meta_tokens, v_norm_pre_mix, v_norm_post_mix, v_w_in, v_b_in, v_attn_sinks, v_w_attn_proj, v_conv_dw_w, v_conv_dw_b, v_conv_ln_g, v_conv_ln_b, v_w_conv_proj, v_b_conv_proj, v_w_out, v_norm_pre_ffn, v_norm_post_ffn, v_w_up, v_ffn_dw_w, v_ffn_dw_b, v_w_down):
    args = locals()
    w = {n: args[n] for n in NAMES}
    m = {n: args["m_" + n] for n in NAMES}
    v = {n: args["v_" + n] for n in NAMES}
    tiny_names = [part[0] for part in TINY_PARTS]
    big = list(MATMUL) + tiny_names

    def shard_2d(a, name):
        return _two_d(a).T if name == "w_in" else _two_d(a)

    own = {n: shard_2d(w[n], n).astype(BF16) for n in MATMUL}
    own["tiny"] = _tiny_pack({n: _two_d(w[n]) for n in tiny_names})
    dist = _Dist([own[n] for n, _, _, _ in SHARES])
    W = {n: _two_d(w[n]) for n in SMALL}
    W.update(dist.weights(FIRST_SHARES, _alone(dist.gather_hook(FIRST_SHARES), "gather_first")))

    loss_cols, grad_x, grads = local_step(x[0], loss_target[0], W, dist)

    small = dict(grads)
    small["loss"] = loss_cols
    g_small = _unpack_small(_small_allreduce(_pack_small(small)))
    loss = jnp.sum(g_small["loss"])
    g_big = {n: dist.reduced[n] for n in MATMUL}
    g_big.update(_tiny_unpack(dist.reduced["tiny"]))
    g_big["meta_tokens"] = lax.dynamic_slice(g_small["meta_tokens"].reshape(NMETA, D), (0, dist.chip * DQ), (NMETA, DQ))

    g, delta, new_m, new_v = {}, {}, {}, {}
    for n in big:
        shape = w[n].shape
        back = (lambda a: a.T.reshape(shape)) if n == "w_in" else (lambda a: a.reshape(shape))
        outs = _adamw(shard_2d(w[n], n), g_big[n], shard_2d(m[n], n), shard_2d(v[n], n), "adamw_" + n)
        g[n], delta[n], new_m[n], new_v[n] = (back(a) for a in (g_big[n], *outs))
    ud, um, uv = _adamw_vectors(*[[_two_d(d[n]) for n in SMALL] for d in (w, g_small, m, v)])
    for j, n in enumerate(SMALL):
        g[n], delta[n], new_m[n], new_v[n] = g_small[n], ud[j], um[j], uv[j]

    return (loss, grad_x[None], *[g[n] for n in NAMES], *[delta[n] for n in NAMES],
            *[new_m[n] for n in NAMES], *[new_v[n] for n in NAMES])
```

```python
import jax
import jax.numpy as jnp
from jax import lax
from jax.experimental import pallas as pl
from jax.experimental.pallas import tpu as pltpu

F32, BF16 = jnp.float32, jnp.bfloat16

D = 1024
NH, NKV, HD = 16, 2, 64
GH = NH // NKV
NMETA, BLK = 16, 128
PAD = BLK - NMETA
ROT = HD // 4
THETA = 500000.0
CONV_K = 31
FFN = 2816
FFN_K = 3
IN_W = 5376
QKV_W, GLU_W, GATE_W = 1280, 2048, 2048
RMS_EPS, LN_EPS, NEG = 1e-6, 1e-5, -1e30
LR, B1, B2, ADAM_EPS, WD, STEP = 0.001, 0.9, 0.999, 1e-08, 0.01, 10

VMEM_LIMIT = 56 * 2 ** 20
MESH = pl.DeviceIdType.MESH

NT_DIMS = (((1,), (1,)), ((), ()))
TN_DIMS = (((0,), (0,)), ((), ()))


def _params(sem, **kw):
    return pltpu.CompilerParams(dimension_semantics=sem, vmem_limit_bytes=VMEM_LIMIT, **kw)


def _tile(n, pref, mult=16):
    for t in range(min(pref, n), 0, -1):
        if n % t == 0 and t % mult == 0:
            return t
    return n


def _row(tr, w, col=0):
    return pl.BlockSpec((tr, w), lambda i: (i, col))


def _rrow(tr, w, nt, col=0):
    return pl.BlockSpec((tr, w), lambda t: (nt - 1 - t, col))


def _const(shape):
    return pl.BlockSpec(shape, lambda *_: (0,) * len(shape))


def _sds(shape, dt):
    return jax.ShapeDtypeStruct(shape, dt)


class _Hook:
    def __init__(self, operands, out_shape, n_sem, start, finish, mid=None):
        self.operands, self.out_shape, self.n_sem = list(operands), list(out_shape), n_sem
        self.start, self.mid, self.finish = start, mid, finish

    def scratch(self):
        return [pltpu.SemaphoreType.DMA((self.n_sem,)), pltpu.SemaphoreType.DMA((self.n_sem,))]


def _call(body, hook, steps, *, name, grid, in_specs, out_specs, out_shape, operands, semantics, scratch_shapes=()):
    in_specs, out_specs, out_shape = list(in_specs), list(out_specs), list(out_shape)
    if hook is None:
        return pl.pallas_call(body, name=name, grid=grid, in_specs=in_specs, out_specs=out_specs, out_shape=out_shape,
                              scratch_shapes=list(scratch_shapes), compiler_params=_params(semantics))(*operands)
    n_in, n_out, n_hi, n_ho = len(in_specs), len(out_specs), len(hook.operands), len(hook.out_shape)

    def wrapped(*refs):
        ins, hi = refs[:n_in], refs[n_in:n_in + n_hi]
        o0 = n_in + n_hi
        outs, ho = refs[o0:o0 + n_out], refs[o0 + n_out:o0 + n_out + n_ho]
        scratch, (ssem, rsem) = refs[o0 + n_out + n_ho:len(refs) - 2], refs[len(refs) - 2:]
        first, middle, last = steps()

        @pl.when(first)
        def _():
            hook.start(hi, ho, ssem, rsem)

        body(*ins, *outs, *scratch)
        if hook.mid is not None:
            @pl.when(middle)
            def _():
                hook.mid(hi, ho, ssem, rsem)

        @pl.when(last)
        def _():
            hook.finish(hi, ho, ssem, rsem)

    any_spec = pl.BlockSpec(memory_space=pl.ANY)
    return pl.pallas_call(
        wrapped, name=name, grid=grid, in_specs=in_specs + [any_spec] * n_hi, out_specs=out_specs + [any_spec] * n_ho,
        out_shape=out_shape + hook.out_shape, scratch_shapes=list(scratch_shapes) + hook.scratch(),
        compiler_params=_params(semantics))(*operands, *hook.operands)


def _alone(hook, name):
    n_hi = len(hook.operands)

    def body(*refs):
        hi, ho, (ssem, rsem) = refs[:n_hi], refs[n_hi:len(refs) - 2], refs[len(refs) - 2:]
        hook.start(hi, ho, ssem, rsem)
        if hook.mid is not None:
            hook.mid(hi, ho, ssem, rsem)
        hook.finish(hi, ho, ssem, rsem)

    any_spec = pl.BlockSpec(memory_space=pl.ANY)
    return pl.pallas_call(body, name=name, in_specs=[any_spec] * n_hi, out_specs=[any_spec] * len(hook.out_shape),
                          out_shape=hook.out_shape, scratch_shapes=hook.scratch())(*hook.operands)


def _steps_1d(n, mid):
    def steps():
        i = pl.program_id(0)
        return i == 0, i == min(mid, n - 1), i == n - 1
    return steps


def _rms(x, g):
    r = lax.rsqrt(jnp.mean(x * x, -1, keepdims=True) + RMS_EPS)
    return x * r * g, r


def _rms_bwd(dy, x, r, g):
    gy = dy * g
    return r * gy - x * (r * r * r) * jnp.mean(x * gy, -1, keepdims=True)


def _colsum(x):
    return jnp.sum(x, axis=0, keepdims=True)


def _rope(x, c, sa, sb):
    n = x.shape[1]
    return x * c + pltpu.roll(x, n - 8, 1) * sa + pltpu.roll(x, 8, 1) * sb


def _rope_bwd(d, c, sa, sb):
    n = d.shape[1]
    return d * c + pltpu.roll(d * sa, 8, 1) + pltpu.roll(d * sb, n - 8, 1)


def _rope_tables(R):
    half = ROT // 2
    lane = jnp.arange(2 * HD) % HD
    inv = THETA ** (-(lane % half).astype(F32) * 2.0 / ROT)
    pos = (jnp.arange(R) - PAD).astype(F32)
    ang = pos[:, None] * inv[None, :]
    cos, sin = jnp.cos(ang), jnp.sin(ang)
    c = jnp.where(lane < ROT, cos, 1.0)
    sa = jnp.where(lane < half, -sin, 0.0)
    sb = jnp.where((lane >= half) & (lane < ROT), sin, 0.0)
    return c, sa, sb


IN_CHUNKS = ([(0, 512, True), (512, 1024, True), (1024, 1152, True), (1152, 1280, False)]
             + [(c, c + 512, False) for c in range(1280, IN_W, 512)])


def _in_proj(h0, g_pre, w_in, b_in, rope, tr, hook=None):
    R = h0.shape[0]
    nt = R // tr

    def body(h_ref, g_ref, w_ref, b_ref, c_ref, sa_ref, sb_ref, qkv_ref, glu_ref, gate_ref, n1_ref):
        n, _ = _rms(h_ref[...], g_ref[...])
        nb = n.astype(BF16)
        n1_ref[...] = nb
        for c0, c1, rot in IN_CHUNKS:
            acc = lax.dot_general(nb, w_ref[c0:c1, :], NT_DIMS, preferred_element_type=F32) + b_ref[:, c0:c1]
            if rot:
                reps = (c1 - c0) // 128
                acc = _rope(acc, jnp.tile(c_ref[...], (1, reps)), jnp.tile(sa_ref[...], (1, reps)),
                            jnp.tile(sb_ref[...], (1, reps)))
            val = acc.astype(BF16)
            if c1 <= QKV_W:
                qkv_ref[:, c0:c1] = val
            elif c1 <= QKV_W + GLU_W:
                glu_ref[:, c0 - QKV_W:c1 - QKV_W] = val
            else:
                gate_ref[:, c0 - QKV_W - GLU_W:c1 - QKV_W - GLU_W] = val

    return _call(
        body, hook, _steps_1d(nt, (3 * nt) // 4), name="in_proj", grid=(nt,),
        in_specs=[_row(tr, D), _const((1, D)), _const((IN_W, D)), _const((1, IN_W)),
                  _row(tr, 128), _row(tr, 128), _row(tr, 128)],
        out_specs=[_row(tr, QKV_W), _row(tr, GLU_W), _row(tr, GATE_W), _row(tr, D)],
        out_shape=[_sds((R, QKV_W), BF16), _sds((R, GLU_W), BF16), _sds((R, GATE_W), BF16), _sds((R, D), BF16)],
        operands=(h0, g_pre, w_in, b_in, *rope), semantics=("arbitrary",))


def _attn_mask(n, keys_first=False):
    shape = (3 * BLK, BLK) if keys_first else (BLK, 3 * BLK)
    qi = lax.broadcasted_iota(jnp.int32, shape, 1 if keys_first else 0)
    kj = lax.broadcasted_iota(jnp.int32, shape, 0 if keys_first else 1)
    tq = n * BLK + qi - PAD
    t_meta = kj - PAD
    t_loc = (n - 1) * BLK + (kj - BLK) - PAD
    meta_ok = (kj < BLK) & (t_meta >= 0) & (t_meta <= tq)
    loc_ok = (kj >= BLK) & (t_loc >= NMETA) & (t_loc <= tq) & (tq - t_loc < BLK)
    return meta_ok | loc_ok


def _dup_heads(ref0, refp, refc, low, transposed=False):
    a = jnp.concatenate([ref0[...], refp[...], refc[...]], 0).astype(F32)
    sw = pltpu.roll(a, HD, 1)
    heads = [jnp.where(low, a, sw), jnp.where(low, sw, a)]
    return [(h.T if transposed else h).astype(BF16) for h in heads]


def _kv_specs(nb, rev):
    def blk(col, which):
        def idx(t):
            n = nb - 1 - t if rev else t
            return ({"meta": 0, "prev": jnp.maximum(n - 1, 0), "own": n}[which], col)
        return pl.BlockSpec((BLK, BLK), idx)
    return [blk(col, w) for col in (8, 9) for w in ("meta", "prev", "own")]


def _attn_fwd(qkv, sinks, hook=None):
    R = qkv.shape[0]
    nb = R // BLK

    def body(s_ref, q_ref, k0, kp, kc, v0, vp, vc, o_ref, lse_ref):
        n = pl.program_id(0)
        lane = lax.broadcasted_iota(jnp.int32, (1, BLK), 1)
        low = lane < HD
        kd, vd_t = _dup_heads(k0, kp, kc, low), _dup_heads(v0, vp, vc, low, transposed=True)
        mask = _attn_mask(n, keys_first=True)
        zero = jnp.zeros((), BF16)
        lses = []
        for g in range(NKV):
            tiles = []
            for pair in range(GH // 2 * g, GH // 2 * (g + 1)):
                qp = q_ref[:, pair * BLK:(pair + 1) * BLK] * jnp.asarray(HD ** -0.5, BF16)
                tiles += [jnp.where(low, qp, zero), jnp.where(low, zero, qp)]
            st = lax.dot_general(kd[g], jnp.concatenate(tiles, 0), NT_DIMS, preferred_element_type=F32)
            ps, inv = [], []
            for j in range(GH):
                s = jnp.where(mask, st[:, j * BLK:(j + 1) * BLK], NEG)
                sk = s_ref[GH * g + j]
                m = jnp.maximum(jnp.max(s, 0, keepdims=True), sk)
                p = jnp.exp(s - m)
                l = jnp.sum(p, 0, keepdims=True) + jnp.exp(sk - m)
                ps.append(p.astype(BF16))
                inv.append(1.0 / l)
                lses.append(m + jnp.log(l))
            ot = jnp.dot(vd_t[g], jnp.concatenate(ps, 1), preferred_element_type=F32)
            for j in range(GH // 2):
                pair = GH // 2 * g + j
                o = [(ot[:, h * BLK:(h + 1) * BLK] * inv[h]).T for h in (2 * j, 2 * j + 1)]
                o_ref[:, pair * BLK:(pair + 1) * BLK] = jnp.where(low, o[0], o[1]).astype(BF16)
        lse_ref[...] = jnp.concatenate(lses, 0)

    return _call(
        body, hook, _steps_1d(nb, (3 * nb) // 4), name="attn_fwd", grid=(nb,),
        in_specs=[pl.BlockSpec(memory_space=pltpu.SMEM), pl.BlockSpec((BLK, D), lambda n: (n, 0))] + _kv_specs(nb, False),
        out_specs=[_row(BLK, D), _row(NH, BLK)],
        out_shape=[_sds((R, D), BF16), _sds((nb * NH, BLK), F32)],
        operands=(sinks, qkv, *([qkv] * 6)), semantics=("arbitrary",))


CONV_TCH, CONV_SUB, HALO = 256, 64, 32


def _tap_windows(buf, r0, offset_of):
    span = CONV_SUB + HALO
    x = buf[pl.ds(r0, span), :]
    by_phase = {}
    for k in range(CONV_K):
        by_phase.setdefault(offset_of(k) % 8, []).append(k)
    for phase, taps in sorted(by_phase.items()):
        y = x if phase == 0 else pltpu.roll(x, span - phase, 0)
        for k in taps:
            d = offset_of(k) - phase
            yield k, y[d:d + CONV_SUB, :]


def _conv_fwd(glu, w, b, tr):
    R = glu.shape[0]
    nc = D // CONV_TCH

    def body(a_ref, g_ref, w_ref, b_ref, o_ref, buf):
        i = pl.program_id(1)

        @pl.when(i == 0)
        def _():
            buf[0:HALO, :] = jnp.zeros((HALO, CONV_TCH), F32)

        @pl.when(i > 0)
        def _():
            buf[0:HALO, :] = buf[tr:tr + HALO, :]

        row = i * tr + lax.broadcasted_iota(jnp.int32, (tr, 1), 0)
        a, g = a_ref[...].astype(F32), g_ref[...].astype(F32)
        buf[HALO:HALO + tr, :] = jnp.where(row >= PAD, a * jax.nn.sigmoid(g), 0.0)
        for r0 in range(0, tr, CONV_SUB):
            acc = jnp.broadcast_to(b_ref[...], (CONV_SUB, CONV_TCH))
            for k, win in _tap_windows(buf, r0, lambda k: HALO - (CONV_K - 1) + k):
                acc = acc + w_ref[k:k + 1, :] * win
            o_ref[r0:r0 + CONV_SUB, :] = acc.astype(BF16)

    return pl.pallas_call(
        body, name="conv_fwd", grid=(nc, R // tr),
        in_specs=[pl.BlockSpec((tr, CONV_TCH), lambda c, i: (i, c)),
                  pl.BlockSpec((tr, CONV_TCH), lambda c, i: (i, nc + c)),
                  pl.BlockSpec((None, HALO, CONV_TCH), lambda c, i: (c, 0, 0)),
                  pl.BlockSpec((1, CONV_TCH), lambda c, i: (0, c))],
        out_specs=pl.BlockSpec((tr, CONV_TCH), lambda c, i: (i, c)),
        out_shape=_sds((R, D), BF16),
        scratch_shapes=[pltpu.VMEM((tr + HALO, CONV_TCH), F32)],
        compiler_params=_params(("arbitrary", "arbitrary")),
    )(glu, glu, w, b)


def _ln_silu(c1, lg, lb):
    mu = jnp.mean(c1, -1, keepdims=True)
    xc = c1 - mu
    rs = lax.rsqrt(jnp.mean(xc * xc, -1, keepdims=True) + LN_EPS)
    xh = xc * rs
    c2 = xh * lg + lb
    sg = jax.nn.sigmoid(c2)
    return xh, rs, c2, sg


def _mix_out(attn, c1, gates, h0, w_ap, w_cp, w_out, lg, lb, b_cp, g_post, g_ffn, tr):
    R = attn.shape[0]

    def body(at_ref, c1_ref, ga_ref, gc_ref, h0_ref, wap, wcp, wo, lg_ref, lb_ref, bcp, gp, gf,
             ao_ref, co_ref, c3_ref, mg_ref, mix_ref, h1_ref, n2_ref):
        ao = jnp.dot(at_ref[...], wap[...], preferred_element_type=F32)
        _, _, c2, sg = _ln_silu(c1_ref[...].astype(F32), lg_ref[...], lb_ref[...])
        c3 = (c2 * sg).astype(BF16)
        c3_ref[...] = c3
        co = jnp.dot(c3, wcp[...], preferred_element_type=F32) + bcp[...]
        ao_b, co_b = ao.astype(BF16), co.astype(BF16)
        ao_ref[...] = ao_b
        co_ref[...] = co_b
        merged = (jax.nn.sigmoid(ga_ref[...].astype(F32)) * ao_b.astype(F32)
                  + jax.nn.sigmoid(gc_ref[...].astype(F32)) * co_b.astype(F32)).astype(BF16)
        mg_ref[...] = merged
        mix = jnp.dot(merged, wo[...], preferred_element_type=F32).astype(BF16)
        mix_ref[...] = mix
        y, _ = _rms(mix.astype(F32), gp[...])
        h1 = h0_ref[...] + y
        h1_ref[...] = h1
        n2, _ = _rms(h1, gf[...])
        row = pl.program_id(0) * tr + lax.broadcasted_iota(jnp.int32, (tr, 1), 0)
        n2_ref[...] = jnp.where(row >= PAD, n2, 0.0).astype(BF16)

    vec = _const((1, D))
    return pl.pallas_call(
        body, name="mix_out", grid=(R // tr,),
        in_specs=[_row(tr, D), _row(tr, D), _row(tr, D, 0), _row(tr, D, 1), _row(tr, D),
                  _const((D, D)), _const((D, D)), _const((D, D)), vec, vec, vec, vec, vec],
        out_specs=[_row(tr, D)] * 7,
        out_shape=[_sds((R, D), BF16)] * 5 + [_sds((R, D), F32), _sds((R, D), BF16)],
        compiler_params=_params(("arbitrary",)),
    )(attn, c1, gates, gates, h0, w_ap, w_cp, w_out, lg, lb, b_cp, g_post, g_ffn)


FFN_CH = 256
N_CHIPS = 4
UPQ = 2 * FFN // N_CHIPS
UP_CHUNKS = [(q, c0, min(c0 + 512, UPQ)) for q in range(N_CHIPS // 2) for c0 in range(0, UPQ, 512)]
TINY_ROWS, TINY_CONV, TINY_FFN, TINY_META = 64, 0, 32, 40


def _shift_down(x, k, halo):
    tr = x.shape[0]
    row = lax.broadcasted_iota(jnp.int32, (tr, 1), 0)
    y = pltpu.roll(x, k, 0)
    for j in range(k):
        y = jnp.where(row == j, halo[8 - k + j:8 - k + j + 1, :], y)
    return y


def _shift_up(x, k, halo):
    tr = x.shape[0]
    row = lax.broadcasted_iota(jnp.int32, (tr, 1), 0)
    y = pltpu.roll(x, tr - k, 0)
    for j in range(k):
        y = jnp.where(row == tr - k + j, halo[j:j + 1, :], y)
    return y


def _conv3(x, halo, w, b):
    return w[2:3, :] * x + w[1:2, :] * _shift_down(x, 1, halo) + w[0:1, :] * _shift_down(x, 2, halo) + b


def _ffn_up(n2, w_up, fw, fb, tr):
    R = n2.shape[0]

    def body(n_ref, w_ref, fw_ref, fb_ref, up_ref, act_ref, carry):
        @pl.when(pl.program_id(0) == 0)
        def _():
            carry[...] = jnp.zeros_like(carry)

        nb = n_ref[...]
        for q, c0, c1 in UP_CHUNKS:
            us = []
            for qq in (q, q + N_CHIPS // 2):
                cs = slice(qq * UPQ + c0, qq * UPQ + c1)
                x = jnp.dot(nb, w_ref[qq, :, c0:c1], preferred_element_type=F32).astype(BF16)
                up_ref[:, cs] = x
                x = x.astype(F32)
                us.append(_conv3(x, carry[:, cs], fw_ref[qq, TINY_FFN:TINY_FFN + 8, c0:c1], fb_ref[:, cs]))
                carry[:, cs] = x[tr - 8:tr, :]
            act_ref[:, q * UPQ + c0:q * UPQ + c1] = (us[0] * jax.nn.sigmoid(us[0]) * us[1]).astype(BF16)

    return pl.pallas_call(
        body, name="ffn_up", grid=(R // tr,),
        in_specs=[_row(tr, D), _const((N_CHIPS, D, UPQ)), _const((N_CHIPS, TINY_ROWS, UPQ)), _const((1, 2 * FFN))],
        out_specs=[_row(tr, 2 * FFN), _row(tr, FFN)],
        out_shape=[_sds((R, 2 * FFN), BF16), _sds((R, FFN), BF16)],
        scratch_shapes=[pltpu.VMEM((8, 2 * FFN), F32)],
        compiler_params=_params(("arbitrary",)),
    )(n2, w_up, fw, fb)


def _ffn_down(act, w_down, h1, tgt, g_post, tr):
    R = act.shape[0]
    m = tr // BLK

    def body(a_ref, w_ref, h1_ref, g_ref, *rest):
        t_refs, (dh2_ref, dffn_ref, loss_ref, dg_ref) = rest[:m], rest[m:]

        @pl.when(pl.program_id(0) == 0)
        def _():
            loss_ref[...] = jnp.zeros_like(loss_ref)
            dg_ref[...] = jnp.zeros_like(dg_ref)

        f = jnp.dot(a_ref[...], w_ref[...], preferred_element_type=F32)
        g = g_ref[...]
        y, r = _rms(f, g)
        row = pl.program_id(0) * tr + lax.broadcasted_iota(jnp.int32, (tr, 1), 0)
        tgt_rows = jnp.concatenate([t[...] for t in t_refs], 0)
        e = jnp.where(row >= BLK, h1_ref[...] + y - tgt_rows, 0.0)
        loss_ref[...] += _colsum(e * e) * (0.5 / D)
        dy = e * (1.0 / D)
        dh2_ref[...] = dy
        dffn_ref[...] = _rms_bwd(dy, f, r, g).astype(BF16)
        dg_ref[...] += _colsum(dy * f * r)

    return pl.pallas_call(
        body, name="ffn_down", grid=(R // tr,),
        in_specs=[_row(tr, FFN), _const((FFN, D)), _row(tr, D), _const((1, D))]
                 + [pl.BlockSpec((BLK, D), lambda i, k=k: (jnp.maximum(m * i - 1 + k, 0), 0)) for k in range(m)],
        out_specs=[_row(tr, D), _row(tr, D), _const((1, D)), _const((1, D))],
        out_shape=[_sds((R, D), F32), _sds((R, D), BF16), _sds((1, D), F32), _sds((1, D), F32)],
        compiler_params=_params(("arbitrary",)),
    )(act, w_down, h1, g_post, *([tgt] * m))


def _ffn_bwd_act(dffn, w_down, up, fw, fb, tr):
    R = dffn.shape[0]
    nt = R // tr

    def body(d_ref, w_ref, up_ref, hal_ref, fw_ref, fb_ref, dup_ref, dfw_ref, dfb_ref, carry):
        t = pl.program_id(0)
        i = nt - 1 - t

        @pl.when(t == 0)
        def _():
            carry[...] = jnp.zeros_like(carry)
            dfw_ref[...] = jnp.zeros_like(dfw_ref)
            dfb_ref[...] = jnp.zeros_like(dfb_ref)

        dff = d_ref[...]
        row = i * tr + lax.broadcasted_iota(jnp.int32, (tr, 1), 0)
        first = i == 0
        for q, c0, c1 in UP_CHUNKS:
            dact = lax.dot_general(dff, w_ref[q * UPQ + c0:q * UPQ + c1, :], NT_DIMS, preferred_element_type=F32)
            chips = (q, q + N_CHIPS // 2)
            xs, us = [], []
            for qq in chips:
                cs = slice(qq * UPQ + c0, qq * UPQ + c1)
                x = up_ref[:, cs].astype(F32)
                halo = jnp.where(first, 0.0, hal_ref[:, cs].astype(F32))
                x1, x2 = _shift_down(x, 1, halo), _shift_down(x, 2, halo)
                w = fw_ref[qq, TINY_FFN:TINY_FFN + 8, c0:c1]
                us.append(w[2:3, :] * x + w[1:2, :] * x1 + w[0:1, :] * x2 + fb_ref[:, cs])
                xs.append((x, x1, x2))
            sg = jax.nn.sigmoid(us[0])
            silu = us[0] * sg
            dus = [dact * us[1] * sg * (1.0 + us[0] * (1.0 - sg)), dact * silu]
            for (x, x1, x2), du, qq in zip(xs, dus, chips):
                cs = slice(qq * UPQ + c0, qq * UPQ + c1)
                w = fw_ref[qq, TINY_FFN:TINY_FFN + 8, c0:c1]
                nxt = carry[:, cs]
                dx = w[2:3, :] * du + w[1:2, :] * _shift_up(du, 1, nxt) + w[0:1, :] * _shift_up(du, 2, nxt)
                dup_ref[:, cs] = jnp.where(row >= PAD, dx, 0.0).astype(BF16)
                dfw_ref[qq, 0:1, c0:c1] += _colsum(x2 * du)
                dfw_ref[qq, 1:2, c0:c1] += _colsum(x1 * du)
                dfw_ref[qq, 2:3, c0:c1] += _colsum(x * du)
                dfb_ref[:, cs] += _colsum(du)
                carry[:, cs] = du[0:8, :]

    halo_spec = pl.BlockSpec((8, 2 * FFN), lambda t: (jnp.maximum((nt - 1 - t) * (tr // 8) - 1, 0), 0))
    return pl.pallas_call(
        body, name="ffn_bwd_act", grid=(nt,),
        in_specs=[_rrow(tr, D, nt), _const((FFN, D)), _rrow(tr, 2 * FFN, nt), halo_spec,
                  _const((N_CHIPS, TINY_ROWS, UPQ)), _const((1, 2 * FFN))],
        out_specs=[_rrow(tr, 2 * FFN, nt), _const((N_CHIPS, 8, UPQ)), _const((1, 2 * FFN))],
        out_shape=[_sds((R, 2 * FFN), BF16), _sds((N_CHIPS, 8, UPQ), F32), _sds((1, 2 * FFN), F32)],
        scratch_shapes=[pltpu.VMEM((8, 2 * FFN), F32)],
        compiler_params=_params(("arbitrary",)),
    )(dffn, w_down, up, up, fw, fb)


def _ffn_bwd_in(dup, w_up, h1, dh2, mix, g_ffn, g_post, tr):
    R = dup.shape[0]

    def body(d_ref, w_ref, h1_ref, dh2_ref, mix_ref, gf_ref, gp_ref, dh1_ref, dmix_ref, dgf_ref, dgp_ref):
        @pl.when(pl.program_id(0) == 0)
        def _():
            dgf_ref[...] = jnp.zeros_like(dgf_ref)
            dgp_ref[...] = jnp.zeros_like(dgp_ref)

        dn2 = sum(lax.dot_general(d_ref[:, q * UPQ:(q + 1) * UPQ], w_ref[q], NT_DIMS, preferred_element_type=F32)
                  for q in range(N_CHIPS))
        h1 = h1_ref[...]
        _, r2 = _rms(h1, gf_ref[...])
        dh1 = dh2_ref[...] + _rms_bwd(dn2, h1, r2, gf_ref[...])
        dgf_ref[...] += _colsum(dn2 * h1 * r2)
        dh1_ref[...] = dh1
        m = mix_ref[...].astype(F32)
        _, rm = _rms(m, gp_ref[...])
        dmix_ref[...] = _rms_bwd(dh1, m, rm, gp_ref[...]).astype(BF16)
        dgp_ref[...] += _colsum(dh1 * m * rm)

    vec = _const((1, D))
    return pl.pallas_call(
        body, name="ffn_bwd_in", grid=(R // tr,),
        in_specs=[_row(tr, 2 * FFN), _const((N_CHIPS, D, UPQ)), _row(tr, D), _row(tr, D), _row(tr, D), vec, vec],
        out_specs=[_row(tr, D), _row(tr, D), vec, vec],
        out_shape=[_sds((R, D), F32), _sds((R, D), BF16), _sds((1, D), F32), _sds((1, D), F32)],
        compiler_params=_params(("arbitrary",)),
    )(dup, w_up, h1, dh2, mix, g_ffn, g_post)


def _mix_bwd(dmix, ao, co, gates, c1, w_out, w_ap, w_cp, lg, lb, tr):
    R = dmix.shape[0]

    def body(dm_ref, ao_ref, co_ref, ga_ref, gc_ref, c1_ref, wo, wap, wcp, lg_ref, lb_ref,
             dao_ref, dco_ref, dgate_ref, dattn_ref, dc1_ref, dbcp_ref, dlg_ref, dlb_ref, dcb_ref):
        @pl.when(pl.program_id(0) == 0)
        def _():
            for ref in (dbcp_ref, dlg_ref, dlb_ref, dcb_ref):
                ref[...] = jnp.zeros_like(ref)

        dmg = lax.dot_general(dm_ref[...], wo[...], NT_DIMS, preferred_element_type=F32)
        sa = jax.nn.sigmoid(ga_ref[...].astype(F32))
        sc = jax.nn.sigmoid(gc_ref[...].astype(F32))
        dao = (dmg * sa).astype(BF16)
        dco = (dmg * sc).astype(BF16)
        dao_ref[...] = dao
        dco_ref[...] = dco
        dgate_ref[:, 0:D] = (dmg * ao_ref[...].astype(F32) * sa * (1.0 - sa)).astype(BF16)
        dgate_ref[:, D:2 * D] = (dmg * co_ref[...].astype(F32) * sc * (1.0 - sc)).astype(BF16)
        dbcp_ref[...] += _colsum(dco.astype(F32))
        dattn_ref[...] = lax.dot_general(dao, wap[...], NT_DIMS, preferred_element_type=F32).astype(BF16)
        dc3 = lax.dot_general(dco, wcp[...], NT_DIMS, preferred_element_type=F32)
        xh, rs, c2, sg = _ln_silu(c1_ref[...].astype(F32), lg_ref[...], lb_ref[...])
        dc2 = dc3 * sg * (1.0 + c2 * (1.0 - sg))
        dlg_ref[...] += _colsum(dc2 * xh)
        dlb_ref[...] += _colsum(dc2)
        dxh = dc2 * lg_ref[...]
        dc1 = rs * (dxh - jnp.mean(dxh, -1, keepdims=True) - xh * jnp.mean(dxh * xh, -1, keepdims=True))
        dc1_ref[...] = dc1
        dcb_ref[...] += _colsum(dc1)

    vec = _const((1, D))
    return pl.pallas_call(
        body, name="mix_bwd", grid=(R // tr,),
        in_specs=[_row(tr, D), _row(tr, D), _row(tr, D), _row(tr, D, 0), _row(tr, D, 1), _row(tr, D),
                  _const((D, D)), _const((D, D)), _const((D, D)), vec, vec],
        out_specs=[_row(tr, D), _row(tr, D), _row(tr, 2 * D), _row(tr, D), _row(tr, D), vec, vec, vec, vec],
        out_shape=[_sds((R, D), BF16), _sds((R, D), BF16), _sds((R, 2 * D), BF16), _sds((R, D), BF16),
                   _sds((R, D), F32)] + [_sds((1, D), F32)] * 4,
        compiler_params=_params(("arbitrary",)),
    )(dmix, ao, co, gates, gates, c1, w_out, w_ap, w_cp, lg, lb)


def _conv_bwd(dc1, glu, w, tr, hook=None):
    R = dc1.shape[0]
    nt, nc = R // tr, D // CONV_TCH

    def body(d_ref, a_ref, g_ref, w_ref, dglu_a, dglu_g, dw_ref, buf, dw_acc):
        t = pl.program_id(1)
        i = nt - 1 - t

        @pl.when(t == 0)
        def _():
            buf[tr:tr + HALO, :] = jnp.zeros((HALO, CONV_TCH), F32)
            dw_acc[...] = jnp.zeros_like(dw_acc)

        @pl.when(t > 0)
        def _():
            buf[tr:tr + HALO, :] = buf[0:HALO, :]

        buf[0:tr, :] = d_ref[...]
        for r0 in range(0, tr, CONV_SUB):
            rs = slice(r0, r0 + CONV_SUB)
            row = i * tr + r0 + lax.broadcasted_iota(jnp.int32, (CONV_SUB, 1), 0)
            a, g = a_ref[rs, :].astype(F32), g_ref[rs, :].astype(F32)
            sg = jax.nn.sigmoid(g)
            glu = jnp.where(row >= PAD, a * sg, 0.0)
            acc = jnp.zeros((CONV_SUB, CONV_TCH), F32)
            for k, win in _tap_windows(buf, r0, lambda k: CONV_K - 1 - k):
                acc = acc + w_ref[k:k + 1, :] * win
                dw_acc[k] += jnp.sum((glu * win).reshape(CONV_SUB // 8, 8, CONV_TCH), axis=0)
            dglu = jnp.where(row >= PAD, acc, 0.0)
            dglu_a[rs, :] = (dglu * sg).astype(BF16)
            dglu_g[rs, :] = (dglu * a * sg * (1.0 - sg)).astype(BF16)

        @pl.when(t == nt - 1)
        def _():
            dw_ref[...] = jnp.sum(dw_acc[...], axis=1)

    def rspec(col0):
        return pl.BlockSpec((tr, CONV_TCH), lambda c, t: (nt - 1 - t, col0 + c))

    def steps():
        c, t = pl.program_id(0), pl.program_id(1)
        return (c == 0) & (t == 0), False, (c == nc - 1) & (t == nt - 1)

    return _call(
        body, hook, steps, name="conv_bwd", grid=(nc, nt),
        in_specs=[rspec(0), rspec(0), rspec(nc), pl.BlockSpec((None, HALO, CONV_TCH), lambda c, t: (c, 0, 0))],
        out_specs=[rspec(0), rspec(0), pl.BlockSpec((None, HALO, CONV_TCH), lambda c, t: (c, 0, 0))],
        out_shape=[_sds((R, D), BF16), _sds((R, D), BF16), _sds((N_CHIPS, HALO, CONV_TCH), F32)],
        scratch_shapes=[pltpu.VMEM((tr + HALO, CONV_TCH), F32), pltpu.VMEM((HALO, 8, CONV_TCH), F32)],
        operands=(dc1, glu, glu, w), semantics=("arbitrary", "arbitrary"))


def _attn_bwd(qkv, do, lse, sinks, rope, hook=None):
    R = qkv.shape[0]
    nb = R // BLK

    def body(s_ref, q_ref, k0, kp, kc, v0, vp, vc, do_ref, lse_ref, c_ref, sa_ref, sb_ref,
             dqkv_ref, dsink_ref, car_k, car_v, met_k, met_v):
        t = pl.program_id(0)
        n = nb - 1 - t

        @pl.when(t == 0)
        def _():
            for ref in (car_k, car_v, met_k, met_v, dsink_ref):
                ref[...] = jnp.zeros_like(ref)

        lane = lax.broadcasted_iota(jnp.int32, (1, BLK), 1)
        low = lane < HD
        kd, vd = _dup_heads(k0, kp, kc, low), _dup_heads(v0, vp, vc, low)
        kd_t = _dup_heads(k0, kp, kc, low, transposed=True)
        mask = _attn_mask(n, keys_first=True)
        tabs = (c_ref[...], sa_ref[...], sb_ref[...])
        zero = jnp.zeros((), BF16)
        dk_acc, dv_acc = [], []
        dsink = jnp.zeros((1, BLK), F32)
        for g in range(NKV):
            q_tiles, do_tiles = [], []
            for pair in range(GH // 2 * g, GH // 2 * (g + 1)):
                cs = slice(pair * BLK, (pair + 1) * BLK)
                qp, dop = q_ref[:, cs] * jnp.asarray(HD ** -0.5, BF16), do_ref[:, cs]
                q_tiles += [jnp.where(low, qp, zero), jnp.where(low, zero, qp)]
                do_tiles += [jnp.where(low, dop, zero), jnp.where(low, zero, dop)]
            qs, dos = jnp.concatenate(q_tiles, 0), jnp.concatenate(do_tiles, 0)
            st = lax.dot_general(kd[g], qs, NT_DIMS, preferred_element_type=F32)
            dpt = lax.dot_general(vd[g], dos, NT_DIMS, preferred_element_type=F32)
            ps, dss = [], []
            for j in range(GH):
                h = GH * g + j
                cs = slice(j * BLK, (j + 1) * BLK)
                lse_h = lse_ref[h:h + 1, :]
                p = jnp.where(mask, jnp.exp(st[:, cs] - lse_h), 0.0)
                dp = dpt[:, cs]
                delta = jnp.sum(p * dp, 0, keepdims=True)
                ps.append(p.astype(BF16))
                dss.append((p * (dp - delta)).astype(BF16))
                dsink = dsink + jnp.where(lane == h, -jnp.sum(jnp.exp(s_ref[h] - lse_h) * delta), 0.0)
            ds_t, p_t = jnp.concatenate(dss, 1), jnp.concatenate(ps, 1)
            dk_acc.append(jnp.dot(ds_t, qs, preferred_element_type=F32))
            dv_acc.append(jnp.dot(p_t, dos, preferred_element_type=F32))
            dq_t = jnp.dot(kd_t[g], ds_t, preferred_element_type=F32) * (HD ** -0.5)
            for j in range(GH // 2):
                pair = GH // 2 * g + j
                dq = [dq_t[:, h * BLK:(h + 1) * BLK].T for h in (2 * j, 2 * j + 1)]
                dqkv_ref[:, pair * BLK:(pair + 1) * BLK] = _rope_bwd(jnp.where(low, dq[0], dq[1]), *tabs).astype(BF16)
        dsink_ref[0:1, :] += dsink

        def fold(acc):
            tot = [a + pltpu.roll(a, HD, 1) for a in acc]
            return jnp.where(low, tot[0], tot[1])

        dk_all, dv_all = fold(dk_acc), fold(dv_acc)
        met_k[...] += dk_all[0:BLK, :]
        met_v[...] += dv_all[0:BLK, :]
        last = jnp.where(n == 0, 1.0, 0.0)
        dk_n = dk_all[2 * BLK:3 * BLK, :] + car_k[...] + last * met_k[...]
        dv_n = dv_all[2 * BLK:3 * BLK, :] + car_v[...] + last * met_v[...]
        dqkv_ref[:, D:D + BLK] = _rope_bwd(dk_n, *tabs).astype(BF16)
        dqkv_ref[:, D + BLK:QKV_W] = dv_n.astype(BF16)
        car_k[...] = dk_all[BLK:2 * BLK, :]
        car_v[...] = dv_all[BLK:2 * BLK, :]

    rblk = lambda w: pl.BlockSpec((BLK, w), lambda t: (nb - 1 - t, 0))
    return _call(
        body, hook, _steps_1d(nb, nb), name="attn_bwd", grid=(nb,),
        in_specs=[pl.BlockSpec(memory_space=pltpu.SMEM), rblk(D)] + _kv_specs(nb, True)
                 + [rblk(D), pl.BlockSpec((NH, BLK), lambda t: (nb - 1 - t, 0)), rblk(BLK), rblk(BLK), rblk(BLK)],
        out_specs=[rblk(QKV_W), _const((8, BLK))],
        out_shape=[_sds((R, QKV_W), BF16), _sds((8, BLK), F32)],
        scratch_shapes=[pltpu.VMEM((BLK, BLK), F32)] * 4,
        operands=(sinks, qkv, *([qkv] * 6), do, lse, *rope), semantics=("arbitrary",))


def _in_bwd(dproj, w_in, h0, dh1, g_pre, tr, hook=None):
    R = h0.shape[0]
    n = len(dproj)
    widths = [p.shape[1] for p in dproj]
    starts = [sum(widths[:j]) for j in range(n)]

    def body(*refs):
        d_refs, (w_ref, h0_ref, dh1_ref, g_ref, dh0_ref, dg_ref, db_ref) = refs[:n], refs[n:]

        @pl.when(pl.program_id(0) == 0)
        def _():
            dg_ref[...] = jnp.zeros_like(dg_ref)
            db_ref[...] = jnp.zeros_like(db_ref)

        dn1 = jnp.zeros((tr, D), F32)
        for d_ref, c0, wd in zip(d_refs, starts, widths):
            d = d_ref[...]
            dn1 = dn1 + jnp.dot(d, w_ref[c0:c0 + wd, :], preferred_element_type=F32)
            db_ref[:, c0:c0 + wd] += _colsum(d.astype(F32))
        h0 = h0_ref[...]
        _, r = _rms(h0, g_ref[...])
        dh0_ref[...] = dh1_ref[...] + _rms_bwd(dn1, h0, r, g_ref[...])
        dg_ref[...] += _colsum(dn1 * h0 * r)

    return _call(
        body, hook, _steps_1d(R // tr, R // tr), name="in_bwd", grid=(R // tr,),
        in_specs=[_row(tr, wd) for wd in widths] + [_const((IN_W, D)), _row(tr, D), _row(tr, D), _const((1, D))],
        out_specs=[_row(tr, D), _const((1, D)), _const((1, IN_W))],
        out_shape=[_sds((R, D), F32), _sds((1, D), F32), _sds((1, IN_W), F32)],
        operands=(*dproj, w_in, h0, dh1, g_pre), semantics=("arbitrary",))


def _dw(a, b, name, tn, tr, by_chip=False, ta=None, rows_of=None, row0=0, into=None):
    R, ka = a.shape
    n = b.shape[1]
    nt = R // tr
    ta = ta or ka
    k0 = 0
    if by_chip:
        out_spec = pl.BlockSpec((None, ta, tn), lambda k, j, i: (j, k, 0))
        out_shape = _sds((n // tn, ka, tn), BF16)
    else:
        if rows_of is not None:
            k0 = row0 // ta
        out_spec = pl.BlockSpec((ta, tn), lambda k, j, i: (k0 + k, j))
        out_shape = _sds((ka if rows_of is None else rows_of, n), BF16)
    extra = [] if into is None else [into]

    def body(a_ref, b_ref, *rest):
        o_ref, acc = rest[len(extra):]
        i = pl.program_id(2)

        @pl.when(i == 0)
        def _():
            acc[...] = jnp.zeros_like(acc)

        acc[...] += lax.dot_general(a_ref[...], b_ref[...], TN_DIMS, preferred_element_type=F32)

        @pl.when(i == nt - 1)
        def _():
            o_ref[...] = acc[...].astype(BF16)

    return pl.pallas_call(
        body, name=name, grid=(ka // ta, n // tn, nt),
        in_specs=[pl.BlockSpec((tr, ta), lambda k, j, i: (i, k)), pl.BlockSpec((tr, tn), lambda k, j, i: (i, j))]
                 + [pl.BlockSpec(memory_space=pl.ANY)] * len(extra),
        out_specs=out_spec, out_shape=out_shape,
        input_output_aliases={2: 0} if extra else {},
        scratch_shapes=[pltpu.VMEM((ta, tn), F32)],
        compiler_params=_params(("arbitrary", "arbitrary", "arbitrary")),
    )(a, b, *extra)


SMALL = ["norm_pre_mix", "norm_post_mix", "b_in", "attn_sinks", "conv_dw_b", "conv_ln_g", "conv_ln_b",
         "b_conv_proj", "norm_pre_ffn", "norm_post_ffn", "ffn_dw_b"]


def local_step(x, tgt, W, dist=None):
    W = dict(W)
    S = x.shape[0]
    R = S + BLK
    tr = _tile(R, 384, BLK)
    trw = _tile(R, 1056)
    rope = _rope_tables(R)
    meta = _cols_joined(W["tiny"][:, TINY_META:TINY_META + NMETA, 0:DQ])
    h0 = jnp.concatenate([jnp.zeros((PAD, D), F32), meta, x], 0)

    qkv, glu, gates, n1, *got = _in_proj(h0, W["norm_pre_mix"], W["w_in"], W["b_in"], rope, tr,
                                         dist and dist.gather_hook(GATHER_IN_PROJ))
    if dist:
        W.update(dist.weights(GATHER_IN_PROJ, got))
    sinks = W["attn_sinks"].reshape(NH)
    attn, lse, *got = _attn_fwd(qkv, sinks, dist and dist.gather_hook(GATHER_ATTN))
    if dist:
        W.update(dist.weights(GATHER_ATTN, got))
    c1 = _conv_fwd(glu, W["tiny"], W["conv_dw_b"], tr)
    ao, co, c3, merged, mix, h1, n2 = _mix_out(
        attn, c1, gates, h0, W["w_attn_proj"], W["w_conv_proj"], W["w_out"], W["conv_ln_g"], W["conv_ln_b"],
        W["b_conv_proj"], W["norm_post_mix"], W["norm_pre_ffn"], tr)
    up, act = _ffn_up(n2, W["w_up"], W["tiny"], W["ffn_dw_b"], tr)
    dh2, dffn, loss_cols, dg_post_ffn = _ffn_down(act, W["w_down"], h1, tgt, W["norm_post_ffn"], tr)

    dw_down = _dw(act, dffn, "dw_down", 512, trw)
    dup, dfw, dfb = _ffn_bwd_act(dffn, W["w_down"], up, W["tiny"], W["ffn_dw_b"], tr)
    dw_up = _dw(n2, dup, "dw_up", UPQ, trw, by_chip=True)
    ffn_sums = dist and dist.pair_sums(FFN_SHARES, [dw_up, dw_down.reshape(N_CHIPS, -1, D)], "ffn")
    dh1, dmix, dg_pre_ffn, dg_post_mix = _ffn_bwd_in(dup, W["w_up"], h1, dh2, mix, W["norm_pre_ffn"],
                                                      W["norm_post_mix"], tr)
    dao, dco, dgates, dattn, dc1, db_cp, dlg, dlb, dcb = _mix_bwd(
        dmix, ao, co, gates, c1, W["w_out"], W["w_attn_proj"], W["w_conv_proj"], W["conv_ln_g"], W["conv_ln_b"], tr)
    dglu_a, dglu_g, dcw, *ffn_got = _conv_bwd(dc1, glu, W["tiny"], tr, dist and dist.chip_hook(FFN_SHARES, ffn_sums))
    branch = [_dw(attn, dao, "dw_attn_proj", D, trw), _dw(c3, dco, "dw_conv_proj", D, trw),
              _dw(merged, dmix, "dw_out", D, trw),
              _tiny_pack({"conv_dw_w": dcw, "ffn_dw_w": dfw, "meta_tokens": jnp.zeros((N_CHIPS, NMETA, DQ), F32)})]
    branch_sums = dist and dist.pair_sums(BRANCH_TINY_SHARES, [a.reshape(N_CHIPS, -1, a.shape[-1]) for a in branch], "branch")
    dqkv, dsink, *got = _attn_bwd(qkv, dattn, lse, sinks, rope,
                                  dist and dist.chip_hook(BRANCH_TINY_SHARES, branch_sums))
    if dist:
        dist.finish(FFN_SHARES + BRANCH_TINY_SHARES, ffn_sums + branch_sums, ffn_got + got, "ffn_branch")
    dproj = [dqkv, dglu_a, dglu_g, dgates]
    dw_in, row0 = None, 0
    for j, p in enumerate(dproj):
        ta = _tile(p.shape[1], D, BLK) if j == 0 else 2 * BLK
        dw_in = _dw(p, n1, "dw_in_%d" % j, D, trw, ta=ta, rows_of=IN_W, row0=row0, into=dw_in)
        row0 += p.shape[1]
    in_sums = dist and dist.pair_sums(IN_SHARES, [dw_in.reshape(N_CHIPS, -1, D)], "in")
    dh0, dg_pre_mix, db_in, *got = _in_bwd(dproj, W["w_in"], h0, dh1, W["norm_pre_mix"], tr,
                                           dist and dist.chip_hook(IN_SHARES, in_sums))
    if dist:
        dist.finish(IN_SHARES, in_sums, got, "in")

    grads = {
        "w_in": dw_in, "w_attn_proj": branch[0], "w_conv_proj": branch[1], "w_out": branch[2],
        "w_up": dw_up,
        "w_down": dw_down,
        "tiny": branch[3],
        "meta_tokens": dh0[PAD:BLK],
        "norm_pre_mix": dg_pre_mix, "norm_post_mix": dg_post_mix, "b_in": db_in,
        "attn_sinks": dsink[0:1, 0:NH], "conv_dw_b": dcb, "conv_ln_g": dlg, "conv_ln_b": dlb,
        "b_conv_proj": db_cp, "norm_pre_ffn": dg_pre_ffn, "norm_post_ffn": dg_post_ffn, "ffn_dw_b": dfb,
    }
    return loss_cols, dh0[BLK:], grads


INQ = IN_W // N_CHIPS
DQ = D // N_CHIPS
SHARES = [("w_in", INQ, D, BF16), ("w_attn_proj", DQ, D, BF16), ("w_conv_proj", DQ, D, BF16), ("w_out", DQ, D, BF16),
          ("w_up", D, UPQ, BF16), ("w_down", FFN // N_CHIPS, D, BF16), ("tiny", TINY_ROWS, UPQ, F32)]
TINY_PARTS = [("conv_dw_w", TINY_CONV, CONV_K, TINY_FFN - TINY_CONV, DQ), ("ffn_dw_w", TINY_FFN, FFN_K, TINY_META - TINY_FFN, UPQ),
              ("meta_tokens", TINY_META, NMETA, NMETA, DQ)]


def _tiny_pack(parts):
    rows = []
    for name, _, _, reserved, _ in TINY_PARTS:
        a = parts[name].astype(F32)
        pad = [(0, 0)] * (a.ndim - 2) + [(0, reserved - a.shape[-2]), (0, UPQ - a.shape[-1])]
        rows.append(jnp.pad(a, pad))
    used = sum(r.shape[-2] for r in rows)
    rows.append(jnp.zeros(rows[0].shape[:-2] + (TINY_ROWS - used, UPQ), F32))
    return jnp.concatenate(rows, axis=-2)


def _tiny_unpack(tiny):
    return {name: tiny[..., r0:r0 + k, 0:cols] for name, r0, k, _, cols in TINY_PARTS}


def _cols_by_chip(a):
    rows, n = a.shape
    return a.reshape(rows, N_CHIPS, n // N_CHIPS).transpose(1, 0, 2)


def _cols_joined(a):
    _, rows, cols = a.shape
    return a.transpose(1, 0, 2).reshape(rows, N_CHIPS * cols)


def _to_planes(a, rows):
    return jnp.pad(a, [(0, rows * D - a.shape[-1])]).reshape(rows, D)


ANY = pl.BlockSpec(memory_space=pl.ANY)


def _place():
    x, y, c = lax.axis_index("x"), lax.axis_index("y"), lax.axis_index("c")
    chips = [(1 - x, y), (x, 1 - y), (1 - x, 1 - y)]
    return x, y, c, chips


def _rcopy(src, dst, ssem, rsem, to):
    return pltpu.make_async_remote_copy(src_ref=src, dst_ref=dst, send_sem=ssem, recv_sem=rsem,
                                        device_id=to, device_id_type=MESH)


def _halves(ref_or_rows, c):
    half = ref_or_rows // 2
    return pl.ds(c * half, half), pl.ds((1 - c) * half, half)


FIRST_SHARES, BRANCH_SHARES, FFN_SHARES = [0, 6], [1, 2, 3], [4, 5]
IN_SHARES, BRANCH_TINY_SHARES = [0], [1, 2, 3, 6]
GATHER_IN_PROJ, GATHER_ATTN = [1, 2, 3, 5], [4]


def _gather_hook(own, idx):
    n = len(idx)

    def copies(kind, ins, outs, ssem, rsem):
        x, y, c, chips = _place()
        q = 2 * x + y
        sib = (x, y, 1 - c)
        out = []
        for i, a in enumerate(idx):
            mine, other = _halves(SHARES[a][1], c)
            for j, (cx, cy) in enumerate(chips):
                k, to = 3 * i + j, (cx, cy, c)
                landed, theirs = outs[i].at[2 * cx + cy, mine], outs[i].at[2 * cx + cy, other]
                if kind == "send":
                    out.append(_rcopy(ins[i].at[mine], outs[i].at[q, mine], ssem.at[k], rsem.at[k], to))
                elif kind == "landing":
                    out.append(_rcopy(ins[i].at[mine], landed, ssem.at[k], rsem.at[k], to))
                elif kind == "pass":
                    out.append(_rcopy(landed, landed, ssem.at[3 * n + k], rsem.at[3 * n + k], sib))
                else:
                    out.append(_rcopy(theirs, theirs, ssem.at[3 * n + k], rsem.at[3 * n + k], sib))
        return out

    def own_copies(ins, outs, ssem, rsem):
        x, y, c, _ = _place()
        q = 2 * x + y
        return [_rcopy(ins[i], outs[i].at[q], ssem.at[6 * n + i], rsem.at[6 * n + i], (x, y, 1 - c)) for i in range(n)]

    def start(*refs):
        for cp in copies("send", *refs) + own_copies(*refs):
            cp.start()

    def mid(*refs):
        for landed, cp in zip(copies("landing", *refs), copies("pass", *refs)):
            landed.wait_recv()
            cp.start()

    def finish(*refs):
        for cp in copies("arrival", *refs):
            cp.wait_recv()
        for cp in copies("send", *refs) + copies("pass", *refs):
            cp.wait_send()
        for cp in own_copies(*refs):
            cp.wait()

    shapes = [_sds((N_CHIPS,) + SHARES[a][1:3], SHARES[a][3]) for a in idx]
    return _Hook(own, shapes, 7 * n, start, finish, mid)


def _chip_hook(sums, idx):
    def copies(ins, outs, ssem, rsem):
        x, y, c, chips = _place()
        return [_rcopy(ins[i].at[2 * cx + cy], outs[i].at[j], ssem.at[3 * i + j], rsem.at[3 * i + j], (cx, cy, c))
                for i in range(len(idx)) for j, (cx, cy) in enumerate(chips)]

    def start(*refs):
        for cp in copies(*refs):
            cp.start()

    def finish(*refs):
        for cp in copies(*refs):
            cp.wait()

    shapes = [_sds((N_CHIPS - 1, SHARES[a][1] // 2, SHARES[a][2]), SHARES[a][3]) for a in idx]
    return _Hook(sums, shapes, 3 * len(idx), start, finish)


def _sibling_swap(parts, idx, tag):
    def copies(ins, outs, ssem, rsem):
        x, y, c, _ = _place()
        return [_rcopy(ins[i].at[:, _halves(SHARES[a][1], c)[1]], outs[i], ssem.at[i], rsem.at[i], (x, y, 1 - c))
                for i, a in enumerate(idx)]

    def start(*refs):
        for cp in copies(*refs):
            cp.start()

    def finish(*refs):
        for cp in copies(*refs):
            cp.wait()

    shapes = [_sds((N_CHIPS, SHARES[a][1] // 2, SHARES[a][2]), SHARES[a][3]) for a in idx]
    return _alone(_Hook(parts, shapes, len(idx), start, finish), "sibling_swap_" + tag)


def _sum_pair(parts, recvs, c, idx, tag):
    steps, n = 2, len(idx)

    def body(c_ref, *refs):
        for i, a in enumerate(idx):
            refs[2 * n + i][...] = (refs[i][...].astype(F32) + refs[n + i][...].astype(F32)).astype(SHARES[a][3])

    own, got, out, views, shapes = [], [], [], [], []
    for p, a in zip(parts, idx):
        _, rows, cols, dt = SHARES[a]
        blk = rows // 2 // steps
        own.append(pl.BlockSpec((None, None, blk, cols), lambda q, i, c_ref: (q, c_ref[0], i, 0)))
        got.append(pl.BlockSpec((None, blk, cols), lambda q, i, c_ref: (q, i, 0)))
        out.append(pl.BlockSpec((None, blk, cols), lambda q, i, c_ref: (q, i, 0)))
        views.append(p.reshape(N_CHIPS, 2, rows // 2, cols))
        shapes.append(_sds((N_CHIPS, rows // 2, cols), dt))
    grid_spec = pltpu.PrefetchScalarGridSpec(num_scalar_prefetch=1, grid=(N_CHIPS, steps),
                                             in_specs=own + got, out_specs=out)
    return pl.pallas_call(body, name="sum_pair_" + tag, grid_spec=grid_spec, out_shape=shapes,
                          compiler_params=_params(("arbitrary", "arbitrary")))(c, *views, *recvs)


def _sum_chips(sums, recvs, qc, idx, tag):
    steps, n = 2, len(idx)

    def body(qc_ref, *refs):
        for i in range(n):
            acc = refs[i][...].astype(F32)
            for j in range(1, N_CHIPS):
                acc = acc + refs[j * n + i][...].astype(F32)
            refs[N_CHIPS * n + i][...] = acc

    own, got, out, shapes = [], [[], [], []], [], []
    for a in idx:
        _, rows, cols, _ = SHARES[a]
        blk = rows // 2 // steps
        own.append(pl.BlockSpec((None, blk, cols), lambda i, qc_ref: (qc_ref[0], i, 0)))
        for j in range(N_CHIPS - 1):
            got[j].append(pl.BlockSpec((None, blk, cols), lambda i, qc_ref, j=j: (j, i, 0)))
        out.append(pl.BlockSpec((None, blk, cols), lambda i, qc_ref: (qc_ref[1], i, 0)))
        shapes.append(_sds((2, rows // 2, cols), F32))
    grid_spec = pltpu.PrefetchScalarGridSpec(num_scalar_prefetch=1, grid=(steps,),
                                             in_specs=own + got[0] + got[1] + got[2], out_specs=out)
    return pl.pallas_call(body, name="sum_chips_" + tag, grid_spec=grid_spec, out_shape=shapes,
                          compiler_params=_params(("arbitrary",)))(qc, *sums, *recvs, *recvs, *recvs)


def _sibling_share(halves, idx, tag):
    n = len(idx)

    def body(*refs):
        outs, (ssem, rsem) = refs[n:2 * n], refs[2 * n:]
        x, y, c, _ = _place()
        copies = []
        for i in range(n):
            cp = _rcopy(outs[i].at[c], outs[i].at[c], ssem.at[i], rsem.at[i], (x, y, 1 - c))
            cp.start()
            copies.append(cp)
        for i in range(n):
            theirs = outs[i].at[1 - c]
            _rcopy(theirs, theirs, ssem.at[i], rsem.at[i], (x, y, 1 - c)).wait_recv()
        for cp in copies:
            cp.wait_send()

    return pl.pallas_call(
        body, name="sibling_share_" + tag, in_specs=[ANY] * n, out_specs=[ANY] * n,
        out_shape=[_sds((2, SHARES[a][1] // 2, SHARES[a][2]), F32) for a in idx],
        input_output_aliases={i: i for i in range(n)},
        scratch_shapes=[pltpu.SemaphoreType.DMA((n,)), pltpu.SemaphoreType.DMA((n,))],
    )(*halves)


class _Dist:
    def __init__(self, own):
        self.own = own
        self.core = lax.axis_index("c")
        self.chip = 2 * lax.axis_index("x") + lax.axis_index("y")
        self.reduced = {}

    def gather_hook(self, idx):
        return _gather_hook([self.own[a] for a in idx], idx)

    def weights(self, idx, gathered):
        out = {}
        for a, full in zip(idx, gathered):
            name = SHARES[a][0]
            out[name] = full if name in ("w_up", "tiny") else full.reshape(-1, D)
        return out

    def pair_sums(self, idx, parts, tag):
        return _sum_pair(parts, _sibling_swap(parts, idx, tag), self.core.reshape(1), idx, tag)

    def chip_hook(self, idx, sums):
        return _chip_hook(sums, idx)

    def finish(self, idx, sums, recvs, tag):
        halves = _sum_chips(sums, recvs, jnp.stack([self.chip, self.core]), idx, tag)
        for a, full in zip(idx, _sibling_share(halves, idx, tag)):
            self.reduced[SHARES[a][0]] = full.reshape(SHARES[a][1:3])


N_DEV = 8
SMALL_ROWS = 40


def _small_allreduce(sm):
    def body(s_ref, o_ref, buf, ssem, rsem):
        x, y, c, _ = _place()
        me = 4 * x + 2 * y + c
        buf[me] = s_ref[...]
        copies = []
        for d in range(1, N_DEV):
            dx, dy, dc = d >> 2, (d >> 1) & 1, d & 1
            to = (x ^ dx, y ^ dy, c ^ dc)
            cp = _rcopy(s_ref, buf.at[me], ssem.at[d - 1], rsem.at[d - 1], to)
            cp.start()
            copies.append(cp)
        for d in range(1, N_DEV):
            src = me ^ d
            _rcopy(s_ref, buf.at[src], ssem.at[d - 1], rsem.at[d - 1], (x, y, c)).wait_recv()
        for cp in copies:
            cp.wait_send()
        acc = buf[0]
        for k in range(1, N_DEV):
            acc = acc + buf[k]
        o_ref[...] = acc

    vm = pl.BlockSpec(memory_space=pltpu.VMEM)
    return pl.pallas_call(
        body, name="small_allreduce", in_specs=[vm], out_specs=vm,
        out_shape=_sds((SMALL_ROWS, D), F32),
        scratch_shapes=[pltpu.VMEM((N_DEV, SMALL_ROWS, D), F32),
                        pltpu.SemaphoreType.DMA((N_DEV - 1,)), pltpu.SemaphoreType.DMA((N_DEV - 1,))],
    )(sm)


SMALL_PLAN = [("norm_pre_mix", D), ("norm_post_mix", D), ("b_in", IN_W), ("attn_sinks", NH), ("conv_dw_b", D),
              ("conv_ln_g", D), ("conv_ln_b", D), ("b_conv_proj", D), ("norm_pre_ffn", D), ("norm_post_ffn", D),
              ("ffn_dw_b", 2 * FFN), ("loss", D), ("meta_tokens", NMETA * D)]


def _pack_small(parts):
    rows = [_to_planes(parts[name].reshape(-1), -(-n // D)) for name, n in SMALL_PLAN]
    used = sum(r.shape[0] for r in rows)
    return jnp.concatenate(rows + [jnp.zeros((SMALL_ROWS - used, D), F32)], 0)


def _unpack_small(packed):
    out, r0 = {}, 0
    for name, n in SMALL_PLAN:
        rows = -(-n // D)
        out[name] = packed[r0:r0 + rows].reshape(-1)[:n].reshape(1, n)
        r0 += rows
    return out


def _adamw_update(w_ref, g_ref, m_ref, v_ref, d_ref, nm_ref, nv_ref):
    g = g_ref[...]
    m = B1 * m_ref[...] + (1.0 - B1) * g
    v = B2 * v_ref[...] + (1.0 - B2) * (g * g)
    nm_ref[...] = m
    nv_ref[...] = v
    m_hat = m / (1.0 - B1 ** STEP)
    v_hat = v / (1.0 - B2 ** STEP)
    d_ref[...] = -LR * (m_hat / (jnp.sqrt(v_hat) + ADAM_EPS) + WD * w_ref[...])


def _adamw_vectors(ws, gs, ms, vs):
    n = len(ws)

    def body(*refs):
        for j in range(n):
            _adamw_update(*[refs[k * n + j] for k in range(7)])

    vm = pl.BlockSpec(memory_space=pltpu.VMEM)
    outs = pl.pallas_call(body, name="adamw_vectors", in_specs=[vm] * (4 * n), out_specs=[vm] * (3 * n),
                          out_shape=[_sds(w.shape, F32) for w in ws] * 3)(*ws, *gs, *ms, *vs)
    return outs[:n], outs[n:2 * n], outs[2 * n:]


def _adamw(w, g, m, v, name):
    rows, cols = w.shape
    tr = _tile(rows, 256, 8) if rows % 8 == 0 else rows

    def body(*refs):
        _adamw_update(*refs)

    spec = pl.BlockSpec((tr, cols), lambda i: (i, 0))
    return pl.pallas_call(
        body, name=name, grid=(rows // tr,), in_specs=[spec] * 4, out_specs=[spec] * 3,
        out_shape=[_sds((rows, cols), F32)] * 3, compiler_params=_params(("arbitrary",)),
    )(w, g, m, v)


NAMES = ["meta_tokens", "norm_pre_mix", "norm_post_mix", "w_in", "b_in", "attn_sinks", "w_attn_proj", "conv_dw_w",
         "conv_dw_b", "conv_ln_g", "conv_ln_b", "w_conv_proj", "b_conv_proj", "w_out", "norm_pre_ffn", "norm_post_ffn",
         "w_up", "ffn_dw_w", "ffn_dw_b", "w_down"]
MATMUL = ("w_in", "w_attn_proj", "w_conv_proj", "w_out", "w_up", "w_down")


def _two_d(a):
    return a.reshape(a.shape[-2:])


def kernel(x, meta_tokens, norm_pre_mix, norm_post_mix, w_in, b_in, attn_sinks, w_attn_proj, conv_dw_w, conv_dw_b, conv_ln_g, conv_ln_b, w_conv_proj, b_conv_proj, w_out, norm_pre_ffn, norm_post_ffn, w_up, ffn_dw_w, ffn_dw_b, w_down, loss_target, m_meta_tokens, m_norm_pre_mix, m_norm_post_mix, m_w_in, m_b_in, m_attn_sinks, m_w_attn_proj, m_conv_dw_w, m_conv_dw_b, m_conv_ln_g, m_conv_ln_b, m_w_conv_proj, m_b_conv_proj, m_w_out, m_norm_pre_ffn, m_norm_post_ffn, m_w_up, m_ffn_dw_w, m_ffn_dw_b, m_w_down, v_meta_tokens, v_norm_pre_mix, v_norm_post_mix, v_w_in, v_b_in, v_attn_sinks, v_w_attn_proj, v_conv_dw_w, v_conv_dw_b, v_conv_ln_g, v_conv_ln_b, v_w_conv_proj, v_b_conv_proj, v_w_out, v_norm_pre_ffn, v_norm_post_ffn, v_w_up, v_ffn_dw_w, v_ffn_dw_b, v_w_down):
    args = locals()
    w = {n: args[n] for n in NAMES}
    m = {n: args["m_" + n] for n in NAMES}
    v = {n: args["v_" + n] for n in NAMES}
    tiny_names = [part[0] for part in TINY_PARTS]
    big = list(MATMUL) + tiny_names

    def shard_2d(a, name):
        return _two_d(a).T if name == "w_in" else _two_d(a)

    own = {n: shard_2d(w[n], n).astype(BF16) for n in MATMUL}
    own["tiny"] = _tiny_pack({n: _two_d(w[n]) for n in tiny_names})
    dist = _Dist([own[n] for n, _, _, _ in SHARES])
    W = {n: _two_d(w[n]) for n in SMALL}
    W.update(dist.weights(FIRST_SHARES, _alone(dist.gather_hook(FIRST_SHARES), "gather_first")))

    loss_cols, grad_x, grads = local_step(x[0], loss_target[0], W, dist)

    small = dict(grads)
    small["loss"] = loss_cols
    g_small = _unpack_small(_small_allreduce(_pack_small(small)))
    loss = jnp.sum(g_small["loss"])
    g_big = {n: dist.reduced[n] for n in MATMUL}
    g_big.update(_tiny_unpack(dist.reduced["tiny"]))
    g_big["meta_tokens"] = lax.dynamic_slice(g_small["meta_tokens"].reshape(NMETA, D), (0, dist.chip * DQ), (NMETA, DQ))

    g, delta, new_m, new_v = {}, {}, {}, {}
    for n in big:
        shape = w[n].shape
        back = (lambda a: a.T.reshape(shape)) if n == "w_in" else (lambda a: a.reshape(shape))
        outs = _adamw(shard_2d(w[n], n), g_big[n], shard_2d(m[n], n), shard_2d(v[n], n), "adamw_" + n)
        g[n], delta[n], new_m[n], new_v[n] = (back(a) for a in (g_big[n], *outs))
    ud, um, uv = _adamw_vectors(*[[_two_d(d[n]) for n in SMALL] for d in (w, g_small, m, v)])
    for j, n in enumerate(SMALL):
        g[n], delta[n], new_m[n], new_v[n] = g_small[n], ud[j], um[j], uv[j]

    return (loss, grad_x[None], *[g[n] for n in NAMES], *[delta[n] for n in NAMES],
            *[new_m[n] for n in NAMES], *[new_v[n] for n in NAMES])
```

```python
import jax
import jax.numpy as jnp
from jax import lax
from jax.experimental import pallas as pl
from jax.experimental.pallas import tpu as pltpu

F32, BF16 = jnp.float32, jnp.bfloat16

D = 1024
NH, NKV, HD = 16, 2, 64
GH = NH // NKV
NMETA, BLK = 16, 128
PAD = BLK - NMETA
ROT = HD // 4
THETA = 500000.0
CONV_K = 31
FFN = 2816
FFN_K = 3
IN_W = 5376
QKV_W, GLU_W, GATE_W = 1280, 2048, 2048
RMS_EPS, LN_EPS, NEG = 1e-6, 1e-5, -1e30
LR, B1, B2, ADAM_EPS, WD, STEP = 0.001, 0.9, 0.999, 1e-08, 0.01, 10

VMEM_LIMIT = 56 * 2 ** 20
MESH = pl.DeviceIdType.MESH

NT_DIMS = (((1,), (1,)), ((), ()))
TN_DIMS = (((0,), (0,)), ((), ()))


def _params(sem, **kw):
    return pltpu.CompilerParams(dimension_semantics=sem, vmem_limit_bytes=VMEM_LIMIT, **kw)


def _tile(n, pref, mult=16):
    for t in range(min(pref, n), 0, -1):
        if n % t == 0 and t % mult == 0:
            return t
    return n


def _row(tr, w, col=0):
    return pl.BlockSpec((tr, w), lambda i: (i, col))


def _rrow(tr, w, nt, col=0):
    return pl.BlockSpec((tr, w), lambda t: (nt - 1 - t, col))


def _const(shape):
    return pl.BlockSpec(shape, lambda *_: (0,) * len(shape))


def _sds(shape, dt):
    return jax.ShapeDtypeStruct(shape, dt)


class _Hook:
    def __init__(self, operands, out_shape, n_sem, start, finish, mid=None):
        self.operands, self.out_shape, self.n_sem = list(operands), list(out_shape), n_sem
        self.start, self.mid, self.finish = start, mid, finish

    def scratch(self):
        return [pltpu.SemaphoreType.DMA((self.n_sem,)), pltpu.SemaphoreType.DMA((self.n_sem,))]


def _call(body, hook, steps, *, name, grid, in_specs, out_specs, out_shape, operands, semantics, scratch_shapes=()):
    in_specs, out_specs, out_shape = list(in_specs), list(out_specs), list(out_shape)
    if hook is None:
        return pl.pallas_call(body, name=name, grid=grid, in_specs=in_specs, out_specs=out_specs, out_shape=out_shape,
                              scratch_shapes=list(scratch_shapes), compiler_params=_params(semantics))(*operands)
    n_in, n_out, n_hi, n_ho = len(in_specs), len(out_specs), len(hook.operands), len(hook.out_shape)

    def wrapped(*refs):
        ins, hi = refs[:n_in], refs[n_in:n_in + n_hi]
        o0 = n_in + n_hi
        outs, ho = refs[o0:o0 + n_out], refs[o0 + n_out:o0 + n_out + n_ho]
        scratch, (ssem, rsem) = refs[o0 + n_out + n_ho:len(refs) - 2], refs[len(refs) - 2:]
        first, middle, last = steps()

        @pl.when(first)
        def _():
            hook.start(hi, ho, ssem, rsem)

        body(*ins, *outs, *scratch)
        if hook.mid is not None:
            @pl.when(middle)
            def _():
                hook.mid(hi, ho, ssem, rsem)

        @pl.when(last)
        def _():
            hook.finish(hi, ho, ssem, rsem)

    any_spec = pl.BlockSpec(memory_space=pl.ANY)
    return pl.pallas_call(
        wrapped, name=name, grid=grid, in_specs=in_specs + [any_spec] * n_hi, out_specs=out_specs + [any_spec] * n_ho,
        out_shape=out_shape + hook.out_shape, scratch_shapes=list(scratch_shapes) + hook.scratch(),
        compiler_params=_params(semantics))(*operands, *hook.operands)


def _alone(hook, name):
    n_hi = len(hook.operands)

    def body(*refs):
        hi, ho, (ssem, rsem) = refs[:n_hi], refs[n_hi:len(refs) - 2], refs[len(refs) - 2:]
        hook.start(hi, ho, ssem, rsem)
        if hook.mid is not None:
            hook.mid(hi, ho, ssem, rsem)
        hook.finish(hi, ho, ssem, rsem)

    any_spec = pl.BlockSpec(memory_space=pl.ANY)
    return pl.pallas_call(body, name=name, in_specs=[any_spec] * n_hi, out_specs=[any_spec] * len(hook.out_shape),
                          out_shape=hook.out_shape, scratch_shapes=hook.scratch())(*hook.operands)


def _steps_1d(n, mid):
    def steps():
        i = pl.program_id(0)
        return i == 0, i == min(mid, n - 1), i == n - 1
    return steps


def _rms(x, g):
    r = lax.rsqrt(jnp.mean(x * x, -1, keepdims=True) + RMS_EPS)
    return x * r * g, r


def _rms_bwd(dy, x, r, g):
    gy = dy * g
    return r * gy - x * (r * r * r) * jnp.mean(x * gy, -1, keepdims=True)


def _colsum(x):
    return jnp.sum(x, axis=0, keepdims=True)


def _rope(x, c, sa, sb):
    n = x.shape[1]
    return x * c + pltpu.roll(x, n - 8, 1) * sa + pltpu.roll(x, 8, 1) * sb


def _rope_bwd(d, c, sa, sb):
    n = d.shape[1]
    return d * c + pltpu.roll(d * sa, 8, 1) + pltpu.roll(d * sb, n - 8, 1)


def _rope_tables(R):
    half = ROT // 2
    lane = jnp.arange(2 * HD) % HD
    inv = THETA ** (-(lane % half).astype(F32) * 2.0 / ROT)
    pos = (jnp.arange(R) - PAD).astype(F32)
    ang = pos[:, None] * inv[None, :]
    cos, sin = jnp.cos(ang), jnp.sin(ang)
    c = jnp.where(lane < ROT, cos, 1.0)
    sa = jnp.where(lane < half, -sin, 0.0)
    sb = jnp.where((lane >= half) & (lane < ROT), sin, 0.0)
    return c, sa, sb


IN_CHUNKS = ([(0, 512, True), (512, 1024, True), (1024, 1152, True), (1152, 1280, False)]
             + [(c, c + 512, False) for c in range(1280, IN_W, 512)])


def _in_proj(h0, g_pre, w_in, b_in, rope, tr, hook=None):
    R = h0.shape[0]
    nt = R // tr

    def body(h_ref, g_ref, w_ref, b_ref, c_ref, sa_ref, sb_ref, qkv_ref, glu_ref, gate_ref, n1_ref):
        n, _ = _rms(h_ref[...], g_ref[...])
        nb = n.astype(BF16)
        n1_ref[...] = nb
        for c0, c1, rot in IN_CHUNKS:
            acc = lax.dot_general(nb, w_ref[c0:c1, :], NT_DIMS, preferred_element_type=F32) + b_ref[:, c0:c1]
            if rot:
                reps = (c1 - c0) // 128
                acc = _rope(acc, jnp.tile(c_ref[...], (1, reps)), jnp.tile(sa_ref[...], (1, reps)),
                            jnp.tile(sb_ref[...], (1, reps)))
            val = acc.astype(BF16)
            if c1 <= QKV_W:
                qkv_ref[:, c0:c1] = val
            elif c1 <= QKV_W + GLU_W:
                glu_ref[:, c0 - QKV_W:c1 - QKV_W] = val
            else:
                gate_ref[:, c0 - QKV_W - GLU_W:c1 - QKV_W - GLU_W] = val

    return _call(
        body, hook, _steps_1d(nt, (3 * nt) // 4), name="in_proj", grid=(nt,),
        in_specs=[_row(tr, D), _const((1, D)), _const((IN_W, D)), _const((1, IN_W)),
                  _row(tr, 128), _row(tr, 128), _row(tr, 128)],
        out_specs=[_row(tr, QKV_W), _row(tr, GLU_W), _row(tr, GATE_W), _row(tr, D)],
        out_shape=[_sds((R, QKV_W), BF16), _sds((R, GLU_W), BF16), _sds((R, GATE_W), BF16), _sds((R, D), BF16)],
        operands=(h0, g_pre, w_in, b_in, *rope), semantics=("arbitrary",))


def _attn_mask(n, keys_first=False):
    shape = (3 * BLK, BLK) if keys_first else (BLK, 3 * BLK)
    qi = lax.broadcasted_iota(jnp.int32, shape, 1 if keys_first else 0)
    kj = lax.broadcasted_iota(jnp.int32, shape, 0 if keys_first else 1)
    tq = n * BLK + qi - PAD
    t_meta = kj - PAD
    t_loc = (n - 1) * BLK + (kj - BLK) - PAD
    meta_ok = (kj < BLK) & (t_meta >= 0) & (t_meta <= tq)
    loc_ok = (kj >= BLK) & (t_loc >= NMETA) & (t_loc <= tq) & (tq - t_loc < BLK)
    return meta_ok | loc_ok


def _dup_heads(ref0, refp, refc, low, transposed=False):
    a = jnp.concatenate([ref0[...], refp[...], refc[...]], 0).astype(F32)
    sw = pltpu.roll(a, HD, 1)
    heads = [jnp.where(low, a, sw), jnp.where(low, sw, a)]
    return [(h.T if transposed else h).astype(BF16) for h in heads]


def _kv_specs(nb, rev):
    def blk(col, which):
        def idx(t):
            n = nb - 1 - t if rev else t
            return ({"meta": 0, "prev": jnp.maximum(n - 1, 0), "own": n}[which], col)
        return pl.BlockSpec((BLK, BLK), idx)
    return [blk(col, w) for col in (8, 9) for w in ("meta", "prev", "own")]


def _attn_fwd(qkv, sinks, hook=None):
    R = qkv.shape[0]
    nb = R // BLK

    def body(s_ref, q_ref, k0, kp, kc, v0, vp, vc, o_ref, lse_ref):
        n = pl.program_id(0)
        lane = lax.broadcasted_iota(jnp.int32, (1, BLK), 1)
        low = lane < HD
        kd, vd_t = _dup_heads(k0, kp, kc, low), _dup_heads(v0, vp, vc, low, transposed=True)
        mask = _attn_mask(n, keys_first=True)
        zero = jnp.zeros((), BF16)
        lses = []
        for g in range(NKV):
            tiles = []
            for pair in range(GH // 2 * g, GH // 2 * (g + 1)):
                qp = q_ref[:, pair * BLK:(pair + 1) * BLK] * jnp.asarray(HD ** -0.5, BF16)
                tiles += [jnp.where(low, qp, zero), jnp.where(low, zero, qp)]
            st = lax.dot_general(kd[g], jnp.concatenate(tiles, 0), NT_DIMS, preferred_element_type=F32)
            ps, inv = [], []
            for j in range(GH):
                s = jnp.where(mask, st[:, j * BLK:(j + 1) * BLK], NEG)
                sk = s_ref[GH * g + j]
                m = jnp.maximum(jnp.max(s, 0, keepdims=True), sk)
                p = jnp.exp(s - m)
                l = jnp.sum(p, 0, keepdims=True) + jnp.exp(sk - m)
                ps.append(p.astype(BF16))
                inv.append(1.0 / l)
                lses.append(m + jnp.log(l))
            ot = jnp.dot(vd_t[g], jnp.concatenate(ps, 1), preferred_element_type=F32)
            for j in range(GH // 2):
                pair = GH // 2 * g + j
                o = [(ot[:, h * BLK:(h + 1) * BLK] * inv[h]).T for h in (2 * j, 2 * j + 1)]
                o_ref[:, pair * BLK:(pair + 1) * BLK] = jnp.where(low, o[0], o[1]).astype(BF16)
        lse_ref[...] = jnp.concatenate(lses, 0)

    return _call(
        body, hook, _steps_1d(nb, (3 * nb) // 4), name="attn_fwd", grid=(nb,),
        in_specs=[pl.BlockSpec(memory_space=pltpu.SMEM), pl.BlockSpec((BLK, D), lambda n: (n, 0))] + _kv_specs(nb, False),
        out_specs=[_row(BLK, D), _row(NH, BLK)],
        out_shape=[_sds((R, D), BF16), _sds((nb * NH, BLK), F32)],
        operands=(sinks, qkv, *([qkv] * 6)), semantics=("arbitrary",))


CONV_TCH, CONV_SUB, HALO = 256, 64, 32


def _tap_windows(buf, r0, offset_of):
    span = CONV_SUB + HALO
    x = buf[pl.ds(r0, span), :]
    by_phase = {}
    for k in range(CONV_K):
        by_phase.setdefault(offset_of(k) % 8, []).append(k)
    for phase, taps in sorted(by_phase.items()):
        y = x if phase == 0 else pltpu.roll(x, span - phase, 0)
        for k in taps:
            d = offset_of(k) - phase
            yield k, y[d:d + CONV_SUB, :]


def _conv_fwd(glu, w, b, tr, hook=None):
    R = glu.shape[0]
    nc = D // CONV_TCH

    def body(a_ref, g_ref, w_ref, b_ref, o_ref, buf):
        i = pl.program_id(1)

        @pl.when(i == 0)
        def _():
            buf[0:HALO, :] = jnp.zeros((HALO, CONV_TCH), F32)

        @pl.when(i > 0)
        def _():
            buf[0:HALO, :] = buf[tr:tr + HALO, :]

        row = i * tr + lax.broadcasted_iota(jnp.int32, (tr, 1), 0)
        a, g = a_ref[...].astype(F32), g_ref[...].astype(F32)
        buf[HALO:HALO + tr, :] = jnp.where(row >= PAD, a * jax.nn.sigmoid(g), 0.0)
        for r0 in range(0, tr, CONV_SUB):
            acc = jnp.broadcast_to(b_ref[...], (CONV_SUB, CONV_TCH))
            for k, win in _tap_windows(buf, r0, lambda k: HALO - (CONV_K - 1) + k):
                acc = acc + w_ref[k:k + 1, :] * win
            o_ref[r0:r0 + CONV_SUB, :] = acc.astype(BF16)

    nt = R // tr

    def steps():
        c, i = pl.program_id(0), pl.program_id(1)
        return (c == 0) & (i == 0), (c == nc - 1) & (i == 0), (c == nc - 1) & (i == nt - 1)

    return _call(
        body, hook, steps, name="conv_fwd", grid=(nc, nt),
        in_specs=[pl.BlockSpec((tr, CONV_TCH), lambda c, i: (i, c)),
                  pl.BlockSpec((tr, CONV_TCH), lambda c, i: (i, nc + c)),
                  pl.BlockSpec((None, HALO, CONV_TCH), lambda c, i: (c, 0, 0)),
                  pl.BlockSpec((1, CONV_TCH), lambda c, i: (0, c))],
        out_specs=[pl.BlockSpec((tr, CONV_TCH), lambda c, i: (i, c))],
        out_shape=[_sds((R, D), BF16)],
        scratch_shapes=[pltpu.VMEM((tr + HALO, CONV_TCH), F32)],
        operands=(glu, glu, w, b), semantics=("arbitrary", "arbitrary"))


def _ln_silu(c1, lg, lb):
    mu = jnp.mean(c1, -1, keepdims=True)
    xc = c1 - mu
    rs = lax.rsqrt(jnp.mean(xc * xc, -1, keepdims=True) + LN_EPS)
    xh = xc * rs
    c2 = xh * lg + lb
    sg = jax.nn.sigmoid(c2)
    return xh, rs, c2, sg


def _mix_out(attn, c1, gates, h0, w_ap, w_cp, w_out, lg, lb, b_cp, g_post, g_ffn, tr):
    R = attn.shape[0]

    def body(at_ref, c1_ref, ga_ref, gc_ref, h0_ref, wap, wcp, wo, lg_ref, lb_ref, bcp, gp, gf,
             ao_ref, co_ref, c3_ref, mg_ref, mix_ref, h1_ref, n2_ref):
        ao = jnp.dot(at_ref[...], wap[...], preferred_element_type=F32)
        _, _, c2, sg = _ln_silu(c1_ref[...].astype(F32), lg_ref[...], lb_ref[...])
        c3 = (c2 * sg).astype(BF16)
        c3_ref[...] = c3
        co = jnp.dot(c3, wcp[...], preferred_element_type=F32) + bcp[...]
        ao_b, co_b = ao.astype(BF16), co.astype(BF16)
        ao_ref[...] = ao_b
        co_ref[...] = co_b
        merged = (jax.nn.sigmoid(ga_ref[...].astype(F32)) * ao_b.astype(F32)
                  + jax.nn.sigmoid(gc_ref[...].astype(F32)) * co_b.astype(F32)).astype(BF16)
        mg_ref[...] = merged
        mix = jnp.dot(merged, wo[...], preferred_element_type=F32).astype(BF16)
        mix_ref[...] = mix
        y, _ = _rms(mix.astype(F32), gp[...])
        h1 = h0_ref[...] + y
        h1_ref[...] = h1
        n2, _ = _rms(h1, gf[...])
        row = pl.program_id(0) * tr + lax.broadcasted_iota(jnp.int32, (tr, 1), 0)
        n2_ref[...] = jnp.where(row >= PAD, n2, 0.0).astype(BF16)

    vec = _const((1, D))
    return pl.pallas_call(
        body, name="mix_out", grid=(R // tr,),
        in_specs=[_row(tr, D), _row(tr, D), _row(tr, D, 0), _row(tr, D, 1), _row(tr, D),
                  _const((D, D)), _const((D, D)), _const((D, D)), vec, vec, vec, vec, vec],
        out_specs=[_row(tr, D)] * 7,
        out_shape=[_sds((R, D), BF16)] * 5 + [_sds((R, D), F32), _sds((R, D), BF16)],
        compiler_params=_params(("arbitrary",)),
    )(attn, c1, gates, gates, h0, w_ap, w_cp, w_out, lg, lb, b_cp, g_post, g_ffn)


FFN_CH = 256
N_CHIPS = 4
UPQ = 2 * FFN // N_CHIPS
UP_CHUNKS = [(q, c0, min(c0 + 512, UPQ)) for q in range(N_CHIPS // 2) for c0 in range(0, UPQ, 512)]
TINY_ROWS, TINY_CONV, TINY_FFN, TINY_META = 64, 0, 32, 40


def _shift_down(x, k, halo):
    tr = x.shape[0]
    row = lax.broadcasted_iota(jnp.int32, (tr, 1), 0)
    y = pltpu.roll(x, k, 0)
    for j in range(k):
        y = jnp.where(row == j, halo[8 - k + j:8 - k + j + 1, :], y)
    return y


def _shift_up(x, k, halo):
    tr = x.shape[0]
    row = lax.broadcasted_iota(jnp.int32, (tr, 1), 0)
    y = pltpu.roll(x, tr - k, 0)
    for j in range(k):
        y = jnp.where(row == tr - k + j, halo[j:j + 1, :], y)
    return y


def _conv3(x, halo, w, b):
    return w[2:3, :] * x + w[1:2, :] * _shift_down(x, 1, halo) + w[0:1, :] * _shift_down(x, 2, halo) + b


def _ffn_up(n2, w_up, fw, fb, tr):
    R = n2.shape[0]

    def body(n_ref, w_ref, fw_ref, fb_ref, up_ref, act_ref, carry):
        @pl.when(pl.program_id(0) == 0)
        def _():
            carry[...] = jnp.zeros_like(carry)

        nb = n_ref[...]
        for q, c0, c1 in UP_CHUNKS:
            us = []
            for qq in (q, q + N_CHIPS // 2):
                cs = slice(qq * UPQ + c0, qq * UPQ + c1)
                x = jnp.dot(nb, w_ref[qq, :, c0:c1], preferred_element_type=F32).astype(BF16)
                up_ref[:, cs] = x
                x = x.astype(F32)
                us.append(_conv3(x, carry[:, cs], fw_ref[qq, TINY_FFN:TINY_FFN + 8, c0:c1], fb_ref[:, cs]))
                carry[:, cs] = x[tr - 8:tr, :]
            act_ref[:, q * UPQ + c0:q * UPQ + c1] = (us[0] * jax.nn.sigmoid(us[0]) * us[1]).astype(BF16)

    return pl.pallas_call(
        body, name="ffn_up", grid=(R // tr,),
        in_specs=[_row(tr, D), _const((N_CHIPS, D, UPQ)), _const((N_CHIPS, TINY_ROWS, UPQ)), _const((1, 2 * FFN))],
        out_specs=[_row(tr, 2 * FFN), _row(tr, FFN)],
        out_shape=[_sds((R, 2 * FFN), BF16), _sds((R, FFN), BF16)],
        scratch_shapes=[pltpu.VMEM((8, 2 * FFN), F32)],
        compiler_params=_params(("arbitrary",)),
    )(n2, w_up, fw, fb)


def _ffn_down(act, w_down, h1, tgt, g_post, tr):
    R = act.shape[0]
    m = tr // BLK

    def body(a_ref, w_ref, h1_ref, g_ref, *rest):
        t_refs, (dh2_ref, dffn_ref, loss_ref, dg_ref) = rest[:m], rest[m:]

        @pl.when(pl.program_id(0) == 0)
        def _():
            loss_ref[...] = jnp.zeros_like(loss_ref)
            dg_ref[...] = jnp.zeros_like(dg_ref)

        f = jnp.dot(a_ref[...], w_ref[...], preferred_element_type=F32)
        g = g_ref[...]
        y, r = _rms(f, g)
        row = pl.program_id(0) * tr + lax.broadcasted_iota(jnp.int32, (tr, 1), 0)
        tgt_rows = jnp.concatenate([t[...] for t in t_refs], 0)
        e = jnp.where(row >= BLK, h1_ref[...] + y - tgt_rows, 0.0)
        loss_ref[...] += _colsum(e * e) * (0.5 / D)
        dy = e * (1.0 / D)
        dh2_ref[...] = dy
        dffn_ref[...] = _rms_bwd(dy, f, r, g).astype(BF16)
        dg_ref[...] += _colsum(dy * f * r)

    return pl.pallas_call(
        body, name="ffn_down", grid=(R // tr,),
        in_specs=[_row(tr, FFN), _const((FFN, D)), _row(tr, D), _const((1, D))]
                 + [pl.BlockSpec((BLK, D), lambda i, k=k: (jnp.maximum(m * i - 1 + k, 0), 0)) for k in range(m)],
        out_specs=[_row(tr, D), _row(tr, D), _const((1, D)), _const((1, D))],
        out_shape=[_sds((R, D), F32), _sds((R, D), BF16), _sds((1, D), F32), _sds((1, D), F32)],
        compiler_params=_params(("arbitrary",)),
    )(act, w_down, h1, g_post, *([tgt] * m))


def _ffn_bwd_act(dffn, w_down, up, fw, fb, tr):
    R = dffn.shape[0]
    nt = R // tr

    def body(d_ref, w_ref, up_ref, hal_ref, fw_ref, fb_ref, dup_ref, dfw_ref, dfb_ref, carry):
        t = pl.program_id(0)
        i = nt - 1 - t

        @pl.when(t == 0)
        def _():
            carry[...] = jnp.zeros_like(carry)
            dfw_ref[...] = jnp.zeros_like(dfw_ref)
            dfb_ref[...] = jnp.zeros_like(dfb_ref)

        dff = d_ref[...]
        row = i * tr + lax.broadcasted_iota(jnp.int32, (tr, 1), 0)
        first = i == 0
        for q, c0, c1 in UP_CHUNKS:
            dact = lax.dot_general(dff, w_ref[q * UPQ + c0:q * UPQ + c1, :], NT_DIMS, preferred_element_type=F32)
            chips = (q, q + N_CHIPS // 2)
            xs, us = [], []
            for qq in chips:
                cs = slice(qq * UPQ + c0, qq * UPQ + c1)
                x = up_ref[:, cs].astype(F32)
                halo = jnp.where(first, 0.0, hal_ref[:, cs].astype(F32))
                x1, x2 = _shift_down(x, 1, halo), _shift_down(x, 2, halo)
                w = fw_ref[qq, TINY_FFN:TINY_FFN + 8, c0:c1]
                us.append(w[2:3, :] * x + w[1:2, :] * x1 + w[0:1, :] * x2 + fb_ref[:, cs])
                xs.append((x, x1, x2))
            sg = jax.nn.sigmoid(us[0])
            silu = us[0] * sg
            dus = [dact * us[1] * sg * (1.0 + us[0] * (1.0 - sg)), dact * silu]
            for (x, x1, x2), du, qq in zip(xs, dus, chips):
                cs = slice(qq * UPQ + c0, qq * UPQ + c1)
                w = fw_ref[qq, TINY_FFN:TINY_FFN + 8, c0:c1]
                nxt = carry[:, cs]
                dx = w[2:3, :] * du + w[1:2, :] * _shift_up(du, 1, nxt) + w[0:1, :] * _shift_up(du, 2, nxt)
                dup_ref[:, cs] = jnp.where(row >= PAD, dx, 0.0).astype(BF16)
                dfw_ref[qq, 0:1, c0:c1] += _colsum(x2 * du)
                dfw_ref[qq, 1:2, c0:c1] += _colsum(x1 * du)
                dfw_ref[qq, 2:3, c0:c1] += _colsum(x * du)
                dfb_ref[:, cs] += _colsum(du)
                carry[:, cs] = du[0:8, :]

    halo_spec = pl.BlockSpec((8, 2 * FFN), lambda t: (jnp.maximum((nt - 1 - t) * (tr // 8) - 1, 0), 0))
    return pl.pallas_call(
        body, name="ffn_bwd_act", grid=(nt,),
        in_specs=[_rrow(tr, D, nt), _const((FFN, D)), _rrow(tr, 2 * FFN, nt), halo_spec,
                  _const((N_CHIPS, TINY_ROWS, UPQ)), _const((1, 2 * FFN))],
        out_specs=[_rrow(tr, 2 * FFN, nt), _const((N_CHIPS, 8, UPQ)), _const((1, 2 * FFN))],
        out_shape=[_sds((R, 2 * FFN), BF16), _sds((N_CHIPS, 8, UPQ), F32), _sds((1, 2 * FFN), F32)],
        scratch_shapes=[pltpu.VMEM((8, 2 * FFN), F32)],
        compiler_params=_params(("arbitrary",)),
    )(dffn, w_down, up, up, fw, fb)


def _ffn_bwd_in(dup, w_up, h1, dh2, mix, g_ffn, g_post, tr):
    R = dup.shape[0]

    def body(d_ref, w_ref, h1_ref, dh2_ref, mix_ref, gf_ref, gp_ref, dh1_ref, dmix_ref, dgf_ref, dgp_ref):
        @pl.when(pl.program_id(0) == 0)
        def _():
            dgf_ref[...] = jnp.zeros_like(dgf_ref)
            dgp_ref[...] = jnp.zeros_like(dgp_ref)

        dn2 = sum(lax.dot_general(d_ref[:, q * UPQ:(q + 1) * UPQ], w_ref[q], NT_DIMS, preferred_element_type=F32)
                  for q in range(N_CHIPS))
        h1 = h1_ref[...]
        _, r2 = _rms(h1, gf_ref[...])
        dh1 = dh2_ref[...] + _rms_bwd(dn2, h1, r2, gf_ref[...])
        dgf_ref[...] += _colsum(dn2 * h1 * r2)
        dh1_ref[...] = dh1
        m = mix_ref[...].astype(F32)
        _, rm = _rms(m, gp_ref[...])
        dmix_ref[...] = _rms_bwd(dh1, m, rm, gp_ref[...]).astype(BF16)
        dgp_ref[...] += _colsum(dh1 * m * rm)

    vec = _const((1, D))
    return pl.pallas_call(
        body, name="ffn_bwd_in", grid=(R // tr,),
        in_specs=[_row(tr, 2 * FFN), _const((N_CHIPS, D, UPQ)), _row(tr, D), _row(tr, D), _row(tr, D), vec, vec],
        out_specs=[_row(tr, D), _row(tr, D), vec, vec],
        out_shape=[_sds((R, D), F32), _sds((R, D), BF16), _sds((1, D), F32), _sds((1, D), F32)],
        compiler_params=_params(("arbitrary",)),
    )(dup, w_up, h1, dh2, mix, g_ffn, g_post)


def _mix_bwd(dmix, ao, co, gates, c1, w_out, w_ap, w_cp, lg, lb, tr):
    R = dmix.shape[0]

    def body(dm_ref, ao_ref, co_ref, ga_ref, gc_ref, c1_ref, wo, wap, wcp, lg_ref, lb_ref,
             dao_ref, dco_ref, dgate_ref, dattn_ref, dc1_ref, dbcp_ref, dlg_ref, dlb_ref, dcb_ref):
        @pl.when(pl.program_id(0) == 0)
        def _():
            for ref in (dbcp_ref, dlg_ref, dlb_ref, dcb_ref):
                ref[...] = jnp.zeros_like(ref)

        dmg = lax.dot_general(dm_ref[...], wo[...], NT_DIMS, preferred_element_type=F32)
        sa = jax.nn.sigmoid(ga_ref[...].astype(F32))
        sc = jax.nn.sigmoid(gc_ref[...].astype(F32))
        dao = (dmg * sa).astype(BF16)
        dco = (dmg * sc).astype(BF16)
        dao_ref[...] = dao
        dco_ref[...] = dco
        dgate_ref[:, 0:D] = (dmg * ao_ref[...].astype(F32) * sa * (1.0 - sa)).astype(BF16)
        dgate_ref[:, D:2 * D] = (dmg * co_ref[...].astype(F32) * sc * (1.0 - sc)).astype(BF16)
        dbcp_ref[...] += _colsum(dco.astype(F32))
        dattn_ref[...] = lax.dot_general(dao, wap[...], NT_DIMS, preferred_element_type=F32).astype(BF16)
        dc3 = lax.dot_general(dco, wcp[...], NT_DIMS, preferred_element_type=F32)
        xh, rs, c2, sg = _ln_silu(c1_ref[...].astype(F32), lg_ref[...], lb_ref[...])
        dc2 = dc3 * sg * (1.0 + c2 * (1.0 - sg))
        dlg_ref[...] += _colsum(dc2 * xh)
        dlb_ref[...] += _colsum(dc2)
        dxh = dc2 * lg_ref[...]
        dc1 = rs * (dxh - jnp.mean(dxh, -1, keepdims=True) - xh * jnp.mean(dxh * xh, -1, keepdims=True))
        dc1_ref[...] = dc1
        dcb_ref[...] += _colsum(dc1)

    vec = _const((1, D))
    return pl.pallas_call(
        body, name="mix_bwd", grid=(R // tr,),
        in_specs=[_row(tr, D), _row(tr, D), _row(tr, D), _row(tr, D, 0), _row(tr, D, 1), _row(tr, D),
                  _const((D, D)), _const((D, D)), _const((D, D)), vec, vec],
        out_specs=[_row(tr, D), _row(tr, D), _row(tr, 2 * D), _row(tr, D), _row(tr, D), vec, vec, vec, vec],
        out_shape=[_sds((R, D), BF16), _sds((R, D), BF16), _sds((R, 2 * D), BF16), _sds((R, D), BF16),
                   _sds((R, D), F32)] + [_sds((1, D), F32)] * 4,
        compiler_params=_params(("arbitrary",)),
    )(dmix, ao, co, gates, gates, c1, w_out, w_ap, w_cp, lg, lb)


def _conv_bwd(dc1, glu, w, tr, hook=None):
    R = dc1.shape[0]
    nt, nc = R // tr, D // CONV_TCH

    def body(d_ref, a_ref, g_ref, w_ref, dglu_a, dglu_g, dw_ref, buf, dw_acc):
        t = pl.program_id(1)
        i = nt - 1 - t

        @pl.when(t == 0)
        def _():
            buf[tr:tr + HALO, :] = jnp.zeros((HALO, CONV_TCH), F32)
            dw_acc[...] = jnp.zeros_like(dw_acc)

        @pl.when(t > 0)
        def _():
            buf[tr:tr + HALO, :] = buf[0:HALO, :]

        buf[0:tr, :] = d_ref[...]
        for r0 in range(0, tr, CONV_SUB):
            rs = slice(r0, r0 + CONV_SUB)
            row = i * tr + r0 + lax.broadcasted_iota(jnp.int32, (CONV_SUB, 1), 0)
            a, g = a_ref[rs, :].astype(F32), g_ref[rs, :].astype(F32)
            sg = jax.nn.sigmoid(g)
            glu = jnp.where(row >= PAD, a * sg, 0.0)
            acc = jnp.zeros((CONV_SUB, CONV_TCH), F32)
            for k, win in _tap_windows(buf, r0, lambda k: CONV_K - 1 - k):
                acc = acc + w_ref[k:k + 1, :] * win
                dw_acc[k] += jnp.sum((glu * win).reshape(CONV_SUB // 8, 8, CONV_TCH), axis=0)
            dglu = jnp.where(row >= PAD, acc, 0.0)
            dglu_a[rs, :] = (dglu * sg).astype(BF16)
            dglu_g[rs, :] = (dglu * a * sg * (1.0 - sg)).astype(BF16)

        @pl.when(t == nt - 1)
        def _():
            dw_ref[...] = jnp.sum(dw_acc[...], axis=1)

    def rspec(col0):
        return pl.BlockSpec((tr, CONV_TCH), lambda c, t: (nt - 1 - t, col0 + c))

    def steps():
        c, t = pl.program_id(0), pl.program_id(1)
        return (c == 0) & (t == 0), False, (c == nc - 1) & (t == nt - 1)

    return _call(
        body, hook, steps, name="conv_bwd", grid=(nc, nt),
        in_specs=[rspec(0), rspec(0), rspec(nc), pl.BlockSpec((None, HALO, CONV_TCH), lambda c, t: (c, 0, 0))],
        out_specs=[rspec(0), rspec(0), pl.BlockSpec((None, HALO, CONV_TCH), lambda c, t: (c, 0, 0))],
        out_shape=[_sds((R, D), BF16), _sds((R, D), BF16), _sds((N_CHIPS, HALO, CONV_TCH), F32)],
        scratch_shapes=[pltpu.VMEM((tr + HALO, CONV_TCH), F32), pltpu.VMEM((HALO, 8, CONV_TCH), F32)],
        operands=(dc1, glu, glu, w), semantics=("arbitrary", "arbitrary"))


def _attn_bwd(qkv, do, lse, sinks, rope, hook=None):
    R = qkv.shape[0]
    nb = R // BLK

    def body(s_ref, q_ref, k0, kp, kc, v0, vp, vc, do_ref, lse_ref, c_ref, sa_ref, sb_ref,
             dqkv_ref, dsink_ref, car_k, car_v, met_k, met_v):
        t = pl.program_id(0)
        n = nb - 1 - t

        @pl.when(t == 0)
        def _():
            for ref in (car_k, car_v, met_k, met_v, dsink_ref):
                ref[...] = jnp.zeros_like(ref)

        lane = lax.broadcasted_iota(jnp.int32, (1, BLK), 1)
        low = lane < HD
        kd, vd = _dup_heads(k0, kp, kc, low), _dup_heads(v0, vp, vc, low)
        kd_t = _dup_heads(k0, kp, kc, low, transposed=True)
        mask = _attn_mask(n, keys_first=True)
        tabs = (c_ref[...], sa_ref[...], sb_ref[...])
        zero = jnp.zeros((), BF16)
        dk_acc, dv_acc = [], []
        dsink = jnp.zeros((1, BLK), F32)
        for g in range(NKV):
            q_tiles, do_tiles = [], []
            for pair in range(GH // 2 * g, GH // 2 * (g + 1)):
                cs = slice(pair * BLK, (pair + 1) * BLK)
                qp, dop = q_ref[:, cs] * jnp.asarray(HD ** -0.5, BF16), do_ref[:, cs]
                q_tiles += [jnp.where(low, qp, zero), jnp.where(low, zero, qp)]
                do_tiles += [jnp.where(low, dop, zero), jnp.where(low, zero, dop)]
            qs, dos = jnp.concatenate(q_tiles, 0), jnp.concatenate(do_tiles, 0)
            st = lax.dot_general(kd[g], qs, NT_DIMS, preferred_element_type=F32)
            dpt = lax.dot_general(vd[g], dos, NT_DIMS, preferred_element_type=F32)
            ps, dss = [], []
            for j in range(GH):
                h = GH * g + j
                cs = slice(j * BLK, (j + 1) * BLK)
                lse_h = lse_ref[h:h + 1, :]
                p = jnp.where(mask, jnp.exp(st[:, cs] - lse_h), 0.0)
                dp = dpt[:, cs]
                delta = jnp.sum(p * dp, 0, keepdims=True)
                ps.append(p.astype(BF16))
                dss.append((p * (dp - delta)).astype(BF16))
                dsink = dsink + jnp.where(lane == h, -jnp.sum(jnp.exp(s_ref[h] - lse_h) * delta), 0.0)
            ds_t, p_t = jnp.concatenate(dss, 1), jnp.concatenate(ps, 1)
            dk_acc.append(jnp.dot(ds_t, qs, preferred_element_type=F32))
            dv_acc.append(jnp.dot(p_t, dos, preferred_element_type=F32))
            dq_t = jnp.dot(kd_t[g], ds_t, preferred_element_type=F32) * (HD ** -0.5)
            for j in range(GH // 2):
                pair = GH // 2 * g + j
                dq = [dq_t[:, h * BLK:(h + 1) * BLK].T for h in (2 * j, 2 * j + 1)]
                dqkv_ref[:, pair * BLK:(pair + 1) * BLK] = _rope_bwd(jnp.where(low, dq[0], dq[1]), *tabs).astype(BF16)
        dsink_ref[0:1, :] += dsink

        def fold(acc):
            tot = [a + pltpu.roll(a, HD, 1) for a in acc]
            return jnp.where(low, tot[0], tot[1])

        dk_all, dv_all = fold(dk_acc), fold(dv_acc)
        met_k[...] += dk_all[0:BLK, :]
        met_v[...] += dv_all[0:BLK, :]
        last = jnp.where(n == 0, 1.0, 0.0)
        dk_n = dk_all[2 * BLK:3 * BLK, :] + car_k[...] + last * met_k[...]
        dv_n = dv_all[2 * BLK:3 * BLK, :] + car_v[...] + last * met_v[...]
        dqkv_ref[:, D:D + BLK] = _rope_bwd(dk_n, *tabs).astype(BF16)
        dqkv_ref[:, D + BLK:QKV_W] = dv_n.astype(BF16)
        car_k[...] = dk_all[BLK:2 * BLK, :]
        car_v[...] = dv_all[BLK:2 * BLK, :]

    rblk = lambda w: pl.BlockSpec((BLK, w), lambda t: (nb - 1 - t, 0))
    return _call(
        body, hook, _steps_1d(nb, nb), name="attn_bwd", grid=(nb,),
        in_specs=[pl.BlockSpec(memory_space=pltpu.SMEM), rblk(D)] + _kv_specs(nb, True)
                 + [rblk(D), pl.BlockSpec((NH, BLK), lambda t: (nb - 1 - t, 0)), rblk(BLK), rblk(BLK), rblk(BLK)],
        out_specs=[rblk(QKV_W), _const((8, BLK))],
        out_shape=[_sds((R, QKV_W), BF16), _sds((8, BLK), F32)],
        scratch_shapes=[pltpu.VMEM((BLK, BLK), F32)] * 4,
        operands=(sinks, qkv, *([qkv] * 6), do, lse, *rope), semantics=("arbitrary",))


def _in_bwd(dproj, w_in, h0, dh1, g_pre, tr, hook=None):
    R = h0.shape[0]
    n = len(dproj)
    widths = [p.shape[1] for p in dproj]
    starts = [sum(widths[:j]) for j in range(n)]

    def body(*refs):
        d_refs, (w_ref, h0_ref, dh1_ref, g_ref, dh0_ref, dg_ref, db_ref) = refs[:n], refs[n:]

        @pl.when(pl.program_id(0) == 0)
        def _():
            dg_ref[...] = jnp.zeros_like(dg_ref)
            db_ref[...] = jnp.zeros_like(db_ref)

        dn1 = jnp.zeros((tr, D), F32)
        for d_ref, c0, wd in zip(d_refs, starts, widths):
            d = d_ref[...]
            dn1 = dn1 + jnp.dot(d, w_ref[c0:c0 + wd, :], preferred_element_type=F32)
            db_ref[:, c0:c0 + wd] += _colsum(d.astype(F32))
        h0 = h0_ref[...]
        _, r = _rms(h0, g_ref[...])
        dh0_ref[...] = dh1_ref[...] + _rms_bwd(dn1, h0, r, g_ref[...])
        dg_ref[...] += _colsum(dn1 * h0 * r)

    return _call(
        body, hook, _steps_1d(R // tr, R // tr), name="in_bwd", grid=(R // tr,),
        in_specs=[_row(tr, wd) for wd in widths] + [_const((IN_W, D)), _row(tr, D), _row(tr, D), _const((1, D))],
        out_specs=[_row(tr, D), _const((1, D)), _const((1, IN_W))],
        out_shape=[_sds((R, D), F32), _sds((1, D), F32), _sds((1, IN_W), F32)],
        operands=(*dproj, w_in, h0, dh1, g_pre), semantics=("arbitrary",))


def _dw(a, b, name, tn, tr, by_chip=False, ta=None, rows_of=None, row0=0, into=None):
    R, ka = a.shape
    n = b.shape[1]
    nt = R // tr
    ta = ta or ka
    k0 = 0
    if by_chip:
        out_spec = pl.BlockSpec((None, ta, tn), lambda k, j, i: (j, k, 0))
        out_shape = _sds((n // tn, ka, tn), BF16)
    else:
        if rows_of is not None:
            k0 = row0 // ta
        out_spec = pl.BlockSpec((ta, tn), lambda k, j, i: (k0 + k, j))
        out_shape = _sds((ka if rows_of is None else rows_of, n), BF16)
    extra = [] if into is None else [into]

    def body(a_ref, b_ref, *rest):
        o_ref, acc = rest[len(extra):]
        i = pl.program_id(2)

        @pl.when(i == 0)
        def _():
            acc[...] = jnp.zeros_like(acc)

        acc[...] += lax.dot_general(a_ref[...], b_ref[...], TN_DIMS, preferred_element_type=F32)

        @pl.when(i == nt - 1)
        def _():
            o_ref[...] = acc[...].astype(BF16)

    return pl.pallas_call(
        body, name=name, grid=(ka // ta, n // tn, nt),
        in_specs=[pl.BlockSpec((tr, ta), lambda k, j, i: (i, k)), pl.BlockSpec((tr, tn), lambda k, j, i: (i, j))]
                 + [pl.BlockSpec(memory_space=pl.ANY)] * len(extra),
        out_specs=out_spec, out_shape=out_shape,
        input_output_aliases={2: 0} if extra else {},
        scratch_shapes=[pltpu.VMEM((ta, tn), F32)],
        compiler_params=_params(("arbitrary", "arbitrary", "arbitrary")),
    )(a, b, *extra)


SMALL = ["norm_pre_mix", "norm_post_mix", "b_in", "attn_sinks", "conv_dw_b", "conv_ln_g", "conv_ln_b",
         "b_conv_proj", "norm_pre_ffn", "norm_post_ffn", "ffn_dw_b"]


def local_step(x, tgt, W, dist=None):
    W = dict(W)
    S = x.shape[0]
    R = S + BLK
    tr = _tile(R, 384, BLK)
    trw = _tile(R, 1056)
    rope = _rope_tables(R)
    meta = _cols_joined(W["tiny"][:, TINY_META:TINY_META + NMETA, 0:DQ])
    h0 = jnp.concatenate([jnp.zeros((PAD, D), F32), meta, x], 0)

    qkv, glu, gates, n1, *got = _in_proj(h0, W["norm_pre_mix"], W["w_in"], W["b_in"], rope, tr,
                                         dist and dist.gather_hook(GATHER_IN_PROJ))
    if dist:
        W.update(dist.weights(GATHER_IN_PROJ, got))
    sinks = W["attn_sinks"].reshape(NH)
    attn, lse, *got = _attn_fwd(qkv, sinks, dist and dist.gather_hook(GATHER_ATTN))
    if dist:
        W.update(dist.weights(GATHER_ATTN, got))
    c1, *got = _conv_fwd(glu, W["tiny"], W["conv_dw_b"], tr, dist and dist.gather_hook(GATHER_CONV))
    if dist:
        W.update(dist.weights(GATHER_CONV, got))
    ao, co, c3, merged, mix, h1, n2 = _mix_out(
        attn, c1, gates, h0, W["w_attn_proj"], W["w_conv_proj"], W["w_out"], W["conv_ln_g"], W["conv_ln_b"],
        W["b_conv_proj"], W["norm_post_mix"], W["norm_pre_ffn"], tr)
    up, act = _ffn_up(n2, W["w_up"], W["tiny"], W["ffn_dw_b"], tr)
    dh2, dffn, loss_cols, dg_post_ffn = _ffn_down(act, W["w_down"], h1, tgt, W["norm_post_ffn"], tr)

    dw_down = _dw(act, dffn, "dw_down", 512, trw)
    dup, dfw, dfb = _ffn_bwd_act(dffn, W["w_down"], up, W["tiny"], W["ffn_dw_b"], tr)
    dw_up = _dw(n2, dup, "dw_up", UPQ, trw, by_chip=True)
    ffn_sums = dist and dist.pair_sums(FFN_SHARES, [dw_up, dw_down.reshape(N_CHIPS, -1, D)], "ffn")
    dh1, dmix, dg_pre_ffn, dg_post_mix = _ffn_bwd_in(dup, W["w_up"], h1, dh2, mix, W["norm_pre_ffn"],
                                                      W["norm_post_mix"], tr)
    dao, dco, dgates, dattn, dc1, db_cp, dlg, dlb, dcb = _mix_bwd(
        dmix, ao, co, gates, c1, W["w_out"], W["w_attn_proj"], W["w_conv_proj"], W["conv_ln_g"], W["conv_ln_b"], tr)
    dglu_a, dglu_g, dcw, *ffn_got = _conv_bwd(dc1, glu, W["tiny"], tr, dist and dist.chip_hook(FFN_SHARES, ffn_sums))
    branch = [_dw(attn, dao, "dw_attn_proj", D, trw), _dw(c3, dco, "dw_conv_proj", D, trw),
              _dw(merged, dmix, "dw_out", D, trw),
              _tiny_pack({"conv_dw_w": dcw, "ffn_dw_w": dfw, "meta_tokens": jnp.zeros((N_CHIPS, NMETA, DQ), F32)})]
    branch_sums = dist and dist.pair_sums(BRANCH_TINY_SHARES, [a.reshape(N_CHIPS, -1, a.shape[-1]) for a in branch], "branch")
    dqkv, dsink, *got = _attn_bwd(qkv, dattn, lse, sinks, rope,
                                  dist and dist.chip_hook(BRANCH_TINY_SHARES, branch_sums))
    if dist:
        dist.finish(FFN_SHARES + BRANCH_TINY_SHARES, ffn_sums + branch_sums, ffn_got + got, "ffn_branch")
    dproj = [dqkv, dglu_a, dglu_g, dgates]
    dw_in, row0 = None, 0
    for j, p in enumerate(dproj):
        ta = _tile(p.shape[1], D, BLK) if j == 0 else 2 * BLK
        dw_in = _dw(p, n1, "dw_in_%d" % j, D, trw, ta=ta, rows_of=IN_W, row0=row0, into=dw_in)
        row0 += p.shape[1]
    in_sums = dist and dist.pair_sums(IN_SHARES, [dw_in.reshape(N_CHIPS, -1, D)], "in")
    dh0, dg_pre_mix, db_in, *got = _in_bwd(dproj, W["w_in"], h0, dh1, W["norm_pre_mix"], tr,
                                           dist and dist.chip_hook(IN_SHARES, in_sums))
    if dist:
        dist.finish(IN_SHARES, in_sums, got, "in")

    grads = {
        "w_in": dw_in, "w_attn_proj": branch[0], "w_conv_proj": branch[1], "w_out": branch[2],
        "w_up": dw_up,
        "w_down": dw_down,
        "tiny": branch[3],
        "meta_tokens": dh0[PAD:BLK],
        "norm_pre_mix": dg_pre_mix, "norm_post_mix": dg_post_mix, "b_in": db_in,
        "attn_sinks": dsink[0:1, 0:NH], "conv_dw_b": dcb, "conv_ln_g": dlg, "conv_ln_b": dlb,
        "b_conv_proj": db_cp, "norm_pre_ffn": dg_pre_ffn, "norm_post_ffn": dg_post_ffn, "ffn_dw_b": dfb,
    }
    return loss_cols, dh0[BLK:], grads


INQ = IN_W // N_CHIPS
DQ = D // N_CHIPS
SHARES = [("w_in", INQ, D, BF16), ("w_attn_proj", DQ, D, BF16), ("w_conv_proj", DQ, D, BF16), ("w_out", DQ, D, BF16),
          ("w_up", D, UPQ, BF16), ("w_down", FFN // N_CHIPS, D, BF16), ("tiny", TINY_ROWS, UPQ, F32)]
TINY_PARTS = [("conv_dw_w", TINY_CONV, CONV_K, TINY_FFN - TINY_CONV, DQ), ("ffn_dw_w", TINY_FFN, FFN_K, TINY_META - TINY_FFN, UPQ),
              ("meta_tokens", TINY_META, NMETA, NMETA, DQ)]


def _tiny_pack(parts):
    rows = []
    for name, _, _, reserved, _ in TINY_PARTS:
        a = parts[name].astype(F32)
        pad = [(0, 0)] * (a.ndim - 2) + [(0, reserved - a.shape[-2]), (0, UPQ - a.shape[-1])]
        rows.append(jnp.pad(a, pad))
    used = sum(r.shape[-2] for r in rows)
    rows.append(jnp.zeros(rows[0].shape[:-2] + (TINY_ROWS - used, UPQ), F32))
    return jnp.concatenate(rows, axis=-2)


def _tiny_unpack(tiny):
    return {name: tiny[..., r0:r0 + k, 0:cols] for name, r0, k, _, cols in TINY_PARTS}


def _cols_by_chip(a):
    rows, n = a.shape
    return a.reshape(rows, N_CHIPS, n // N_CHIPS).transpose(1, 0, 2)


def _cols_joined(a):
    _, rows, cols = a.shape
    return a.transpose(1, 0, 2).reshape(rows, N_CHIPS * cols)


def _to_planes(a, rows):
    return jnp.pad(a, [(0, rows * D - a.shape[-1])]).reshape(rows, D)


ANY = pl.BlockSpec(memory_space=pl.ANY)


def _place():
    x, y, c = lax.axis_index("x"), lax.axis_index("y"), lax.axis_index("c")
    chips = [(1 - x, y), (x, 1 - y), (1 - x, 1 - y)]
    return x, y, c, chips


def _rcopy(src, dst, ssem, rsem, to):
    return pltpu.make_async_remote_copy(src_ref=src, dst_ref=dst, send_sem=ssem, recv_sem=rsem,
                                        device_id=to, device_id_type=MESH)


def _halves(ref_or_rows, c):
    half = ref_or_rows // 2
    return pl.ds(c * half, half), pl.ds((1 - c) * half, half)


FIRST_SHARES, BRANCH_SHARES, FFN_SHARES = [0, 6], [1, 2, 3], [4, 5]
IN_SHARES, BRANCH_TINY_SHARES = [0], [1, 2, 3, 6]
GATHER_IN_PROJ, GATHER_ATTN, GATHER_CONV = [1, 2, 3], [4], [5]


def _gather_hook(own, idx):
    n = len(idx)

    def copies(kind, ins, outs, ssem, rsem):
        x, y, c, chips = _place()
        q = 2 * x + y
        sib = (x, y, 1 - c)
        out = []
        for i, a in enumerate(idx):
            mine, other = _halves(SHARES[a][1], c)
            for j, (cx, cy) in enumerate(chips):
                k, to = 3 * i + j, (cx, cy, c)
                landed, theirs = outs[i].at[2 * cx + cy, mine], outs[i].at[2 * cx + cy, other]
                if kind == "send":
                    out.append(_rcopy(ins[i].at[mine], outs[i].at[q, mine], ssem.at[k], rsem.at[k], to))
                elif kind == "landing":
                    out.append(_rcopy(ins[i].at[mine], landed, ssem.at[k], rsem.at[k], to))
                elif kind == "pass":
                    out.append(_rcopy(landed, landed, ssem.at[3 * n + k], rsem.at[3 * n + k], sib))
                else:
                    out.append(_rcopy(theirs, theirs, ssem.at[3 * n + k], rsem.at[3 * n + k], sib))
        return out

    def own_copies(ins, outs, ssem, rsem):
        x, y, c, _ = _place()
        q = 2 * x + y
        return [_rcopy(ins[i], outs[i].at[q], ssem.at[6 * n + i], rsem.at[6 * n + i], (x, y, 1 - c)) for i in range(n)]

    def start(*refs):
        for cp in copies("send", *refs) + own_copies(*refs):
            cp.start()

    def mid(*refs):
        for landed, cp in zip(copies("landing", *refs), copies("pass", *refs)):
            landed.wait_recv()
            cp.start()

    def finish(*refs):
        for cp in copies("arrival", *refs):
            cp.wait_recv()
        for cp in copies("send", *refs) + copies("pass", *refs):
            cp.wait_send()
        for cp in own_copies(*refs):
            cp.wait()

    shapes = [_sds((N_CHIPS,) + SHARES[a][1:3], SHARES[a][3]) for a in idx]
    return _Hook(own, shapes, 7 * n, start, finish, mid)


def _chip_hook(sums, idx):
    def copies(ins, outs, ssem, rsem):
        x, y, c, chips = _place()
        return [_rcopy(ins[i].at[2 * cx + cy], outs[i].at[j], ssem.at[3 * i + j], rsem.at[3 * i + j], (cx, cy, c))
                for i in range(len(idx)) for j, (cx, cy) in enumerate(chips)]

    def start(*refs):
        for cp in copies(*refs):
            cp.start()

    def finish(*refs):
        for cp in copies(*refs):
            cp.wait()

    shapes = [_sds((N_CHIPS - 1, SHARES[a][1] // 2, SHARES[a][2]), SHARES[a][3]) for a in idx]
    return _Hook(sums, shapes, 3 * len(idx), start, finish)


def _sibling_swap(parts, idx, tag):
    def copies(ins, outs, ssem, rsem):
        x, y, c, _ = _place()
        return [_rcopy(ins[i].at[:, _halves(SHARES[a][1], c)[1]], outs[i], ssem.at[i], rsem.at[i], (x, y, 1 - c))
                for i, a in enumerate(idx)]

    def start(*refs):
        for cp in copies(*refs):
            cp.start()

    def finish(*refs):
        for cp in copies(*refs):
            cp.wait()

    shapes = [_sds((N_CHIPS, SHARES[a][1] // 2, SHARES[a][2]), SHARES[a][3]) for a in idx]
    return _alone(_Hook(parts, shapes, len(idx), start, finish), "sibling_swap_" + tag)


def _sum_pair(parts, recvs, c, idx, tag):
    steps, n = 2, len(idx)

    def body(c_ref, *refs):
        for i, a in enumerate(idx):
            refs[2 * n + i][...] = (refs[i][...].astype(F32) + refs[n + i][...].astype(F32)).astype(SHARES[a][3])

    own, got, out, views, shapes = [], [], [], [], []
    for p, a in zip(parts, idx):
        _, rows, cols, dt = SHARES[a]
        blk = rows // 2 // steps
        own.append(pl.BlockSpec((None, None, blk, cols), lambda q, i, c_ref: (q, c_ref[0], i, 0)))
        got.append(pl.BlockSpec((None, blk, cols), lambda q, i, c_ref: (q, i, 0)))
        out.append(pl.BlockSpec((None, blk, cols), lambda q, i, c_ref: (q, i, 0)))
        views.append(p.reshape(N_CHIPS, 2, rows // 2, cols))
        shapes.append(_sds((N_CHIPS, rows // 2, cols), dt))
    grid_spec = pltpu.PrefetchScalarGridSpec(num_scalar_prefetch=1, grid=(N_CHIPS, steps),
                                             in_specs=own + got, out_specs=out)
    return pl.pallas_call(body, name="sum_pair_" + tag, grid_spec=grid_spec, out_shape=shapes,
                          compiler_params=_params(("arbitrary", "arbitrary")))(c, *views, *recvs)


def _sum_chips(sums, recvs, qc, idx, tag):
    steps, n = 2, len(idx)

    def body(qc_ref, *refs):
        for i in range(n):
            acc = refs[i][...].astype(F32)
            for j in range(1, N_CHIPS):
                acc = acc + refs[j * n + i][...].astype(F32)
            refs[N_CHIPS * n + i][...] = acc

    own, got, out, shapes = [], [[], [], []], [], []
    for a in idx:
        _, rows, cols, _ = SHARES[a]
        blk = rows // 2 // steps
        own.append(pl.BlockSpec((None, blk, cols), lambda i, qc_ref: (qc_ref[0], i, 0)))
        for j in range(N_CHIPS - 1):
            got[j].append(pl.BlockSpec((None, blk, cols), lambda i, qc_ref, j=j: (j, i, 0)))
        out.append(pl.BlockSpec((None, blk, cols), lambda i, qc_ref: (qc_ref[1], i, 0)))
        shapes.append(_sds((2, rows // 2, cols), F32))
    grid_spec = pltpu.PrefetchScalarGridSpec(num_scalar_prefetch=1, grid=(steps,),
                                             in_specs=own + got[0] + got[1] + got[2], out_specs=out)
    return pl.pallas_call(body, name="sum_chips_" + tag, grid_spec=grid_spec, out_shape=shapes,
                          compiler_params=_params(("arbitrary",)))(qc, *sums, *recvs, *recvs, *recvs)


def _sibling_share(halves, idx, tag):
    n = len(idx)

    def body(*refs):
        outs, (ssem, rsem) = refs[n:2 * n], refs[2 * n:]
        x, y, c, _ = _place()
        copies = []
        for i in range(n):
            cp = _rcopy(outs[i].at[c], outs[i].at[c], ssem.at[i], rsem.at[i], (x, y, 1 - c))
            cp.start()
            copies.append(cp)
        for i in range(n):
            theirs = outs[i].at[1 - c]
            _rcopy(theirs, theirs, ssem.at[i], rsem.at[i], (x, y, 1 - c)).wait_recv()
        for cp in copies:
            cp.wait_send()

    return pl.pallas_call(
        body, name="sibling_share_" + tag, in_specs=[ANY] * n, out_specs=[ANY] * n,
        out_shape=[_sds((2, SHARES[a][1] // 2, SHARES[a][2]), F32) for a in idx],
        input_output_aliases={i: i for i in range(n)},
        scratch_shapes=[pltpu.SemaphoreType.DMA((n,)), pltpu.SemaphoreType.DMA((n,))],
    )(*halves)


class _Dist:
    def __init__(self, own):
        self.own = own
        self.core = lax.axis_index("c")
        self.chip = 2 * lax.axis_index("x") + lax.axis_index("y")
        self.reduced = {}

    def gather_hook(self, idx):
        return _gather_hook([self.own[a] for a in idx], idx)

    def weights(self, idx, gathered):
        out = {}
        for a, full in zip(idx, gathered):
            name = SHARES[a][0]
            out[name] = full if name in ("w_up", "tiny") else full.reshape(-1, D)
        return out

    def pair_sums(self, idx, parts, tag):
        return _sum_pair(parts, _sibling_swap(parts, idx, tag), self.core.reshape(1), idx, tag)

    def chip_hook(self, idx, sums):
        return _chip_hook(sums, idx)

    def finish(self, idx, sums, recvs, tag):
        halves = _sum_chips(sums, recvs, jnp.stack([self.chip, self.core]), idx, tag)
        for a, full in zip(idx, _sibling_share(halves, idx, tag)):
            self.reduced[SHARES[a][0]] = full.reshape(SHARES[a][1:3])


N_DEV = 8
SMALL_ROWS = 40


def _small_allreduce(sm):
    def body(s_ref, o_ref, buf, ssem, rsem):
        x, y, c, _ = _place()
        me = 4 * x + 2 * y + c
        buf[me] = s_ref[...]
        copies = []
        for d in range(1, N_DEV):
            dx, dy, dc = d >> 2, (d >> 1) & 1, d & 1
            to = (x ^ dx, y ^ dy, c ^ dc)
            cp = _rcopy(s_ref, buf.at[me], ssem.at[d - 1], rsem.at[d - 1], to)
            cp.start()
            copies.append(cp)
        for d in range(1, N_DEV):
            src = me ^ d
            _rcopy(s_ref, buf.at[src], ssem.at[d - 1], rsem.at[d - 1], (x, y, c)).wait_recv()
        for cp in copies:
            cp.wait_send()
        acc = buf[0]
        for k in range(1, N_DEV):
            acc = acc + buf[k]
        o_ref[...] = acc

    vm = pl.BlockSpec(memory_space=pltpu.VMEM)
    return pl.pallas_call(
        body, name="small_allreduce", in_specs=[vm], out_specs=vm,
        out_shape=_sds((SMALL_ROWS, D), F32),
        scratch_shapes=[pltpu.VMEM((N_DEV, SMALL_ROWS, D), F32),
                        pltpu.SemaphoreType.DMA((N_DEV - 1,)), pltpu.SemaphoreType.DMA((N_DEV - 1,))],
    )(sm)


SMALL_PLAN = [("norm_pre_mix", D), ("norm_post_mix", D), ("b_in", IN_W), ("attn_sinks", NH), ("conv_dw_b", D),
              ("conv_ln_g", D), ("conv_ln_b", D), ("b_conv_proj", D), ("norm_pre_ffn", D), ("norm_post_ffn", D),
              ("ffn_dw_b", 2 * FFN), ("loss", D), ("meta_tokens", NMETA * D)]


def _pack_small(parts):
    rows = [_to_planes(parts[name].reshape(-1), -(-n // D)) for name, n in SMALL_PLAN]
    used = sum(r.shape[0] for r in rows)
    return jnp.concatenate(rows + [jnp.zeros((SMALL_ROWS - used, D), F32)], 0)


def _unpack_small(packed):
    out, r0 = {}, 0
    for name, n in SMALL_PLAN:
        rows = -(-n // D)
        out[name] = packed[r0:r0 + rows].reshape(-1)[:n].reshape(1, n)
        r0 += rows
    return out


def _adamw_update(w_ref, g_ref, m_ref, v_ref, d_ref, nm_ref, nv_ref):
    g = g_ref[...]
    m = B1 * m_ref[...] + (1.0 - B1) * g
    v = B2 * v_ref[...] + (1.0 - B2) * (g * g)
    nm_ref[...] = m
    nv_ref[...] = v
    m_hat = m / (1.0 - B1 ** STEP)
    v_hat = v / (1.0 - B2 ** STEP)
    d_ref[...] = -LR * (m_hat / (jnp.sqrt(v_hat) + ADAM_EPS) + WD * w_ref[...])


def _adamw_vectors(ws, gs, ms, vs):
    n = len(ws)

    def body(*refs):
        for j in range(n):
            _adamw_update(*[refs[k * n + j] for k in range(7)])

    vm = pl.BlockSpec(memory_space=pltpu.VMEM)
    outs = pl.pallas_call(body, name="adamw_vectors", in_specs=[vm] * (4 * n), out_specs=[vm] * (3 * n),
                          out_shape=[_sds(w.shape, F32) for w in ws] * 3)(*ws, *gs, *ms, *vs)
    return outs[:n], outs[n:2 * n], outs[2 * n:]


def _adamw(w, g, m, v, name):
    rows, cols = w.shape
    tr = _tile(rows, 256, 8) if rows % 8 == 0 else rows

    def body(*refs):
        _adamw_update(*refs)

    spec = pl.BlockSpec((tr, cols), lambda i: (i, 0))
    return pl.pallas_call(
        body, name=name, grid=(rows // tr,), in_specs=[spec] * 4, out_specs=[spec] * 3,
        out_shape=[_sds((rows, cols), F32)] * 3, compiler_params=_params(("arbitrary",)),
    )(w, g, m, v)


NAMES = ["meta_tokens", "norm_pre_mix", "norm_post_mix", "w_in", "b_in", "attn_sinks", "w_attn_proj", "conv_dw_w",
         "conv_dw_b", "conv_ln_g", "conv_ln_b", "w_conv_proj", "b_conv_proj", "w_out", "norm_pre_ffn", "norm_post_ffn",
         "w_up", "ffn_dw_w", "ffn_dw_b", "w_down"]
MATMUL = ("w_in", "w_attn_proj", "w_conv_proj", "w_out", "w_up", "w_down")


def _two_d(a):
    return a.reshape(a.shape[-2:])


def kernel(x, meta_tokens, norm_pre_mix, norm_post_mix, w_in, b_in, attn_sinks, w_attn_proj, conv_dw_w, conv_dw_b, conv_ln_g, conv_ln_b, w_conv_proj, b_conv_proj, w_out, norm_pre_ffn, norm_post_ffn, w_up, ffn_dw_w, ffn_dw_b, w_down, loss_target, m_meta_tokens, m_norm_pre_mix, m_norm_post_mix, m_w_in, m_b_in, m_attn_sinks, m_w_attn_proj, m_conv_dw_w, m_conv_dw_b, m_conv_ln_g, m_conv_ln_b, m_w_conv_proj, m_b_conv_proj, m_w_out, m_norm_pre_ffn, m_norm_post_ffn, m_w_up, m_ffn_dw_w, m_ffn_dw_b, m_w_down, v_meta_tokens, v_norm_pre_mix, v_norm_post_mix, v_w_in, v_b_in, v_attn_sinks, v_w_attn_proj, v_conv_dw_w, v_conv_dw_b, v_conv_ln_g, v_conv_ln_b, v_w_conv_proj, v_b_conv_proj, v_w_out, v_norm_pre_ffn, v_norm_post_ffn, v_w_up, v_ffn_dw_w, v_ffn_dw_b, v_w_down):
    args = locals()
    w = {n: args[n] for n in NAMES}
    m = {n: args["m_" + n] for n in NAMES}
    v = {n: args["v_" + n] for n in NAMES}
    tiny_names = [part[0] for part in TINY_PARTS]
    big = list(MATMUL) + tiny_names

    def shard_2d(a, name):
        return _two_d(a).T if name == "w_in" else _two_d(a)

    own = {n: shard_2d(w[n], n).astype(BF16) for n in MATMUL}
    own["tiny"] = _tiny_pack({n: _two_d(w[n]) for n in tiny_names})
    dist = _Dist([own[n] for n, _, _, _ in SHARES])
    W = {n: _two_d(w[n]) for n in SMALL}
    W.update(dist.weights(FIRST_SHARES, _alone(dist.gather_hook(FIRST_SHARES), "gather_first")))

    loss_cols, grad_x, grads = local_step(x[0], loss_target[0], W, dist)

    small = dict(grads)
    small["loss"] = loss_cols
    g_small = _unpack_small(_small_allreduce(_pack_small(small)))
    loss = jnp.sum(g_small["loss"])
    g_big = {n: dist.reduced[n] for n in MATMUL}
    g_big.update(_tiny_unpack(dist.reduced["tiny"]))
    g_big["meta_tokens"] = lax.dynamic_slice(g_small["meta_tokens"].reshape(NMETA, D), (0, dist.chip * DQ), (NMETA, DQ))

    g, delta, new_m, new_v = {}, {}, {}, {}
    for n in big:
        shape = w[n].shape
        back = (lambda a: a.T.reshape(shape)) if n == "w_in" else (lambda a: a.reshape(shape))
        outs = _adamw(shard_2d(w[n], n), g_big[n], shard_2d(m[n], n), shard_2d(v[n], n), "adamw_" + n)
        g[n], delta[n], new_m[n], new_v[n] = (back(a) for a in (g_big[n], *outs))
    ud, um, uv = _adamw_vectors(*[[_two_d(d[n]) for n in SMALL] for d in (w, g_small, m, v)])
    for j, n in enumerate(SMALL):
        g[n], delta[n], new_m[n], new_v[n] = g_small[n], ud[j], um[j], uv[j]

    return (loss, grad_x[None], *[g[n] for n in NAMES], *[delta[n] for n in NAMES],
            *[new_m[n] for n in NAMES], *[new_v[n] for n in NAMES])
```

```python
import jax
import jax.numpy as jnp
from jax import lax
from jax.experimental import pallas as pl
from jax.experimental.pallas import tpu as pltpu

F32, BF16 = jnp.float32, jnp.bfloat16

D = 1024
NH, NKV, HD = 16, 2, 64
GH = NH // NKV
NMETA, BLK = 16, 128
PAD = BLK - NMETA
ROT = HD // 4
THETA = 500000.0
CONV_K = 31
FFN = 2816
FFN_K = 3
IN_W = 5376
QKV_W, GLU_W, GATE_W = 1280, 2048, 2048
RMS_EPS, LN_EPS, NEG = 1e-6, 1e-5, -1e30
LR, B1, B2, ADAM_EPS, WD, STEP = 0.001, 0.9, 0.999, 1e-08, 0.01, 10

VMEM_LIMIT = 56 * 2 ** 20
MESH = pl.DeviceIdType.MESH

NT_DIMS = (((1,), (1,)), ((), ()))
TN_DIMS = (((0,), (0,)), ((), ()))


def _params(sem, **kw):
    return pltpu.CompilerParams(dimension_semantics=sem, vmem_limit_bytes=VMEM_LIMIT, **kw)


def _tile(n, pref, mult=16):
    for t in range(min(pref, n), 0, -1):
        if n % t == 0 and t % mult == 0:
            return t
    return n


def _row(tr, w, col=0):
    return pl.BlockSpec((tr, w), lambda i: (i, col))


def _rrow(tr, w, nt, col=0):
    return pl.BlockSpec((tr, w), lambda t: (nt - 1 - t, col))


def _const(shape):
    return pl.BlockSpec(shape, lambda *_: (0,) * len(shape))


def _sds(shape, dt):
    return jax.ShapeDtypeStruct(shape, dt)


class _Hook:
    def __init__(self, operands, out_shape, n_sem, start, finish, mid=None):
        self.operands, self.out_shape, self.n_sem = list(operands), list(out_shape), n_sem
        self.start, self.mid, self.finish = start, mid, finish

    def scratch(self):
        return [pltpu.SemaphoreType.DMA((self.n_sem,)), pltpu.SemaphoreType.DMA((self.n_sem,))]


def _call(body, hook, steps, *, name, grid, in_specs, out_specs, out_shape, operands, semantics, scratch_shapes=()):
    in_specs, out_specs, out_shape = list(in_specs), list(out_specs), list(out_shape)
    if hook is None:
        return pl.pallas_call(body, name=name, grid=grid, in_specs=in_specs, out_specs=out_specs, out_shape=out_shape,
                              scratch_shapes=list(scratch_shapes), compiler_params=_params(semantics))(*operands)
    n_in, n_out, n_hi, n_ho = len(in_specs), len(out_specs), len(hook.operands), len(hook.out_shape)

    def wrapped(*refs):
        ins, hi = refs[:n_in], refs[n_in:n_in + n_hi]
        o0 = n_in + n_hi
        outs, ho = refs[o0:o0 + n_out], refs[o0 + n_out:o0 + n_out + n_ho]
        scratch, (ssem, rsem) = refs[o0 + n_out + n_ho:len(refs) - 2], refs[len(refs) - 2:]
        first, middle, last = steps()

        @pl.when(first)
        def _():
            hook.start(hi, ho, ssem, rsem)

        body(*ins, *outs, *scratch)
        if hook.mid is not None:
            @pl.when(middle)
            def _():
                hook.mid(hi, ho, ssem, rsem)

        @pl.when(last)
        def _():
            hook.finish(hi, ho, ssem, rsem)

    any_spec = pl.BlockSpec(memory_space=pl.ANY)
    return pl.pallas_call(
        wrapped, name=name, grid=grid, in_specs=in_specs + [any_spec] * n_hi, out_specs=out_specs + [any_spec] * n_ho,
        out_shape=out_shape + hook.out_shape, scratch_shapes=list(scratch_shapes) + hook.scratch(),
        compiler_params=_params(semantics))(*operands, *hook.operands)


def _alone(hook, name):
    n_hi = len(hook.operands)

    def body(*refs):
        hi, ho, (ssem, rsem) = refs[:n_hi], refs[n_hi:len(refs) - 2], refs[len(refs) - 2:]
        hook.start(hi, ho, ssem, rsem)
        if hook.mid is not None:
            hook.mid(hi, ho, ssem, rsem)
        hook.finish(hi, ho, ssem, rsem)

    any_spec = pl.BlockSpec(memory_space=pl.ANY)
    return pl.pallas_call(body, name=name, in_specs=[any_spec] * n_hi, out_specs=[any_spec] * len(hook.out_shape),
                          out_shape=hook.out_shape, scratch_shapes=hook.scratch())(*hook.operands)


def _steps_1d(n, mid):
    def steps():
        i = pl.program_id(0)
        return i == 0, i == min(mid, n - 1), i == n - 1
    return steps


def _rms(x, g):
    r = lax.rsqrt(jnp.mean(x * x, -1, keepdims=True) + RMS_EPS)
    return x * r * g, r


def _rms_bwd(dy, x, r, g):
    gy = dy * g
    return r * gy - x * (r * r * r) * jnp.mean(x * gy, -1, keepdims=True)


def _colsum(x):
    return jnp.sum(x, axis=0, keepdims=True)


def _rope(x, c, sa, sb):
    n = x.shape[1]
    return x * c + pltpu.roll(x, n - 8, 1) * sa + pltpu.roll(x, 8, 1) * sb


def _rope_bwd(d, c, sa, sb):
    n = d.shape[1]
    return d * c + pltpu.roll(d * sa, 8, 1) + pltpu.roll(d * sb, n - 8, 1)


def _rope_tables(R):
    half = ROT // 2
    lane = jnp.arange(2 * HD) % HD
    inv = THETA ** (-(lane % half).astype(F32) * 2.0 / ROT)
    pos = (jnp.arange(R) - PAD).astype(F32)
    ang = pos[:, None] * inv[None, :]
    cos, sin = jnp.cos(ang), jnp.sin(ang)
    c = jnp.where(lane < ROT, cos, 1.0)
    sa = jnp.where(lane < half, -sin, 0.0)
    sb = jnp.where((lane >= half) & (lane < ROT), sin, 0.0)
    return c, sa, sb


IN_CHUNKS = ([(0, 512, True), (512, 1024, True), (1024, 1152, True), (1152, 1280, False)]
             + [(c, c + 512, False) for c in range(1280, IN_W, 512)])


def _in_proj(h0, g_pre, w_in, b_in, rope, tr, hook=None):
    R = h0.shape[0]
    nt = R // tr

    def body(h_ref, g_ref, w_ref, b_ref, c_ref, sa_ref, sb_ref, qkv_ref, glu_ref, gate_ref, n1_ref):
        n, _ = _rms(h_ref[...], g_ref[...])
        nb = n.astype(BF16)
        n1_ref[...] = nb
        for c0, c1, rot in IN_CHUNKS:
            acc = lax.dot_general(nb, w_ref[c0:c1, :], NT_DIMS, preferred_element_type=F32) + b_ref[:, c0:c1]
            if rot:
                reps = (c1 - c0) // 128
                acc = _rope(acc, jnp.tile(c_ref[...], (1, reps)), jnp.tile(sa_ref[...], (1, reps)),
                            jnp.tile(sb_ref[...], (1, reps)))
            val = acc.astype(BF16)
            if c1 <= QKV_W:
                qkv_ref[:, c0:c1] = val
            elif c1 <= QKV_W + GLU_W:
                glu_ref[:, c0 - QKV_W:c1 - QKV_W] = val
            else:
                gate_ref[:, c0 - QKV_W - GLU_W:c1 - QKV_W - GLU_W] = val

    return _call(
        body, hook, _steps_1d(nt, (3 * nt) // 4), name="in_proj", grid=(nt,),
        in_specs=[_row(tr, D), _const((1, D)), _const((IN_W, D)), _const((1, IN_W)),
                  _row(tr, 128), _row(tr, 128), _row(tr, 128)],
        out_specs=[_row(tr, QKV_W), _row(tr, GLU_W), _row(tr, GATE_W), _row(tr, D)],
        out_shape=[_sds((R, QKV_W), BF16), _sds((R, GLU_W), BF16), _sds((R, GATE_W), BF16), _sds((R, D), BF16)],
        operands=(h0, g_pre, w_in, b_in, *rope), semantics=("arbitrary",))


def _attn_mask(n, keys_first=False):
    shape = (3 * BLK, BLK) if keys_first else (BLK, 3 * BLK)
    qi = lax.broadcasted_iota(jnp.int32, shape, 1 if keys_first else 0)
    kj = lax.broadcasted_iota(jnp.int32, shape, 0 if keys_first else 1)
    tq = n * BLK + qi - PAD
    t_meta = kj - PAD
    t_loc = (n - 1) * BLK + (kj - BLK) - PAD
    meta_ok = (kj < BLK) & (t_meta >= 0) & (t_meta <= tq)
    loc_ok = (kj >= BLK) & (t_loc >= NMETA) & (t_loc <= tq) & (tq - t_loc < BLK)
    return meta_ok | loc_ok


def _dup_heads(ref0, refp, refc, low, transposed=False):
    a = jnp.concatenate([ref0[...], refp[...], refc[...]], 0).astype(F32)
    sw = pltpu.roll(a, HD, 1)
    heads = [jnp.where(low, a, sw), jnp.where(low, sw, a)]
    return [(h.T if transposed else h).astype(BF16) for h in heads]


def _kv_specs(nb, rev):
    def blk(col, which):
        def idx(t):
            n = nb - 1 - t if rev else t
            return ({"meta": 0, "prev": jnp.maximum(n - 1, 0), "own": n}[which], col)
        return pl.BlockSpec((BLK, BLK), idx)
    return [blk(col, w) for col in (8, 9) for w in ("meta", "prev", "own")]


def _attn_fwd(qkv, sinks, hook=None):
    R = qkv.shape[0]
    nb = R // BLK

    def body(s_ref, q_ref, k0, kp, kc, v0, vp, vc, o_ref, lse_ref):
        n = pl.program_id(0)
        lane = lax.broadcasted_iota(jnp.int32, (1, BLK), 1)
        low = lane < HD
        kd, vd_t = _dup_heads(k0, kp, kc, low), _dup_heads(v0, vp, vc, low, transposed=True)
        mask = _attn_mask(n, keys_first=True)
        zero = jnp.zeros((), BF16)
        lses = []
        for g in range(NKV):
            tiles = []
            for pair in range(GH // 2 * g, GH // 2 * (g + 1)):
                qp = q_ref[:, pair * BLK:(pair + 1) * BLK] * jnp.asarray(HD ** -0.5, BF16)
                tiles += [jnp.where(low, qp, zero), jnp.where(low, zero, qp)]
            st = lax.dot_general(kd[g], jnp.concatenate(tiles, 0), NT_DIMS, preferred_element_type=F32)
            ps, inv = [], []
            for j in range(GH):
                s = jnp.where(mask, st[:, j * BLK:(j + 1) * BLK], NEG)
                sk = s_ref[GH * g + j]
                m = jnp.maximum(jnp.max(s, 0, keepdims=True), sk)
                p = jnp.exp(s - m)
                l = jnp.sum(p, 0, keepdims=True) + jnp.exp(sk - m)
                ps.append(p.astype(BF16))
                inv.append(1.0 / l)
                lses.append(m + jnp.log(l))
            ot = jnp.dot(vd_t[g], jnp.concatenate(ps, 1), preferred_element_type=F32)
            for j in range(GH // 2):
                pair = GH // 2 * g + j
                o = [(ot[:, h * BLK:(h + 1) * BLK] * inv[h]).T for h in (2 * j, 2 * j + 1)]
                o_ref[:, pair * BLK:(pair + 1) * BLK] = jnp.where(low, o[0], o[1]).astype(BF16)
        lse_ref[...] = jnp.concatenate(lses, 0)

    return _call(
        body, hook, _steps_1d(nb, (3 * nb) // 4), name="attn_fwd", grid=(nb,),
        in_specs=[pl.BlockSpec(memory_space=pltpu.SMEM), pl.BlockSpec((BLK, D), lambda n: (n, 0))] + _kv_specs(nb, False),
        out_specs=[_row(BLK, D), _row(NH, BLK)],
        out_shape=[_sds((R, D), BF16), _sds((nb * NH, BLK), F32)],
        operands=(sinks, qkv, *([qkv] * 6)), semantics=("arbitrary",))


CONV_TCH, CONV_SUB, HALO = 256, 64, 32


def _tap_windows(buf, r0, offset_of):
    span = CONV_SUB + HALO
    x = buf[pl.ds(r0, span), :]
    by_phase = {}
    for k in range(CONV_K):
        by_phase.setdefault(offset_of(k) % 8, []).append(k)
    for phase, taps in sorted(by_phase.items()):
        y = x if phase == 0 else pltpu.roll(x, span - phase, 0)
        for k in taps:
            d = offset_of(k) - phase
            yield k, y[d:d + CONV_SUB, :]


def _conv_fwd(glu, w, b, tr, hook=None):
    R = glu.shape[0]
    nc = D // CONV_TCH

    def body(a_ref, g_ref, w_ref, b_ref, o_ref, buf):
        i = pl.program_id(1)

        @pl.when(i == 0)
        def _():
            buf[0:HALO, :] = jnp.zeros((HALO, CONV_TCH), F32)

        @pl.when(i > 0)
        def _():
            buf[0:HALO, :] = buf[tr:tr + HALO, :]

        row = i * tr + lax.broadcasted_iota(jnp.int32, (tr, 1), 0)
        a, g = a_ref[...].astype(F32), g_ref[...].astype(F32)
        buf[HALO:HALO + tr, :] = jnp.where(row >= PAD, a * jax.nn.sigmoid(g), 0.0)
        for r0 in range(0, tr, CONV_SUB):
            acc = jnp.broadcast_to(b_ref[...], (CONV_SUB, CONV_TCH))
            for k, win in _tap_windows(buf, r0, lambda k: HALO - (CONV_K - 1) + k):
                acc = acc + w_ref[k:k + 1, :] * win
            o_ref[r0:r0 + CONV_SUB, :] = acc.astype(BF16)

    nt = R // tr

    def steps():
        c, i = pl.program_id(0), pl.program_id(1)
        return (c == 0) & (i == 0), (c == nc - 1) & (i == 0), (c == nc - 1) & (i == nt - 1)

    return _call(
        body, hook, steps, name="conv_fwd", grid=(nc, nt),
        in_specs=[pl.BlockSpec((tr, CONV_TCH), lambda c, i: (i, c)),
                  pl.BlockSpec((tr, CONV_TCH), lambda c, i: (i, nc + c)),
                  pl.BlockSpec((None, HALO, CONV_TCH), lambda c, i: (c, 0, 0)),
                  pl.BlockSpec((1, CONV_TCH), lambda c, i: (0, c))],
        out_specs=[pl.BlockSpec((tr, CONV_TCH), lambda c, i: (i, c))],
        out_shape=[_sds((R, D), BF16)],
        scratch_shapes=[pltpu.VMEM((tr + HALO, CONV_TCH), F32)],
        operands=(glu, glu, w, b), semantics=("arbitrary", "arbitrary"))


def _ln_silu(c1, lg, lb):
    mu = jnp.mean(c1, -1, keepdims=True)
    xc = c1 - mu
    rs = lax.rsqrt(jnp.mean(xc * xc, -1, keepdims=True) + LN_EPS)
    xh = xc * rs
    c2 = xh * lg + lb
    sg = jax.nn.sigmoid(c2)
    return xh, rs, c2, sg


def _mix_out(attn, c1, gates, h0, w_ap, w_cp, w_out, lg, lb, b_cp, g_post, g_ffn, tr):
    R = attn.shape[0]

    def body(at_ref, c1_ref, ga_ref, gc_ref, h0_ref, wap, wcp, wo, lg_ref, lb_ref, bcp, gp, gf,
             ao_ref, co_ref, c3_ref, mg_ref, mix_ref, h1_ref, n2_ref):
        ao = jnp.dot(at_ref[...], wap[...], preferred_element_type=F32)
        _, _, c2, sg = _ln_silu(c1_ref[...].astype(F32), lg_ref[...], lb_ref[...])
        c3 = (c2 * sg).astype(BF16)
        c3_ref[...] = c3
        co = jnp.dot(c3, wcp[...], preferred_element_type=F32) + bcp[...]
        ao_b, co_b = ao.astype(BF16), co.astype(BF16)
        ao_ref[...] = ao_b
        co_ref[...] = co_b
        merged = (jax.nn.sigmoid(ga_ref[...].astype(F32)) * ao_b.astype(F32)
                  + jax.nn.sigmoid(gc_ref[...].astype(F32)) * co_b.astype(F32)).astype(BF16)
        mg_ref[...] = merged
        mix = jnp.dot(merged, wo[...], preferred_element_type=F32).astype(BF16)
        mix_ref[...] = mix
        y, _ = _rms(mix.astype(F32), gp[...])
        h1 = h0_ref[...] + y
        h1_ref[...] = h1
        n2, _ = _rms(h1, gf[...])
        row = pl.program_id(0) * tr + lax.broadcasted_iota(jnp.int32, (tr, 1), 0)
        n2_ref[...] = jnp.where(row >= PAD, n2, 0.0).astype(BF16)

    vec = _const((1, D))
    return pl.pallas_call(
        body, name="mix_out", grid=(R // tr,),
        in_specs=[_row(tr, D), _row(tr, D), _row(tr, D, 0), _row(tr, D, 1), _row(tr, D),
                  _const((D, D)), _const((D, D)), _const((D, D)), vec, vec, vec, vec, vec],
        out_specs=[_row(tr, D)] * 7,
        out_shape=[_sds((R, D), BF16)] * 5 + [_sds((R, D), F32), _sds((R, D), BF16)],
        compiler_params=_params(("arbitrary",)),
    )(attn, c1, gates, gates, h0, w_ap, w_cp, w_out, lg, lb, b_cp, g_post, g_ffn)


FFN_CH = 256
N_CHIPS = 4
UPQ = 2 * FFN // N_CHIPS
UP_CHUNKS = [(q, c0, min(c0 + 512, UPQ)) for q in range(N_CHIPS // 2) for c0 in range(0, UPQ, 512)]
TINY_ROWS, TINY_CONV, TINY_FFN, TINY_META = 64, 0, 32, 40


def _shift_down(x, k, halo):
    tr = x.shape[0]
    row = lax.broadcasted_iota(jnp.int32, (tr, 1), 0)
    y = pltpu.roll(x, k, 0)
    for j in range(k):
        y = jnp.where(row == j, halo[8 - k + j:8 - k + j + 1, :], y)
    return y


def _shift_up(x, k, halo):
    tr = x.shape[0]
    row = lax.broadcasted_iota(jnp.int32, (tr, 1), 0)
    y = pltpu.roll(x, tr - k, 0)
    for j in range(k):
        y = jnp.where(row == tr - k + j, halo[j:j + 1, :], y)
    return y


def _conv3(x, halo, w, b):
    return w[2:3, :] * x + w[1:2, :] * _shift_down(x, 1, halo) + w[0:1, :] * _shift_down(x, 2, halo) + b


def _ffn_up(n2, w_up, fw, fb, tr):
    R = n2.shape[0]

    def body(n_ref, w_ref, fw_ref, fb_ref, up_ref, act_ref, carry):
        @pl.when(pl.program_id(0) == 0)
        def _():
            carry[...] = jnp.zeros_like(carry)

        nb = n_ref[...]
        for q, c0, c1 in UP_CHUNKS:
            us = []
            for qq in (q, q + N_CHIPS // 2):
                cs = slice(qq * UPQ + c0, qq * UPQ + c1)
                x = jnp.dot(nb, w_ref[qq, :, c0:c1], preferred_element_type=F32).astype(BF16)
                up_ref[:, cs] = x
                x = x.astype(F32)
                us.append(_conv3(x, carry[:, cs], fw_ref[qq, TINY_FFN:TINY_FFN + 8, c0:c1], fb_ref[:, cs]))
                carry[:, cs] = x[tr - 8:tr, :]
            act_ref[:, q * UPQ + c0:q * UPQ + c1] = (us[0] * jax.nn.sigmoid(us[0]) * us[1]).astype(BF16)

    return pl.pallas_call(
        body, name="ffn_up", grid=(R // tr,),
        in_specs=[_row(tr, D), _const((N_CHIPS, D, UPQ)), _const((N_CHIPS, TINY_ROWS, UPQ)), _const((1, 2 * FFN))],
        out_specs=[_row(tr, 2 * FFN), _row(tr, FFN)],
        out_shape=[_sds((R, 2 * FFN), BF16), _sds((R, FFN), BF16)],
        scratch_shapes=[pltpu.VMEM((8, 2 * FFN), F32)],
        compiler_params=_params(("arbitrary",)),
    )(n2, w_up, fw, fb)


def _ffn_down(act, w_down, h1, tgt, g_post, tr):
    R = act.shape[0]
    m = tr // BLK

    def body(a_ref, w_ref, h1_ref, g_ref, *rest):
        t_refs, (dh2_ref, dffn_ref, loss_ref, dg_ref) = rest[:m], rest[m:]

        @pl.when(pl.program_id(0) == 0)
        def _():
            loss_ref[...] = jnp.zeros_like(loss_ref)
            dg_ref[...] = jnp.zeros_like(dg_ref)

        f = jnp.dot(a_ref[...], w_ref[...], preferred_element_type=F32)
        g = g_ref[...]
        y, r = _rms(f, g)
        row = pl.program_id(0) * tr + lax.broadcasted_iota(jnp.int32, (tr, 1), 0)
        tgt_rows = jnp.concatenate([t[...] for t in t_refs], 0)
        e = jnp.where(row >= BLK, h1_ref[...] + y - tgt_rows, 0.0)
        loss_ref[...] += _colsum(e * e) * (0.5 / D)
        dy = e * (1.0 / D)
        dh2_ref[...] = dy
        dffn_ref[...] = _rms_bwd(dy, f, r, g).astype(BF16)
        dg_ref[...] += _colsum(dy * f * r)

    return pl.pallas_call(
        body, name="ffn_down", grid=(R // tr,),
        in_specs=[_row(tr, FFN), _const((FFN, D)), _row(tr, D), _const((1, D))]
                 + [pl.BlockSpec((BLK, D), lambda i, k=k: (jnp.maximum(m * i - 1 + k, 0), 0)) for k in range(m)],
        out_specs=[_row(tr, D), _row(tr, D), _const((1, D)), _const((1, D))],
        out_shape=[_sds((R, D), F32), _sds((R, D), BF16), _sds((1, D), F32), _sds((1, D), F32)],
        compiler_params=_params(("arbitrary",)),
    )(act, w_down, h1, g_post, *([tgt] * m))


def _ffn_bwd_act(dffn, w_down, up, fw, fb, tr):
    R = dffn.shape[0]
    nt = R // tr

    def body(d_ref, w_ref, up_ref, hal_ref, fw_ref, fb_ref, dup_ref, dfw_ref, dfb_ref, carry):
        t = pl.program_id(0)
        i = nt - 1 - t

        @pl.when(t == 0)
        def _():
            carry[...] = jnp.zeros_like(carry)
            dfw_ref[...] = jnp.zeros_like(dfw_ref)
            dfb_ref[...] = jnp.zeros_like(dfb_ref)

        dff = d_ref[...]
        row = i * tr + lax.broadcasted_iota(jnp.int32, (tr, 1), 0)
        first = i == 0
        for q, c0, c1 in UP_CHUNKS:
            dact = lax.dot_general(dff, w_ref[q * UPQ + c0:q * UPQ + c1, :], NT_DIMS, preferred_element_type=F32)
            chips = (q, q + N_CHIPS // 2)
            xs, us = [], []
            for qq in chips:
                cs = slice(qq * UPQ + c0, qq * UPQ + c1)
                x = up_ref[:, cs].astype(F32)
                halo = jnp.where(first, 0.0, hal_ref[:, cs].astype(F32))
                x1, x2 = _shift_down(x, 1, halo), _shift_down(x, 2, halo)
                w = fw_ref[qq, TINY_FFN:TINY_FFN + 8, c0:c1]
                us.append(w[2:3, :] * x + w[1:2, :] * x1 + w[0:1, :] * x2 + fb_ref[:, cs])
                xs.append((x, x1, x2))
            sg = jax.nn.sigmoid(us[0])
            silu = us[0] * sg
            dus = [dact * us[1] * sg * (1.0 + us[0] * (1.0 - sg)), dact * silu]
            for (x, x1, x2), du, qq in zip(xs, dus, chips):
                cs = slice(qq * UPQ + c0, qq * UPQ + c1)
                w = fw_ref[qq, TINY_FFN:TINY_FFN + 8, c0:c1]
                nxt = carry[:, cs]
                dx = w[2:3, :] * du + w[1:2, :] * _shift_up(du, 1, nxt) + w[0:1, :] * _shift_up(du, 2, nxt)
                dup_ref[:, cs] = jnp.where(row >= PAD, dx, 0.0).astype(BF16)
                dfw_ref[qq, 0:1, c0:c1] += _colsum(x2 * du)
                dfw_ref[qq, 1:2, c0:c1] += _colsum(x1 * du)
                dfw_ref[qq, 2:3, c0:c1] += _colsum(x * du)
                dfb_ref[:, cs] += _colsum(du)
                carry[:, cs] = du[0:8, :]

    halo_spec = pl.BlockSpec((8, 2 * FFN), lambda t: (jnp.maximum((nt - 1 - t) * (tr // 8) - 1, 0), 0))
    return pl.pallas_call(
        body, name="ffn_bwd_act", grid=(nt,),
        in_specs=[_rrow(tr, D, nt), _const((FFN, D)), _rrow(tr, 2 * FFN, nt), halo_spec,
                  _const((N_CHIPS, TINY_ROWS, UPQ)), _const((1, 2 * FFN))],
        out_specs=[_rrow(tr, 2 * FFN, nt), _const((N_CHIPS, 8, UPQ)), _const((1, 2 * FFN))],
        out_shape=[_sds((R, 2 * FFN), BF16), _sds((N_CHIPS, 8, UPQ), F32), _sds((1, 2 * FFN), F32)],
        scratch_shapes=[pltpu.VMEM((8, 2 * FFN), F32)],
        compiler_params=_params(("arbitrary",)),
    )(dffn, w_down, up, up, fw, fb)


def _ffn_bwd_in(dup, w_up, h1, dh2, mix, g_ffn, g_post, tr):
    R = dup.shape[0]

    def body(d_ref, w_ref, h1_ref, dh2_ref, mix_ref, gf_ref, gp_ref, dh1_ref, dmix_ref, dgf_ref, dgp_ref):
        @pl.when(pl.program_id(0) == 0)
        def _():
            dgf_ref[...] = jnp.zeros_like(dgf_ref)
            dgp_ref[...] = jnp.zeros_like(dgp_ref)

        dn2 = sum(lax.dot_general(d_ref[:, q * UPQ:(q + 1) * UPQ], w_ref[q], NT_DIMS, preferred_element_type=F32)
                  for q in range(N_CHIPS))
        h1 = h1_ref[...]
        _, r2 = _rms(h1, gf_ref[...])
        dh1 = dh2_ref[...] + _rms_bwd(dn2, h1, r2, gf_ref[...])
        dgf_ref[...] += _colsum(dn2 * h1 * r2)
        dh1_ref[...] = dh1
        m = mix_ref[...].astype(F32)
        _, rm = _rms(m, gp_ref[...])
        dmix_ref[...] = _rms_bwd(dh1, m, rm, gp_ref[...]).astype(BF16)
        dgp_ref[...] += _colsum(dh1 * m * rm)

    vec = _const((1, D))
    return pl.pallas_call(
        body, name="ffn_bwd_in", grid=(R // tr,),
        in_specs=[_row(tr, 2 * FFN), _const((N_CHIPS, D, UPQ)), _row(tr, D), _row(tr, D), _row(tr, D), vec, vec],
        out_specs=[_row(tr, D), _row(tr, D), vec, vec],
        out_shape=[_sds((R, D), F32), _sds((R, D), BF16), _sds((1, D), F32), _sds((1, D), F32)],
        compiler_params=_params(("arbitrary",)),
    )(dup, w_up, h1, dh2, mix, g_ffn, g_post)


def _mix_bwd(dmix, ao, co, gates, c1, w_out, w_ap, w_cp, lg, lb, tr):
    R = dmix.shape[0]

    def body(dm_ref, ao_ref, co_ref, ga_ref, gc_ref, c1_ref, wo, wap, wcp, lg_ref, lb_ref,
             dao_ref, dco_ref, dgate_ref, dattn_ref, dc1_ref, dbcp_ref, dlg_ref, dlb_ref, dcb_ref):
        @pl.when(pl.program_id(0) == 0)
        def _():
            for ref in (dbcp_ref, dlg_ref, dlb_ref, dcb_ref):
                ref[...] = jnp.zeros_like(ref)

        dmg = lax.dot_general(dm_ref[...], wo[...], NT_DIMS, preferred_element_type=F32)
        sa = jax.nn.sigmoid(ga_ref[...].astype(F32))
        sc = jax.nn.sigmoid(gc_ref[...].astype(F32))
        dao = (dmg * sa).astype(BF16)
        dco = (dmg * sc).astype(BF16)
        dao_ref[...] = dao
        dco_ref[...] = dco
        dgate_ref[:, 0:D] = (dmg * ao_ref[...].astype(F32) * sa * (1.0 - sa)).astype(BF16)
        dgate_ref[:, D:2 * D] = (dmg * co_ref[...].astype(F32) * sc * (1.0 - sc)).astype(BF16)
        dbcp_ref[...] += _colsum(dco.astype(F32))
        dattn_ref[...] = lax.dot_general(dao, wap[...], NT_DIMS, preferred_element_type=F32).astype(BF16)
        dc3 = lax.dot_general(dco, wcp[...], NT_DIMS, preferred_element_type=F32)
        xh, rs, c2, sg = _ln_silu(c1_ref[...].astype(F32), lg_ref[...], lb_ref[...])
        dc2 = dc3 * sg * (1.0 + c2 * (1.0 - sg))
        dlg_ref[...] += _colsum(dc2 * xh)
        dlb_ref[...] += _colsum(dc2)
        dxh = dc2 * lg_ref[...]
        dc1 = rs * (dxh - jnp.mean(dxh, -1, keepdims=True) - xh * jnp.mean(dxh * xh, -1, keepdims=True))
        dc1_ref[...] = dc1
        dcb_ref[...] += _colsum(dc1)

    vec = _const((1, D))
    return pl.pallas_call(
        body, name="mix_bwd", grid=(R // tr,),
        in_specs=[_row(tr, D), _row(tr, D), _row(tr, D), _row(tr, D, 0), _row(tr, D, 1), _row(tr, D),
                  _const((D, D)), _const((D, D)), _const((D, D)), vec, vec],
        out_specs=[_row(tr, D), _row(tr, D), _row(tr, 2 * D), _row(tr, D), _row(tr, D), vec, vec, vec, vec],
        out_shape=[_sds((R, D), BF16), _sds((R, D), BF16), _sds((R, 2 * D), BF16), _sds((R, D), BF16),
                   _sds((R, D), F32)] + [_sds((1, D), F32)] * 4,
        compiler_params=_params(("arbitrary",)),
    )(dmix, ao, co, gates, gates, c1, w_out, w_ap, w_cp, lg, lb)


def _conv_bwd(dc1, glu, w, tr, hook=None):
    R = dc1.shape[0]
    nt, nc = R // tr, D // CONV_TCH

    def body(d_ref, a_ref, g_ref, w_ref, dglu_a, dglu_g, dw_ref, buf, dw_acc):
        t = pl.program_id(1)
        i = nt - 1 - t

        @pl.when(t == 0)
        def _():
            buf[tr:tr + HALO, :] = jnp.zeros((HALO, CONV_TCH), F32)
            dw_acc[...] = jnp.zeros_like(dw_acc)

        @pl.when(t > 0)
        def _():
            buf[tr:tr + HALO, :] = buf[0:HALO, :]

        buf[0:tr, :] = d_ref[...]
        for r0 in range(0, tr, CONV_SUB):
            rs = slice(r0, r0 + CONV_SUB)
            row = i * tr + r0 + lax.broadcasted_iota(jnp.int32, (CONV_SUB, 1), 0)
            a, g = a_ref[rs, :].astype(F32), g_ref[rs, :].astype(F32)
            sg = jax.nn.sigmoid(g)
            glu = jnp.where(row >= PAD, a * sg, 0.0)
            acc = jnp.zeros((CONV_SUB, CONV_TCH), F32)
            for k, win in _tap_windows(buf, r0, lambda k: CONV_K - 1 - k):
                acc = acc + w_ref[k:k + 1, :] * win
                dw_acc[k] += jnp.sum((glu * win).reshape(CONV_SUB // 8, 8, CONV_TCH), axis=0)
            dglu = jnp.where(row >= PAD, acc, 0.0)
            dglu_a[rs, :] = (dglu * sg).astype(BF16)
            dglu_g[rs, :] = (dglu * a * sg * (1.0 - sg)).astype(BF16)

        @pl.when(t == nt - 1)
        def _():
            dw_ref[...] = jnp.sum(dw_acc[...], axis=1)

    def rspec(col0):
        return pl.BlockSpec((tr, CONV_TCH), lambda c, t: (nt - 1 - t, col0 + c))

    def steps():
        c, t = pl.program_id(0), pl.program_id(1)
        return (c == 0) & (t == 0), False, (c == nc - 1) & (t == nt - 1)

    return _call(
        body, hook, steps, name="conv_bwd", grid=(nc, nt),
        in_specs=[rspec(0), rspec(0), rspec(nc), pl.BlockSpec((None, HALO, CONV_TCH), lambda c, t: (c, 0, 0))],
        out_specs=[rspec(0), rspec(0), pl.BlockSpec((None, HALO, CONV_TCH), lambda c, t: (c, 0, 0))],
        out_shape=[_sds((R, D), BF16), _sds((R, D), BF16), _sds((N_CHIPS, HALO, CONV_TCH), F32)],
        scratch_shapes=[pltpu.VMEM((tr + HALO, CONV_TCH), F32), pltpu.VMEM((HALO, 8, CONV_TCH), F32)],
        operands=(dc1, glu, glu, w), semantics=("arbitrary", "arbitrary"))


def _attn_bwd(qkv, do, lse, sinks, rope, hook=None):
    R = qkv.shape[0]
    nb = R // BLK

    def body(s_ref, q_ref, k0, kp, kc, v0, vp, vc, do_ref, lse_ref, c_ref, sa_ref, sb_ref,
             dqkv_ref, dsink_ref, car_k, car_v, met_k, met_v):
        t = pl.program_id(0)
        n = nb - 1 - t

        @pl.when(t == 0)
        def _():
            for ref in (car_k, car_v, met_k, met_v, dsink_ref):
                ref[...] = jnp.zeros_like(ref)

        lane = lax.broadcasted_iota(jnp.int32, (1, BLK), 1)
        low = lane < HD
        kd, vd = _dup_heads(k0, kp, kc, low), _dup_heads(v0, vp, vc, low)
        kd_t = _dup_heads(k0, kp, kc, low, transposed=True)
        mask = _attn_mask(n, keys_first=True)
        tabs = (c_ref[...], sa_ref[...], sb_ref[...])
        zero = jnp.zeros((), BF16)
        dk_acc, dv_acc = [], []
        dsink = jnp.zeros((1, BLK), F32)
        for g in range(NKV):
            q_tiles, do_tiles = [], []
            for pair in range(GH // 2 * g, GH // 2 * (g + 1)):
                cs = slice(pair * BLK, (pair + 1) * BLK)
                qp, dop = q_ref[:, cs] * jnp.asarray(HD ** -0.5, BF16), do_ref[:, cs]
                q_tiles += [jnp.where(low, qp, zero), jnp.where(low, zero, qp)]
                do_tiles += [jnp.where(low, dop, zero), jnp.where(low, zero, dop)]
            qs, dos = jnp.concatenate(q_tiles, 0), jnp.concatenate(do_tiles, 0)
            st = lax.dot_general(kd[g], qs, NT_DIMS, preferred_element_type=F32)
            dpt = lax.dot_general(vd[g], dos, NT_DIMS, preferred_element_type=F32)
            ps, dss = [], []
            for j in range(GH):
                h = GH * g + j
                cs = slice(j * BLK, (j + 1) * BLK)
                lse_h = lse_ref[h:h + 1, :]
                p = jnp.where(mask, jnp.exp(st[:, cs] - lse_h), 0.0)
                dp = dpt[:, cs]
                delta = jnp.sum(p * dp, 0, keepdims=True)
                ps.append(p.astype(BF16))
                dss.append((p * (dp - delta)).astype(BF16))
                dsink = dsink + jnp.where(lane == h, -jnp.sum(jnp.exp(s_ref[h] - lse_h) * delta), 0.0)
            ds_t, p_t = jnp.concatenate(dss, 1), jnp.concatenate(ps, 1)
            dk_acc.append(jnp.dot(ds_t, qs, preferred_element_type=F32))
            dv_acc.append(jnp.dot(p_t, dos, preferred_element_type=F32))
            dq_t = jnp.dot(kd_t[g], ds_t, preferred_element_type=F32) * (HD ** -0.5)
            for j in range(GH // 2):
                pair = GH // 2 * g + j
                dq = [dq_t[:, h * BLK:(h + 1) * BLK].T for h in (2 * j, 2 * j + 1)]
                dqkv_ref[:, pair * BLK:(pair + 1) * BLK] = _rope_bwd(jnp.where(low, dq[0], dq[1]), *tabs).astype(BF16)
        dsink_ref[0:1, :] += dsink

        def fold(acc):
            tot = [a + pltpu.roll(a, HD, 1) for a in acc]
            return jnp.where(low, tot[0], tot[1])

        dk_all, dv_all = fold(dk_acc), fold(dv_acc)
        met_k[...] += dk_all[0:BLK, :]
        met_v[...] += dv_all[0:BLK, :]
        last = jnp.where(n == 0, 1.0, 0.0)
        dk_n = dk_all[2 * BLK:3 * BLK, :] + car_k[...] + last * met_k[...]
        dv_n = dv_all[2 * BLK:3 * BLK, :] + car_v[...] + last * met_v[...]
        dqkv_ref[:, D:D + BLK] = _rope_bwd(dk_n, *tabs).astype(BF16)
        dqkv_ref[:, D + BLK:QKV_W] = dv_n.astype(BF16)
        car_k[...] = dk_all[BLK:2 * BLK, :]
        car_v[...] = dv_all[BLK:2 * BLK, :]

    rblk = lambda w: pl.BlockSpec((BLK, w), lambda t: (nb - 1 - t, 0))
    return _call(
        body, hook, _steps_1d(nb, nb), name="attn_bwd", grid=(nb,),
        in_specs=[pl.BlockSpec(memory_space=pltpu.SMEM), rblk(D)] + _kv_specs(nb, True)
                 + [rblk(D), pl.BlockSpec((NH, BLK), lambda t: (nb - 1 - t, 0)), rblk(BLK), rblk(BLK), rblk(BLK)],
        out_specs=[rblk(QKV_W), _const((8, BLK))],
        out_shape=[_sds((R, QKV_W), BF16), _sds((8, BLK), F32)],
        scratch_shapes=[pltpu.VMEM((BLK, BLK), F32)] * 4,
        operands=(sinks, qkv, *([qkv] * 6), do, lse, *rope), semantics=("arbitrary",))


def _in_bwd(dproj, w_in, h0, dh1, g_pre, tr, hook=None):
    R = h0.shape[0]
    n = len(dproj)
    widths = [p.shape[1] for p in dproj]
    starts = [sum(widths[:j]) for j in range(n)]

    def body(*refs):
        d_refs, (w_ref, h0_ref, dh1_ref, g_ref, dh0_ref, dg_ref, db_ref) = refs[:n], refs[n:]

        @pl.when(pl.program_id(0) == 0)
        def _():
            dg_ref[...] = jnp.zeros_like(dg_ref)
            db_ref[...] = jnp.zeros_like(db_ref)

        dn1 = jnp.zeros((tr, D), F32)
        for d_ref, c0, wd in zip(d_refs, starts, widths):
            d = d_ref[...]
            dn1 = dn1 + jnp.dot(d, w_ref[c0:c0 + wd, :], preferred_element_type=F32)
            db_ref[:, c0:c0 + wd] += _colsum(d.astype(F32))
        h0 = h0_ref[...]
        _, r = _rms(h0, g_ref[...])
        dh0_ref[...] = dh1_ref[...] + _rms_bwd(dn1, h0, r, g_ref[...])
        dg_ref[...] += _colsum(dn1 * h0 * r)

    return _call(
        body, hook, _steps_1d(R // tr, R // tr), name="in_bwd", grid=(R // tr,),
        in_specs=[_row(tr, wd) for wd in widths] + [_const((IN_W, D)), _row(tr, D), _row(tr, D), _const((1, D))],
        out_specs=[_row(tr, D), _const((1, D)), _const((1, IN_W))],
        out_shape=[_sds((R, D), F32), _sds((1, D), F32), _sds((1, IN_W), F32)],
        operands=(*dproj, w_in, h0, dh1, g_pre), semantics=("arbitrary",))


def _dw(a, b, name, tn, tr, by_chip=False, ta=None, rows_of=None, row0=0, into=None):
    R, ka = a.shape
    n = b.shape[1]
    nt = R // tr
    ta = ta or ka
    k0 = 0
    if by_chip:
        out_spec = pl.BlockSpec((None, ta, tn), lambda k, j, i: (j, k, 0))
        out_shape = _sds((n // tn, ka, tn), BF16)
    else:
        if rows_of is not None:
            k0 = row0 // ta
        out_spec = pl.BlockSpec((ta, tn), lambda k, j, i: (k0 + k, j))
        out_shape = _sds((ka if rows_of is None else rows_of, n), BF16)
    extra = [] if into is None else [into]

    def body(a_ref, b_ref, *rest):
        o_ref, acc = rest[len(extra):]
        i = pl.program_id(2)

        @pl.when(i == 0)
        def _():
            acc[...] = jnp.zeros_like(acc)

        acc[...] += lax.dot_general(a_ref[...], b_ref[...], TN_DIMS, preferred_element_type=F32)

        @pl.when(i == nt - 1)
        def _():
            o_ref[...] = acc[...].astype(BF16)

    return pl.pallas_call(
        body, name=name, grid=(ka // ta, n // tn, nt),
        in_specs=[pl.BlockSpec((tr, ta), lambda k, j, i: (i, k)), pl.BlockSpec((tr, tn), lambda k, j, i: (i, j))]
                 + [pl.BlockSpec(memory_space=pl.ANY)] * len(extra),
        out_specs=out_spec, out_shape=out_shape,
        input_output_aliases={2: 0} if extra else {},
        scratch_shapes=[pltpu.VMEM((ta, tn), F32)],
        compiler_params=_params(("arbitrary", "arbitrary", "arbitrary")),
    )(a, b, *extra)


SMALL = ["norm_pre_mix", "norm_post_mix", "b_in", "attn_sinks", "conv_dw_b", "conv_ln_g", "conv_ln_b",
         "b_conv_proj", "norm_pre_ffn", "norm_post_ffn", "ffn_dw_b"]


def local_step(x, tgt, W, dist=None):
    W = dict(W)
    S = x.shape[0]
    R = S + BLK
    tr = _tile(R, 384, BLK)
    trw = _tile(R, 1056)
    rope = _rope_tables(R)
    meta = _cols_joined(W["tiny"][:, TINY_META:TINY_META + NMETA, 0:DQ])
    h0 = jnp.concatenate([jnp.zeros((PAD, D), F32), meta, x], 0)

    qkv, glu, gates, n1, *got = _in_proj(h0, W["norm_pre_mix"], W["w_in"], W["b_in"], rope, tr,
                                         dist and dist.gather_hook(GATHER_IN_PROJ))
    if dist:
        W.update(dist.weights(GATHER_IN_PROJ, got))
    sinks = W["attn_sinks"].reshape(NH)
    attn, lse, *got = _attn_fwd(qkv, sinks, dist and dist.gather_hook(GATHER_ATTN))
    if dist:
        W.update(dist.weights(GATHER_ATTN, got))
    c1, *got = _conv_fwd(glu, W["tiny"], W["conv_dw_b"], tr, dist and dist.gather_hook(GATHER_CONV))
    if dist:
        W.update(dist.weights(GATHER_CONV, got))
    ao, co, c3, merged, mix, h1, n2 = _mix_out(
        attn, c1, gates, h0, W["w_attn_proj"], W["w_conv_proj"], W["w_out"], W["conv_ln_g"], W["conv_ln_b"],
        W["b_conv_proj"], W["norm_post_mix"], W["norm_pre_ffn"], tr)
    up, act = _ffn_up(n2, W["w_up"], W["tiny"], W["ffn_dw_b"], tr)
    dh2, dffn, loss_cols, dg_post_ffn = _ffn_down(act, W["w_down"], h1, tgt, W["norm_post_ffn"], tr)

    dw_down = _dw(act, dffn, "dw_down", 512, trw)
    dup, dfw, dfb = _ffn_bwd_act(dffn, W["w_down"], up, W["tiny"], W["ffn_dw_b"], tr)
    dw_up = _dw(n2, dup, "dw_up", UPQ, trw, by_chip=True)
    ffn_sums = dist and dist.pair_sums(FFN_SHARES, [dw_up, dw_down.reshape(N_CHIPS, -1, D)], "ffn")
    dh1, dmix, dg_pre_ffn, dg_post_mix = _ffn_bwd_in(dup, W["w_up"], h1, dh2, mix, W["norm_pre_ffn"],
                                                      W["norm_post_mix"], tr)
    dao, dco, dgates, dattn, dc1, db_cp, dlg, dlb, dcb = _mix_bwd(
        dmix, ao, co, gates, c1, W["w_out"], W["w_attn_proj"], W["w_conv_proj"], W["conv_ln_g"], W["conv_ln_b"], tr)
    dglu_a, dglu_g, dcw, *ffn_got = _conv_bwd(dc1, glu, W["tiny"], tr, dist and dist.chip_hook(FFN_SHARES, ffn_sums))
    branch = [_dw(attn, dao, "dw_attn_proj", D, trw), _dw(c3, dco, "dw_conv_proj", D, trw),
              _dw(merged, dmix, "dw_out", D, trw),
              _tiny_pack({"conv_dw_w": dcw, "ffn_dw_w": dfw, "meta_tokens": jnp.zeros((N_CHIPS, NMETA, DQ), F32)})]
    branch_sums = dist and dist.pair_sums(BRANCH_TINY_SHARES, [a.reshape(N_CHIPS, -1, a.shape[-1]) for a in branch], "branch")
    dqkv, dsink, *got = _attn_bwd(qkv, dattn, lse, sinks, rope,
                                  dist and dist.chip_hook(BRANCH_TINY_SHARES, branch_sums))
    if dist:
        dist.finish(FFN_SHARES + BRANCH_TINY_SHARES, ffn_sums + branch_sums, ffn_got + got, "ffn_branch")
    dproj = [dqkv, dglu_a, dglu_g, dgates]
    dw_in, row0 = None, 0
    for j, p in enumerate(dproj):
        ta = _tile(p.shape[1], D, BLK) if j == 0 else 2 * BLK
        dw_in = _dw(p, n1, "dw_in_%d" % j, D, _tile(R, 2 * trw), ta=ta, rows_of=IN_W, row0=row0, into=dw_in)
        row0 += p.shape[1]
    in_sums = dist and dist.pair_sums(IN_SHARES, [dw_in.reshape(N_CHIPS, -1, D)], "in")
    dh0, dg_pre_mix, db_in, *got = _in_bwd(dproj, W["w_in"], h0, dh1, W["norm_pre_mix"], tr,
                                           dist and dist.chip_hook(IN_SHARES, in_sums))
    if dist:
        dist.finish(IN_SHARES, in_sums, got, "in")

    grads = {
        "w_in": dw_in, "w_attn_proj": branch[0], "w_conv_proj": branch[1], "w_out": branch[2],
        "w_up": dw_up,
        "w_down": dw_down,
        "tiny": branch[3],
        "meta_tokens": dh0[PAD:BLK],
        "norm_pre_mix": dg_pre_mix, "norm_post_mix": dg_post_mix, "b_in": db_in,
        "attn_sinks": dsink[0:1, 0:NH], "conv_dw_b": dcb, "conv_ln_g": dlg, "conv_ln_b": dlb,
        "b_conv_proj": db_cp, "norm_pre_ffn": dg_pre_ffn, "norm_post_ffn": dg_post_ffn, "ffn_dw_b": dfb,
    }
    return loss_cols, dh0[BLK:], grads


INQ = IN_W // N_CHIPS
DQ = D // N_CHIPS
SHARES = [("w_in", INQ, D, BF16), ("w_attn_proj", DQ, D, BF16), ("w_conv_proj", DQ, D, BF16), ("w_out", DQ, D, BF16),
          ("w_up", D, UPQ, BF16), ("w_down", FFN // N_CHIPS, D, BF16), ("tiny", TINY_ROWS, UPQ, F32)]
TINY_PARTS = [("conv_dw_w", TINY_CONV, CONV_K, TINY_FFN - TINY_CONV, DQ), ("ffn_dw_w", TINY_FFN, FFN_K, TINY_META - TINY_FFN, UPQ),
              ("meta_tokens", TINY_META, NMETA, NMETA, DQ)]


def _tiny_pack(parts):
    rows = []
    for name, _, _, reserved, _ in TINY_PARTS:
        a = parts[name].astype(F32)
        pad = [(0, 0)] * (a.ndim - 2) + [(0, reserved - a.shape[-2]), (0, UPQ - a.shape[-1])]
        rows.append(jnp.pad(a, pad))
    used = sum(r.shape[-2] for r in rows)
    rows.append(jnp.zeros(rows[0].shape[:-2] + (TINY_ROWS - used, UPQ), F32))
    return jnp.concatenate(rows, axis=-2)


def _tiny_unpack(tiny):
    return {name: tiny[..., r0:r0 + k, 0:cols] for name, r0, k, _, cols in TINY_PARTS}


def _cols_by_chip(a):
    rows, n = a.shape
    return a.reshape(rows, N_CHIPS, n // N_CHIPS).transpose(1, 0, 2)


def _cols_joined(a):
    _, rows, cols = a.shape
    return a.transpose(1, 0, 2).reshape(rows, N_CHIPS * cols)


def _to_planes(a, rows):
    return jnp.pad(a, [(0, rows * D - a.shape[-1])]).reshape(rows, D)


ANY = pl.BlockSpec(memory_space=pl.ANY)


def _place():
    x, y, c = lax.axis_index("x"), lax.axis_index("y"), lax.axis_index("c")
    chips = [(1 - x, y), (x, 1 - y), (1 - x, 1 - y)]
    return x, y, c, chips


def _rcopy(src, dst, ssem, rsem, to):
    return pltpu.make_async_remote_copy(src_ref=src, dst_ref=dst, send_sem=ssem, recv_sem=rsem,
                                        device_id=to, device_id_type=MESH)


def _halves(ref_or_rows, c):
    half = ref_or_rows // 2
    return pl.ds(c * half, half), pl.ds((1 - c) * half, half)


FIRST_SHARES, BRANCH_SHARES, FFN_SHARES = [0, 6], [1, 2, 3], [4, 5]
IN_SHARES, BRANCH_TINY_SHARES = [0], [1, 2, 3, 6]
GATHER_IN_PROJ, GATHER_ATTN, GATHER_CONV = [1, 2, 3], [4], [5]


def _gather_hook(own, idx):
    n = len(idx)

    def copies(kind, ins, outs, ssem, rsem):
        x, y, c, chips = _place()
        q = 2 * x + y
        sib = (x, y, 1 - c)
        out = []
        for i, a in enumerate(idx):
            mine, other = _halves(SHARES[a][1], c)
            for j, (cx, cy) in enumerate(chips):
                k, to = 3 * i + j, (cx, cy, c)
                landed, theirs = outs[i].at[2 * cx + cy, mine], outs[i].at[2 * cx + cy, other]
                if kind == "send":
                    out.append(_rcopy(ins[i].at[mine], outs[i].at[q, mine], ssem.at[k], rsem.at[k], to))
                elif kind == "landing":
                    out.append(_rcopy(ins[i].at[mine], landed, ssem.at[k], rsem.at[k], to))
                elif kind == "pass":
                    out.append(_rcopy(landed, landed, ssem.at[3 * n + k], rsem.at[3 * n + k], sib))
                else:
                    out.append(_rcopy(theirs, theirs, ssem.at[3 * n + k], rsem.at[3 * n + k], sib))
        return out

    def own_copies(ins, outs, ssem, rsem):
        x, y, c, _ = _place()
        q = 2 * x + y
        return [_rcopy(ins[i], outs[i].at[q], ssem.at[6 * n + i], rsem.at[6 * n + i], (x, y, 1 - c)) for i in range(n)]

    def start(*refs):
        for cp in copies("send", *refs) + own_copies(*refs):
            cp.start()

    def mid(*refs):
        for landed, cp in zip(copies("landing", *refs), copies("pass", *refs)):
            landed.wait_recv()
            cp.start()

    def finish(*refs):
        for cp in copies("arrival", *refs):
            cp.wait_recv()
        for cp in copies("send", *refs) + copies("pass", *refs):
            cp.wait_send()
        for cp in own_copies(*refs):
            cp.wait()

    shapes = [_sds((N_CHIPS,) + SHARES[a][1:3], SHARES[a][3]) for a in idx]
    return _Hook(own, shapes, 7 * n, start, finish, mid)


def _chip_hook(sums, idx):
    def copies(ins, outs, ssem, rsem):
        x, y, c, chips = _place()
        return [_rcopy(ins[i].at[2 * cx + cy], outs[i].at[j], ssem.at[3 * i + j], rsem.at[3 * i + j], (cx, cy, c))
                for i in range(len(idx)) for j, (cx, cy) in enumerate(chips)]

    def start(*refs):
        for cp in copies(*refs):
            cp.start()

    def finish(*refs):
        for cp in copies(*refs):
            cp.wait()

    shapes = [_sds((N_CHIPS - 1, SHARES[a][1] // 2, SHARES[a][2]), SHARES[a][3]) for a in idx]
    return _Hook(sums, shapes, 3 * len(idx), start, finish)


def _sibling_swap(parts, idx, tag):
    def copies(ins, outs, ssem, rsem):
        x, y, c, _ = _place()
        return [_rcopy(ins[i].at[:, _halves(SHARES[a][1], c)[1]], outs[i], ssem.at[i], rsem.at[i], (x, y, 1 - c))
                for i, a in enumerate(idx)]

    def start(*refs):
        for cp in copies(*refs):
            cp.start()

    def finish(*refs):
        for cp in copies(*refs):
            cp.wait()

    shapes = [_sds((N_CHIPS, SHARES[a][1] // 2, SHARES[a][2]), SHARES[a][3]) for a in idx]
    return _alone(_Hook(parts, shapes, len(idx), start, finish), "sibling_swap_" + tag)


def _sum_pair(parts, recvs, c, idx, tag):
    steps, n = 2, len(idx)

    def body(c_ref, *refs):
        for i, a in enumerate(idx):
            refs[2 * n + i][...] = (refs[i][...].astype(F32) + refs[n + i][...].astype(F32)).astype(SHARES[a][3])

    own, got, out, views, shapes = [], [], [], [], []
    for p, a in zip(parts, idx):
        _, rows, cols, dt = SHARES[a]
        blk = rows // 2 // steps
        own.append(pl.BlockSpec((None, None, blk, cols), lambda q, i, c_ref: (q, c_ref[0], i, 0)))
        got.append(pl.BlockSpec((None, blk, cols), lambda q, i, c_ref: (q, i, 0)))
        out.append(pl.BlockSpec((None, blk, cols), lambda q, i, c_ref: (q, i, 0)))
        views.append(p.reshape(N_CHIPS, 2, rows // 2, cols))
        shapes.append(_sds((N_CHIPS, rows // 2, cols), dt))
    grid_spec = pltpu.PrefetchScalarGridSpec(num_scalar_prefetch=1, grid=(N_CHIPS, steps),
                                             in_specs=own + got, out_specs=out)
    return pl.pallas_call(body, name="sum_pair_" + tag, grid_spec=grid_spec, out_shape=shapes,
                          compiler_params=_params(("arbitrary", "arbitrary")))(c, *views, *recvs)


def _sum_chips(sums, recvs, qc, idx, tag):
    steps, n = 2, len(idx)

    def body(qc_ref, *refs):
        for i in range(n):
            acc = refs[i][...].astype(F32)
            for j in range(1, N_CHIPS):
                acc = acc + refs[j * n + i][...].astype(F32)
            refs[N_CHIPS * n + i][...] = acc

    own, got, out, shapes = [], [[], [], []], [], []
    for a in idx:
        _, rows, cols, _ = SHARES[a]
        blk = rows // 2 // steps
        own.append(pl.BlockSpec((None, blk, cols), lambda i, qc_ref: (qc_ref[0], i, 0)))
        for j in range(N_CHIPS - 1):
            got[j].append(pl.BlockSpec((None, blk, cols), lambda i, qc_ref, j=j: (j, i, 0)))
        out.append(pl.BlockSpec((None, blk, cols), lambda i, qc_ref: (qc_ref[1], i, 0)))
        shapes.append(_sds((2, rows // 2, cols), F32))
    grid_spec = pltpu.PrefetchScalarGridSpec(num_scalar_prefetch=1, grid=(steps,),
                                             in_specs=own + got[0] + got[1] + got[2], out_specs=out)
    return pl.pallas_call(body, name="sum_chips_" + tag, grid_spec=grid_spec, out_shape=shapes,
                          compiler_params=_params(("arbitrary",)))(qc, *sums, *recvs, *recvs, *recvs)


def _sibling_share(halves, idx, tag):
    n = len(idx)

    def body(*refs):
        outs, (ssem, rsem) = refs[n:2 * n], refs[2 * n:]
        x, y, c, _ = _place()
        copies = []
        for i in range(n):
            cp = _rcopy(outs[i].at[c], outs[i].at[c], ssem.at[i], rsem.at[i], (x, y, 1 - c))
            cp.start()
            copies.append(cp)
        for i in range(n):
            theirs = outs[i].at[1 - c]
            _rcopy(theirs, theirs, ssem.at[i], rsem.at[i], (x, y, 1 - c)).wait_recv()
        for cp in copies:
            cp.wait_send()

    return pl.pallas_call(
        body, name="sibling_share_" + tag, in_specs=[ANY] * n, out_specs=[ANY] * n,
        out_shape=[_sds((2, SHARES[a][1] // 2, SHARES[a][2]), F32) for a in idx],
        input_output_aliases={i: i for i in range(n)},
        scratch_shapes=[pltpu.SemaphoreType.DMA((n,)), pltpu.SemaphoreType.DMA((n,))],
    )(*halves)


class _Dist:
    def __init__(self, own):
        self.own = own
        self.core = lax.axis_index("c")
        self.chip = 2 * lax.axis_index("x") + lax.axis_index("y")
        self.reduced = {}

    def gather_hook(self, idx):
        return _gather_hook([self.own[a] for a in idx], idx)

    def weights(self, idx, gathered):
        out = {}
        for a, full in zip(idx, gathered):
            name = SHARES[a][0]
            out[name] = full if name in ("w_up", "tiny") else full.reshape(-1, D)
        return out

    def pair_sums(self, idx, parts, tag):
        return _sum_pair(parts, _sibling_swap(parts, idx, tag), self.core.reshape(1), idx, tag)

    def chip_hook(self, idx, sums):
        return _chip_hook(sums, idx)

    def finish(self, idx, sums, recvs, tag):
        halves = _sum_chips(sums, recvs, jnp.stack([self.chip, self.core]), idx, tag)
        for a, full in zip(idx, _sibling_share(halves, idx, tag)):
            self.reduced[SHARES[a][0]] = full.reshape(SHARES[a][1:3])


N_DEV = 8
SMALL_ROWS = 40


def _small_allreduce(sm):
    def body(s_ref, o_ref, buf, ssem, rsem):
        x, y, c, _ = _place()
        me = 4 * x + 2 * y + c
        buf[me] = s_ref[...]
        copies = []
        for d in range(1, N_DEV):
            dx, dy, dc = d >> 2, (d >> 1) & 1, d & 1
            to = (x ^ dx, y ^ dy, c ^ dc)
            cp = _rcopy(s_ref, buf.at[me], ssem.at[d - 1], rsem.at[d - 1], to)
            cp.start()
            copies.append(cp)
        for d in range(1, N_DEV):
            src = me ^ d
            _rcopy(s_ref, buf.at[src], ssem.at[d - 1], rsem.at[d - 1], (x, y, c)).wait_recv()
        for cp in copies:
            cp.wait_send()
        acc = buf[0]
        for k in range(1, N_DEV):
            acc = acc + buf[k]
        o_ref[...] = acc

    vm = pl.BlockSpec(memory_space=pltpu.VMEM)
    return pl.pallas_call(
        body, name="small_allreduce", in_specs=[vm], out_specs=vm,
        out_shape=_sds((SMALL_ROWS, D), F32),
        scratch_shapes=[pltpu.VMEM((N_DEV, SMALL_ROWS, D), F32),
                        pltpu.SemaphoreType.DMA((N_DEV - 1,)), pltpu.SemaphoreType.DMA((N_DEV - 1,))],
    )(sm)


SMALL_PLAN = [("norm_pre_mix", D), ("norm_post_mix", D), ("b_in", IN_W), ("attn_sinks", NH), ("conv_dw_b", D),
              ("conv_ln_g", D), ("conv_ln_b", D), ("b_conv_proj", D), ("norm_pre_ffn", D), ("norm_post_ffn", D),
              ("ffn_dw_b", 2 * FFN), ("loss", D), ("meta_tokens", NMETA * D)]


def _pack_small(parts):
    rows = [_to_planes(parts[name].reshape(-1), -(-n // D)) for name, n in SMALL_PLAN]
    used = sum(r.shape[0] for r in rows)
    return jnp.concatenate(rows + [jnp.zeros((SMALL_ROWS - used, D), F32)], 0)


def _unpack_small(packed):
    out, r0 = {}, 0
    for name, n in SMALL_PLAN:
        rows = -(-n // D)
        out[name] = packed[r0:r0 + rows].reshape(-1)[:n].reshape(1, n)
        r0 += rows
    return out


def _adamw_update(w_ref, g_ref, m_ref, v_ref, d_ref, nm_ref, nv_ref):
    g = g_ref[...]
    m = B1 * m_ref[...] + (1.0 - B1) * g
    v = B2 * v_ref[...] + (1.0 - B2) * (g * g)
    nm_ref[...] = m
    nv_ref[...] = v
    m_hat = m / (1.0 - B1 ** STEP)
    v_hat = v / (1.0 - B2 ** STEP)
    d_ref[...] = -LR * (m_hat / (jnp.sqrt(v_hat) + ADAM_EPS) + WD * w_ref[...])


def _adamw_vectors(ws, gs, ms, vs):
    n = len(ws)

    def body(*refs):
        for j in range(n):
            _adamw_update(*[refs[k * n + j] for k in range(7)])

    vm = pl.BlockSpec(memory_space=pltpu.VMEM)
    outs = pl.pallas_call(body, name="adamw_vectors", in_specs=[vm] * (4 * n), out_specs=[vm] * (3 * n),
                          out_shape=[_sds(w.shape, F32) for w in ws] * 3)(*ws, *gs, *ms, *vs)
    return outs[:n], outs[n:2 * n], outs[2 * n:]


def _adamw(w, g, m, v, name):
    rows, cols = w.shape
    tr = _tile(rows, 256, 8) if rows % 8 == 0 else rows

    def body(*refs):
        _adamw_update(*refs)

    spec = pl.BlockSpec((tr, cols), lambda i: (i, 0))
    return pl.pallas_call(
        body, name=name, grid=(rows // tr,), in_specs=[spec] * 4, out_specs=[spec] * 3,
        out_shape=[_sds((rows, cols), F32)] * 3, compiler_params=_params(("arbitrary",)),
    )(w, g, m, v)


NAMES = ["meta_tokens", "norm_pre_mix", "norm_post_mix", "w_in", "b_in", "attn_sinks", "w_attn_proj", "conv_dw_w",
         "conv_dw_b", "conv_ln_g", "conv_ln_b", "w_conv_proj", "b_conv_proj", "w_out", "norm_pre_ffn", "norm_post_ffn",
         "w_up", "ffn_dw_w", "ffn_dw_b", "w_down"]
MATMUL = ("w_in", "w_attn_proj", "w_conv_proj", "w_out", "w_up", "w_down")


def _two_d(a):
    return a.reshape(a.shape[-2:])


def kernel(x, meta_tokens, norm_pre_mix, norm_post_mix, w_in, b_in, attn_sinks, w_attn_proj, conv_dw_w, conv_dw_b, conv_ln_g, conv_ln_b, w_conv_proj, b_conv_proj, w_out, norm_pre_ffn, norm_post_ffn, w_up, ffn_dw_w, ffn_dw_b, w_down, loss_target, m_meta_tokens, m_norm_pre_mix, m_norm_post_mix, m_w_in, m_b_in, m_attn_sinks, m_w_attn_proj, m_conv_dw_w, m_conv_dw_b, m_conv_ln_g, m_conv_ln_b, m_w_conv_proj, m_b_conv_proj, m_w_out, m_norm_pre_ffn, m_norm_post_ffn, m_w_up, m_ffn_dw_w, m_ffn_dw_b, m_w_down, v_meta_tokens, v_norm_pre_mix, v_norm_post_mix, v_w_in, v_b_in, v_attn_sinks, v_w_attn_proj, v_conv_dw_w, v_conv_dw_b, v_conv_ln_g, v_conv_ln_b, v_w_conv_proj, v_b_conv_proj, v_w_out, v_norm_pre_ffn, v_norm_post_ffn, v_w_up, v_ffn_dw_w, v_ffn_dw_b, v_w_down):
    args = locals()
    w = {n: args[n] for n in NAMES}
    m = {n: args["m_" + n] for n in NAMES}
    v = {n: args["v_" + n] for n in NAMES}
    tiny_names = [part[0] for part in TINY_PARTS]
    big = list(MATMUL) + tiny_names

    def shard_2d(a, name):
        return _two_d(a).T if name == "w_in" else _two_d(a)

    own = {n: shard_2d(w[n], n).astype(BF16) for n in MATMUL}
    own["tiny"] = _tiny_pack({n: _two_d(w[n]) for n in tiny_names})
    dist = _Dist([own[n] for n, _, _, _ in SHARES])
    W = {n: _two_d(w[n]) for n in SMALL}
    W.update(dist.weights(FIRST_SHARES, _alone(dist.gather_hook(FIRST_SHARES), "gather_first")))

    loss_cols, grad_x, grads = local_step(x[0], loss_target[0], W, dist)

    small = dict(grads)
    small["loss"] = loss_cols
    g_small = _unpack_small(_small_allreduce(_pack_small(small)))
    loss = jnp.sum(g_small["loss"])
    g_big = {n: dist.reduced[n] for n in MATMUL}
    g_big.update(_tiny_unpack(dist.reduced["tiny"]))
    g_big["meta_tokens"] = lax.dynamic_slice(g_small["meta_tokens"].reshape(NMETA, D), (0, dist.chip * DQ), (NMETA, DQ))

    g, delta, new_m, new_v = {}, {}, {}, {}
    for n in big:
        shape = w[n].shape
        back = (lambda a: a.T.reshape(shape)) if n == "w_in" else (lambda a: a.reshape(shape))
        outs = _adamw(shard_2d(w[n], n), g_big[n], shard_2d(m[n], n), shard_2d(v[n], n), "adamw_" + n)
        g[n], delta[n], new_m[n], new_v[n] = (back(a) for a in (g_big[n], *outs))
    ud, um, uv = _adamw_vectors(*[[_two_d(d[n]) for n in SMALL] for d in (w, g_small, m, v)])
    for j, n in enumerate(SMALL):
        g[n], delta[n], new_m[n], new_v[n] = g_small[n], ud[j], um[j], uv[j]

    return (loss, grad_x[None], *[g[n] for n in NAMES], *[delta[n] for n in NAMES],
            *[new_m[n] for n in NAMES], *[new_v[n] for n in NAMES])
```

```python
import jax
import jax.numpy as jnp
from jax import lax
from jax.experimental import pallas as pl
from jax.experimental.pallas import tpu as pltpu

F32, BF16 = jnp.float32, jnp.bfloat16

D = 1024
NH, NKV, HD = 16, 2, 64
GH = NH // NKV
NMETA, BLK = 16, 128
PAD = BLK - NMETA
ROT = HD // 4
THETA = 500000.0
CONV_K = 31
FFN = 2816
FFN_K = 3
IN_W = 5376
QKV_W, GLU_W, GATE_W = 1280, 2048, 2048
RMS_EPS, LN_EPS, NEG = 1e-6, 1e-5, -1e30
LR, B1, B2, ADAM_EPS, WD, STEP = 0.001, 0.9, 0.999, 1e-08, 0.01, 10

VMEM_LIMIT = 56 * 2 ** 20
MESH = pl.DeviceIdType.MESH

NT_DIMS = (((1,), (1,)), ((), ()))
TN_DIMS = (((0,), (0,)), ((), ()))


def _params(sem, **kw):
    return pltpu.CompilerParams(dimension_semantics=sem, vmem_limit_bytes=VMEM_LIMIT, **kw)


def _tile(n, pref, mult=16):
    for t in range(min(pref, n), 0, -1):
        if n % t == 0 and t % mult == 0:
            return t
    return n


def _row(tr, w, col=0):
    return pl.BlockSpec((tr, w), lambda i: (i, col))


def _rrow(tr, w, nt, col=0):
    return pl.BlockSpec((tr, w), lambda t: (nt - 1 - t, col))


def _const(shape):
    return pl.BlockSpec(shape, lambda *_: (0,) * len(shape))


def _sds(shape, dt):
    return jax.ShapeDtypeStruct(shape, dt)


class _Hook:
    def __init__(self, operands, out_shape, n_sem, start, finish, mid=None):
        self.operands, self.out_shape, self.n_sem = list(operands), list(out_shape), n_sem
        self.start, self.mid, self.finish = start, mid, finish

    def scratch(self):
        return [pltpu.SemaphoreType.DMA((self.n_sem,)), pltpu.SemaphoreType.DMA((self.n_sem,))]


def _call(body, hook, steps, *, name, grid, in_specs, out_specs, out_shape, operands, semantics, scratch_shapes=()):
    in_specs, out_specs, out_shape = list(in_specs), list(out_specs), list(out_shape)
    if hook is None:
        return pl.pallas_call(body, name=name, grid=grid, in_specs=in_specs, out_specs=out_specs, out_shape=out_shape,
                              scratch_shapes=list(scratch_shapes), compiler_params=_params(semantics))(*operands)
    n_in, n_out, n_hi, n_ho = len(in_specs), len(out_specs), len(hook.operands), len(hook.out_shape)

    def wrapped(*refs):
        ins, hi = refs[:n_in], refs[n_in:n_in + n_hi]
        o0 = n_in + n_hi
        outs, ho = refs[o0:o0 + n_out], refs[o0 + n_out:o0 + n_out + n_ho]
        scratch, (ssem, rsem) = refs[o0 + n_out + n_ho:len(refs) - 2], refs[len(refs) - 2:]
        first, middle, last = steps()

        @pl.when(first)
        def _():
            hook.start(hi, ho, ssem, rsem)

        body(*ins, *outs, *scratch)
        if hook.mid is not None:
            @pl.when(middle)
            def _():
                hook.mid(hi, ho, ssem, rsem)

        @pl.when(last)
        def _():
            hook.finish(hi, ho, ssem, rsem)

    any_spec = pl.BlockSpec(memory_space=pl.ANY)
    return pl.pallas_call(
        wrapped, name=name, grid=grid, in_specs=in_specs + [any_spec] * n_hi, out_specs=out_specs + [any_spec] * n_ho,
        out_shape=out_shape + hook.out_shape, scratch_shapes=list(scratch_shapes) + hook.scratch(),
        compiler_params=_params(semantics))(*operands, *hook.operands)


def _alone(hook, name):
    n_hi = len(hook.operands)

    def body(*refs):
        hi, ho, (ssem, rsem) = refs[:n_hi], refs[n_hi:len(refs) - 2], refs[len(refs) - 2:]
        hook.start(hi, ho, ssem, rsem)
        if hook.mid is not None:
            hook.mid(hi, ho, ssem, rsem)
        hook.finish(hi, ho, ssem, rsem)

    any_spec = pl.BlockSpec(memory_space=pl.ANY)
    return pl.pallas_call(body, name=name, in_specs=[any_spec] * n_hi, out_specs=[any_spec] * len(hook.out_shape),
                          out_shape=hook.out_shape, scratch_shapes=hook.scratch())(*hook.operands)


def _steps_1d(n, mid):
    def steps():
        i = pl.program_id(0)
        return i == 0, i == min(mid, n - 1), i == n - 1
    return steps


def _rms(x, g):
    r = lax.rsqrt(jnp.mean(x * x, -1, keepdims=True) + RMS_EPS)
    return x * r * g, r


def _rms_bwd(dy, x, r, g):
    gy = dy * g
    return r * gy - x * (r * r * r) * jnp.mean(x * gy, -1, keepdims=True)


def _colsum(x):
    return jnp.sum(x, axis=0, keepdims=True)


def _rope(x, c, sa, sb):
    n = x.shape[1]
    return x * c + pltpu.roll(x, n - 8, 1) * sa + pltpu.roll(x, 8, 1) * sb


def _rope_bwd(d, c, sa, sb):
    n = d.shape[1]
    return d * c + pltpu.roll(d * sa, 8, 1) + pltpu.roll(d * sb, n - 8, 1)


def _rope_tables(R):
    half = ROT // 2
    lane = jnp.arange(2 * HD) % HD
    inv = THETA ** (-(lane % half).astype(F32) * 2.0 / ROT)
    pos = (jnp.arange(R) - PAD).astype(F32)
    ang = pos[:, None] * inv[None, :]
    cos, sin = jnp.cos(ang), jnp.sin(ang)
    c = jnp.where(lane < ROT, cos, 1.0)
    sa = jnp.where(lane < half, -sin, 0.0)
    sb = jnp.where((lane >= half) & (lane < ROT), sin, 0.0)
    return c, sa, sb


IN_CHUNKS = ([(0, 512, True), (512, 1024, True), (1024, 1152, True), (1152, 1280, False)]
             + [(c, c + 512, False) for c in range(1280, IN_W, 512)])


def _in_proj(h0, g_pre, w_in, b_in, rope, tr, hook=None):
    R = h0.shape[0]
    nt = R // tr

    def body(h_ref, g_ref, w_ref, b_ref, c_ref, sa_ref, sb_ref, qkv_ref, glu_ref, gate_ref, n1_ref):
        n, _ = _rms(h_ref[...], g_ref[...])
        nb = n.astype(BF16)
        n1_ref[...] = nb
        for c0, c1, rot in IN_CHUNKS:
            acc = lax.dot_general(nb, w_ref[c0:c1, :], NT_DIMS, preferred_element_type=F32) + b_ref[:, c0:c1]
            if rot:
                reps = (c1 - c0) // 128
                acc = _rope(acc, jnp.tile(c_ref[...], (1, reps)), jnp.tile(sa_ref[...], (1, reps)),
                            jnp.tile(sb_ref[...], (1, reps)))
            val = acc.astype(BF16)
            if c1 <= QKV_W:
                qkv_ref[:, c0:c1] = val
            elif c1 <= QKV_W + GLU_W:
                glu_ref[:, c0 - QKV_W:c1 - QKV_W] = val
            else:
                gate_ref[:, c0 - QKV_W - GLU_W:c1 - QKV_W - GLU_W] = val

    return _call(
        body, hook, _steps_1d(nt, (3 * nt) // 4), name="in_proj", grid=(nt,),
        in_specs=[_row(tr, D), _const((1, D)), _const((IN_W, D)), _const((1, IN_W)),
                  _row(tr, 128), _row(tr, 128), _row(tr, 128)],
        out_specs=[_row(tr, QKV_W), _row(tr, GLU_W), _row(tr, GATE_W), _row(tr, D)],
        out_shape=[_sds((R, QKV_W), BF16), _sds((R, GLU_W), BF16), _sds((R, GATE_W), BF16), _sds((R, D), BF16)],
        operands=(h0, g_pre, w_in, b_in, *rope), semantics=("arbitrary",))


def _attn_mask(n, keys_first=False):
    shape = (3 * BLK, BLK) if keys_first else (BLK, 3 * BLK)
    qi = lax.broadcasted_iota(jnp.int32, shape, 1 if keys_first else 0)
    kj = lax.broadcasted_iota(jnp.int32, shape, 0 if keys_first else 1)
    tq = n * BLK + qi - PAD
    t_meta = kj - PAD
    t_loc = (n - 1) * BLK + (kj - BLK) - PAD
    meta_ok = (kj < BLK) & (t_meta >= 0) & (t_meta <= tq)
    loc_ok = (kj >= BLK) & (t_loc >= NMETA) & (t_loc <= tq) & (tq - t_loc < BLK)
    return meta_ok | loc_ok


def _dup_heads(ref0, refp, refc, low, transposed=False):
    a = jnp.concatenate([ref0[...], refp[...], refc[...]], 0).astype(F32)
    sw = pltpu.roll(a, HD, 1)
    heads = [jnp.where(low, a, sw), jnp.where(low, sw, a)]
    return [(h.T if transposed else h).astype(BF16) for h in heads]


def _kv_specs(nb, rev):
    def blk(col, which):
        def idx(t):
            n = nb - 1 - t if rev else t
            return ({"meta": 0, "prev": jnp.maximum(n - 1, 0), "own": n}[which], col)
        return pl.BlockSpec((BLK, BLK), idx)
    return [blk(col, w) for col in (8, 9) for w in ("meta", "prev", "own")]


def _attn_fwd(qkv, sinks, hook=None):
    R = qkv.shape[0]
    nb = R // BLK

    def body(s_ref, q_ref, k0, kp, kc, v0, vp, vc, o_ref, lse_ref):
        n = pl.program_id(0)
        lane = lax.broadcasted_iota(jnp.int32, (1, BLK), 1)
        low = lane < HD
        kd, vd_t = _dup_heads(k0, kp, kc, low), _dup_heads(v0, vp, vc, low, transposed=True)
        mask = _attn_mask(n, keys_first=True)
        zero = jnp.zeros((), BF16)
        lses = []
        for g in range(NKV):
            tiles = []
            for pair in range(GH // 2 * g, GH // 2 * (g + 1)):
                qp = q_ref[:, pair * BLK:(pair + 1) * BLK] * jnp.asarray(HD ** -0.5, BF16)
                tiles += [jnp.where(low, qp, zero), jnp.where(low, zero, qp)]
            st = lax.dot_general(kd[g], jnp.concatenate(tiles, 0), NT_DIMS, preferred_element_type=F32)
            ps, inv = [], []
            for j in range(GH):
                s = jnp.where(mask, st[:, j * BLK:(j + 1) * BLK], NEG)
                sk = s_ref[GH * g + j]
                m = jnp.maximum(jnp.max(s, 0, keepdims=True), sk)
                p = jnp.exp(s - m)
                l = jnp.sum(p, 0, keepdims=True) + jnp.exp(sk - m)
                ps.append(p.astype(BF16))
                inv.append(1.0 / l)
                lses.append(m + jnp.log(l))
            ot = jnp.dot(vd_t[g], jnp.concatenate(ps, 1), preferred_element_type=F32)
            for j in range(GH // 2):
                pair = GH // 2 * g + j
                o = [(ot[:, h * BLK:(h + 1) * BLK] * inv[h]).T for h in (2 * j, 2 * j + 1)]
                o_ref[:, pair * BLK:(pair + 1) * BLK] = jnp.where(low, o[0], o[1]).astype(BF16)
        lse_ref[...] = jnp.concatenate(lses, 0)

    return _call(
        body, hook, _steps_1d(nb, (3 * nb) // 4), name="attn_fwd", grid=(nb,),
        in_specs=[pl.BlockSpec(memory_space=pltpu.SMEM), pl.BlockSpec((BLK, D), lambda n: (n, 0))] + _kv_specs(nb, False),
        out_specs=[_row(BLK, D), _row(NH, BLK)],
        out_shape=[_sds((R, D), BF16), _sds((nb * NH, BLK), F32)],
        operands=(sinks, qkv, *([qkv] * 6)), semantics=("arbitrary",))


CONV_TCH, CONV_SUB, HALO = 256, 64, 32


def _tap_windows(buf, r0, offset_of):
    span = CONV_SUB + HALO
    x = buf[pl.ds(r0, span), :]
    by_phase = {}
    for k in range(CONV_K):
        by_phase.setdefault(offset_of(k) % 8, []).append(k)
    for phase, taps in sorted(by_phase.items()):
        y = x if phase == 0 else pltpu.roll(x, span - phase, 0)
        for k in taps:
            d = offset_of(k) - phase
            yield k, y[d:d + CONV_SUB, :]


def _conv_fwd(glu, w, b, tr, hook=None):
    R = glu.shape[0]
    nc = D // CONV_TCH

    def body(a_ref, g_ref, w_ref, b_ref, o_ref, buf):
        i = pl.program_id(1)

        @pl.when(i == 0)
        def _():
            buf[0:HALO, :] = jnp.zeros((HALO, CONV_TCH), F32)

        @pl.when(i > 0)
        def _():
            buf[0:HALO, :] = buf[tr:tr + HALO, :]

        row = i * tr + lax.broadcasted_iota(jnp.int32, (tr, 1), 0)
        a, g = a_ref[...].astype(F32), g_ref[...].astype(F32)
        buf[HALO:HALO + tr, :] = jnp.where(row >= PAD, a * jax.nn.sigmoid(g), 0.0)
        for r0 in range(0, tr, CONV_SUB):
            acc = jnp.broadcast_to(b_ref[...], (CONV_SUB, CONV_TCH))
            for k, win in _tap_windows(buf, r0, lambda k: HALO - (CONV_K - 1) + k):
                acc = acc + w_ref[k:k + 1, :] * win
            o_ref[r0:r0 + CONV_SUB, :] = acc.astype(BF16)

    nt = R // tr

    def steps():
        c, i = pl.program_id(0), pl.program_id(1)
        return (c == 0) & (i == 0), (c == nc - 1) & (i == 0), (c == nc - 1) & (i == nt - 1)

    return _call(
        body, hook, steps, name="conv_fwd", grid=(nc, nt),
        in_specs=[pl.BlockSpec((tr, CONV_TCH), lambda c, i: (i, c)),
                  pl.BlockSpec((tr, CONV_TCH), lambda c, i: (i, nc + c)),
                  pl.BlockSpec((None, HALO, CONV_TCH), lambda c, i: (c, 0, 0)),
                  pl.BlockSpec((1, CONV_TCH), lambda c, i: (0, c))],
        out_specs=[pl.BlockSpec((tr, CONV_TCH), lambda c, i: (i, c))],
        out_shape=[_sds((R, D), BF16)],
        scratch_shapes=[pltpu.VMEM((tr + HALO, CONV_TCH), F32)],
        operands=(glu, glu, w, b), semantics=("arbitrary", "arbitrary"))


def _ln_silu(c1, lg, lb):
    mu = jnp.mean(c1, -1, keepdims=True)
    xc = c1 - mu
    rs = lax.rsqrt(jnp.mean(xc * xc, -1, keepdims=True) + LN_EPS)
    xh = xc * rs
    c2 = xh * lg + lb
    sg = jax.nn.sigmoid(c2)
    return xh, rs, c2, sg


def _mix_out(attn, c1, gates, h0, w_ap, w_cp, w_out, lg, lb, b_cp, g_post, g_ffn, tr):
    R = attn.shape[0]

    def body(at_ref, c1_ref, ga_ref, gc_ref, h0_ref, wap, wcp, wo, lg_ref, lb_ref, bcp, gp, gf,
             ao_ref, co_ref, c3_ref, mg_ref, mix_ref, h1_ref, n2_ref):
        ao = jnp.dot(at_ref[...], wap[...], preferred_element_type=F32)
        _, _, c2, sg = _ln_silu(c1_ref[...].astype(F32), lg_ref[...], lb_ref[...])
        c3 = (c2 * sg).astype(BF16)
        c3_ref[...] = c3
        co = jnp.dot(c3, wcp[...], preferred_element_type=F32) + bcp[...]
        ao_b, co_b = ao.astype(BF16), co.astype(BF16)
        ao_ref[...] = ao_b
        co_ref[...] = co_b
        merged = (jax.nn.sigmoid(ga_ref[...].astype(F32)) * ao_b.astype(F32)
                  + jax.nn.sigmoid(gc_ref[...].astype(F32)) * co_b.astype(F32)).astype(BF16)
        mg_ref[...] = merged
        mix = jnp.dot(merged, wo[...], preferred_element_type=F32).astype(BF16)
        mix_ref[...] = mix
        y, _ = _rms(mix.astype(F32), gp[...])
        h1 = h0_ref[...] + y
        h1_ref[...] = h1
        n2, _ = _rms(h1, gf[...])
        row = pl.program_id(0) * tr + lax.broadcasted_iota(jnp.int32, (tr, 1), 0)
        n2_ref[...] = jnp.where(row >= PAD, n2, 0.0).astype(BF16)

    vec = _const((1, D))
    return pl.pallas_call(
        body, name="mix_out", grid=(R // tr,),
        in_specs=[_row(tr, D), _row(tr, D), _row(tr, D, 0), _row(tr, D, 1), _row(tr, D),
                  _const((D, D)), _const((D, D)), _const((D, D)), vec, vec, vec, vec, vec],
        out_specs=[_row(tr, D)] * 7,
        out_shape=[_sds((R, D), BF16)] * 5 + [_sds((R, D), F32), _sds((R, D), BF16)],
        compiler_params=_params(("arbitrary",)),
    )(attn, c1, gates, gates, h0, w_ap, w_cp, w_out, lg, lb, b_cp, g_post, g_ffn)


FFN_CH = 256
N_CHIPS = 4
UPQ = 2 * FFN // N_CHIPS
UP_CHUNKS = [(q, c0, min(c0 + 512, UPQ)) for q in range(N_CHIPS // 2) for c0 in range(0, UPQ, 512)]
TINY_ROWS, TINY_CONV, TINY_FFN, TINY_META = 64, 0, 32, 40


def _shift_down(x, k, halo):
    tr = x.shape[0]
    row = lax.broadcasted_iota(jnp.int32, (tr, 1), 0)
    y = pltpu.roll(x, k, 0)
    for j in range(k):
        y = jnp.where(row == j, halo[8 - k + j:8 - k + j + 1, :], y)
    return y


def _shift_up(x, k, halo):
    tr = x.shape[0]
    row = lax.broadcasted_iota(jnp.int32, (tr, 1), 0)
    y = pltpu.roll(x, tr - k, 0)
    for j in range(k):
        y = jnp.where(row == tr - k + j, halo[j:j + 1, :], y)
    return y


def _conv3(x, halo, w, b):
    return w[2:3, :] * x + w[1:2, :] * _shift_down(x, 1, halo) + w[0:1, :] * _shift_down(x, 2, halo) + b


def _ffn_up(n2, w_up, fw, fb, tr):
    R = n2.shape[0]

    def body(n_ref, w_ref, fw_ref, fb_ref, up_ref, act_ref, carry):
        @pl.when(pl.program_id(0) == 0)
        def _():
            carry[...] = jnp.zeros_like(carry)

        nb = n_ref[...]
        for q, c0, c1 in UP_CHUNKS:
            us = []
            for qq in (q, q + N_CHIPS // 2):
                cs = slice(qq * UPQ + c0, qq * UPQ + c1)
                x = jnp.dot(nb, w_ref[qq, :, c0:c1], preferred_element_type=F32).astype(BF16)
                up_ref[:, cs] = x
                x = x.astype(F32)
                us.append(_conv3(x, carry[:, cs], fw_ref[qq, TINY_FFN:TINY_FFN + 8, c0:c1], fb_ref[:, cs]))
                carry[:, cs] = x[tr - 8:tr, :]
            act_ref[:, q * UPQ + c0:q * UPQ + c1] = (us[0] * jax.nn.sigmoid(us[0]) * us[1]).astype(BF16)

    return pl.pallas_call(
        body, name="ffn_up", grid=(R // tr,),
        in_specs=[_row(tr, D), _const((N_CHIPS, D, UPQ)), _const((N_CHIPS, TINY_ROWS, UPQ)), _const((1, 2 * FFN))],
        out_specs=[_row(tr, 2 * FFN), _row(tr, FFN)],
        out_shape=[_sds((R, 2 * FFN), BF16), _sds((R, FFN), BF16)],
        scratch_shapes=[pltpu.VMEM((8, 2 * FFN), F32)],
        compiler_params=_params(("arbitrary",)),
    )(n2, w_up, fw, fb)


def _ffn_down(act, w_down, h1, tgt, g_post, tr):
    R = act.shape[0]
    m = tr // BLK

    def body(a_ref, w_ref, h1_ref, g_ref, *rest):
        t_refs, (dh2_ref, dffn_ref, loss_ref, dg_ref) = rest[:m], rest[m:]

        @pl.when(pl.program_id(0) == 0)
        def _():
            loss_ref[...] = jnp.zeros_like(loss_ref)
            dg_ref[...] = jnp.zeros_like(dg_ref)

        f = jnp.dot(a_ref[...], w_ref[...], preferred_element_type=F32)
        g = g_ref[...]
        y, r = _rms(f, g)
        row = pl.program_id(0) * tr + lax.broadcasted_iota(jnp.int32, (tr, 1), 0)
        tgt_rows = jnp.concatenate([t[...] for t in t_refs], 0)
        e = jnp.where(row >= BLK, h1_ref[...] + y - tgt_rows, 0.0)
        loss_ref[...] += _colsum(e * e) * (0.5 / D)
        dy = e * (1.0 / D)
        dh2_ref[...] = dy
        dffn_ref[...] = _rms_bwd(dy, f, r, g).astype(BF16)
        dg_ref[...] += _colsum(dy * f * r)

    return pl.pallas_call(
        body, name="ffn_down", grid=(R // tr,),
        in_specs=[_row(tr, FFN), _const((FFN, D)), _row(tr, D), _const((1, D))]
                 + [pl.BlockSpec((BLK, D), lambda i, k=k: (jnp.maximum(m * i - 1 + k, 0), 0)) for k in range(m)],
        out_specs=[_row(tr, D), _row(tr, D), _const((1, D)), _const((1, D))],
        out_shape=[_sds((R, D), F32), _sds((R, D), BF16), _sds((1, D), F32), _sds((1, D), F32)],
        compiler_params=_params(("arbitrary",)),
    )(act, w_down, h1, g_post, *([tgt] * m))


def _ffn_bwd_act(dffn, w_down, up, fw, fb, tr):
    R = dffn.shape[0]
    nt = R // tr

    def body(d_ref, w_ref, up_ref, hal_ref, fw_ref, fb_ref, dup_ref, dfw_ref, dfb_ref, carry):
        t = pl.program_id(0)
        i = nt - 1 - t

        @pl.when(t == 0)
        def _():
            carry[...] = jnp.zeros_like(carry)
            dfw_ref[...] = jnp.zeros_like(dfw_ref)
            dfb_ref[...] = jnp.zeros_like(dfb_ref)

        dff = d_ref[...]
        row = i * tr + lax.broadcasted_iota(jnp.int32, (tr, 1), 0)
        first = i == 0
        for q, c0, c1 in UP_CHUNKS:
            dact = lax.dot_general(dff, w_ref[q * UPQ + c0:q * UPQ + c1, :], NT_DIMS, preferred_element_type=F32)
            chips = (q, q + N_CHIPS // 2)
            xs, us = [], []
            for qq in chips:
                cs = slice(qq * UPQ + c0, qq * UPQ + c1)
                x = up_ref[:, cs].astype(F32)
                halo = jnp.where(first, 0.0, hal_ref[:, cs].astype(F32))
                x1, x2 = _shift_down(x, 1, halo), _shift_down(x, 2, halo)
                w = fw_ref[qq, TINY_FFN:TINY_FFN + 8, c0:c1]
                us.append(w[2:3, :] * x + w[1:2, :] * x1 + w[0:1, :] * x2 + fb_ref[:, cs])
                xs.append((x, x1, x2))
            sg = jax.nn.sigmoid(us[0])
            silu = us[0] * sg
            dus = [dact * us[1] * sg * (1.0 + us[0] * (1.0 - sg)), dact * silu]
            for (x, x1, x2), du, qq in zip(xs, dus, chips):
                cs = slice(qq * UPQ + c0, qq * UPQ + c1)
                w = fw_ref[qq, TINY_FFN:TINY_FFN + 8, c0:c1]
                nxt = carry[:, cs]
                dx = w[2:3, :] * du + w[1:2, :] * _shift_up(du, 1, nxt) + w[0:1, :] * _shift_up(du, 2, nxt)
                dup_ref[:, cs] = jnp.where(row >= PAD, dx, 0.0).astype(BF16)
                dfw_ref[qq, 0:1, c0:c1] += _colsum(x2 * du)
                dfw_ref[qq, 1:2, c0:c1] += _colsum(x1 * du)
                dfw_ref[qq, 2:3, c0:c1] += _colsum(x * du)
                dfb_ref[:, cs] += _colsum(du)
                carry[:, cs] = du[0:8, :]

    halo_spec = pl.BlockSpec((8, 2 * FFN), lambda t: (jnp.maximum((nt - 1 - t) * (tr // 8) - 1, 0), 0))
    return pl.pallas_call(
        body, name="ffn_bwd_act", grid=(nt,),
        in_specs=[_rrow(tr, D, nt), _const((FFN, D)), _rrow(tr, 2 * FFN, nt), halo_spec,
                  _const((N_CHIPS, TINY_ROWS, UPQ)), _const((1, 2 * FFN))],
        out_specs=[_rrow(tr, 2 * FFN, nt), _const((N_CHIPS, 8, UPQ)), _const((1, 2 * FFN))],
        out_shape=[_sds((R, 2 * FFN), BF16), _sds((N_CHIPS, 8, UPQ), F32), _sds((1, 2 * FFN), F32)],
        scratch_shapes=[pltpu.VMEM((8, 2 * FFN), F32)],
        compiler_params=_params(("arbitrary",)),
    )(dffn, w_down, up, up, fw, fb)


def _ffn_bwd_in(dup, w_up, h1, dh2, mix, g_ffn, g_post, tr, hook=None):
    R = dup.shape[0]

    def body(d_ref, w_ref, h1_ref, dh2_ref, mix_ref, gf_ref, gp_ref, dh1_ref, dmix_ref, dgf_ref, dgp_ref):
        @pl.when(pl.program_id(0) == 0)
        def _():
            dgf_ref[...] = jnp.zeros_like(dgf_ref)
            dgp_ref[...] = jnp.zeros_like(dgp_ref)

        dn2 = sum(lax.dot_general(d_ref[:, q * UPQ:(q + 1) * UPQ], w_ref[q], NT_DIMS, preferred_element_type=F32)
                  for q in range(N_CHIPS))
        h1 = h1_ref[...]
        _, r2 = _rms(h1, gf_ref[...])
        dh1 = dh2_ref[...] + _rms_bwd(dn2, h1, r2, gf_ref[...])
        dgf_ref[...] += _colsum(dn2 * h1 * r2)
        dh1_ref[...] = dh1
        m = mix_ref[...].astype(F32)
        _, rm = _rms(m, gp_ref[...])
        dmix_ref[...] = _rms_bwd(dh1, m, rm, gp_ref[...]).astype(BF16)
        dgp_ref[...] += _colsum(dh1 * m * rm)

    vec = _const((1, D))
    return _call(
        body, hook, _steps_1d(R // tr, R // tr), name="ffn_bwd_in", grid=(R // tr,),
        in_specs=[_row(tr, 2 * FFN), _const((N_CHIPS, D, UPQ)), _row(tr, D), _row(tr, D), _row(tr, D), vec, vec],
        out_specs=[_row(tr, D), _row(tr, D), vec, vec],
        out_shape=[_sds((R, D), F32), _sds((R, D), BF16), _sds((1, D), F32), _sds((1, D), F32)],
        operands=(dup, w_up, h1, dh2, mix, g_ffn, g_post), semantics=("arbitrary",))


def _mix_bwd(dmix, ao, co, gates, c1, w_out, w_ap, w_cp, lg, lb, tr):
    R = dmix.shape[0]

    def body(dm_ref, ao_ref, co_ref, ga_ref, gc_ref, c1_ref, wo, wap, wcp, lg_ref, lb_ref,
             dao_ref, dco_ref, dgate_ref, dattn_ref, dc1_ref, dbcp_ref, dlg_ref, dlb_ref, dcb_ref):
        @pl.when(pl.program_id(0) == 0)
        def _():
            for ref in (dbcp_ref, dlg_ref, dlb_ref, dcb_ref):
                ref[...] = jnp.zeros_like(ref)

        dmg = lax.dot_general(dm_ref[...], wo[...], NT_DIMS, preferred_element_type=F32)
        sa = jax.nn.sigmoid(ga_ref[...].astype(F32))
        sc = jax.nn.sigmoid(gc_ref[...].astype(F32))
        dao = (dmg * sa).astype(BF16)
        dco = (dmg * sc).astype(BF16)
        dao_ref[...] = dao
        dco_ref[...] = dco
        dgate_ref[:, 0:D] = (dmg * ao_ref[...].astype(F32) * sa * (1.0 - sa)).astype(BF16)
        dgate_ref[:, D:2 * D] = (dmg * co_ref[...].astype(F32) * sc * (1.0 - sc)).astype(BF16)
        dbcp_ref[...] += _colsum(dco.astype(F32))
        dattn_ref[...] = lax.dot_general(dao, wap[...], NT_DIMS, preferred_element_type=F32).astype(BF16)
        dc3 = lax.dot_general(dco, wcp[...], NT_DIMS, preferred_element_type=F32)
        xh, rs, c2, sg = _ln_silu(c1_ref[...].astype(F32), lg_ref[...], lb_ref[...])
        dc2 = dc3 * sg * (1.0 + c2 * (1.0 - sg))
        dlg_ref[...] += _colsum(dc2 * xh)
        dlb_ref[...] += _colsum(dc2)
        dxh = dc2 * lg_ref[...]
        dc1 = rs * (dxh - jnp.mean(dxh, -1, keepdims=True) - xh * jnp.mean(dxh * xh, -1, keepdims=True))
        dc1_ref[...] = dc1
        dcb_ref[...] += _colsum(dc1)

    vec = _const((1, D))
    return pl.pallas_call(
        body, name="mix_bwd", grid=(R // tr,),
        in_specs=[_row(tr, D), _row(tr, D), _row(tr, D), _row(tr, D, 0), _row(tr, D, 1), _row(tr, D),
                  _const((D, D)), _const((D, D)), _const((D, D)), vec, vec],
        out_specs=[_row(tr, D), _row(tr, D), _row(tr, 2 * D), _row(tr, D), _row(tr, D), vec, vec, vec, vec],
        out_shape=[_sds((R, D), BF16), _sds((R, D), BF16), _sds((R, 2 * D), BF16), _sds((R, D), BF16),
                   _sds((R, D), F32)] + [_sds((1, D), F32)] * 4,
        compiler_params=_params(("arbitrary",)),
    )(dmix, ao, co, gates, gates, c1, w_out, w_ap, w_cp, lg, lb)


def _conv_bwd(dc1, glu, w, tr, hook=None):
    R = dc1.shape[0]
    nt, nc = R // tr, D // CONV_TCH

    def body(d_ref, a_ref, g_ref, w_ref, dglu_a, dglu_g, dw_ref, buf, dw_acc):
        t = pl.program_id(1)
        i = nt - 1 - t

        @pl.when(t == 0)
        def _():
            buf[tr:tr + HALO, :] = jnp.zeros((HALO, CONV_TCH), F32)
            dw_acc[...] = jnp.zeros_like(dw_acc)

        @pl.when(t > 0)
        def _():
            buf[tr:tr + HALO, :] = buf[0:HALO, :]

        buf[0:tr, :] = d_ref[...]
        for r0 in range(0, tr, CONV_SUB):
            rs = slice(r0, r0 + CONV_SUB)
            row = i * tr + r0 + lax.broadcasted_iota(jnp.int32, (CONV_SUB, 1), 0)
            a, g = a_ref[rs, :].astype(F32), g_ref[rs, :].astype(F32)
            sg = jax.nn.sigmoid(g)
            glu = jnp.where(row >= PAD, a * sg, 0.0)
            acc = jnp.zeros((CONV_SUB, CONV_TCH), F32)
            for k, win in _tap_windows(buf, r0, lambda k: CONV_K - 1 - k):
                acc = acc + w_ref[k:k + 1, :] * win
                dw_acc[k] += jnp.sum((glu * win).reshape(CONV_SUB // 8, 8, CONV_TCH), axis=0)
            dglu = jnp.where(row >= PAD, acc, 0.0)
            dglu_a[rs, :] = (dglu * sg).astype(BF16)
            dglu_g[rs, :] = (dglu * a * sg * (1.0 - sg)).astype(BF16)

        @pl.when(t == nt - 1)
        def _():
            dw_ref[...] = jnp.sum(dw_acc[...], axis=1)

    def rspec(col0):
        return pl.BlockSpec((tr, CONV_TCH), lambda c, t: (nt - 1 - t, col0 + c))

    def steps():
        c, t = pl.program_id(0), pl.program_id(1)
        return (c == 0) & (t == 0), False, (c == nc - 1) & (t == nt - 1)

    return _call(
        body, hook, steps, name="conv_bwd", grid=(nc, nt),
        in_specs=[rspec(0), rspec(0), rspec(nc), pl.BlockSpec((None, HALO, CONV_TCH), lambda c, t: (c, 0, 0))],
        out_specs=[rspec(0), rspec(0), pl.BlockSpec((None, HALO, CONV_TCH), lambda c, t: (c, 0, 0))],
        out_shape=[_sds((R, D), BF16), _sds((R, D), BF16), _sds((N_CHIPS, HALO, CONV_TCH), F32)],
        scratch_shapes=[pltpu.VMEM((tr + HALO, CONV_TCH), F32), pltpu.VMEM((HALO, 8, CONV_TCH), F32)],
        operands=(dc1, glu, glu, w), semantics=("arbitrary", "arbitrary"))


def _attn_bwd(qkv, do, lse, sinks, rope, hook=None):
    R = qkv.shape[0]
    nb = R // BLK

    def body(s_ref, q_ref, k0, kp, kc, v0, vp, vc, do_ref, lse_ref, c_ref, sa_ref, sb_ref,
             dqkv_ref, dsink_ref, car_k, car_v, met_k, met_v):
        t = pl.program_id(0)
        n = nb - 1 - t

        @pl.when(t == 0)
        def _():
            for ref in (car_k, car_v, met_k, met_v, dsink_ref):
                ref[...] = jnp.zeros_like(ref)

        lane = lax.broadcasted_iota(jnp.int32, (1, BLK), 1)
        low = lane < HD
        kd, vd = _dup_heads(k0, kp, kc, low), _dup_heads(v0, vp, vc, low)
        kd_t = _dup_heads(k0, kp, kc, low, transposed=True)
        mask = _attn_mask(n, keys_first=True)
        tabs = (c_ref[...], sa_ref[...], sb_ref[...])
        zero = jnp.zeros((), BF16)
        dk_acc, dv_acc = [], []
        dsink = jnp.zeros((1, BLK), F32)
        for g in range(NKV):
            q_tiles, do_tiles = [], []
            for pair in range(GH // 2 * g, GH // 2 * (g + 1)):
                cs = slice(pair * BLK, (pair + 1) * BLK)
                qp, dop = q_ref[:, cs] * jnp.asarray(HD ** -0.5, BF16), do_ref[:, cs]
                q_tiles += [jnp.where(low, qp, zero), jnp.where(low, zero, qp)]
                do_tiles += [jnp.where(low, dop, zero), jnp.where(low, zero, dop)]
            qs, dos = jnp.concatenate(q_tiles, 0), jnp.concatenate(do_tiles, 0)
            st = lax.dot_general(kd[g], qs, NT_DIMS, preferred_element_type=F32)
            dpt = lax.dot_general(vd[g], dos, NT_DIMS, preferred_element_type=F32)
            ps, dss = [], []
            for j in range(GH):
                h = GH * g + j
                cs = slice(j * BLK, (j + 1) * BLK)
                lse_h = lse_ref[h:h + 1, :]
                p = jnp.where(mask, jnp.exp(st[:, cs] - lse_h), 0.0)
                dp = dpt[:, cs]
                delta = jnp.sum(p * dp, 0, keepdims=True)
                ps.append(p.astype(BF16))
                dss.append((p * (dp - delta)).astype(BF16))
                dsink = dsink + jnp.where(lane == h, -jnp.sum(jnp.exp(s_ref[h] - lse_h) * delta), 0.0)
            ds_t, p_t = jnp.concatenate(dss, 1), jnp.concatenate(ps, 1)
            dk_acc.append(jnp.dot(ds_t, qs, preferred_element_type=F32))
            dv_acc.append(jnp.dot(p_t, dos, preferred_element_type=F32))
            dq_t = jnp.dot(kd_t[g], ds_t, preferred_element_type=F32) * (HD ** -0.5)
            for j in range(GH // 2):
                pair = GH // 2 * g + j
                dq = [dq_t[:, h * BLK:(h + 1) * BLK].T for h in (2 * j, 2 * j + 1)]
                dqkv_ref[:, pair * BLK:(pair + 1) * BLK] = _rope_bwd(jnp.where(low, dq[0], dq[1]), *tabs).astype(BF16)
        dsink_ref[0:1, :] += dsink

        def fold(acc):
            tot = [a + pltpu.roll(a, HD, 1) for a in acc]
            return jnp.where(low, tot[0], tot[1])

        dk_all, dv_all = fold(dk_acc), fold(dv_acc)
        met_k[...] += dk_all[0:BLK, :]
        met_v[...] += dv_all[0:BLK, :]
        last = jnp.where(n == 0, 1.0, 0.0)
        dk_n = dk_all[2 * BLK:3 * BLK, :] + car_k[...] + last * met_k[...]
        dv_n = dv_all[2 * BLK:3 * BLK, :] + car_v[...] + last * met_v[...]
        dqkv_ref[:, D:D + BLK] = _rope_bwd(dk_n, *tabs).astype(BF16)
        dqkv_ref[:, D + BLK:QKV_W] = dv_n.astype(BF16)
        car_k[...] = dk_all[BLK:2 * BLK, :]
        car_v[...] = dv_all[BLK:2 * BLK, :]

    rblk = lambda w: pl.BlockSpec((BLK, w), lambda t: (nb - 1 - t, 0))
    return _call(
        body, hook, _steps_1d(nb, nb), name="attn_bwd", grid=(nb,),
        in_specs=[pl.BlockSpec(memory_space=pltpu.SMEM), rblk(D)] + _kv_specs(nb, True)
                 + [rblk(D), pl.BlockSpec((NH, BLK), lambda t: (nb - 1 - t, 0)), rblk(BLK), rblk(BLK), rblk(BLK)],
        out_specs=[rblk(QKV_W), _const((8, BLK))],
        out_shape=[_sds((R, QKV_W), BF16), _sds((8, BLK), F32)],
        scratch_shapes=[pltpu.VMEM((BLK, BLK), F32)] * 4,
        operands=(sinks, qkv, *([qkv] * 6), do, lse, *rope), semantics=("arbitrary",))


def _in_bwd(dproj, w_in, h0, dh1, g_pre, tr, hook=None):
    R = h0.shape[0]
    n = len(dproj)
    widths = [p.shape[1] for p in dproj]
    starts = [sum(widths[:j]) for j in range(n)]

    def body(*refs):
        d_refs, (w_ref, h0_ref, dh1_ref, g_ref, dh0_ref, dg_ref, db_ref) = refs[:n], refs[n:]

        @pl.when(pl.program_id(0) == 0)
        def _():
            dg_ref[...] = jnp.zeros_like(dg_ref)
            db_ref[...] = jnp.zeros_like(db_ref)

        dn1 = jnp.zeros((tr, D), F32)
        for d_ref, c0, wd in zip(d_refs, starts, widths):
            d = d_ref[...]
            dn1 = dn1 + jnp.dot(d, w_ref[c0:c0 + wd, :], preferred_element_type=F32)
            db_ref[:, c0:c0 + wd] += _colsum(d.astype(F32))
        h0 = h0_ref[...]
        _, r = _rms(h0, g_ref[...])
        dh0_ref[...] = dh1_ref[...] + _rms_bwd(dn1, h0, r, g_ref[...])
        dg_ref[...] += _colsum(dn1 * h0 * r)

    return _call(
        body, hook, _steps_1d(R // tr, R // tr), name="in_bwd", grid=(R // tr,),
        in_specs=[_row(tr, wd) for wd in widths] + [_const((IN_W, D)), _row(tr, D), _row(tr, D), _const((1, D))],
        out_specs=[_row(tr, D), _const((1, D)), _const((1, IN_W))],
        out_shape=[_sds((R, D), F32), _sds((1, D), F32), _sds((1, IN_W), F32)],
        operands=(*dproj, w_in, h0, dh1, g_pre), semantics=("arbitrary",))


def _dw(a, b, name, tn, tr, by_chip=False, ta=None, rows_of=None, row0=0, into=None):
    R, ka = a.shape
    n = b.shape[1]
    nt = R // tr
    ta = ta or ka
    k0 = 0
    if by_chip:
        out_spec = pl.BlockSpec((None, ta, tn), lambda k, j, i: (j, k, 0))
        out_shape = _sds((n // tn, ka, tn), BF16)
    else:
        if rows_of is not None:
            k0 = row0 // ta
        out_spec = pl.BlockSpec((ta, tn), lambda k, j, i: (k0 + k, j))
        out_shape = _sds((ka if rows_of is None else rows_of, n), BF16)
    extra = [] if into is None else [into]

    def body(a_ref, b_ref, *rest):
        o_ref, acc = rest[len(extra):]
        i = pl.program_id(2)

        @pl.when(i == 0)
        def _():
            acc[...] = jnp.zeros_like(acc)

        acc[...] += lax.dot_general(a_ref[...], b_ref[...], TN_DIMS, preferred_element_type=F32)

        @pl.when(i == nt - 1)
        def _():
            o_ref[...] = acc[...].astype(BF16)

    return pl.pallas_call(
        body, name=name, grid=(ka // ta, n // tn, nt),
        in_specs=[pl.BlockSpec((tr, ta), lambda k, j, i: (i, k)), pl.BlockSpec((tr, tn), lambda k, j, i: (i, j))]
                 + [pl.BlockSpec(memory_space=pl.ANY)] * len(extra),
        out_specs=out_spec, out_shape=out_shape,
        input_output_aliases={2: 0} if extra else {},
        scratch_shapes=[pltpu.VMEM((ta, tn), F32)],
        compiler_params=_params(("arbitrary", "arbitrary", "arbitrary")),
    )(a, b, *extra)


SMALL = ["norm_pre_mix", "norm_post_mix", "b_in", "attn_sinks", "conv_dw_b", "conv_ln_g", "conv_ln_b",
         "b_conv_proj", "norm_pre_ffn", "norm_post_ffn", "ffn_dw_b"]


def local_step(x, tgt, W, dist=None):
    W = dict(W)
    S = x.shape[0]
    R = S + BLK
    tr = _tile(R, 384, BLK)
    trw = _tile(R, 1056)
    rope = _rope_tables(R)
    meta = _cols_joined(W["tiny"][:, TINY_META:TINY_META + NMETA, 0:DQ])
    h0 = jnp.concatenate([jnp.zeros((PAD, D), F32), meta, x], 0)

    qkv, glu, gates, n1, *got = _in_proj(h0, W["norm_pre_mix"], W["w_in"], W["b_in"], rope, tr,
                                         dist and dist.gather_hook(GATHER_IN_PROJ))
    if dist:
        W.update(dist.weights(GATHER_IN_PROJ, got))
    sinks = W["attn_sinks"].reshape(NH)
    attn, lse, *got = _attn_fwd(qkv, sinks, dist and dist.gather_hook(GATHER_ATTN))
    if dist:
        W.update(dist.weights(GATHER_ATTN, got))
    c1, *got = _conv_fwd(glu, W["tiny"], W["conv_dw_b"], tr, dist and dist.gather_hook(GATHER_CONV))
    if dist:
        W.update(dist.weights(GATHER_CONV, got))
    ao, co, c3, merged, mix, h1, n2 = _mix_out(
        attn, c1, gates, h0, W["w_attn_proj"], W["w_conv_proj"], W["w_out"], W["conv_ln_g"], W["conv_ln_b"],
        W["b_conv_proj"], W["norm_post_mix"], W["norm_pre_ffn"], tr)
    up, act = _ffn_up(n2, W["w_up"], W["tiny"], W["ffn_dw_b"], tr)
    dh2, dffn, loss_cols, dg_post_ffn = _ffn_down(act, W["w_down"], h1, tgt, W["norm_post_ffn"], tr)

    dw_down = _dw(act, dffn, "dw_down", 512, trw)
    dup, dfw, dfb = _ffn_bwd_act(dffn, W["w_down"], up, W["tiny"], W["ffn_dw_b"], tr)
    dw_up = _dw(n2, dup, "dw_up", UPQ, trw, by_chip=True)
    ffn_parts = [dw_up, dw_down.reshape(N_CHIPS, -1, D)]
    dh1, dmix, dg_pre_ffn, dg_post_mix, *got = _ffn_bwd_in(dup, W["w_up"], h1, dh2, mix, W["norm_pre_ffn"],
                                                            W["norm_post_mix"], tr,
                                                            dist and dist.swap_hook(FFN_SHARES, ffn_parts))
    ffn_sums = dist and dist.pair_sums(FFN_SHARES, ffn_parts, "ffn", swapped=got)
    dao, dco, dgates, dattn, dc1, db_cp, dlg, dlb, dcb = _mix_bwd(
        dmix, ao, co, gates, c1, W["w_out"], W["w_attn_proj"], W["w_conv_proj"], W["conv_ln_g"], W["conv_ln_b"], tr)
    dglu_a, dglu_g, dcw, *ffn_got = _conv_bwd(dc1, glu, W["tiny"], tr, dist and dist.chip_hook(FFN_SHARES, ffn_sums))
    branch = [_dw(attn, dao, "dw_attn_proj", D, trw), _dw(c3, dco, "dw_conv_proj", D, trw),
              _dw(merged, dmix, "dw_out", D, trw),
              _tiny_pack({"conv_dw_w": dcw, "ffn_dw_w": dfw, "meta_tokens": jnp.zeros((N_CHIPS, NMETA, DQ), F32)})]
    branch_sums = dist and dist.pair_sums(BRANCH_TINY_SHARES, [a.reshape(N_CHIPS, -1, a.shape[-1]) for a in branch], "branch")
    dqkv, dsink, *got = _attn_bwd(qkv, dattn, lse, sinks, rope,
                                  dist and dist.chip_hook(BRANCH_TINY_SHARES, branch_sums))
    if dist:
        dist.finish(FFN_SHARES + BRANCH_TINY_SHARES, ffn_sums + branch_sums, ffn_got + got, "ffn_branch")
    dproj = [dqkv, dglu_a, dglu_g, dgates]
    dw_in, row0 = None, 0
    for j, p in enumerate(dproj):
        ta = _tile(p.shape[1], D, BLK) if j == 0 else 2 * BLK
        dw_in = _dw(p, n1, "dw_in_%d" % j, D, _tile(R, 2 * trw), ta=ta, rows_of=IN_W, row0=row0, into=dw_in)
        row0 += p.shape[1]
    in_sums = dist and dist.pair_sums(IN_SHARES, [dw_in.reshape(N_CHIPS, -1, D)], "in")
    dh0, dg_pre_mix, db_in, *got = _in_bwd(dproj, W["w_in"], h0, dh1, W["norm_pre_mix"], tr,
                                           dist and dist.chip_hook(IN_SHARES, in_sums))
    if dist:
        dist.finish(IN_SHARES, in_sums, got, "in")

    grads = {
        "w_in": dw_in, "w_attn_proj": branch[0], "w_conv_proj": branch[1], "w_out": branch[2],
        "w_up": dw_up,
        "w_down": dw_down,
        "tiny": branch[3],
        "meta_tokens": dh0[PAD:BLK],
        "norm_pre_mix": dg_pre_mix, "norm_post_mix": dg_post_mix, "b_in": db_in,
        "attn_sinks": dsink[0:1, 0:NH], "conv_dw_b": dcb, "conv_ln_g": dlg, "conv_ln_b": dlb,
        "b_conv_proj": db_cp, "norm_pre_ffn": dg_pre_ffn, "norm_post_ffn": dg_post_ffn, "ffn_dw_b": dfb,
    }
    return loss_cols, dh0[BLK:], grads


INQ = IN_W // N_CHIPS
DQ = D // N_CHIPS
SHARES = [("w_in", INQ, D, BF16), ("w_attn_proj", DQ, D, BF16), ("w_conv_proj", DQ, D, BF16), ("w_out", DQ, D, BF16),
          ("w_up", D, UPQ, BF16), ("w_down", FFN // N_CHIPS, D, BF16), ("tiny", TINY_ROWS, UPQ, F32)]
TINY_PARTS = [("conv_dw_w", TINY_CONV, CONV_K, TINY_FFN - TINY_CONV, DQ), ("ffn_dw_w", TINY_FFN, FFN_K, TINY_META - TINY_FFN, UPQ),
              ("meta_tokens", TINY_META, NMETA, NMETA, DQ)]


def _tiny_pack(parts):
    rows = []
    for name, _, _, reserved, _ in TINY_PARTS:
        a = parts[name].astype(F32)
        pad = [(0, 0)] * (a.ndim - 2) + [(0, reserved - a.shape[-2]), (0, UPQ - a.shape[-1])]
        rows.append(jnp.pad(a, pad))
    used = sum(r.shape[-2] for r in rows)
    rows.append(jnp.zeros(rows[0].shape[:-2] + (TINY_ROWS - used, UPQ), F32))
    return jnp.concatenate(rows, axis=-2)


def _tiny_unpack(tiny):
    return {name: tiny[..., r0:r0 + k, 0:cols] for name, r0, k, _, cols in TINY_PARTS}


def _cols_by_chip(a):
    rows, n = a.shape
    return a.reshape(rows, N_CHIPS, n // N_CHIPS).transpose(1, 0, 2)


def _cols_joined(a):
    _, rows, cols = a.shape
    return a.transpose(1, 0, 2).reshape(rows, N_CHIPS * cols)


def _to_planes(a, rows):
    return jnp.pad(a, [(0, rows * D - a.shape[-1])]).reshape(rows, D)


ANY = pl.BlockSpec(memory_space=pl.ANY)


def _place():
    x, y, c = lax.axis_index("x"), lax.axis_index("y"), lax.axis_index("c")
    chips = [(1 - x, y), (x, 1 - y), (1 - x, 1 - y)]
    return x, y, c, chips


def _rcopy(src, dst, ssem, rsem, to):
    return pltpu.make_async_remote_copy(src_ref=src, dst_ref=dst, send_sem=ssem, recv_sem=rsem,
                                        device_id=to, device_id_type=MESH)


def _halves(ref_or_rows, c):
    half = ref_or_rows // 2
    return pl.ds(c * half, half), pl.ds((1 - c) * half, half)


FIRST_SHARES, BRANCH_SHARES, FFN_SHARES = [0, 6], [1, 2, 3], [4, 5]
IN_SHARES, BRANCH_TINY_SHARES = [0], [1, 2, 3, 6]
GATHER_IN_PROJ, GATHER_ATTN, GATHER_CONV = [1, 2, 3], [4], [5]


def _gather_hook(own, idx):
    n = len(idx)

    def copies(kind, ins, outs, ssem, rsem):
        x, y, c, chips = _place()
        q = 2 * x + y
        sib = (x, y, 1 - c)
        out = []
        for i, a in enumerate(idx):
            mine, other = _halves(SHARES[a][1], c)
            for j, (cx, cy) in enumerate(chips):
                k, to = 3 * i + j, (cx, cy, c)
                landed, theirs = outs[i].at[2 * cx + cy, mine], outs[i].at[2 * cx + cy, other]
                if kind == "send":
                    out.append(_rcopy(ins[i].at[mine], outs[i].at[q, mine], ssem.at[k], rsem.at[k], to))
                elif kind == "landing":
                    out.append(_rcopy(ins[i].at[mine], landed, ssem.at[k], rsem.at[k], to))
                elif kind == "pass":
                    out.append(_rcopy(landed, landed, ssem.at[3 * n + k], rsem.at[3 * n + k], sib))
                else:
                    out.append(_rcopy(theirs, theirs, ssem.at[3 * n + k], rsem.at[3 * n + k], sib))
        return out

    def own_copies(ins, outs, ssem, rsem):
        x, y, c, _ = _place()
        q = 2 * x + y
        return [_rcopy(ins[i], outs[i].at[q], ssem.at[6 * n + i], rsem.at[6 * n + i], (x, y, 1 - c)) for i in range(n)]

    def start(*refs):
        for cp in copies("send", *refs) + own_copies(*refs):
            cp.start()

    def mid(*refs):
        for landed, cp in zip(copies("landing", *refs), copies("pass", *refs)):
            landed.wait_recv()
            cp.start()

    def finish(*refs):
        for cp in copies("arrival", *refs):
            cp.wait_recv()
        for cp in copies("send", *refs) + copies("pass", *refs):
            cp.wait_send()
        for cp in own_copies(*refs):
            cp.wait()

    shapes = [_sds((N_CHIPS,) + SHARES[a][1:3], SHARES[a][3]) for a in idx]
    return _Hook(own, shapes, 7 * n, start, finish, mid)


def _chip_hook(sums, idx):
    def copies(ins, outs, ssem, rsem):
        x, y, c, chips = _place()
        return [_rcopy(ins[i].at[2 * cx + cy], outs[i].at[j], ssem.at[3 * i + j], rsem.at[3 * i + j], (cx, cy, c))
                for i in range(len(idx)) for j, (cx, cy) in enumerate(chips)]

    def start(*refs):
        for cp in copies(*refs):
            cp.start()

    def finish(*refs):
        for cp in copies(*refs):
            cp.wait()

    shapes = [_sds((N_CHIPS - 1, SHARES[a][1] // 2, SHARES[a][2]), SHARES[a][3]) for a in idx]
    return _Hook(sums, shapes, 3 * len(idx), start, finish)


def _swap_hook(parts, idx):
    def copies(ins, outs, ssem, rsem):
        x, y, c, _ = _place()
        return [_rcopy(ins[i].at[:, _halves(SHARES[a][1], c)[1]], outs[i], ssem.at[i], rsem.at[i], (x, y, 1 - c))
                for i, a in enumerate(idx)]

    def start(*refs):
        for cp in copies(*refs):
            cp.start()

    def finish(*refs):
        for cp in copies(*refs):
            cp.wait()

    shapes = [_sds((N_CHIPS, SHARES[a][1] // 2, SHARES[a][2]), SHARES[a][3]) for a in idx]
    return _Hook(parts, shapes, len(idx), start, finish)


def _sum_pair(parts, recvs, c, idx, tag):
    steps, n = 2, len(idx)

    def body(c_ref, *refs):
        for i, a in enumerate(idx):
            refs[2 * n + i][...] = (refs[i][...].astype(F32) + refs[n + i][...].astype(F32)).astype(SHARES[a][3])

    own, got, out, views, shapes = [], [], [], [], []
    for p, a in zip(parts, idx):
        _, rows, cols, dt = SHARES[a]
        blk = rows // 2 // steps
        own.append(pl.BlockSpec((None, None, blk, cols), lambda q, i, c_ref: (q, c_ref[0], i, 0)))
        got.append(pl.BlockSpec((None, blk, cols), lambda q, i, c_ref: (q, i, 0)))
        out.append(pl.BlockSpec((None, blk, cols), lambda q, i, c_ref: (q, i, 0)))
        views.append(p.reshape(N_CHIPS, 2, rows // 2, cols))
        shapes.append(_sds((N_CHIPS, rows // 2, cols), dt))
    grid_spec = pltpu.PrefetchScalarGridSpec(num_scalar_prefetch=1, grid=(N_CHIPS, steps),
                                             in_specs=own + got, out_specs=out)
    return pl.pallas_call(body, name="sum_pair_" + tag, grid_spec=grid_spec, out_shape=shapes,
                          compiler_params=_params(("arbitrary", "arbitrary")))(c, *views, *recvs)


def _sum_chips(sums, recvs, qc, idx, tag):
    steps, n = 2, len(idx)

    def body(qc_ref, *refs):
        for i in range(n):
            acc = refs[i][...].astype(F32)
            for j in range(1, N_CHIPS):
                acc = acc + refs[j * n + i][...].astype(F32)
            refs[N_CHIPS * n + i][...] = acc

    own, got, out, shapes = [], [[], [], []], [], []
    for a in idx:
        _, rows, cols, _ = SHARES[a]
        blk = rows // 2 // steps
        own.append(pl.BlockSpec((None, blk, cols), lambda i, qc_ref: (qc_ref[0], i, 0)))
        for j in range(N_CHIPS - 1):
            got[j].append(pl.BlockSpec((None, blk, cols), lambda i, qc_ref, j=j: (j, i, 0)))
        out.append(pl.BlockSpec((None, blk, cols), lambda i, qc_ref: (qc_ref[1], i, 0)))
        shapes.append(_sds((2, rows // 2, cols), F32))
    grid_spec = pltpu.PrefetchScalarGridSpec(num_scalar_prefetch=1, grid=(steps,),
                                             in_specs=own + got[0] + got[1] + got[2], out_specs=out)
    return pl.pallas_call(body, name="sum_chips_" + tag, grid_spec=grid_spec, out_shape=shapes,
                          compiler_params=_params(("arbitrary",)))(qc, *sums, *recvs, *recvs, *recvs)


def _sibling_share(halves, idx, tag):
    n = len(idx)

    def body(*refs):
        outs, (ssem, rsem) = refs[n:2 * n], refs[2 * n:]
        x, y, c, _ = _place()
        copies = []
        for i in range(n):
            cp = _rcopy(outs[i].at[c], outs[i].at[c], ssem.at[i], rsem.at[i], (x, y, 1 - c))
            cp.start()
            copies.append(cp)
        for i in range(n):
            theirs = outs[i].at[1 - c]
            _rcopy(theirs, theirs, ssem.at[i], rsem.at[i], (x, y, 1 - c)).wait_recv()
        for cp in copies:
            cp.wait_send()

    return pl.pallas_call(
        body, name="sibling_share_" + tag, in_specs=[ANY] * n, out_specs=[ANY] * n,
        out_shape=[_sds((2, SHARES[a][1] // 2, SHARES[a][2]), F32) for a in idx],
        input_output_aliases={i: i for i in range(n)},
        scratch_shapes=[pltpu.SemaphoreType.DMA((n,)), pltpu.SemaphoreType.DMA((n,))],
    )(*halves)


class _Dist:
    def __init__(self, own):
        self.own = own
        self.core = lax.axis_index("c")
        self.chip = 2 * lax.axis_index("x") + lax.axis_index("y")
        self.reduced = {}

    def gather_hook(self, idx):
        return _gather_hook([self.own[a] for a in idx], idx)

    def weights(self, idx, gathered):
        out = {}
        for a, full in zip(idx, gathered):
            name = SHARES[a][0]
            out[name] = full if name in ("w_up", "tiny") else full.reshape(-1, D)
        return out

    def pair_sums(self, idx, parts, tag, swapped=None):
        if swapped is None:
            swapped = _alone(_swap_hook(parts, idx), "sibling_swap_" + tag)
        return _sum_pair(parts, swapped, self.core.reshape(1), idx, tag)

    def swap_hook(self, idx, parts):
        return _swap_hook(parts, idx)

    def chip_hook(self, idx, sums):
        return _chip_hook(sums, idx)

    def finish(self, idx, sums, recvs, tag):
        halves = _sum_chips(sums, recvs, jnp.stack([self.chip, self.core]), idx, tag)
        for a, full in zip(idx, _sibling_share(halves, idx, tag)):
            self.reduced[SHARES[a][0]] = full.reshape(SHARES[a][1:3])


N_DEV = 8
SMALL_ROWS = 40


def _small_allreduce(sm):
    def body(s_ref, o_ref, buf, ssem, rsem):
        x, y, c, _ = _place()
        me = 4 * x + 2 * y + c
        buf[me] = s_ref[...]
        copies = []
        for d in range(1, N_DEV):
            dx, dy, dc = d >> 2, (d >> 1) & 1, d & 1
            to = (x ^ dx, y ^ dy, c ^ dc)
            cp = _rcopy(s_ref, buf.at[me], ssem.at[d - 1], rsem.at[d - 1], to)
            cp.start()
            copies.append(cp)
        for d in range(1, N_DEV):
            src = me ^ d
            _rcopy(s_ref, buf.at[src], ssem.at[d - 1], rsem.at[d - 1], (x, y, c)).wait_recv()
        for cp in copies:
            cp.wait_send()
        acc = buf[0]
        for k in range(1, N_DEV):
            acc = acc + buf[k]
        o_ref[...] = acc

    vm = pl.BlockSpec(memory_space=pltpu.VMEM)
    return pl.pallas_call(
        body, name="small_allreduce", in_specs=[vm], out_specs=vm,
        out_shape=_sds((SMALL_ROWS, D), F32),
        scratch_shapes=[pltpu.VMEM((N_DEV, SMALL_ROWS, D), F32),
                        pltpu.SemaphoreType.DMA((N_DEV - 1,)), pltpu.SemaphoreType.DMA((N_DEV - 1,))],
    )(sm)


SMALL_PLAN = [("norm_pre_mix", D), ("norm_post_mix", D), ("b_in", IN_W), ("attn_sinks", NH), ("conv_dw_b", D),
              ("conv_ln_g", D), ("conv_ln_b", D), ("b_conv_proj", D), ("norm_pre_ffn", D), ("norm_post_ffn", D),
              ("ffn_dw_b", 2 * FFN), ("loss", D), ("meta_tokens", NMETA * D)]


def _pack_small(parts):
    rows = [_to_planes(parts[name].reshape(-1), -(-n // D)) for name, n in SMALL_PLAN]
    used = sum(r.shape[0] for r in rows)
    return jnp.concatenate(rows + [jnp.zeros((SMALL_ROWS - used, D), F32)], 0)


def _unpack_small(packed):
    out, r0 = {}, 0
    for name, n in SMALL_PLAN:
        rows = -(-n // D)
        out[name] = packed[r0:r0 + rows].reshape(-1)[:n].reshape(1, n)
        r0 += rows
    return out


def _adamw_update(w_ref, g_ref, m_ref, v_ref, d_ref, nm_ref, nv_ref):
    g = g_ref[...]
    m = B1 * m_ref[...] + (1.0 - B1) * g
    v = B2 * v_ref[...] + (1.0 - B2) * (g * g)
    nm_ref[...] = m
    nv_ref[...] = v
    m_hat = m / (1.0 - B1 ** STEP)
    v_hat = v / (1.0 - B2 ** STEP)
    d_ref[...] = -LR * (m_hat / (jnp.sqrt(v_hat) + ADAM_EPS) + WD * w_ref[...])


def _adamw_vectors(ws, gs, ms, vs):
    n = len(ws)

    def body(*refs):
        for j in range(n):
            _adamw_update(*[refs[k * n + j] for k in range(7)])

    vm = pl.BlockSpec(memory_space=pltpu.VMEM)
    outs = pl.pallas_call(body, name="adamw_vectors", in_specs=[vm] * (4 * n), out_specs=[vm] * (3 * n),
                          out_shape=[_sds(w.shape, F32) for w in ws] * 3)(*ws, *gs, *ms, *vs)
    return outs[:n], outs[n:2 * n], outs[2 * n:]


def _adamw(w, g, m, v, name):
    rows, cols = w.shape
    tr = _tile(rows, 256, 8) if rows % 8 == 0 else rows

    def body(*refs):
        _adamw_update(*refs)

    spec = pl.BlockSpec((tr, cols), lambda i: (i, 0))
    return pl.pallas_call(
        body, name=name, grid=(rows // tr,), in_specs=[spec] * 4, out_specs=[spec] * 3,
        out_shape=[_sds((rows, cols), F32)] * 3, compiler_params=_params(("arbitrary",)),
    )(w, g, m, v)


NAMES = ["meta_tokens", "norm_pre_mix", "norm_post_mix", "w_in", "b_in", "attn_sinks", "w_attn_proj", "conv_dw_w",
         "conv_dw_b", "conv_ln_g", "conv_ln_b", "w_conv_proj", "b_conv_proj", "w_out", "norm_pre_ffn", "norm_post_ffn",
         "w_up", "ffn_dw_w", "ffn_dw_b", "w_down"]
MATMUL = ("w_in", "w_attn_proj", "w_conv_proj", "w_out", "w_up", "w_down")


def _two_d(a):
    return a.reshape(a.shape[-2:])


def kernel(x, meta_tokens, norm_pre_mix, norm_post_mix, w_in, b_in, attn_sinks, w_attn_proj, conv_dw_w, conv_dw_b, conv_ln_g, conv_ln_b, w_conv_proj, b_conv_proj, w_out, norm_pre_ffn, norm_post_ffn, w_up, ffn_dw_w, ffn_dw_b, w_down, loss_target, m_meta_tokens, m_norm_pre_mix, m_norm_post_mix, m_w_in, m_b_in, m_attn_sinks, m_w_attn_proj, m_conv_dw_w, m_conv_dw_b, m_conv_ln_g, m_conv_ln_b, m_w_conv_proj, m_b_conv_proj, m_w_out, m_norm_pre_ffn, m_norm_post_ffn, m_w_up, m_ffn_dw_w, m_ffn_dw_b, m_w_down, v_meta_tokens, v_norm_pre_mix, v_norm_post_mix, v_w_in, v_b_in, v_attn_sinks, v_w_attn_proj, v_conv_dw_w, v_conv_dw_b, v_conv_ln_g, v_conv_ln_b, v_w_conv_proj, v_b_conv_proj, v_w_out, v_norm_pre_ffn, v_norm_post_ffn, v_w_up, v_ffn_dw_w, v_ffn_dw_b, v_w_down):
    args = locals()
    w = {n: args[n] for n in NAMES}
    m = {n: args["m_" + n] for n in NAMES}
    v = {n: args["v_" + n] for n in NAMES}
    tiny_names = [part[0] for part in TINY_PARTS]
    big = list(MATMUL) + tiny_names

    def shard_2d(a, name):
        return _two_d(a).T if name == "w_in" else _two_d(a)

    own = {n: shard_2d(w[n], n).astype(BF16) for n in MATMUL}
    own["tiny"] = _tiny_pack({n: _two_d(w[n]) for n in tiny_names})
    dist = _Dist([own[n] for n, _, _, _ in SHARES])
    W = {n: _two_d(w[n]) for n in SMALL}
    W.update(dist.weights(FIRST_SHARES, _alone(dist.gather_hook(FIRST_SHARES), "gather_first")))

    loss_cols, grad_x, grads = local_step(x[0], loss_target[0], W, dist)

    small = dict(grads)
    small["loss"] = loss_cols
    g_small = _unpack_small(_small_allreduce(_pack_small(small)))
    loss = jnp.sum(g_small["loss"])
    g_big = {n: dist.reduced[n] for n in MATMUL}
    g_big.update(_tiny_unpack(dist.reduced["tiny"]))
    g_big["meta_tokens"] = lax.dynamic_slice(g_small["meta_tokens"].reshape(NMETA, D), (0, dist.chip * DQ), (NMETA, DQ))

    g, delta, new_m, new_v = {}, {}, {}, {}
    for n in big:
        shape = w[n].shape
        back = (lambda a: a.T.reshape(shape)) if n == "w_in" else (lambda a: a.reshape(shape))
        outs = _adamw(shard_2d(w[n], n), g_big[n], shard_2d(m[n], n), shard_2d(v[n], n), "adamw_" + n)
        g[n], delta[n], new_m[n], new_v[n] = (back(a) for a in (g_big[n], *outs))
    ud, um, uv = _adamw_vectors(*[[_two_d(d[n]) for n in SMALL] for d in (w, g_small, m, v)])
    for j, n in enumerate(SMALL):
        g[n], delta[n], new_m[n], new_v[n] = g_small[n], ud[j], um[j], uv[j]

    return (loss, grad_x[None], *[g[n] for n in NAMES], *[delta[n] for n in NAMES],
            *[new_m[n] for n in NAMES], *[new_v[n] for n in NAMES])
```

```python
import jax
import jax.numpy as jnp
from jax import lax
from jax.experimental import pallas as pl
from jax.experimental.pallas import tpu as pltpu

F32, BF16 = jnp.float32, jnp.bfloat16

D = 1024
NH, NKV, HD = 16, 2, 64
GH = NH // NKV
NMETA, BLK = 16, 128
PAD = BLK - NMETA
ROT = HD // 4
THETA = 500000.0
CONV_K = 31
FFN = 2816
FFN_K = 3
IN_W = 5376
QKV_W, GLU_W, GATE_W = 1280, 2048, 2048
RMS_EPS, LN_EPS, NEG = 1e-6, 1e-5, -1e30
LR, B1, B2, ADAM_EPS, WD, STEP = 0.001, 0.9, 0.999, 1e-08, 0.01, 10

VMEM_LIMIT = 56 * 2 ** 20
MESH = pl.DeviceIdType.MESH

NT_DIMS = (((1,), (1,)), ((), ()))
TN_DIMS = (((0,), (0,)), ((), ()))


def _params(sem, **kw):
    return pltpu.CompilerParams(dimension_semantics=sem, vmem_limit_bytes=VMEM_LIMIT, **kw)


def _tile(n, pref, mult=16):
    for t in range(min(pref, n), 0, -1):
        if n % t == 0 and t % mult == 0:
            return t
    return n


def _row(tr, w, col=0):
    return pl.BlockSpec((tr, w), lambda i: (i, col))


def _rrow(tr, w, nt, col=0):
    return pl.BlockSpec((tr, w), lambda t: (nt - 1 - t, col))


def _const(shape):
    return pl.BlockSpec(shape, lambda *_: (0,) * len(shape))


def _sds(shape, dt):
    return jax.ShapeDtypeStruct(shape, dt)


class _Hook:
    def __init__(self, operands, out_shape, n_sem, start, finish, mid=None):
        self.operands, self.out_shape, self.n_sem = list(operands), list(out_shape), n_sem
        self.start, self.mid, self.finish = start, mid, finish

    def scratch(self):
        return [pltpu.SemaphoreType.DMA((self.n_sem,)), pltpu.SemaphoreType.DMA((self.n_sem,))]


def _call(body, hook, steps, *, name, grid, in_specs, out_specs, out_shape, operands, semantics, scratch_shapes=()):
    in_specs, out_specs, out_shape = list(in_specs), list(out_specs), list(out_shape)
    if hook is None:
        return pl.pallas_call(body, name=name, grid=grid, in_specs=in_specs, out_specs=out_specs, out_shape=out_shape,
                              scratch_shapes=list(scratch_shapes), compiler_params=_params(semantics))(*operands)
    n_in, n_out, n_hi, n_ho = len(in_specs), len(out_specs), len(hook.operands), len(hook.out_shape)

    def wrapped(*refs):
        ins, hi = refs[:n_in], refs[n_in:n_in + n_hi]
        o0 = n_in + n_hi
        outs, ho = refs[o0:o0 + n_out], refs[o0 + n_out:o0 + n_out + n_ho]
        scratch, (ssem, rsem) = refs[o0 + n_out + n_ho:len(refs) - 2], refs[len(refs) - 2:]
        first, middle, last = steps()

        @pl.when(first)
        def _():
            hook.start(hi, ho, ssem, rsem)

        body(*ins, *outs, *scratch)
        if hook.mid is not None:
            @pl.when(middle)
            def _():
                hook.mid(hi, ho, ssem, rsem)

        @pl.when(last)
        def _():
            hook.finish(hi, ho, ssem, rsem)

    any_spec = pl.BlockSpec(memory_space=pl.ANY)
    return pl.pallas_call(
        wrapped, name=name, grid=grid, in_specs=in_specs + [any_spec] * n_hi, out_specs=out_specs + [any_spec] * n_ho,
        out_shape=out_shape + hook.out_shape, scratch_shapes=list(scratch_shapes) + hook.scratch(),
        compiler_params=_params(semantics))(*operands, *hook.operands)


def _alone(hook, name):
    n_hi = len(hook.operands)

    def body(*refs):
        hi, ho, (ssem, rsem) = refs[:n_hi], refs[n_hi:len(refs) - 2], refs[len(refs) - 2:]
        hook.start(hi, ho, ssem, rsem)
        if hook.mid is not None:
            hook.mid(hi, ho, ssem, rsem)
        hook.finish(hi, ho, ssem, rsem)

    any_spec = pl.BlockSpec(memory_space=pl.ANY)
    return pl.pallas_call(body, name=name, in_specs=[any_spec] * n_hi, out_specs=[any_spec] * len(hook.out_shape),
                          out_shape=hook.out_shape, scratch_shapes=hook.scratch())(*hook.operands)


def _steps_1d(n, mid):
    def steps():
        i = pl.program_id(0)
        return i == 0, i == min(mid, n - 1), i == n - 1
    return steps


def _rms(x, g):
    r = lax.rsqrt(jnp.mean(x * x, -1, keepdims=True) + RMS_EPS)
    return x * r * g, r


def _rms_bwd(dy, x, r, g):
    gy = dy * g
    return r * gy - x * (r * r * r) * jnp.mean(x * gy, -1, keepdims=True)


def _colsum(x):
    return jnp.sum(x, axis=0, keepdims=True)


def _rope(x, c, sa, sb):
    n = x.shape[1]
    return x * c + pltpu.roll(x, n - 8, 1) * sa + pltpu.roll(x, 8, 1) * sb


def _rope_bwd(d, c, sa, sb):
    n = d.shape[1]
    return d * c + pltpu.roll(d * sa, 8, 1) + pltpu.roll(d * sb, n - 8, 1)


def _rope_tables(R):
    half = ROT // 2
    lane = jnp.arange(2 * HD) % HD
    inv = THETA ** (-(lane % half).astype(F32) * 2.0 / ROT)
    pos = (jnp.arange(R) - PAD).astype(F32)
    ang = pos[:, None] * inv[None, :]
    cos, sin = jnp.cos(ang), jnp.sin(ang)
    c = jnp.where(lane < ROT, cos, 1.0)
    sa = jnp.where(lane < half, -sin, 0.0)
    sb = jnp.where((lane >= half) & (lane < ROT), sin, 0.0)
    return c, sa, sb


IN_CHUNKS = ([(0, 512, True), (512, 1024, True), (1024, 1152, True), (1152, 1280, False)]
             + [(c, c + 512, False) for c in range(1280, IN_W, 512)])


def _in_proj(h0, g_pre, w_in, b_in, rope, tr, hook=None):
    R = h0.shape[0]
    nt = R // tr

    def body(h_ref, g_ref, w_ref, b_ref, c_ref, sa_ref, sb_ref, qkv_ref, glu_ref, gate_ref, n1_ref):
        n, _ = _rms(h_ref[...], g_ref[...])
        nb = n.astype(BF16)
        n1_ref[...] = nb
        for c0, c1, rot in IN_CHUNKS:
            acc = lax.dot_general(nb, w_ref[c0:c1, :], NT_DIMS, preferred_element_type=F32) + b_ref[:, c0:c1]
            if rot:
                reps = (c1 - c0) // 128
                acc = _rope(acc, jnp.tile(c_ref[...], (1, reps)), jnp.tile(sa_ref[...], (1, reps)),
                            jnp.tile(sb_ref[...], (1, reps)))
            val = acc.astype(BF16)
            if c1 <= QKV_W:
                qkv_ref[:, c0:c1] = val
            elif c1 <= QKV_W + GLU_W:
                glu_ref[:, c0 - QKV_W:c1 - QKV_W] = val
            else:
                gate_ref[:, c0 - QKV_W - GLU_W:c1 - QKV_W - GLU_W] = val

    return _call(
        body, hook, _steps_1d(nt, (3 * nt) // 4), name="in_proj", grid=(nt,),
        in_specs=[_row(tr, D), _const((1, D)), _const((IN_W, D)), _const((1, IN_W)),
                  _row(tr, 128), _row(tr, 128), _row(tr, 128)],
        out_specs=[_row(tr, QKV_W), _row(tr, GLU_W), _row(tr, GATE_W), _row(tr, D)],
        out_shape=[_sds((R, QKV_W), BF16), _sds((R, GLU_W), BF16), _sds((R, GATE_W), BF16), _sds((R, D), BF16)],
        operands=(h0, g_pre, w_in, b_in, *rope), semantics=("arbitrary",))


def _attn_mask(n, keys_first=False):
    shape = (3 * BLK, BLK) if keys_first else (BLK, 3 * BLK)
    qi = lax.broadcasted_iota(jnp.int32, shape, 1 if keys_first else 0)
    kj = lax.broadcasted_iota(jnp.int32, shape, 0 if keys_first else 1)
    tq = n * BLK + qi - PAD
    t_meta = kj - PAD
    t_loc = (n - 1) * BLK + (kj - BLK) - PAD
    meta_ok = (kj < BLK) & (t_meta >= 0) & (t_meta <= tq)
    loc_ok = (kj >= BLK) & (t_loc >= NMETA) & (t_loc <= tq) & (tq - t_loc < BLK)
    return meta_ok | loc_ok


def _dup_heads(ref0, refp, refc, low, transposed=False):
    a = jnp.concatenate([ref0[...], refp[...], refc[...]], 0).astype(F32)
    sw = pltpu.roll(a, HD, 1)
    heads = [jnp.where(low, a, sw), jnp.where(low, sw, a)]
    return [(h.T if transposed else h).astype(BF16) for h in heads]


def _kv_specs(nb, rev):
    def blk(col, which):
        def idx(t):
            n = nb - 1 - t if rev else t
            return ({"meta": 0, "prev": jnp.maximum(n - 1, 0), "own": n}[which], col)
        return pl.BlockSpec((BLK, BLK), idx)
    return [blk(col, w) for col in (8, 9) for w in ("meta", "prev", "own")]


def _attn_fwd(qkv, sinks, hook=None):
    R = qkv.shape[0]
    nb = R // BLK

    def body(s_ref, q_ref, k0, kp, kc, v0, vp, vc, o_ref, lse_ref):
        n = pl.program_id(0)
        lane = lax.broadcasted_iota(jnp.int32, (1, BLK), 1)
        low = lane < HD
        kd, vd_t = _dup_heads(k0, kp, kc, low), _dup_heads(v0, vp, vc, low, transposed=True)
        mask = _attn_mask(n, keys_first=True)
        zero = jnp.zeros((), BF16)
        lses = []
        for g in range(NKV):
            tiles = []
            for pair in range(GH // 2 * g, GH // 2 * (g + 1)):
                qp = q_ref[:, pair * BLK:(pair + 1) * BLK] * jnp.asarray(HD ** -0.5, BF16)
                tiles += [jnp.where(low, qp, zero), jnp.where(low, zero, qp)]
            st = lax.dot_general(kd[g], jnp.concatenate(tiles, 0), NT_DIMS, preferred_element_type=F32)
            ps, inv = [], []
            for j in range(GH):
                s = jnp.where(mask, st[:, j * BLK:(j + 1) * BLK], NEG)
                sk = s_ref[GH * g + j]
                m = jnp.maximum(jnp.max(s, 0, keepdims=True), sk)
                p = jnp.exp(s - m)
                l = jnp.sum(p, 0, keepdims=True) + jnp.exp(sk - m)
                ps.append(p.astype(BF16))
                inv.append(1.0 / l)
                lses.append(m + jnp.log(l))
            ot = jnp.dot(vd_t[g], jnp.concatenate(ps, 1), preferred_element_type=F32)
            for j in range(GH // 2):
                pair = GH // 2 * g + j
                o = [(ot[:, h * BLK:(h + 1) * BLK] * inv[h]).T for h in (2 * j, 2 * j + 1)]
                o_ref[:, pair * BLK:(pair + 1) * BLK] = jnp.where(low, o[0], o[1]).astype(BF16)
        lse_ref[...] = jnp.concatenate(lses, 0)

    return _call(
        body, hook, _steps_1d(nb, max(nb - 4, 0)), name="attn_fwd", grid=(nb,),
        in_specs=[pl.BlockSpec(memory_space=pltpu.SMEM), pl.BlockSpec((BLK, D), lambda n: (n, 0))] + _kv_specs(nb, False),
        out_specs=[_row(BLK, D), _row(NH, BLK)],
        out_shape=[_sds((R, D), BF16), _sds((nb * NH, BLK), F32)],
        operands=(sinks, qkv, *([qkv] * 6)), semantics=("arbitrary",))


CONV_TCH, CONV_SUB, HALO = 256, 64, 32


def _tap_windows(buf, r0, offset_of):
    span = CONV_SUB + HALO
    x = buf[pl.ds(r0, span), :]
    by_phase = {}
    for k in range(CONV_K):
        by_phase.setdefault(offset_of(k) % 8, []).append(k)
    for phase, taps in sorted(by_phase.items()):
        y = x if phase == 0 else pltpu.roll(x, span - phase, 0)
        for k in taps:
            d = offset_of(k) - phase
            yield k, y[d:d + CONV_SUB, :]


def _conv_fwd(glu, w, b, tr, hook=None):
    R = glu.shape[0]
    nc = D // CONV_TCH

    def body(a_ref, g_ref, w_ref, b_ref, o_ref, buf):
        i = pl.program_id(1)

        @pl.when(i == 0)
        def _():
            buf[0:HALO, :] = jnp.zeros((HALO, CONV_TCH), F32)

        @pl.when(i > 0)
        def _():
            buf[0:HALO, :] = buf[tr:tr + HALO, :]

        row = i * tr + lax.broadcasted_iota(jnp.int32, (tr, 1), 0)
        a, g = a_ref[...].astype(F32), g_ref[...].astype(F32)
        buf[HALO:HALO + tr, :] = jnp.where(row >= PAD, a * jax.nn.sigmoid(g), 0.0)
        for r0 in range(0, tr, CONV_SUB):
            acc = jnp.broadcast_to(b_ref[...], (CONV_SUB, CONV_TCH))
            for k, win in _tap_windows(buf, r0, lambda k: HALO - (CONV_K - 1) + k):
                acc = acc + w_ref[k:k + 1, :] * win
            o_ref[r0:r0 + CONV_SUB, :] = acc.astype(BF16)

    nt = R // tr

    def steps():
        c, i = pl.program_id(0), pl.program_id(1)
        return (c == 0) & (i == 0), (c == nc - 1) & (i == 0), (c == nc - 1) & (i == nt - 1)

    return _call(
        body, hook, steps, name="conv_fwd", grid=(nc, nt),
        in_specs=[pl.BlockSpec((tr, CONV_TCH), lambda c, i: (i, c)),
                  pl.BlockSpec((tr, CONV_TCH), lambda c, i: (i, nc + c)),
                  pl.BlockSpec((None, HALO, CONV_TCH), lambda c, i: (c, 0, 0)),
                  pl.BlockSpec((1, CONV_TCH), lambda c, i: (0, c))],
        out_specs=[pl.BlockSpec((tr, CONV_TCH), lambda c, i: (i, c))],
        out_shape=[_sds((R, D), BF16)],
        scratch_shapes=[pltpu.VMEM((tr + HALO, CONV_TCH), F32)],
        operands=(glu, glu, w, b), semantics=("arbitrary", "arbitrary"))


def _ln_silu(c1, lg, lb):
    mu = jnp.mean(c1, -1, keepdims=True)
    xc = c1 - mu
    rs = lax.rsqrt(jnp.mean(xc * xc, -1, keepdims=True) + LN_EPS)
    xh = xc * rs
    c2 = xh * lg + lb
    sg = jax.nn.sigmoid(c2)
    return xh, rs, c2, sg


def _mix_out(attn, c1, gates, h0, w_ap, w_cp, w_out, lg, lb, b_cp, g_post, g_ffn, tr):
    R = attn.shape[0]

    def body(at_ref, c1_ref, ga_ref, gc_ref, h0_ref, wap, wcp, wo, lg_ref, lb_ref, bcp, gp, gf,
             ao_ref, co_ref, c3_ref, mg_ref, mix_ref, h1_ref, n2_ref):
        ao = jnp.dot(at_ref[...], wap[...], preferred_element_type=F32)
        _, _, c2, sg = _ln_silu(c1_ref[...].astype(F32), lg_ref[...], lb_ref[...])
        c3 = (c2 * sg).astype(BF16)
        c3_ref[...] = c3
        co = jnp.dot(c3, wcp[...], preferred_element_type=F32) + bcp[...]
        ao_b, co_b = ao.astype(BF16), co.astype(BF16)
        ao_ref[...] = ao_b
        co_ref[...] = co_b
        merged = (jax.nn.sigmoid(ga_ref[...].astype(F32)) * ao_b.astype(F32)
                  + jax.nn.sigmoid(gc_ref[...].astype(F32)) * co_b.astype(F32)).astype(BF16)
        mg_ref[...] = merged
        mix = jnp.dot(merged, wo[...], preferred_element_type=F32).astype(BF16)
        mix_ref[...] = mix
        y, _ = _rms(mix.astype(F32), gp[...])
        h1 = h0_ref[...] + y
        h1_ref[...] = h1
        n2, _ = _rms(h1, gf[...])
        row = pl.program_id(0) * tr + lax.broadcasted_iota(jnp.int32, (tr, 1), 0)
        n2_ref[...] = jnp.where(row >= PAD, n2, 0.0).astype(BF16)

    vec = _const((1, D))
    return pl.pallas_call(
        body, name="mix_out", grid=(R // tr,),
        in_specs=[_row(tr, D), _row(tr, D), _row(tr, D, 0), _row(tr, D, 1), _row(tr, D),
                  _const((D, D)), _const((D, D)), _const((D, D)), vec, vec, vec, vec, vec],
        out_specs=[_row(tr, D)] * 7,
        out_shape=[_sds((R, D), BF16)] * 5 + [_sds((R, D), F32), _sds((R, D), BF16)],
        compiler_params=_params(("arbitrary",)),
    )(attn, c1, gates, gates, h0, w_ap, w_cp, w_out, lg, lb, b_cp, g_post, g_ffn)


FFN_CH = 256
N_CHIPS = 4
UPQ = 2 * FFN // N_CHIPS
UP_CHUNKS = [(q, c0, min(c0 + 512, UPQ)) for q in range(N_CHIPS // 2) for c0 in range(0, UPQ, 512)]
TINY_ROWS, TINY_CONV, TINY_FFN, TINY_META = 64, 0, 32, 40


def _shift_down(x, k, halo):
    tr = x.shape[0]
    row = lax.broadcasted_iota(jnp.int32, (tr, 1), 0)
    y = pltpu.roll(x, k, 0)
    for j in range(k):
        y = jnp.where(row == j, halo[8 - k + j:8 - k + j + 1, :], y)
    return y


def _shift_up(x, k, halo):
    tr = x.shape[0]
    row = lax.broadcasted_iota(jnp.int32, (tr, 1), 0)
    y = pltpu.roll(x, tr - k, 0)
    for j in range(k):
        y = jnp.where(row == tr - k + j, halo[j:j + 1, :], y)
    return y


def _conv3(x, halo, w, b):
    return w[2:3, :] * x + w[1:2, :] * _shift_down(x, 1, halo) + w[0:1, :] * _shift_down(x, 2, halo) + b


def _ffn_up(n2, w_up, fw, fb, tr):
    R = n2.shape[0]

    def body(n_ref, w_ref, fw_ref, fb_ref, up_ref, act_ref, carry):
        @pl.when(pl.program_id(0) == 0)
        def _():
            carry[...] = jnp.zeros_like(carry)

        nb = n_ref[...]
        for q, c0, c1 in UP_CHUNKS:
            us = []
            for qq in (q, q + N_CHIPS // 2):
                cs = slice(qq * UPQ + c0, qq * UPQ + c1)
                x = jnp.dot(nb, w_ref[qq, :, c0:c1], preferred_element_type=F32).astype(BF16)
                up_ref[:, cs] = x
                x = x.astype(F32)
                us.append(_conv3(x, carry[:, cs], fw_ref[qq, TINY_FFN:TINY_FFN + 8, c0:c1], fb_ref[:, cs]))
                carry[:, cs] = x[tr - 8:tr, :]
            act_ref[:, q * UPQ + c0:q * UPQ + c1] = (us[0] * jax.nn.sigmoid(us[0]) * us[1]).astype(BF16)

    return pl.pallas_call(
        body, name="ffn_up", grid=(R // tr,),
        in_specs=[_row(tr, D), _const((N_CHIPS, D, UPQ)), _const((N_CHIPS, TINY_ROWS, UPQ)), _const((1, 2 * FFN))],
        out_specs=[_row(tr, 2 * FFN), _row(tr, FFN)],
        out_shape=[_sds((R, 2 * FFN), BF16), _sds((R, FFN), BF16)],
        scratch_shapes=[pltpu.VMEM((8, 2 * FFN), F32)],
        compiler_params=_params(("arbitrary",)),
    )(n2, w_up, fw, fb)


def _ffn_down(act, w_down, h1, tgt, g_post, tr):
    R = act.shape[0]
    m = tr // BLK

    def body(a_ref, w_ref, h1_ref, g_ref, *rest):
        t_refs, (dh2_ref, dffn_ref, loss_ref, dg_ref) = rest[:m], rest[m:]

        @pl.when(pl.program_id(0) == 0)
        def _():
            loss_ref[...] = jnp.zeros_like(loss_ref)
            dg_ref[...] = jnp.zeros_like(dg_ref)

        f = jnp.dot(a_ref[...], w_ref[...], preferred_element_type=F32)
        g = g_ref[...]
        y, r = _rms(f, g)
        row = pl.program_id(0) * tr + lax.broadcasted_iota(jnp.int32, (tr, 1), 0)
        tgt_rows = jnp.concatenate([t[...] for t in t_refs], 0)
        e = jnp.where(row >= BLK, h1_ref[...] + y - tgt_rows, 0.0)
        loss_ref[...] += _colsum(e * e) * (0.5 / D)
        dy = e * (1.0 / D)
        dh2_ref[...] = dy
        dffn_ref[...] = _rms_bwd(dy, f, r, g).astype(BF16)
        dg_ref[...] += _colsum(dy * f * r)

    return pl.pallas_call(
        body, name="ffn_down", grid=(R // tr,),
        in_specs=[_row(tr, FFN), _const((FFN, D)), _row(tr, D), _const((1, D))]
                 + [pl.BlockSpec((BLK, D), lambda i, k=k: (jnp.maximum(m * i - 1 + k, 0), 0)) for k in range(m)],
        out_specs=[_row(tr, D), _row(tr, D), _const((1, D)), _const((1, D))],
        out_shape=[_sds((R, D), F32), _sds((R, D), BF16), _sds((1, D), F32), _sds((1, D), F32)],
        compiler_params=_params(("arbitrary",)),
    )(act, w_down, h1, g_post, *([tgt] * m))


def _ffn_bwd_act(dffn, w_down, up, fw, fb, tr):
    R = dffn.shape[0]
    nt = R // tr

    def body(d_ref, w_ref, up_ref, hal_ref, fw_ref, fb_ref, dup_ref, dfw_ref, dfb_ref, carry):
        t = pl.program_id(0)
        i = nt - 1 - t

        @pl.when(t == 0)
        def _():
            carry[...] = jnp.zeros_like(carry)
            dfw_ref[...] = jnp.zeros_like(dfw_ref)
            dfb_ref[...] = jnp.zeros_like(dfb_ref)

        dff = d_ref[...]
        row = i * tr + lax.broadcasted_iota(jnp.int32, (tr, 1), 0)
        first = i == 0
        for q, c0, c1 in UP_CHUNKS:
            dact = lax.dot_general(dff, w_ref[q * UPQ + c0:q * UPQ + c1, :], NT_DIMS, preferred_element_type=F32)
            chips = (q, q + N_CHIPS // 2)
            xs, us = [], []
            for qq in chips:
                cs = slice(qq * UPQ + c0, qq * UPQ + c1)
                x = up_ref[:, cs].astype(F32)
                halo = jnp.where(first, 0.0, hal_ref[:, cs].astype(F32))
                x1, x2 = _shift_down(x, 1, halo), _shift_down(x, 2, halo)
                w = fw_ref[qq, TINY_FFN:TINY_FFN + 8, c0:c1]
                us.append(w[2:3, :] * x + w[1:2, :] * x1 + w[0:1, :] * x2 + fb_ref[:, cs])
                xs.append((x, x1, x2))
            sg = jax.nn.sigmoid(us[0])
            silu = us[0] * sg
            dus = [dact * us[1] * sg * (1.0 + us[0] * (1.0 - sg)), dact * silu]
            for (x, x1, x2), du, qq in zip(xs, dus, chips):
                cs = slice(qq * UPQ + c0, qq * UPQ + c1)
                w = fw_ref[qq, TINY_FFN:TINY_FFN + 8, c0:c1]
                nxt = carry[:, cs]
                dx = w[2:3, :] * du + w[1:2, :] * _shift_up(du, 1, nxt) + w[0:1, :] * _shift_up(du, 2, nxt)
                dup_ref[:, cs] = jnp.where(row >= PAD, dx, 0.0).astype(BF16)
                dfw_ref[qq, 0:1, c0:c1] += _colsum(x2 * du)
                dfw_ref[qq, 1:2, c0:c1] += _colsum(x1 * du)
                dfw_ref[qq, 2:3, c0:c1] += _colsum(x * du)
                dfb_ref[:, cs] += _colsum(du)
                carry[:, cs] = du[0:8, :]

    halo_spec = pl.BlockSpec((8, 2 * FFN), lambda t: (jnp.maximum((nt - 1 - t) * (tr // 8) - 1, 0), 0))
    return pl.pallas_call(
        body, name="ffn_bwd_act", grid=(nt,),
        in_specs=[_rrow(tr, D, nt), _const((FFN, D)), _rrow(tr, 2 * FFN, nt), halo_spec,
                  _const((N_CHIPS, TINY_ROWS, UPQ)), _const((1, 2 * FFN))],
        out_specs=[_rrow(tr, 2 * FFN, nt), _const((N_CHIPS, 8, UPQ)), _const((1, 2 * FFN))],
        out_shape=[_sds((R, 2 * FFN), BF16), _sds((N_CHIPS, 8, UPQ), F32), _sds((1, 2 * FFN), F32)],
        scratch_shapes=[pltpu.VMEM((8, 2 * FFN), F32)],
        compiler_params=_params(("arbitrary",)),
    )(dffn, w_down, up, up, fw, fb)


def _ffn_bwd_in(dup, w_up, h1, dh2, mix, g_ffn, g_post, tr, hook=None):
    R = dup.shape[0]

    def body(d_ref, w_ref, h1_ref, dh2_ref, mix_ref, gf_ref, gp_ref, dh1_ref, dmix_ref, dgf_ref, dgp_ref):
        @pl.when(pl.program_id(0) == 0)
        def _():
            dgf_ref[...] = jnp.zeros_like(dgf_ref)
            dgp_ref[...] = jnp.zeros_like(dgp_ref)

        dn2 = sum(lax.dot_general(d_ref[:, q * UPQ:(q + 1) * UPQ], w_ref[q], NT_DIMS, preferred_element_type=F32)
                  for q in range(N_CHIPS))
        h1 = h1_ref[...]
        _, r2 = _rms(h1, gf_ref[...])
        dh1 = dh2_ref[...] + _rms_bwd(dn2, h1, r2, gf_ref[...])
        dgf_ref[...] += _colsum(dn2 * h1 * r2)
        dh1_ref[...] = dh1
        m = mix_ref[...].astype(F32)
        _, rm = _rms(m, gp_ref[...])
        dmix_ref[...] = _rms_bwd(dh1, m, rm, gp_ref[...]).astype(BF16)
        dgp_ref[...] += _colsum(dh1 * m * rm)

    vec = _const((1, D))
    return _call(
        body, hook, _steps_1d(R // tr, R // tr), name="ffn_bwd_in", grid=(R // tr,),
        in_specs=[_row(tr, 2 * FFN), _const((N_CHIPS, D, UPQ)), _row(tr, D), _row(tr, D), _row(tr, D), vec, vec],
        out_specs=[_row(tr, D), _row(tr, D), vec, vec],
        out_shape=[_sds((R, D), F32), _sds((R, D), BF16), _sds((1, D), F32), _sds((1, D), F32)],
        operands=(dup, w_up, h1, dh2, mix, g_ffn, g_post), semantics=("arbitrary",))


def _mix_bwd(dmix, ao, co, gates, c1, w_out, w_ap, w_cp, lg, lb, tr):
    R = dmix.shape[0]

    def body(dm_ref, ao_ref, co_ref, ga_ref, gc_ref, c1_ref, wo, wap, wcp, lg_ref, lb_ref,
             dao_ref, dco_ref, dgate_ref, dattn_ref, dc1_ref, dbcp_ref, dlg_ref, dlb_ref, dcb_ref):
        @pl.when(pl.program_id(0) == 0)
        def _():
            for ref in (dbcp_ref, dlg_ref, dlb_ref, dcb_ref):
                ref[...] = jnp.zeros_like(ref)

        dmg = lax.dot_general(dm_ref[...], wo[...], NT_DIMS, preferred_element_type=F32)
        sa = jax.nn.sigmoid(ga_ref[...].astype(F32))
        sc = jax.nn.sigmoid(gc_ref[...].astype(F32))
        dao = (dmg * sa).astype(BF16)
        dco = (dmg * sc).astype(BF16)
        dao_ref[...] = dao
        dco_ref[...] = dco
        dgate_ref[:, 0:D] = (dmg * ao_ref[...].astype(F32) * sa * (1.0 - sa)).astype(BF16)
        dgate_ref[:, D:2 * D] = (dmg * co_ref[...].astype(F32) * sc * (1.0 - sc)).astype(BF16)
        dbcp_ref[...] += _colsum(dco.astype(F32))
        dattn_ref[...] = lax.dot_general(dao, wap[...], NT_DIMS, preferred_element_type=F32).astype(BF16)
        dc3 = lax.dot_general(dco, wcp[...], NT_DIMS, preferred_element_type=F32)
        xh, rs, c2, sg = _ln_silu(c1_ref[...].astype(F32), lg_ref[...], lb_ref[...])
        dc2 = dc3 * sg * (1.0 + c2 * (1.0 - sg))
        dlg_ref[...] += _colsum(dc2 * xh)
        dlb_ref[...] += _colsum(dc2)
        dxh = dc2 * lg_ref[...]
        dc1 = rs * (dxh - jnp.mean(dxh, -1, keepdims=True) - xh * jnp.mean(dxh * xh, -1, keepdims=True))
        dc1_ref[...] = dc1
        dcb_ref[...] += _colsum(dc1)

    vec = _const((1, D))
    return pl.pallas_call(
        body, name="mix_bwd", grid=(R // tr,),
        in_specs=[_row(tr, D), _row(tr, D), _row(tr, D), _row(tr, D, 0), _row(tr, D, 1), _row(tr, D),
                  _const((D, D)), _const((D, D)), _const((D, D)), vec, vec],
        out_specs=[_row(tr, D), _row(tr, D), _row(tr, 2 * D), _row(tr, D), _row(tr, D), vec, vec, vec, vec],
        out_shape=[_sds((R, D), BF16), _sds((R, D), BF16), _sds((R, 2 * D), BF16), _sds((R, D), BF16),
                   _sds((R, D), F32)] + [_sds((1, D), F32)] * 4,
        compiler_params=_params(("arbitrary",)),
    )(dmix, ao, co, gates, gates, c1, w_out, w_ap, w_cp, lg, lb)


def _conv_bwd(dc1, glu, w, tr, hook=None):
    R = dc1.shape[0]
    nt, nc = R // tr, D // CONV_TCH

    def body(d_ref, a_ref, g_ref, w_ref, dglu_a, dglu_g, dw_ref, buf, dw_acc):
        t = pl.program_id(1)
        i = nt - 1 - t

        @pl.when(t == 0)
        def _():
            buf[tr:tr + HALO, :] = jnp.zeros((HALO, CONV_TCH), F32)
            dw_acc[...] = jnp.zeros_like(dw_acc)

        @pl.when(t > 0)
        def _():
            buf[tr:tr + HALO, :] = buf[0:HALO, :]

        buf[0:tr, :] = d_ref[...]
        for r0 in range(0, tr, CONV_SUB):
            rs = slice(r0, r0 + CONV_SUB)
            row = i * tr + r0 + lax.broadcasted_iota(jnp.int32, (CONV_SUB, 1), 0)
            a, g = a_ref[rs, :].astype(F32), g_ref[rs, :].astype(F32)
            sg = jax.nn.sigmoid(g)
            glu = jnp.where(row >= PAD, a * sg, 0.0)
            acc = jnp.zeros((CONV_SUB, CONV_TCH), F32)
            for k, win in _tap_windows(buf, r0, lambda k: CONV_K - 1 - k):
                acc = acc + w_ref[k:k + 1, :] * win
                dw_acc[k] += jnp.sum((glu * win).reshape(CONV_SUB // 8, 8, CONV_TCH), axis=0)
            dglu = jnp.where(row >= PAD, acc, 0.0)
            dglu_a[rs, :] = (dglu * sg).astype(BF16)
            dglu_g[rs, :] = (dglu * a * sg * (1.0 - sg)).astype(BF16)

        @pl.when(t == nt - 1)
        def _():
            dw_ref[...] = jnp.sum(dw_acc[...], axis=1)

    def rspec(col0):
        return pl.BlockSpec((tr, CONV_TCH), lambda c, t: (nt - 1 - t, col0 + c))

    def steps():
        c, t = pl.program_id(0), pl.program_id(1)
        return (c == 0) & (t == 0), False, (c == nc - 1) & (t == nt - 1)

    return _call(
        body, hook, steps, name="conv_bwd", grid=(nc, nt),
        in_specs=[rspec(0), rspec(0), rspec(nc), pl.BlockSpec((None, HALO, CONV_TCH), lambda c, t: (c, 0, 0))],
        out_specs=[rspec(0), rspec(0), pl.BlockSpec((None, HALO, CONV_TCH), lambda c, t: (c, 0, 0))],
        out_shape=[_sds((R, D), BF16), _sds((R, D), BF16), _sds((N_CHIPS, HALO, CONV_TCH), F32)],
        scratch_shapes=[pltpu.VMEM((tr + HALO, CONV_TCH), F32), pltpu.VMEM((HALO, 8, CONV_TCH), F32)],
        operands=(dc1, glu, glu, w), semantics=("arbitrary", "arbitrary"))


def _attn_bwd(qkv, do, lse, sinks, rope, hook=None):
    R = qkv.shape[0]
    nb = R // BLK

    def body(s_ref, q_ref, k0, kp, kc, v0, vp, vc, do_ref, lse_ref, c_ref, sa_ref, sb_ref,
             dqkv_ref, dsink_ref, car_k, car_v, met_k, met_v):
        t = pl.program_id(0)
        n = nb - 1 - t

        @pl.when(t == 0)
        def _():
            for ref in (car_k, car_v, met_k, met_v, dsink_ref):
                ref[...] = jnp.zeros_like(ref)

        lane = lax.broadcasted_iota(jnp.int32, (1, BLK), 1)
        low = lane < HD
        kd, vd = _dup_heads(k0, kp, kc, low), _dup_heads(v0, vp, vc, low)
        kd_t = _dup_heads(k0, kp, kc, low, transposed=True)
        mask = _attn_mask(n, keys_first=True)
        tabs = (c_ref[...], sa_ref[...], sb_ref[...])
        zero = jnp.zeros((), BF16)
        dk_acc, dv_acc = [], []
        dsink = jnp.zeros((1, BLK), F32)
        for g in range(NKV):
            q_tiles, do_tiles = [], []
            for pair in range(GH // 2 * g, GH // 2 * (g + 1)):
                cs = slice(pair * BLK, (pair + 1) * BLK)
                qp, dop = q_ref[:, cs] * jnp.asarray(HD ** -0.5, BF16), do_ref[:, cs]
                q_tiles += [jnp.where(low, qp, zero), jnp.where(low, zero, qp)]
                do_tiles += [jnp.where(low, dop, zero), jnp.where(low, zero, dop)]
            qs, dos = jnp.concatenate(q_tiles, 0), jnp.concatenate(do_tiles, 0)
            st = lax.dot_general(kd[g], qs, NT_DIMS, preferred_element_type=F32)
            dpt = lax.dot_general(vd[g], dos, NT_DIMS, preferred_element_type=F32)
            ps, dss = [], []
            for j in range(GH):
                h = GH * g + j
                cs = slice(j * BLK, (j + 1) * BLK)
                lse_h = lse_ref[h:h + 1, :]
                p = jnp.where(mask, jnp.exp(st[:, cs] - lse_h), 0.0)
                dp = dpt[:, cs]
                delta = jnp.sum(p * dp, 0, keepdims=True)
                ps.append(p.astype(BF16))
                dss.append((p * (dp - delta)).astype(BF16))
                dsink = dsink + jnp.where(lane == h, -jnp.sum(jnp.exp(s_ref[h] - lse_h) * delta), 0.0)
            ds_t, p_t = jnp.concatenate(dss, 1), jnp.concatenate(ps, 1)
            dk_acc.append(jnp.dot(ds_t, qs, preferred_element_type=F32))
            dv_acc.append(jnp.dot(p_t, dos, preferred_element_type=F32))
            dq_t = jnp.dot(kd_t[g], ds_t, preferred_element_type=F32) * (HD ** -0.5)
            for j in range(GH // 2):
                pair = GH // 2 * g + j
                dq = [dq_t[:, h * BLK:(h + 1) * BLK].T for h in (2 * j, 2 * j + 1)]
                dqkv_ref[:, pair * BLK:(pair + 1) * BLK] = _rope_bwd(jnp.where(low, dq[0], dq[1]), *tabs).astype(BF16)
        dsink_ref[0:1, :] += dsink

        def fold(acc):
            tot = [a + pltpu.roll(a, HD, 1) for a in acc]
            return jnp.where(low, tot[0], tot[1])

        dk_all, dv_all = fold(dk_acc), fold(dv_acc)
        met_k[...] += dk_all[0:BLK, :]
        met_v[...] += dv_all[0:BLK, :]
        last = jnp.where(n == 0, 1.0, 0.0)
        dk_n = dk_all[2 * BLK:3 * BLK, :] + car_k[...] + last * met_k[...]
        dv_n = dv_all[2 * BLK:3 * BLK, :] + car_v[...] + last * met_v[...]
        dqkv_ref[:, D:D + BLK] = _rope_bwd(dk_n, *tabs).astype(BF16)
        dqkv_ref[:, D + BLK:QKV_W] = dv_n.astype(BF16)
        car_k[...] = dk_all[BLK:2 * BLK, :]
        car_v[...] = dv_all[BLK:2 * BLK, :]

    rblk = lambda w: pl.BlockSpec((BLK, w), lambda t: (nb - 1 - t, 0))
    return _call(
        body, hook, _steps_1d(nb, nb), name="attn_bwd", grid=(nb,),
        in_specs=[pl.BlockSpec(memory_space=pltpu.SMEM), rblk(D)] + _kv_specs(nb, True)
                 + [rblk(D), pl.BlockSpec((NH, BLK), lambda t: (nb - 1 - t, 0)), rblk(BLK), rblk(BLK), rblk(BLK)],
        out_specs=[rblk(QKV_W), _const((8, BLK))],
        out_shape=[_sds((R, QKV_W), BF16), _sds((8, BLK), F32)],
        scratch_shapes=[pltpu.VMEM((BLK, BLK), F32)] * 4,
        operands=(sinks, qkv, *([qkv] * 6), do, lse, *rope), semantics=("arbitrary",))


def _in_bwd(dproj, w_in, h0, dh1, g_pre, tr, hook=None):
    R = h0.shape[0]
    n = len(dproj)
    widths = [p.shape[1] for p in dproj]
    starts = [sum(widths[:j]) for j in range(n)]

    def body(*refs):
        d_refs, (w_ref, h0_ref, dh1_ref, g_ref, dh0_ref, dg_ref, db_ref) = refs[:n], refs[n:]

        @pl.when(pl.program_id(0) == 0)
        def _():
            dg_ref[...] = jnp.zeros_like(dg_ref)
            db_ref[...] = jnp.zeros_like(db_ref)

        dn1 = jnp.zeros((tr, D), F32)
        for d_ref, c0, wd in zip(d_refs, starts, widths):
            d = d_ref[...]
            dn1 = dn1 + jnp.dot(d, w_ref[c0:c0 + wd, :], preferred_element_type=F32)
            db_ref[:, c0:c0 + wd] += _colsum(d.astype(F32))
        h0 = h0_ref[...]
        _, r = _rms(h0, g_ref[...])
        dh0_ref[...] = dh1_ref[...] + _rms_bwd(dn1, h0, r, g_ref[...])
        dg_ref[...] += _colsum(dn1 * h0 * r)

    return _call(
        body, hook, _steps_1d(R // tr, R // tr), name="in_bwd", grid=(R // tr,),
        in_specs=[_row(tr, wd) for wd in widths] + [_const((IN_W, D)), _row(tr, D), _row(tr, D), _const((1, D))],
        out_specs=[_row(tr, D), _const((1, D)), _const((1, IN_W))],
        out_shape=[_sds((R, D), F32), _sds((1, D), F32), _sds((1, IN_W), F32)],
        operands=(*dproj, w_in, h0, dh1, g_pre), semantics=("arbitrary",))


def _dw(a, b, name, tn, tr, by_chip=False, ta=None, rows_of=None, row0=0, into=None):
    R, ka = a.shape
    n = b.shape[1]
    nt = R // tr
    ta = ta or ka
    k0 = 0
    if by_chip:
        out_spec = pl.BlockSpec((None, ta, tn), lambda k, j, i: (j, k, 0))
        out_shape = _sds((n // tn, ka, tn), BF16)
    else:
        if rows_of is not None:
            k0 = row0 // ta
        out_spec = pl.BlockSpec((ta, tn), lambda k, j, i: (k0 + k, j))
        out_shape = _sds((ka if rows_of is None else rows_of, n), BF16)
    extra = [] if into is None else [into]

    def body(a_ref, b_ref, *rest):
        o_ref, acc = rest[len(extra):]
        i = pl.program_id(2)

        @pl.when(i == 0)
        def _():
            acc[...] = jnp.zeros_like(acc)

        acc[...] += lax.dot_general(a_ref[...], b_ref[...], TN_DIMS, preferred_element_type=F32)

        @pl.when(i == nt - 1)
        def _():
            o_ref[...] = acc[...].astype(BF16)

    return pl.pallas_call(
        body, name=name, grid=(ka // ta, n // tn, nt),
        in_specs=[pl.BlockSpec((tr, ta), lambda k, j, i: (i, k)), pl.BlockSpec((tr, tn), lambda k, j, i: (i, j))]
                 + [pl.BlockSpec(memory_space=pl.ANY)] * len(extra),
        out_specs=out_spec, out_shape=out_shape,
        input_output_aliases={2: 0} if extra else {},
        scratch_shapes=[pltpu.VMEM((ta, tn), F32)],
        compiler_params=_params(("arbitrary", "arbitrary", "arbitrary")),
    )(a, b, *extra)


SMALL = ["norm_pre_mix", "norm_post_mix", "b_in", "attn_sinks", "conv_dw_b", "conv_ln_g", "conv_ln_b",
         "b_conv_proj", "norm_pre_ffn", "norm_post_ffn", "ffn_dw_b"]


def local_step(x, tgt, W, dist=None):
    W = dict(W)
    S = x.shape[0]
    R = S + BLK
    tr = _tile(R, 384, BLK)
    trw = _tile(R, 1056)
    rope = _rope_tables(R)
    meta = _cols_joined(W["tiny"][:, TINY_META:TINY_META + NMETA, 0:DQ])
    h0 = jnp.concatenate([jnp.zeros((PAD, D), F32), meta, x], 0)

    qkv, glu, gates, n1, *got = _in_proj(h0, W["norm_pre_mix"], W["w_in"], W["b_in"], rope, tr,
                                         dist and dist.gather_hook(GATHER_IN_PROJ))
    if dist:
        W.update(dist.weights(GATHER_IN_PROJ, got))
    sinks = W["attn_sinks"].reshape(NH)
    attn, lse, *got = _attn_fwd(qkv, sinks, dist and dist.gather_hook(GATHER_ATTN))
    if dist:
        W.update(dist.weights(GATHER_ATTN, got))
    c1, *got = _conv_fwd(glu, W["tiny"], W["conv_dw_b"], tr, dist and dist.gather_hook(GATHER_CONV))
    if dist:
        W.update(dist.weights(GATHER_CONV, got))
    ao, co, c3, merged, mix, h1, n2 = _mix_out(
        attn, c1, gates, h0, W["w_attn_proj"], W["w_conv_proj"], W["w_out"], W["conv_ln_g"], W["conv_ln_b"],
        W["b_conv_proj"], W["norm_post_mix"], W["norm_pre_ffn"], tr)
    up, act = _ffn_up(n2, W["w_up"], W["tiny"], W["ffn_dw_b"], tr)
    dh2, dffn, loss_cols, dg_post_ffn = _ffn_down(act, W["w_down"], h1, tgt, W["norm_post_ffn"], tr)

    dw_down = _dw(act, dffn, "dw_down", 512, trw)
    dup, dfw, dfb = _ffn_bwd_act(dffn, W["w_down"], up, W["tiny"], W["ffn_dw_b"], tr)
    dw_up = _dw(n2, dup, "dw_up", UPQ, trw, by_chip=True)
    ffn_parts = [dw_up, dw_down.reshape(N_CHIPS, -1, D)]
    dh1, dmix, dg_pre_ffn, dg_post_mix, *got = _ffn_bwd_in(dup, W["w_up"], h1, dh2, mix, W["norm_pre_ffn"],
                                                            W["norm_post_mix"], tr,
                                                            dist and dist.swap_hook(FFN_SHARES, ffn_parts))
    ffn_sums = dist and dist.pair_sums(FFN_SHARES, ffn_parts, "ffn", swapped=got)
    dao, dco, dgates, dattn, dc1, db_cp, dlg, dlb, dcb = _mix_bwd(
        dmix, ao, co, gates, c1, W["w_out"], W["w_attn_proj"], W["w_conv_proj"], W["conv_ln_g"], W["conv_ln_b"], tr)
    dglu_a, dglu_g, dcw, *ffn_got = _conv_bwd(dc1, glu, W["tiny"], tr, dist and dist.chip_hook(FFN_SHARES, ffn_sums))
    branch = [_dw(attn, dao, "dw_attn_proj", D, trw), _dw(c3, dco, "dw_conv_proj", D, trw),
              _dw(merged, dmix, "dw_out", D, trw),
              _tiny_pack({"conv_dw_w": dcw, "ffn_dw_w": dfw, "meta_tokens": jnp.zeros((N_CHIPS, NMETA, DQ), F32)})]
    branch_sums = dist and dist.pair_sums(BRANCH_TINY_SHARES, [a.reshape(N_CHIPS, -1, a.shape[-1]) for a in branch], "branch")
    dqkv, dsink, *got = _attn_bwd(qkv, dattn, lse, sinks, rope,
                                  dist and dist.chip_hook(BRANCH_TINY_SHARES, branch_sums))
    if dist:
        dist.finish(FFN_SHARES + BRANCH_TINY_SHARES, ffn_sums + branch_sums, ffn_got + got, "ffn_branch")
    dproj = [dqkv, dglu_a, dglu_g, dgates]
    dw_in, row0 = None, 0
    for j, p in enumerate(dproj):
        ta = _tile(p.shape[1], D, BLK) if j == 0 else 2 * BLK
        dw_in = _dw(p, n1, "dw_in_%d" % j, D, _tile(R, 2 * trw), ta=ta, rows_of=IN_W, row0=row0, into=dw_in)
        row0 += p.shape[1]
    in_sums = dist and dist.pair_sums(IN_SHARES, [dw_in.reshape(N_CHIPS, -1, D)], "in")
    dh0, dg_pre_mix, db_in, *got = _in_bwd(dproj, W["w_in"], h0, dh1, W["norm_pre_mix"], tr,
                                           dist and dist.chip_hook(IN_SHARES, in_sums))
    if dist:
        dist.finish(IN_SHARES, in_sums, got, "in")

    grads = {
        "w_in": dw_in, "w_attn_proj": branch[0], "w_conv_proj": branch[1], "w_out": branch[2],
        "w_up": dw_up,
        "w_down": dw_down,
        "tiny": branch[3],
        "meta_tokens": dh0[PAD:BLK],
        "norm_pre_mix": dg_pre_mix, "norm_post_mix": dg_post_mix, "b_in": db_in,
        "attn_sinks": dsink[0:1, 0:NH], "conv_dw_b": dcb, "conv_ln_g": dlg, "conv_ln_b": dlb,
        "b_conv_proj": db_cp, "norm_pre_ffn": dg_pre_ffn, "norm_post_ffn": dg_post_ffn, "ffn_dw_b": dfb,
    }
    return loss_cols, dh0[BLK:], grads


INQ = IN_W // N_CHIPS
DQ = D // N_CHIPS
SHARES = [("w_in", INQ, D, BF16), ("w_attn_proj", DQ, D, BF16), ("w_conv_proj", DQ, D, BF16), ("w_out", DQ, D, BF16),
          ("w_up", D, UPQ, BF16), ("w_down", FFN // N_CHIPS, D, BF16), ("tiny", TINY_ROWS, UPQ, F32)]
TINY_PARTS = [("conv_dw_w", TINY_CONV, CONV_K, TINY_FFN - TINY_CONV, DQ), ("ffn_dw_w", TINY_FFN, FFN_K, TINY_META - TINY_FFN, UPQ),
              ("meta_tokens", TINY_META, NMETA, NMETA, DQ)]


def _tiny_pack(parts):
    rows = []
    for name, _, _, reserved, _ in TINY_PARTS:
        a = parts[name].astype(F32)
        pad = [(0, 0)] * (a.ndim - 2) + [(0, reserved - a.shape[-2]), (0, UPQ - a.shape[-1])]
        rows.append(jnp.pad(a, pad))
    used = sum(r.shape[-2] for r in rows)
    rows.append(jnp.zeros(rows[0].shape[:-2] + (TINY_ROWS - used, UPQ), F32))
    return jnp.concatenate(rows, axis=-2)


def _tiny_unpack(tiny):
    return {name: tiny[..., r0:r0 + k, 0:cols] for name, r0, k, _, cols in TINY_PARTS}


def _cols_by_chip(a):
    rows, n = a.shape
    return a.reshape(rows, N_CHIPS, n // N_CHIPS).transpose(1, 0, 2)


def _cols_joined(a):
    _, rows, cols = a.shape
    return a.transpose(1, 0, 2).reshape(rows, N_CHIPS * cols)


def _to_planes(a, rows):
    return jnp.pad(a, [(0, rows * D - a.shape[-1])]).reshape(rows, D)


ANY = pl.BlockSpec(memory_space=pl.ANY)


def _place():
    x, y, c = lax.axis_index("x"), lax.axis_index("y"), lax.axis_index("c")
    chips = [(1 - x, y), (x, 1 - y), (1 - x, 1 - y)]
    return x, y, c, chips


def _rcopy(src, dst, ssem, rsem, to):
    return pltpu.make_async_remote_copy(src_ref=src, dst_ref=dst, send_sem=ssem, recv_sem=rsem,
                                        device_id=to, device_id_type=MESH)


def _halves(ref_or_rows, c):
    half = ref_or_rows // 2
    return pl.ds(c * half, half), pl.ds((1 - c) * half, half)


FIRST_SHARES, BRANCH_SHARES, FFN_SHARES = [0, 6], [1, 2, 3], [4, 5]
IN_SHARES, BRANCH_TINY_SHARES = [0], [1, 2, 3, 6]
GATHER_IN_PROJ, GATHER_ATTN, GATHER_CONV = [1, 2, 3], [4], [5]


def _gather_hook(own, idx):
    n = len(idx)

    def copies(kind, ins, outs, ssem, rsem):
        x, y, c, chips = _place()
        q = 2 * x + y
        sib = (x, y, 1 - c)
        out = []
        for i, a in enumerate(idx):
            mine, other = _halves(SHARES[a][1], c)
            for j, (cx, cy) in enumerate(chips):
                k, to = 3 * i + j, (cx, cy, c)
                landed, theirs = outs[i].at[2 * cx + cy, mine], outs[i].at[2 * cx + cy, other]
                if kind == "send":
                    out.append(_rcopy(ins[i].at[mine], outs[i].at[q, mine], ssem.at[k], rsem.at[k], to))
                elif kind == "landing":
                    out.append(_rcopy(ins[i].at[mine], landed, ssem.at[k], rsem.at[k], to))
                elif kind == "pass":
                    out.append(_rcopy(landed, landed, ssem.at[3 * n + k], rsem.at[3 * n + k], sib))
                else:
                    out.append(_rcopy(theirs, theirs, ssem.at[3 * n + k], rsem.at[3 * n + k], sib))
        return out

    def own_copies(ins, outs, ssem, rsem):
        x, y, c, _ = _place()
        q = 2 * x + y
        return [_rcopy(ins[i], outs[i].at[q], ssem.at[6 * n + i], rsem.at[6 * n + i], (x, y, 1 - c)) for i in range(n)]

    def start(*refs):
        for cp in copies("send", *refs) + own_copies(*refs):
            cp.start()

    def mid(*refs):
        for landed, cp in zip(copies("landing", *refs), copies("pass", *refs)):
            landed.wait_recv()
            cp.start()

    def finish(*refs):
        for cp in copies("arrival", *refs):
            cp.wait_recv()
        for cp in copies("send", *refs) + copies("pass", *refs):
            cp.wait_send()
        for cp in own_copies(*refs):
            cp.wait()

    shapes = [_sds((N_CHIPS,) + SHARES[a][1:3], SHARES[a][3]) for a in idx]
    return _Hook(own, shapes, 7 * n, start, finish, mid)


def _chip_hook(sums, idx):
    def copies(ins, outs, ssem, rsem):
        x, y, c, chips = _place()
        return [_rcopy(ins[i].at[2 * cx + cy], outs[i].at[j], ssem.at[3 * i + j], rsem.at[3 * i + j], (cx, cy, c))
                for i in range(len(idx)) for j, (cx, cy) in enumerate(chips)]

    def start(*refs):
        for cp in copies(*refs):
            cp.start()

    def finish(*refs):
        for cp in copies(*refs):
            cp.wait()

    shapes = [_sds((N_CHIPS - 1, SHARES[a][1] // 2, SHARES[a][2]), SHARES[a][3]) for a in idx]
    return _Hook(sums, shapes, 3 * len(idx), start, finish)


def _swap_hook(parts, idx):
    def copies(ins, outs, ssem, rsem):
        x, y, c, _ = _place()
        return [_rcopy(ins[i].at[:, _halves(SHARES[a][1], c)[1]], outs[i], ssem.at[i], rsem.at[i], (x, y, 1 - c))
                for i, a in enumerate(idx)]

    def start(*refs):
        for cp in copies(*refs):
            cp.start()

    def finish(*refs):
        for cp in copies(*refs):
            cp.wait()

    shapes = [_sds((N_CHIPS, SHARES[a][1] // 2, SHARES[a][2]), SHARES[a][3]) for a in idx]
    return _Hook(parts, shapes, len(idx), start, finish)


def _sum_pair(parts, recvs, c, idx, tag):
    steps, n = 2, len(idx)

    def body(c_ref, *refs):
        for i, a in enumerate(idx):
            refs[2 * n + i][...] = (refs[i][...].astype(F32) + refs[n + i][...].astype(F32)).astype(SHARES[a][3])

    own, got, out, views, shapes = [], [], [], [], []
    for p, a in zip(parts, idx):
        _, rows, cols, dt = SHARES[a]
        blk = rows // 2 // steps
        own.append(pl.BlockSpec((None, None, blk, cols), lambda q, i, c_ref: (q, c_ref[0], i, 0)))
        got.append(pl.BlockSpec((None, blk, cols), lambda q, i, c_ref: (q, i, 0)))
        out.append(pl.BlockSpec((None, blk, cols), lambda q, i, c_ref: (q, i, 0)))
        views.append(p.reshape(N_CHIPS, 2, rows // 2, cols))
        shapes.append(_sds((N_CHIPS, rows // 2, cols), dt))
    grid_spec = pltpu.PrefetchScalarGridSpec(num_scalar_prefetch=1, grid=(N_CHIPS, steps),
                                             in_specs=own + got, out_specs=out)
    return pl.pallas_call(body, name="sum_pair_" + tag, grid_spec=grid_spec, out_shape=shapes,
                          compiler_params=_params(("arbitrary", "arbitrary")))(c, *views, *recvs)


def _sum_chips(sums, recvs, qc, idx, tag):
    steps, n = 2, len(idx)

    def body(qc_ref, *refs):
        for i in range(n):
            acc = refs[i][...].astype(F32)
            for j in range(1, N_CHIPS):
                acc = acc + refs[j * n + i][...].astype(F32)
            refs[N_CHIPS * n + i][...] = acc

    own, got, out, shapes = [], [[], [], []], [], []
    for a in idx:
        _, rows, cols, _ = SHARES[a]
        blk = rows // 2 // steps
        own.append(pl.BlockSpec((None, blk, cols), lambda i, qc_ref: (qc_ref[0], i, 0)))
        for j in range(N_CHIPS - 1):
            got[j].append(pl.BlockSpec((None, blk, cols), lambda i, qc_ref, j=j: (j, i, 0)))
        out.append(pl.BlockSpec((None, blk, cols), lambda i, qc_ref: (qc_ref[1], i, 0)))
        shapes.append(_sds((2, rows // 2, cols), F32))
    grid_spec = pltpu.PrefetchScalarGridSpec(num_scalar_prefetch=1, grid=(steps,),
                                             in_specs=own + got[0] + got[1] + got[2], out_specs=out)
    return pl.pallas_call(body, name="sum_chips_" + tag, grid_spec=grid_spec, out_shape=shapes,
                          compiler_params=_params(("arbitrary",)))(qc, *sums, *recvs, *recvs, *recvs)


def _sibling_share(halves, idx, tag):
    n = len(idx)

    def body(*refs):
        outs, (ssem, rsem) = refs[n:2 * n], refs[2 * n:]
        x, y, c, _ = _place()
        copies = []
        for i in range(n):
            cp = _rcopy(outs[i].at[c], outs[i].at[c], ssem.at[i], rsem.at[i], (x, y, 1 - c))
            cp.start()
            copies.append(cp)
        for i in range(n):
            theirs = outs[i].at[1 - c]
            _rcopy(theirs, theirs, ssem.at[i], rsem.at[i], (x, y, 1 - c)).wait_recv()
        for cp in copies:
            cp.wait_send()

    return pl.pallas_call(
        body, name="sibling_share_" + tag, in_specs=[ANY] * n, out_specs=[ANY] * n,
        out_shape=[_sds((2, SHARES[a][1] // 2, SHARES[a][2]), F32) for a in idx],
        input_output_aliases={i: i for i in range(n)},
        scratch_shapes=[pltpu.SemaphoreType.DMA((n,)), pltpu.SemaphoreType.DMA((n,))],
    )(*halves)


class _Dist:
    def __init__(self, own):
        self.own = own
        self.core = lax.axis_index("c")
        self.chip = 2 * lax.axis_index("x") + lax.axis_index("y")
        self.reduced = {}

    def gather_hook(self, idx):
        return _gather_hook([self.own[a] for a in idx], idx)

    def weights(self, idx, gathered):
        out = {}
        for a, full in zip(idx, gathered):
            name = SHARES[a][0]
            out[name] = full if name in ("w_up", "tiny") else full.reshape(-1, D)
        return out

    def pair_sums(self, idx, parts, tag, swapped=None):
        if swapped is None:
            swapped = _alone(_swap_hook(parts, idx), "sibling_swap_" + tag)
        return _sum_pair(parts, swapped, self.core.reshape(1), idx, tag)

    def swap_hook(self, idx, parts):
        return _swap_hook(parts, idx)

    def chip_hook(self, idx, sums):
        return _chip_hook(sums, idx)

    def finish(self, idx, sums, recvs, tag):
        halves = _sum_chips(sums, recvs, jnp.stack([self.chip, self.core]), idx, tag)
        for a, full in zip(idx, _sibling_share(halves, idx, tag)):
            self.reduced[SHARES[a][0]] = full.reshape(SHARES[a][1:3])


N_DEV = 8
SMALL_ROWS = 40


def _small_allreduce(sm):
    def body(s_ref, o_ref, buf, ssem, rsem):
        x, y, c, _ = _place()
        me = 4 * x + 2 * y + c
        buf[me] = s_ref[...]
        copies = []
        for d in range(1, N_DEV):
            dx, dy, dc = d >> 2, (d >> 1) & 1, d & 1
            to = (x ^ dx, y ^ dy, c ^ dc)
            cp = _rcopy(s_ref, buf.at[me], ssem.at[d - 1], rsem.at[d - 1], to)
            cp.start()
            copies.append(cp)
        for d in range(1, N_DEV):
            src = me ^ d
            _rcopy(s_ref, buf.at[src], ssem.at[d - 1], rsem.at[d - 1], (x, y, c)).wait_recv()
        for cp in copies:
            cp.wait_send()
        acc = buf[0]
        for k in range(1, N_DEV):
            acc = acc + buf[k]
        o_ref[...] = acc

    vm = pl.BlockSpec(memory_space=pltpu.VMEM)
    return pl.pallas_call(
        body, name="small_allreduce", in_specs=[vm], out_specs=vm,
        out_shape=_sds((SMALL_ROWS, D), F32),
        scratch_shapes=[pltpu.VMEM((N_DEV, SMALL_ROWS, D), F32),
                        pltpu.SemaphoreType.DMA((N_DEV - 1,)), pltpu.SemaphoreType.DMA((N_DEV - 1,))],
    )(sm)


SMALL_PLAN = [("norm_pre_mix", D), ("norm_post_mix", D), ("b_in", IN_W), ("attn_sinks", NH), ("conv_dw_b", D),
              ("conv_ln_g", D), ("conv_ln_b", D), ("b_conv_proj", D), ("norm_pre_ffn", D), ("norm_post_ffn", D),
              ("ffn_dw_b", 2 * FFN), ("loss", D), ("meta_tokens", NMETA * D)]


def _pack_small(parts):
    rows = [_to_planes(parts[name].reshape(-1), -(-n // D)) for name, n in SMALL_PLAN]
    used = sum(r.shape[0] for r in rows)
    return jnp.concatenate(rows + [jnp.zeros((SMALL_ROWS - used, D), F32)], 0)


def _unpack_small(packed):
    out, r0 = {}, 0
    for name, n in SMALL_PLAN:
        rows = -(-n // D)
        out[name] = packed[r0:r0 + rows].reshape(-1)[:n].reshape(1, n)
        r0 += rows
    return out


def _adamw_update(w_ref, g_ref, m_ref, v_ref, d_ref, nm_ref, nv_ref):
    g = g_ref[...]
    m = B1 * m_ref[...] + (1.0 - B1) * g
    v = B2 * v_ref[...] + (1.0 - B2) * (g * g)
    nm_ref[...] = m
    nv_ref[...] = v
    m_hat = m / (1.0 - B1 ** STEP)
    v_hat = v / (1.0 - B2 ** STEP)
    d_ref[...] = -LR * (m_hat / (jnp.sqrt(v_hat) + ADAM_EPS) + WD * w_ref[...])


def _adamw_vectors(ws, gs, ms, vs):
    n = len(ws)

    def body(*refs):
        for j in range(n):
            _adamw_update(*[refs[k * n + j] for k in range(7)])

    vm = pl.BlockSpec(memory_space=pltpu.VMEM)
    outs = pl.pallas_call(body, name="adamw_vectors", in_specs=[vm] * (4 * n), out_specs=[vm] * (3 * n),
                          out_shape=[_sds(w.shape, F32) for w in ws] * 3)(*ws, *gs, *ms, *vs)
    return outs[:n], outs[n:2 * n], outs[2 * n:]


def _adamw(w, g, m, v, name):
    rows, cols = w.shape
    tr = _tile(rows, 256, 8) if rows % 8 == 0 else rows

    def body(*refs):
        _adamw_update(*refs)

    spec = pl.BlockSpec((tr, cols), lambda i: (i, 0))
    return pl.pallas_call(
        body, name=name, grid=(rows // tr,), in_specs=[spec] * 4, out_specs=[spec] * 3,
        out_shape=[_sds((rows, cols), F32)] * 3, compiler_params=_params(("arbitrary",)),
    )(w, g, m, v)


NAMES = ["meta_tokens", "norm_pre_mix", "norm_post_mix", "w_in", "b_in", "attn_sinks", "w_attn_proj", "conv_dw_w",
         "conv_dw_b", "conv_ln_g", "conv_ln_b", "w_conv_proj", "b_conv_proj", "w_out", "norm_pre_ffn", "norm_post_ffn",
         "w_up", "ffn_dw_w", "ffn_dw_b", "w_down"]
MATMUL = ("w_in", "w_attn_proj", "w_conv_proj", "w_out", "w_up", "w_down")


def _two_d(a):
    return a.reshape(a.shape[-2:])


def kernel(x, meta_tokens, norm_pre_mix, norm_post_mix, w_in, b_in, attn_sinks, w_attn_proj, conv_dw_w, conv_dw_b, conv_ln_g, conv_ln_b, w_conv_proj, b_conv_proj, w_out, norm_pre_ffn, norm_post_ffn, w_up, ffn_dw_w, ffn_dw_b, w_down, loss_target, m_meta_tokens, m_norm_pre_mix, m_norm_post_mix, m_w_in, m_b_in, m_attn_sinks, m_w_attn_proj, m_conv_dw_w, m_conv_dw_b, m_conv_ln_g, m_conv_ln_b, m_w_conv_proj, m_b_conv_proj, m_w_out, m_norm_pre_ffn, m_norm_post_ffn, m_w_up, m_ffn_dw_w, m_ffn_dw_b, m_w_down, v_meta_tokens, v_norm_pre_mix, v_norm_post_mix, v_w_in, v_b_in, v_attn_sinks, v_w_attn_proj, v_conv_dw_w, v_conv_dw_b, v_conv_ln_g, v_conv_ln_b, v_w_conv_proj, v_b_conv_proj, v_w_out, v_norm_pre_ffn, v_norm_post_ffn, v_w_up, v_ffn_dw_w, v_ffn_dw_b, v_w_down):
    args = locals()
    w = {n: args[n] for n in NAMES}
    m = {n: args["m_" + n] for n in NAMES}
    v = {n: args["v_" + n] for n in NAMES}
    tiny_names = [part[0] for part in TINY_PARTS]
    big = list(MATMUL) + tiny_names

    def shard_2d(a, name):
        return _two_d(a).T if name == "w_in" else _two_d(a)

    own = {n: shard_2d(w[n], n).astype(BF16) for n in MATMUL}
    own["tiny"] = _tiny_pack({n: _two_d(w[n]) for n in tiny_names})
    dist = _Dist([own[n] for n, _, _, _ in SHARES])
    W = {n: _two_d(w[n]) for n in SMALL}
    W.update(dist.weights(FIRST_SHARES, _alone(dist.gather_hook(FIRST_SHARES), "gather_first")))

    loss_cols, grad_x, grads = local_step(x[0], loss_target[0], W, dist)

    small = dict(grads)
    small["loss"] = loss_cols
    g_small = _unpack_small(_small_allreduce(_pack_small(small)))
    loss = jnp.sum(g_small["loss"])
    g_big = {n: dist.reduced[n] for n in MATMUL}
    g_big.update(_tiny_unpack(dist.reduced["tiny"]))
    g_big["meta_tokens"] = lax.dynamic_slice(g_small["meta_tokens"].reshape(NMETA, D), (0, dist.chip * DQ), (NMETA, DQ))

    g, delta, new_m, new_v = {}, {}, {}, {}
    for n in big:
        shape = w[n].shape
        back = (lambda a: a.T.reshape(shape)) if n == "w_in" else (lambda a: a.reshape(shape))
        outs = _adamw(shard_2d(w[n], n), g_big[n], shard_2d(m[n], n), shard_2d(v[n], n), "adamw_" + n)
        g[n], delta[n], new_m[n], new_v[n] = (back(a) for a in (g_big[n], *outs))
    ud, um, uv = _adamw_vectors(*[[_two_d(d[n]) for n in SMALL] for d in (w, g_small, m, v)])
    for j, n in enumerate(SMALL):
        g[n], delta[n], new_m[n], new_v[n] = g_small[n], ud[j], um[j], uv[j]

    return (loss, grad_x[None], *[g[n] for n in NAMES], *[delta[n] for n in NAMES],
            *[new_m[n] for n in NAMES], *[new_v[n] for n in NAMES])
```

```python
import jax
import jax.numpy as jnp
from jax import lax
from jax.experimental import pallas as pl
from jax.experimental.pallas import tpu as pltpu

F32, BF16 = jnp.float32, jnp.bfloat16

D = 1024
NH, NKV, HD = 16, 2, 64
GH = NH // NKV
NMETA, BLK = 16, 128
PAD = BLK - NMETA
ROT = HD // 4
THETA = 500000.0
CONV_K = 31
FFN = 2816
FFN_K = 3
IN_W = 5376
QKV_W, GLU_W, GATE_W = 1280, 2048, 2048
RMS_EPS, LN_EPS, NEG = 1e-6, 1e-5, -1e30
LR, B1, B2, ADAM_EPS, WD, STEP = 0.001, 0.9, 0.999, 1e-08, 0.01, 10

VMEM_LIMIT = 56 * 2 ** 20
MESH = pl.DeviceIdType.MESH

NT_DIMS = (((1,), (1,)), ((), ()))
TN_DIMS = (((0,), (0,)), ((), ()))


def _params(sem, **kw):
    return pltpu.CompilerParams(dimension_semantics=sem, vmem_limit_bytes=VMEM_LIMIT, **kw)


def _tile(n, pref, mult=16):
    for t in range(min(pref, n), 0, -1):
        if n % t == 0 and t % mult == 0:
            return t
    return n


def _row(tr, w, col=0):
    return pl.BlockSpec((tr, w), lambda i: (i, col))


def _rrow(tr, w, nt, col=0):
    return pl.BlockSpec((tr, w), lambda t: (nt - 1 - t, col))


def _const(shape):
    return pl.BlockSpec(shape, lambda *_: (0,) * len(shape))


def _sds(shape, dt):
    return jax.ShapeDtypeStruct(shape, dt)


class _Hook:
    def __init__(self, operands, out_shape, n_sem, start, finish, mid=None):
        self.operands, self.out_shape, self.n_sem = list(operands), list(out_shape), n_sem
        self.start, self.mid, self.finish = start, mid, finish

    def scratch(self):
        return [pltpu.SemaphoreType.DMA((self.n_sem,)), pltpu.SemaphoreType.DMA((self.n_sem,))]


def _call(body, hook, steps, *, name, grid, in_specs, out_specs, out_shape, operands, semantics, scratch_shapes=()):
    in_specs, out_specs, out_shape = list(in_specs), list(out_specs), list(out_shape)
    if hook is None:
        return pl.pallas_call(body, name=name, grid=grid, in_specs=in_specs, out_specs=out_specs, out_shape=out_shape,
                              scratch_shapes=list(scratch_shapes), compiler_params=_params(semantics))(*operands)
    n_in, n_out, n_hi, n_ho = len(in_specs), len(out_specs), len(hook.operands), len(hook.out_shape)

    def wrapped(*refs):
        ins, hi = refs[:n_in], refs[n_in:n_in + n_hi]
        o0 = n_in + n_hi
        outs, ho = refs[o0:o0 + n_out], refs[o0 + n_out:o0 + n_out + n_ho]
        scratch, (ssem, rsem) = refs[o0 + n_out + n_ho:len(refs) - 2], refs[len(refs) - 2:]
        first, middle, last = steps()

        @pl.when(first)
        def _():
            hook.start(hi, ho, ssem, rsem)

        body(*ins, *outs, *scratch)
        if hook.mid is not None:
            @pl.when(middle)
            def _():
                hook.mid(hi, ho, ssem, rsem)

        @pl.when(last)
        def _():
            hook.finish(hi, ho, ssem, rsem)

    any_spec = pl.BlockSpec(memory_space=pl.ANY)
    return pl.pallas_call(
        wrapped, name=name, grid=grid, in_specs=in_specs + [any_spec] * n_hi, out_specs=out_specs + [any_spec] * n_ho,
        out_shape=out_shape + hook.out_shape, scratch_shapes=list(scratch_shapes) + hook.scratch(),
        compiler_params=_params(semantics))(*operands, *hook.operands)


def _alone(hook, name):
    n_hi = len(hook.operands)

    def body(*refs):
        hi, ho, (ssem, rsem) = refs[:n_hi], refs[n_hi:len(refs) - 2], refs[len(refs) - 2:]
        hook.start(hi, ho, ssem, rsem)
        if hook.mid is not None:
            hook.mid(hi, ho, ssem, rsem)
        hook.finish(hi, ho, ssem, rsem)

    any_spec = pl.BlockSpec(memory_space=pl.ANY)
    return pl.pallas_call(body, name=name, in_specs=[any_spec] * n_hi, out_specs=[any_spec] * len(hook.out_shape),
                          out_shape=hook.out_shape, scratch_shapes=hook.scratch())(*hook.operands)


def _steps_1d(n, mid):
    def steps():
        i = pl.program_id(0)
        return i == 0, i == min(mid, n - 1), i == n - 1
    return steps


def _rms(x, g):
    r = lax.rsqrt(jnp.mean(x * x, -1, keepdims=True) + RMS_EPS)
    return x * r * g, r


def _rms_bwd(dy, x, r, g):
    gy = dy * g
    return r * gy - x * (r * r * r) * jnp.mean(x * gy, -1, keepdims=True)


def _colsum(x):
    return jnp.sum(x, axis=0, keepdims=True)


def _rope(x, c, sa, sb):
    n = x.shape[1]
    return x * c + pltpu.roll(x, n - 8, 1) * sa + pltpu.roll(x, 8, 1) * sb


def _rope_bwd(d, c, sa, sb):
    n = d.shape[1]
    return d * c + pltpu.roll(d * sa, 8, 1) + pltpu.roll(d * sb, n - 8, 1)


def _rope_tables(R):
    half = ROT // 2
    lane = jnp.arange(2 * HD) % HD
    inv = THETA ** (-(lane % half).astype(F32) * 2.0 / ROT)
    pos = (jnp.arange(R) - PAD).astype(F32)
    ang = pos[:, None] * inv[None, :]
    cos, sin = jnp.cos(ang), jnp.sin(ang)
    c = jnp.where(lane < ROT, cos, 1.0)
    sa = jnp.where(lane < half, -sin, 0.0)
    sb = jnp.where((lane >= half) & (lane < ROT), sin, 0.0)
    return c, sa, sb


IN_CHUNKS = ([(0, 512, True), (512, 1024, True), (1024, 1152, True), (1152, 1280, False)]
             + [(c, c + 512, False) for c in range(1280, IN_W, 512)])


def _in_proj(h0, g_pre, w_in, b_in, rope, tr, hook=None):
    R = h0.shape[0]
    nt = R // tr

    def body(h_ref, g_ref, w_ref, b_ref, c_ref, sa_ref, sb_ref, qkv_ref, glu_ref, gate_ref, n1_ref):
        n, _ = _rms(h_ref[...], g_ref[...])
        nb = n.astype(BF16)
        n1_ref[...] = nb
        for c0, c1, rot in IN_CHUNKS:
            acc = lax.dot_general(nb, w_ref[c0:c1, :], NT_DIMS, preferred_element_type=F32) + b_ref[:, c0:c1]
            if rot:
                reps = (c1 - c0) // 128
                acc = _rope(acc, jnp.tile(c_ref[...], (1, reps)), jnp.tile(sa_ref[...], (1, reps)),
                            jnp.tile(sb_ref[...], (1, reps)))
            val = acc.astype(BF16)
            if c1 <= QKV_W:
                qkv_ref[:, c0:c1] = val
            elif c1 <= QKV_W + GLU_W:
                glu_ref[:, c0 - QKV_W:c1 - QKV_W] = val
            else:
                gate_ref[:, c0 - QKV_W - GLU_W:c1 - QKV_W - GLU_W] = val

    return _call(
        body, hook, _steps_1d(nt, (3 * nt) // 4), name="in_proj", grid=(nt,),
        in_specs=[_row(tr, D), _const((1, D)), _const((IN_W, D)), _const((1, IN_W)),
                  _row(tr, 128), _row(tr, 128), _row(tr, 128)],
        out_specs=[_row(tr, QKV_W), _row(tr, GLU_W), _row(tr, GATE_W), _row(tr, D)],
        out_shape=[_sds((R, QKV_W), BF16), _sds((R, GLU_W), BF16), _sds((R, GATE_W), BF16), _sds((R, D), BF16)],
        operands=(h0, g_pre, w_in, b_in, *rope), semantics=("arbitrary",))


def _attn_mask(n, keys_first=False):
    shape = (3 * BLK, BLK) if keys_first else (BLK, 3 * BLK)
    qi = lax.broadcasted_iota(jnp.int32, shape, 1 if keys_first else 0)
    kj = lax.broadcasted_iota(jnp.int32, shape, 0 if keys_first else 1)
    tq = n * BLK + qi - PAD
    t_meta = kj - PAD
    t_loc = (n - 1) * BLK + (kj - BLK) - PAD
    meta_ok = (kj < BLK) & (t_meta >= 0) & (t_meta <= tq)
    loc_ok = (kj >= BLK) & (t_loc >= NMETA) & (t_loc <= tq) & (tq - t_loc < BLK)
    return meta_ok | loc_ok


def _dup_heads(ref0, refp, refc, low, transposed=False):
    a = jnp.concatenate([ref0[...], refp[...], refc[...]], 0).astype(F32)
    sw = pltpu.roll(a, HD, 1)
    heads = [jnp.where(low, a, sw), jnp.where(low, sw, a)]
    return [(h.T if transposed else h).astype(BF16) for h in heads]


def _kv_specs(nb, rev):
    def blk(col, which):
        def idx(t):
            n = nb - 1 - t if rev else t
            return ({"meta": 0, "prev": jnp.maximum(n - 1, 0), "own": n}[which], col)
        return pl.BlockSpec((BLK, BLK), idx)
    return [blk(col, w) for col in (8, 9) for w in ("meta", "prev", "own")]


def _attn_fwd(qkv, sinks, hook=None):
    R = qkv.shape[0]
    nb = R // BLK

    def body(s_ref, q_ref, k0, kp, kc, v0, vp, vc, o_ref, lse_ref):
        n = pl.program_id(0)
        lane = lax.broadcasted_iota(jnp.int32, (1, BLK), 1)
        low = lane < HD
        kd, vd_t = _dup_heads(k0, kp, kc, low), _dup_heads(v0, vp, vc, low, transposed=True)
        mask = _attn_mask(n, keys_first=True)
        zero = jnp.zeros((), BF16)
        lses = []
        for g in range(NKV):
            tiles = []
            for pair in range(GH // 2 * g, GH // 2 * (g + 1)):
                qp = q_ref[:, pair * BLK:(pair + 1) * BLK] * jnp.asarray(HD ** -0.5, BF16)
                tiles += [jnp.where(low, qp, zero), jnp.where(low, zero, qp)]
            st = lax.dot_general(kd[g], jnp.concatenate(tiles, 0), NT_DIMS, preferred_element_type=F32)
            ps, inv = [], []
            for j in range(GH):
                s = jnp.where(mask, st[:, j * BLK:(j + 1) * BLK], NEG)
                sk = s_ref[GH * g + j]
                m = jnp.maximum(jnp.max(s, 0, keepdims=True), sk)
                p = jnp.exp(s - m)
                l = jnp.sum(p, 0, keepdims=True) + jnp.exp(sk - m)
                ps.append(p.astype(BF16))
                inv.append(1.0 / l)
                lses.append(m + jnp.log(l))
            ot = jnp.dot(vd_t[g], jnp.concatenate(ps, 1), preferred_element_type=F32)
            for j in range(GH // 2):
                pair = GH // 2 * g + j
                o = [(ot[:, h * BLK:(h + 1) * BLK] * inv[h]).T for h in (2 * j, 2 * j + 1)]
                o_ref[:, pair * BLK:(pair + 1) * BLK] = jnp.where(low, o[0], o[1]).astype(BF16)
        lse_ref[...] = jnp.concatenate(lses, 0)

    return _call(
        body, hook, _steps_1d(nb, max(nb - 4, 0)), name="attn_fwd", grid=(nb,),
        in_specs=[pl.BlockSpec(memory_space=pltpu.SMEM), pl.BlockSpec((BLK, D), lambda n: (n, 0))] + _kv_specs(nb, False),
        out_specs=[_row(BLK, D), _row(NH, BLK)],
        out_shape=[_sds((R, D), BF16), _sds((nb * NH, BLK), F32)],
        operands=(sinks, qkv, *([qkv] * 6)), semantics=("arbitrary",))


CONV_TCH, CONV_SUB, HALO = 256, 64, 32


def _tap_windows(buf, r0, offset_of):
    span = CONV_SUB + HALO
    x = buf[pl.ds(r0, span), :]
    by_phase = {}
    for k in range(CONV_K):
        by_phase.setdefault(offset_of(k) % 8, []).append(k)
    for phase, taps in sorted(by_phase.items()):
        y = x if phase == 0 else pltpu.roll(x, span - phase, 0)
        for k in taps:
            d = offset_of(k) - phase
            yield k, y[d:d + CONV_SUB, :]


def _conv_fwd(glu, w, b, tr, hook=None):
    R = glu.shape[0]
    nc = D // CONV_TCH

    def body(a_ref, g_ref, w_ref, b_ref, o_ref, buf):
        i = pl.program_id(1)

        @pl.when(i == 0)
        def _():
            buf[0:HALO, :] = jnp.zeros((HALO, CONV_TCH), F32)

        @pl.when(i > 0)
        def _():
            buf[0:HALO, :] = buf[tr:tr + HALO, :]

        row = i * tr + lax.broadcasted_iota(jnp.int32, (tr, 1), 0)
        a, g = a_ref[...].astype(F32), g_ref[...].astype(F32)
        buf[HALO:HALO + tr, :] = jnp.where(row >= PAD, a * jax.nn.sigmoid(g), 0.0)
        for r0 in range(0, tr, CONV_SUB):
            acc = jnp.broadcast_to(b_ref[...], (CONV_SUB, CONV_TCH))
            for k, win in _tap_windows(buf, r0, lambda k: HALO - (CONV_K - 1) + k):
                acc = acc + w_ref[k:k + 1, :] * win
            o_ref[r0:r0 + CONV_SUB, :] = acc.astype(BF16)

    nt = R // tr

    def steps():
        c, i = pl.program_id(0), pl.program_id(1)
        return (c == 0) & (i == 0), (c == nc - 1) & (i == 0), (c == nc - 1) & (i == nt - 1)

    return _call(
        body, hook, steps, name="conv_fwd", grid=(nc, nt),
        in_specs=[pl.BlockSpec((tr, CONV_TCH), lambda c, i: (i, c)),
                  pl.BlockSpec((tr, CONV_TCH), lambda c, i: (i, nc + c)),
                  pl.BlockSpec((None, HALO, CONV_TCH), lambda c, i: (c, 0, 0)),
                  pl.BlockSpec((1, CONV_TCH), lambda c, i: (0, c))],
        out_specs=[pl.BlockSpec((tr, CONV_TCH), lambda c, i: (i, c))],
        out_shape=[_sds((R, D), BF16)],
        scratch_shapes=[pltpu.VMEM((tr + HALO, CONV_TCH), F32)],
        operands=(glu, glu, w, b), semantics=("arbitrary", "arbitrary"))


def _ln_silu(c1, lg, lb):
    mu = jnp.mean(c1, -1, keepdims=True)
    xc = c1 - mu
    rs = lax.rsqrt(jnp.mean(xc * xc, -1, keepdims=True) + LN_EPS)
    xh = xc * rs
    c2 = xh * lg + lb
    sg = jax.nn.sigmoid(c2)
    return xh, rs, c2, sg


def _mix_out(attn, c1, gates, h0, w_ap, w_cp, w_out, lg, lb, b_cp, g_post, g_ffn, tr):
    R = attn.shape[0]

    def body(at_ref, c1_ref, ga_ref, gc_ref, h0_ref, wap, wcp, wo, lg_ref, lb_ref, bcp, gp, gf,
             ao_ref, co_ref, c3_ref, mg_ref, mix_ref, h1_ref, n2_ref):
        ao = jnp.dot(at_ref[...], wap[...], preferred_element_type=F32)
        _, _, c2, sg = _ln_silu(c1_ref[...].astype(F32), lg_ref[...], lb_ref[...])
        c3 = (c2 * sg).astype(BF16)
        c3_ref[...] = c3
        co = jnp.dot(c3, wcp[...], preferred_element_type=F32) + bcp[...]
        ao_b, co_b = ao.astype(BF16), co.astype(BF16)
        ao_ref[...] = ao_b
        co_ref[...] = co_b
        merged = (jax.nn.sigmoid(ga_ref[...].astype(F32)) * ao_b.astype(F32)
                  + jax.nn.sigmoid(gc_ref[...].astype(F32)) * co_b.astype(F32)).astype(BF16)
        mg_ref[...] = merged
        mix = jnp.dot(merged, wo[...], preferred_element_type=F32).astype(BF16)
        mix_ref[...] = mix
        y, _ = _rms(mix.astype(F32), gp[...])
        h1 = h0_ref[...] + y
        h1_ref[...] = h1
        n2, _ = _rms(h1, gf[...])
        row = pl.program_id(0) * tr + lax.broadcasted_iota(jnp.int32, (tr, 1), 0)
        n2_ref[...] = jnp.where(row >= PAD, n2, 0.0).astype(BF16)

    vec = _const((1, D))
    return pl.pallas_call(
        body, name="mix_out", grid=(R // tr,),
        in_specs=[_row(tr, D), _row(tr, D), _row(tr, D, 0), _row(tr, D, 1), _row(tr, D),
                  _const((D, D)), _const((D, D)), _const((D, D)), vec, vec, vec, vec, vec],
        out_specs=[_row(tr, D)] * 7,
        out_shape=[_sds((R, D), BF16)] * 5 + [_sds((R, D), F32), _sds((R, D), BF16)],
        compiler_params=_params(("arbitrary",)),
    )(attn, c1, gates, gates, h0, w_ap, w_cp, w_out, lg, lb, b_cp, g_post, g_ffn)


FFN_CH = 256
N_CHIPS = 4
UPQ = 2 * FFN // N_CHIPS
UP_CHUNKS = [(q, c0, min(c0 + 512, UPQ)) for q in range(N_CHIPS // 2) for c0 in range(0, UPQ, 512)]
TINY_ROWS, TINY_CONV, TINY_FFN, TINY_META = 64, 0, 32, 40


def _shift_down(x, k, halo):
    tr = x.shape[0]
    row = lax.broadcasted_iota(jnp.int32, (tr, 1), 0)
    y = pltpu.roll(x, k, 0)
    for j in range(k):
        y = jnp.where(row == j, halo[8 - k + j:8 - k + j + 1, :], y)
    return y


def _shift_up(x, k, halo):
    tr = x.shape[0]
    row = lax.broadcasted_iota(jnp.int32, (tr, 1), 0)
    y = pltpu.roll(x, tr - k, 0)
    for j in range(k):
        y = jnp.where(row == tr - k + j, halo[j:j + 1, :], y)
    return y


def _conv3(x, halo, w, b):
    return w[2:3, :] * x + w[1:2, :] * _shift_down(x, 1, halo) + w[0:1, :] * _shift_down(x, 2, halo) + b


def _ffn_up(n2, w_up, fw, fb, tr):
    R = n2.shape[0]

    def body(n_ref, w_ref, fw_ref, fb_ref, up_ref, act_ref, carry):
        @pl.when(pl.program_id(0) == 0)
        def _():
            carry[...] = jnp.zeros_like(carry)

        nb = n_ref[...]
        for q, c0, c1 in UP_CHUNKS:
            us = []
            for qq in (q, q + N_CHIPS // 2):
                cs = slice(qq * UPQ + c0, qq * UPQ + c1)
                x = jnp.dot(nb, w_ref[qq, :, c0:c1], preferred_element_type=F32).astype(BF16)
                up_ref[:, cs] = x
                x = x.astype(F32)
                us.append(_conv3(x, carry[:, cs], fw_ref[qq, TINY_FFN:TINY_FFN + 8, c0:c1], fb_ref[:, cs]))
                carry[:, cs] = x[tr - 8:tr, :]
            act_ref[:, q * UPQ + c0:q * UPQ + c1] = (us[0] * jax.nn.sigmoid(us[0]) * us[1]).astype(BF16)

    return pl.pallas_call(
        body, name="ffn_up", grid=(R // tr,),
        in_specs=[_row(tr, D), _const((N_CHIPS, D, UPQ)), _const((N_CHIPS, TINY_ROWS, UPQ)), _const((1, 2 * FFN))],
        out_specs=[_row(tr, 2 * FFN), _row(tr, FFN)],
        out_shape=[_sds((R, 2 * FFN), BF16), _sds((R, FFN), BF16)],
        scratch_shapes=[pltpu.VMEM((8, 2 * FFN), F32)],
        compiler_params=_params(("arbitrary",)),
    )(n2, w_up, fw, fb)


def _ffn_down(act, w_down, h1, tgt, g_post, tr):
    R = act.shape[0]
    m = tr // BLK

    def body(a_ref, w_ref, h1_ref, g_ref, *rest):
        t_refs, (dh2_ref, dffn_ref, loss_ref, dg_ref) = rest[:m], rest[m:]

        @pl.when(pl.program_id(0) == 0)
        def _():
            loss_ref[...] = jnp.zeros_like(loss_ref)
            dg_ref[...] = jnp.zeros_like(dg_ref)

        f = jnp.dot(a_ref[...], w_ref[...], preferred_element_type=F32)
        g = g_ref[...]
        y, r = _rms(f, g)
        row = pl.program_id(0) * tr + lax.broadcasted_iota(jnp.int32, (tr, 1), 0)
        tgt_rows = jnp.concatenate([t[...] for t in t_refs], 0)
        e = jnp.where(row >= BLK, h1_ref[...] + y - tgt_rows, 0.0)
        loss_ref[...] += _colsum(e * e) * (0.5 / D)
        dy = e * (1.0 / D)
        dh2_ref[...] = dy
        dffn_ref[...] = _rms_bwd(dy, f, r, g).astype(BF16)
        dg_ref[...] += _colsum(dy * f * r)

    return pl.pallas_call(
        body, name="ffn_down", grid=(R // tr,),
        in_specs=[_row(tr, FFN), _const((FFN, D)), _row(tr, D), _const((1, D))]
                 + [pl.BlockSpec((BLK, D), lambda i, k=k: (jnp.maximum(m * i - 1 + k, 0), 0)) for k in range(m)],
        out_specs=[_row(tr, D), _row(tr, D), _const((1, D)), _const((1, D))],
        out_shape=[_sds((R, D), F32), _sds((R, D), BF16), _sds((1, D), F32), _sds((1, D), F32)],
        compiler_params=_params(("arbitrary",)),
    )(act, w_down, h1, g_post, *([tgt] * m))


def _ffn_bwd_act(dffn, w_down, up, fw, fb, tr):
    R = dffn.shape[0]
    nt = R // tr

    def body(d_ref, w_ref, up_ref, hal_ref, fw_ref, fb_ref, dup_ref, dfw_ref, dfb_ref, carry):
        t = pl.program_id(0)
        i = nt - 1 - t

        @pl.when(t == 0)
        def _():
            carry[...] = jnp.zeros_like(carry)
            dfw_ref[...] = jnp.zeros_like(dfw_ref)
            dfb_ref[...] = jnp.zeros_like(dfb_ref)

        dff = d_ref[...]
        row = i * tr + lax.broadcasted_iota(jnp.int32, (tr, 1), 0)
        first = i == 0
        for q, c0, c1 in UP_CHUNKS:
            dact = lax.dot_general(dff, w_ref[q * UPQ + c0:q * UPQ + c1, :], NT_DIMS, preferred_element_type=F32)
            chips = (q, q + N_CHIPS // 2)
            xs, us = [], []
            for qq in chips:
                cs = slice(qq * UPQ + c0, qq * UPQ + c1)
                x = up_ref[:, cs].astype(F32)
                halo = jnp.where(first, 0.0, hal_ref[:, cs].astype(F32))
                x1, x2 = _shift_down(x, 1, halo), _shift_down(x, 2, halo)
                w = fw_ref[qq, TINY_FFN:TINY_FFN + 8, c0:c1]
                us.append(w[2:3, :] * x + w[1:2, :] * x1 + w[0:1, :] * x2 + fb_ref[:, cs])
                xs.append((x, x1, x2))
            sg = jax.nn.sigmoid(us[0])
            silu = us[0] * sg
            dus = [dact * us[1] * sg * (1.0 + us[0] * (1.0 - sg)), dact * silu]
            for (x, x1, x2), du, qq in zip(xs, dus, chips):
                cs = slice(qq * UPQ + c0, qq * UPQ + c1)
                w = fw_ref[qq, TINY_FFN:TINY_FFN + 8, c0:c1]
                nxt = carry[:, cs]
                dx = w[2:3, :] * du + w[1:2, :] * _shift_up(du, 1, nxt) + w[0:1, :] * _shift_up(du, 2, nxt)
                dup_ref[:, cs] = jnp.where(row >= PAD, dx, 0.0).astype(BF16)
                dfw_ref[qq, 0:1, c0:c1] += _colsum(x2 * du)
                dfw_ref[qq, 1:2, c0:c1] += _colsum(x1 * du)
                dfw_ref[qq, 2:3, c0:c1] += _colsum(x * du)
                dfb_ref[:, cs] += _colsum(du)
                carry[:, cs] = du[0:8, :]

    halo_spec = pl.BlockSpec((8, 2 * FFN), lambda t: (jnp.maximum((nt - 1 - t) * (tr // 8) - 1, 0), 0))
    return pl.pallas_call(
        body, name="ffn_bwd_act", grid=(nt,),
        in_specs=[_rrow(tr, D, nt), _const((FFN, D)), _rrow(tr, 2 * FFN, nt), halo_spec,
                  _const((N_CHIPS, TINY_ROWS, UPQ)), _const((1, 2 * FFN))],
        out_specs=[_rrow(tr, 2 * FFN, nt), _const((N_CHIPS, 8, UPQ)), _const((1, 2 * FFN))],
        out_shape=[_sds((R, 2 * FFN), BF16), _sds((N_CHIPS, 8, UPQ), F32), _sds((1, 2 * FFN), F32)],
        scratch_shapes=[pltpu.VMEM((8, 2 * FFN), F32)],
        compiler_params=_params(("arbitrary",)),
    )(dffn, w_down, up, up, fw, fb)


def _ffn_bwd_in(dup, w_up, h1, dh2, mix, g_ffn, g_post, tr, hook=None):
    R = dup.shape[0]

    def body(d_ref, w_ref, h1_ref, dh2_ref, mix_ref, gf_ref, gp_ref, dh1_ref, dmix_ref, dgf_ref, dgp_ref):
        @pl.when(pl.program_id(0) == 0)
        def _():
            dgf_ref[...] = jnp.zeros_like(dgf_ref)
            dgp_ref[...] = jnp.zeros_like(dgp_ref)

        dn2 = sum(lax.dot_general(d_ref[:, q * UPQ:(q + 1) * UPQ], w_ref[q], NT_DIMS, preferred_element_type=F32)
                  for q in range(N_CHIPS))
        h1 = h1_ref[...]
        _, r2 = _rms(h1, gf_ref[...])
        dh1 = dh2_ref[...] + _rms_bwd(dn2, h1, r2, gf_ref[...])
        dgf_ref[...] += _colsum(dn2 * h1 * r2)
        dh1_ref[...] = dh1
        m = mix_ref[...].astype(F32)
        _, rm = _rms(m, gp_ref[...])
        dmix_ref[...] = _rms_bwd(dh1, m, rm, gp_ref[...]).astype(BF16)
        dgp_ref[...] += _colsum(dh1 * m * rm)

    vec = _const((1, D))
    return _call(
        body, hook, _steps_1d(R // tr, R // tr), name="ffn_bwd_in", grid=(R // tr,),
        in_specs=[_row(tr, 2 * FFN), _const((N_CHIPS, D, UPQ)), _row(tr, D), _row(tr, D), _row(tr, D), vec, vec],
        out_specs=[_row(tr, D), _row(tr, D), vec, vec],
        out_shape=[_sds((R, D), F32), _sds((R, D), BF16), _sds((1, D), F32), _sds((1, D), F32)],
        operands=(dup, w_up, h1, dh2, mix, g_ffn, g_post), semantics=("arbitrary",))


def _mix_bwd(dmix, ao, co, gates, c1, w_out, w_ap, w_cp, lg, lb, tr):
    R = dmix.shape[0]

    def body(dm_ref, ao_ref, co_ref, ga_ref, gc_ref, c1_ref, wo, wap, wcp, lg_ref, lb_ref,
             dao_ref, dco_ref, dgate_ref, dattn_ref, dc1_ref, dbcp_ref, dlg_ref, dlb_ref, dcb_ref):
        @pl.when(pl.program_id(0) == 0)
        def _():
            for ref in (dbcp_ref, dlg_ref, dlb_ref, dcb_ref):
                ref[...] = jnp.zeros_like(ref)

        dmg = lax.dot_general(dm_ref[...], wo[...], NT_DIMS, preferred_element_type=F32)
        sa = jax.nn.sigmoid(ga_ref[...].astype(F32))
        sc = jax.nn.sigmoid(gc_ref[...].astype(F32))
        dao = (dmg * sa).astype(BF16)
        dco = (dmg * sc).astype(BF16)
        dao_ref[...] = dao
        dco_ref[...] = dco
        dgate_ref[:, 0:D] = (dmg * ao_ref[...].astype(F32) * sa * (1.0 - sa)).astype(BF16)
        dgate_ref[:, D:2 * D] = (dmg * co_ref[...].astype(F32) * sc * (1.0 - sc)).astype(BF16)
        dbcp_ref[...] += _colsum(dco.astype(F32))
        dattn_ref[...] = lax.dot_general(dao, wap[...], NT_DIMS, preferred_element_type=F32).astype(BF16)
        dc3 = lax.dot_general(dco, wcp[...], NT_DIMS, preferred_element_type=F32)
        xh, rs, c2, sg = _ln_silu(c1_ref[...].astype(F32), lg_ref[...], lb_ref[...])
        dc2 = dc3 * sg * (1.0 + c2 * (1.0 - sg))
        dlg_ref[...] += _colsum(dc2 * xh)
        dlb_ref[...] += _colsum(dc2)
        dxh = dc2 * lg_ref[...]
        dc1 = rs * (dxh - jnp.mean(dxh, -1, keepdims=True) - xh * jnp.mean(dxh * xh, -1, keepdims=True))
        dc1_ref[...] = dc1
        dcb_ref[...] += _colsum(dc1)

    vec = _const((1, D))
    return pl.pallas_call(
        body, name="mix_bwd", grid=(R // tr,),
        in_specs=[_row(tr, D), _row(tr, D), _row(tr, D), _row(tr, D, 0), _row(tr, D, 1), _row(tr, D),
                  _const((D, D)), _const((D, D)), _const((D, D)), vec, vec],
        out_specs=[_row(tr, D), _row(tr, D), _row(tr, 2 * D), _row(tr, D), _row(tr, D), vec, vec, vec, vec],
        out_shape=[_sds((R, D), BF16), _sds((R, D), BF16), _sds((R, 2 * D), BF16), _sds((R, D), BF16),
                   _sds((R, D), F32)] + [_sds((1, D), F32)] * 4,
        compiler_params=_params(("arbitrary",)),
    )(dmix, ao, co, gates, gates, c1, w_out, w_ap, w_cp, lg, lb)


def _conv_bwd(dc1, glu, w, tr, hook=None):
    R = dc1.shape[0]
    nt, nc = R // tr, D // CONV_TCH

    def body(d_ref, a_ref, g_ref, w_ref, dglu_a, dglu_g, dw_ref, buf, dw_acc):
        t = pl.program_id(1)
        i = nt - 1 - t

        @pl.when(t == 0)
        def _():
            buf[tr:tr + HALO, :] = jnp.zeros((HALO, CONV_TCH), F32)
            dw_acc[...] = jnp.zeros_like(dw_acc)

        @pl.when(t > 0)
        def _():
            buf[tr:tr + HALO, :] = buf[0:HALO, :]

        buf[0:tr, :] = d_ref[...]
        for r0 in range(0, tr, CONV_SUB):
            rs = slice(r0, r0 + CONV_SUB)
            row = i * tr + r0 + lax.broadcasted_iota(jnp.int32, (CONV_SUB, 1), 0)
            a, g = a_ref[rs, :].astype(F32), g_ref[rs, :].astype(F32)
            sg = jax.nn.sigmoid(g)
            glu = jnp.where(row >= PAD, a * sg, 0.0)
            acc = jnp.zeros((CONV_SUB, CONV_TCH), F32)
            for k, win in _tap_windows(buf, r0, lambda k: CONV_K - 1 - k):
                acc = acc + w_ref[k:k + 1, :] * win
                dw_acc[k] += jnp.sum((glu * win).reshape(CONV_SUB // 8, 8, CONV_TCH), axis=0)
            dglu = jnp.where(row >= PAD, acc, 0.0)
            dglu_a[rs, :] = (dglu * sg).astype(BF16)
            dglu_g[rs, :] = (dglu * a * sg * (1.0 - sg)).astype(BF16)

        @pl.when(t == nt - 1)
        def _():
            dw_ref[...] = jnp.sum(dw_acc[...], axis=1)

    def rspec(col0):
        return pl.BlockSpec((tr, CONV_TCH), lambda c, t: (nt - 1 - t, col0 + c))

    def steps():
        c, t = pl.program_id(0), pl.program_id(1)
        return (c == 0) & (t == 0), False, (c == nc - 1) & (t == nt - 1)

    return _call(
        body, hook, steps, name="conv_bwd", grid=(nc, nt),
        in_specs=[rspec(0), rspec(0), rspec(nc), pl.BlockSpec((None, HALO, CONV_TCH), lambda c, t: (c, 0, 0))],
        out_specs=[rspec(0), rspec(0), pl.BlockSpec((None, HALO, CONV_TCH), lambda c, t: (c, 0, 0))],
        out_shape=[_sds((R, D), BF16), _sds((R, D), BF16), _sds((N_CHIPS, HALO, CONV_TCH), F32)],
        scratch_shapes=[pltpu.VMEM((tr + HALO, CONV_TCH), F32), pltpu.VMEM((HALO, 8, CONV_TCH), F32)],
        operands=(dc1, glu, glu, w), semantics=("arbitrary", "arbitrary"))


def _attn_bwd(qkv, do, lse, sinks, rope, hook=None):
    R = qkv.shape[0]
    nb = R // BLK

    def body(s_ref, q_ref, k0, kp, kc, v0, vp, vc, do_ref, lse_ref, c_ref, sa_ref, sb_ref,
             dqkv_ref, dsink_ref, car_k, car_v, met_k, met_v):
        t = pl.program_id(0)
        n = nb - 1 - t

        @pl.when(t == 0)
        def _():
            for ref in (car_k, car_v, met_k, met_v, dsink_ref):
                ref[...] = jnp.zeros_like(ref)

        lane = lax.broadcasted_iota(jnp.int32, (1, BLK), 1)
        low = lane < HD
        kd, vd = _dup_heads(k0, kp, kc, low), _dup_heads(v0, vp, vc, low)
        kd_t = _dup_heads(k0, kp, kc, low, transposed=True)
        mask = _attn_mask(n, keys_first=True)
        tabs = (c_ref[...], sa_ref[...], sb_ref[...])
        zero = jnp.zeros((), BF16)
        dk_acc, dv_acc = [], []
        dsink = jnp.zeros((1, BLK), F32)
        for g in range(NKV):
            q_tiles, do_tiles = [], []
            for pair in range(GH // 2 * g, GH // 2 * (g + 1)):
                cs = slice(pair * BLK, (pair + 1) * BLK)
                qp, dop = q_ref[:, cs] * jnp.asarray(HD ** -0.5, BF16), do_ref[:, cs]
                q_tiles += [jnp.where(low, qp, zero), jnp.where(low, zero, qp)]
                do_tiles += [jnp.where(low, dop, zero), jnp.where(low, zero, dop)]
            qs, dos = jnp.concatenate(q_tiles, 0), jnp.concatenate(do_tiles, 0)
            st = lax.dot_general(kd[g], qs, NT_DIMS, preferred_element_type=F32)
            dpt = lax.dot_general(vd[g], dos, NT_DIMS, preferred_element_type=F32)
            ps, dss = [], []
            for j in range(GH):
                h = GH * g + j
                cs = slice(j * BLK, (j + 1) * BLK)
                lse_h = lse_ref[h:h + 1, :]
                p = jnp.where(mask, jnp.exp(st[:, cs] - lse_h), 0.0)
                dp = dpt[:, cs]
                delta = jnp.sum(p * dp, 0, keepdims=True)
                ps.append(p.astype(BF16))
                dss.append((p * (dp - delta)).astype(BF16))
                dsink = dsink + jnp.where(lane == h, -jnp.sum(jnp.exp(s_ref[h] - lse_h) * delta), 0.0)
            ds_t, p_t = jnp.concatenate(dss, 1), jnp.concatenate(ps, 1)
            dk_acc.append(jnp.dot(ds_t, qs, preferred_element_type=F32))
            dv_acc.append(jnp.dot(p_t, dos, preferred_element_type=F32))
            dq_t = jnp.dot(kd_t[g], ds_t, preferred_element_type=F32) * (HD ** -0.5)
            for j in range(GH // 2):
                pair = GH // 2 * g + j
                dq = [dq_t[:, h * BLK:(h + 1) * BLK].T for h in (2 * j, 2 * j + 1)]
                dqkv_ref[:, pair * BLK:(pair + 1) * BLK] = _rope_bwd(jnp.where(low, dq[0], dq[1]), *tabs).astype(BF16)
        dsink_ref[0:1, :] += dsink

        def fold(acc):
            tot = [a + pltpu.roll(a, HD, 1) for a in acc]
            return jnp.where(low, tot[0], tot[1])

        dk_all, dv_all = fold(dk_acc), fold(dv_acc)
        met_k[...] += dk_all[0:BLK, :]
        met_v[...] += dv_all[0:BLK, :]
        last = jnp.where(n == 0, 1.0, 0.0)
        dk_n = dk_all[2 * BLK:3 * BLK, :] + car_k[...] + last * met_k[...]
        dv_n = dv_all[2 * BLK:3 * BLK, :] + car_v[...] + last * met_v[...]
        dqkv_ref[:, D:D + BLK] = _rope_bwd(dk_n, *tabs).astype(BF16)
        dqkv_ref[:, D + BLK:QKV_W] = dv_n.astype(BF16)
        car_k[...] = dk_all[BLK:2 * BLK, :]
        car_v[...] = dv_all[BLK:2 * BLK, :]

    rblk = lambda w: pl.BlockSpec((BLK, w), lambda t: (nb - 1 - t, 0))
    return _call(
        body, hook, _steps_1d(nb, nb), name="attn_bwd", grid=(nb,),
        in_specs=[pl.BlockSpec(memory_space=pltpu.SMEM), rblk(D)] + _kv_specs(nb, True)
                 + [rblk(D), pl.BlockSpec((NH, BLK), lambda t: (nb - 1 - t, 0)), rblk(BLK), rblk(BLK), rblk(BLK)],
        out_specs=[rblk(QKV_W), _const((8, BLK))],
        out_shape=[_sds((R, QKV_W), BF16), _sds((8, BLK), F32)],
        scratch_shapes=[pltpu.VMEM((BLK, BLK), F32)] * 4,
        operands=(sinks, qkv, *([qkv] * 6), do, lse, *rope), semantics=("arbitrary",))


def _in_bwd(dproj, w_in, h0, dh1, g_pre, tr, hook=None):
    R = h0.shape[0]
    n = len(dproj)
    widths = [p.shape[1] for p in dproj]
    starts = [sum(widths[:j]) for j in range(n)]

    def body(*refs):
        d_refs, (w_ref, h0_ref, dh1_ref, g_ref, dh0_ref, dg_ref, db_ref) = refs[:n], refs[n:]

        @pl.when(pl.program_id(0) == 0)
        def _():
            dg_ref[...] = jnp.zeros_like(dg_ref)
            db_ref[...] = jnp.zeros_like(db_ref)

        dn1 = jnp.zeros((tr, D), F32)
        for d_ref, c0, wd in zip(d_refs, starts, widths):
            d = d_ref[...]
            dn1 = dn1 + jnp.dot(d, w_ref[c0:c0 + wd, :], preferred_element_type=F32)
            db_ref[:, c0:c0 + wd] += _colsum(d.astype(F32))
        h0 = h0_ref[...]
        _, r = _rms(h0, g_ref[...])
        dh0_ref[...] = dh1_ref[...] + _rms_bwd(dn1, h0, r, g_ref[...])
        dg_ref[...] += _colsum(dn1 * h0 * r)

    return _call(
        body, hook, _steps_1d(R // tr, R // tr), name="in_bwd", grid=(R // tr,),
        in_specs=[_row(tr, wd) for wd in widths] + [_const((IN_W, D)), _row(tr, D), _row(tr, D), _const((1, D))],
        out_specs=[_row(tr, D), _const((1, D)), _const((1, IN_W))],
        out_shape=[_sds((R, D), F32), _sds((1, D), F32), _sds((1, IN_W), F32)],
        operands=(*dproj, w_in, h0, dh1, g_pre), semantics=("arbitrary",))


def _dw(a, b, name, tn, tr, by_chip=False, ta=None, rows_of=None, row0=0, into=None):
    R, ka = a.shape
    n = b.shape[1]
    nt = R // tr
    ta = ta or ka
    k0 = 0
    if by_chip:
        out_spec = pl.BlockSpec((None, ta, tn), lambda k, j, i: (j, k, 0))
        out_shape = _sds((n // tn, ka, tn), BF16)
    else:
        if rows_of is not None:
            k0 = row0 // ta
        out_spec = pl.BlockSpec((ta, tn), lambda k, j, i: (k0 + k, j))
        out_shape = _sds((ka if rows_of is None else rows_of, n), BF16)
    extra = [] if into is None else [into]

    def body(a_ref, b_ref, *rest):
        o_ref, acc = rest[len(extra):]
        i = pl.program_id(2)

        @pl.when(i == 0)
        def _():
            acc[...] = jnp.zeros_like(acc)

        acc[...] += lax.dot_general(a_ref[...], b_ref[...], TN_DIMS, preferred_element_type=F32)

        @pl.when(i == nt - 1)
        def _():
            o_ref[...] = acc[...].astype(BF16)

    return pl.pallas_call(
        body, name=name, grid=(ka // ta, n // tn, nt),
        in_specs=[pl.BlockSpec((tr, ta), lambda k, j, i: (i, k)), pl.BlockSpec((tr, tn), lambda k, j, i: (i, j))]
                 + [pl.BlockSpec(memory_space=pl.ANY)] * len(extra),
        out_specs=out_spec, out_shape=out_shape,
        input_output_aliases={2: 0} if extra else {},
        scratch_shapes=[pltpu.VMEM((ta, tn), F32)],
        compiler_params=_params(("arbitrary", "arbitrary", "arbitrary")),
    )(a, b, *extra)


SMALL = ["norm_pre_mix", "norm_post_mix", "b_in", "attn_sinks", "conv_dw_b", "conv_ln_g", "conv_ln_b",
         "b_conv_proj", "norm_pre_ffn", "norm_post_ffn", "ffn_dw_b"]


def local_step(x, tgt, W, dist=None):
    W = dict(W)
    S = x.shape[0]
    R = S + BLK
    tr = _tile(R, 384, BLK)
    trw = _tile(R, 1056)
    rope = _rope_tables(R)
    meta = _cols_joined(W["tiny"][:, TINY_META:TINY_META + NMETA, 0:DQ])
    h0 = jnp.concatenate([jnp.zeros((PAD, D), F32), meta, x], 0)

    qkv, glu, gates, n1, *got = _in_proj(h0, W["norm_pre_mix"], W["w_in"], W["b_in"], rope, tr,
                                         dist and dist.gather_hook(GATHER_IN_PROJ))
    if dist:
        W.update(dist.weights(GATHER_IN_PROJ, got))
    sinks = W["attn_sinks"].reshape(NH)
    attn, lse, *got = _attn_fwd(qkv, sinks, dist and dist.gather_hook(GATHER_ATTN))
    if dist:
        W.update(dist.weights(GATHER_ATTN, got))
    c1, *got = _conv_fwd(glu, W["tiny"], W["conv_dw_b"], tr, dist and dist.gather_hook(GATHER_CONV))
    if dist:
        W.update(dist.weights(GATHER_CONV, got))
    ao, co, c3, merged, mix, h1, n2 = _mix_out(
        attn, c1, gates, h0, W["w_attn_proj"], W["w_conv_proj"], W["w_out"], W["conv_ln_g"], W["conv_ln_b"],
        W["b_conv_proj"], W["norm_post_mix"], W["norm_pre_ffn"], tr)
    up, act = _ffn_up(n2, W["w_up"], W["tiny"], W["ffn_dw_b"], tr)
    dh2, dffn, loss_cols, dg_post_ffn = _ffn_down(act, W["w_down"], h1, tgt, W["norm_post_ffn"], tr)

    dw_down = _dw(act, dffn, "dw_down", 512, trw)
    dup, dfw, dfb = _ffn_bwd_act(dffn, W["w_down"], up, W["tiny"], W["ffn_dw_b"], tr)
    dw_up = _dw(n2, dup, "dw_up", UPQ, _tile(R, 2 * trw), by_chip=True)
    ffn_parts = [dw_up, dw_down.reshape(N_CHIPS, -1, D)]
    dh1, dmix, dg_pre_ffn, dg_post_mix, *got = _ffn_bwd_in(dup, W["w_up"], h1, dh2, mix, W["norm_pre_ffn"],
                                                            W["norm_post_mix"], tr,
                                                            dist and dist.swap_hook(FFN_SHARES, ffn_parts))
    ffn_sums = dist and dist.pair_sums(FFN_SHARES, ffn_parts, "ffn", swapped=got)
    dao, dco, dgates, dattn, dc1, db_cp, dlg, dlb, dcb = _mix_bwd(
        dmix, ao, co, gates, c1, W["w_out"], W["w_attn_proj"], W["w_conv_proj"], W["conv_ln_g"], W["conv_ln_b"], tr)
    dglu_a, dglu_g, dcw, *ffn_got = _conv_bwd(dc1, glu, W["tiny"], tr, dist and dist.chip_hook(FFN_SHARES, ffn_sums))
    trw2 = _tile(R, 2 * trw)
    branch = [_dw(attn, dao, "dw_attn_proj", D, trw2), _dw(c3, dco, "dw_conv_proj", D, trw2),
              _dw(merged, dmix, "dw_out", D, trw2),
              _tiny_pack({"conv_dw_w": dcw, "ffn_dw_w": dfw, "meta_tokens": jnp.zeros((N_CHIPS, NMETA, DQ), F32)})]
    branch_sums = dist and dist.pair_sums(BRANCH_TINY_SHARES, [a.reshape(N_CHIPS, -1, a.shape[-1]) for a in branch], "branch")
    dqkv, dsink, *got = _attn_bwd(qkv, dattn, lse, sinks, rope,
                                  dist and dist.chip_hook(BRANCH_TINY_SHARES, branch_sums))
    if dist:
        dist.finish(FFN_SHARES + BRANCH_TINY_SHARES, ffn_sums + branch_sums, ffn_got + got, "ffn_branch")
    dproj = [dqkv, dglu_a, dglu_g, dgates]
    dw_in, row0 = None, 0
    for j, p in enumerate(dproj):
        ta = _tile(p.shape[1], D, BLK) if j == 0 else 2 * BLK
        dw_in = _dw(p, n1, "dw_in_%d" % j, D, _tile(R, 2 * trw), ta=ta, rows_of=IN_W, row0=row0, into=dw_in)
        row0 += p.shape[1]
    in_sums = dist and dist.pair_sums(IN_SHARES, [dw_in.reshape(N_CHIPS, -1, D)], "in")
    dh0, dg_pre_mix, db_in, *got = _in_bwd(dproj, W["w_in"], h0, dh1, W["norm_pre_mix"], tr,
                                           dist and dist.chip_hook(IN_SHARES, in_sums))
    if dist:
        dist.finish(IN_SHARES, in_sums, got, "in")

    grads = {
        "w_in": dw_in, "w_attn_proj": branch[0], "w_conv_proj": branch[1], "w_out": branch[2],
        "w_up": dw_up,
        "w_down": dw_down,
        "tiny": branch[3],
        "meta_tokens": dh0[PAD:BLK],
        "norm_pre_mix": dg_pre_mix, "norm_post_mix": dg_post_mix, "b_in": db_in,
        "attn_sinks": dsink[0:1, 0:NH], "conv_dw_b": dcb, "conv_ln_g": dlg, "conv_ln_b": dlb,
        "b_conv_proj": db_cp, "norm_pre_ffn": dg_pre_ffn, "norm_post_ffn": dg_post_ffn, "ffn_dw_b": dfb,
    }
    return loss_cols, dh0[BLK:], grads


INQ = IN_W // N_CHIPS
DQ = D // N_CHIPS
SHARES = [("w_in", INQ, D, BF16), ("w_attn_proj", DQ, D, BF16), ("w_conv_proj", DQ, D, BF16), ("w_out", DQ, D, BF16),
          ("w_up", D, UPQ, BF16), ("w_down", FFN // N_CHIPS, D, BF16), ("tiny", TINY_ROWS, UPQ, F32)]
TINY_PARTS = [("conv_dw_w", TINY_CONV, CONV_K, TINY_FFN - TINY_CONV, DQ), ("ffn_dw_w", TINY_FFN, FFN_K, TINY_META - TINY_FFN, UPQ),
              ("meta_tokens", TINY_META, NMETA, NMETA, DQ)]


def _tiny_pack(parts):
    rows = []
    for name, _, _, reserved, _ in TINY_PARTS:
        a = parts[name].astype(F32)
        pad = [(0, 0)] * (a.ndim - 2) + [(0, reserved - a.shape[-2]), (0, UPQ - a.shape[-1])]
        rows.append(jnp.pad(a, pad))
    used = sum(r.shape[-2] for r in rows)
    rows.append(jnp.zeros(rows[0].shape[:-2] + (TINY_ROWS - used, UPQ), F32))
    return jnp.concatenate(rows, axis=-2)


def _tiny_unpack(tiny):
    return {name: tiny[..., r0:r0 + k, 0:cols] for name, r0, k, _, cols in TINY_PARTS}


def _cols_by_chip(a):
    rows, n = a.shape
    return a.reshape(rows, N_CHIPS, n // N_CHIPS).transpose(1, 0, 2)


def _cols_joined(a):
    _, rows, cols = a.shape
    return a.transpose(1, 0, 2).reshape(rows, N_CHIPS * cols)


def _to_planes(a, rows):
    return jnp.pad(a, [(0, rows * D - a.shape[-1])]).reshape(rows, D)


ANY = pl.BlockSpec(memory_space=pl.ANY)


def _place():
    x, y, c = lax.axis_index("x"), lax.axis_index("y"), lax.axis_index("c")
    chips = [(1 - x, y), (x, 1 - y), (1 - x, 1 - y)]
    return x, y, c, chips


def _rcopy(src, dst, ssem, rsem, to):
    return pltpu.make_async_remote_copy(src_ref=src, dst_ref=dst, send_sem=ssem, recv_sem=rsem,
                                        device_id=to, device_id_type=MESH)


def _halves(ref_or_rows, c):
    half = ref_or_rows // 2
    return pl.ds(c * half, half), pl.ds((1 - c) * half, half)


FIRST_SHARES, BRANCH_SHARES, FFN_SHARES = [0, 6], [1, 2, 3], [4, 5]
IN_SHARES, BRANCH_TINY_SHARES = [0], [1, 2, 3, 6]
GATHER_IN_PROJ, GATHER_ATTN, GATHER_CONV = [1, 2, 3], [4], [5]


def _gather_hook(own, idx):
    n = len(idx)

    def copies(kind, ins, outs, ssem, rsem):
        x, y, c, chips = _place()
        q = 2 * x + y
        sib = (x, y, 1 - c)
        out = []
        for i, a in enumerate(idx):
            mine, other = _halves(SHARES[a][1], c)
            for j, (cx, cy) in enumerate(chips):
                k, to = 3 * i + j, (cx, cy, c)
                landed, theirs = outs[i].at[2 * cx + cy, mine], outs[i].at[2 * cx + cy, other]
                if kind == "send":
                    out.append(_rcopy(ins[i].at[mine], outs[i].at[q, mine], ssem.at[k], rsem.at[k], to))
                elif kind == "landing":
                    out.append(_rcopy(ins[i].at[mine], landed, ssem.at[k], rsem.at[k], to))
                elif kind == "pass":
                    out.append(_rcopy(landed, landed, ssem.at[3 * n + k], rsem.at[3 * n + k], sib))
                else:
                    out.append(_rcopy(theirs, theirs, ssem.at[3 * n + k], rsem.at[3 * n + k], sib))
        return out

    def own_copies(ins, outs, ssem, rsem):
        x, y, c, _ = _place()
        q = 2 * x + y
        return [_rcopy(ins[i], outs[i].at[q], ssem.at[6 * n + i], rsem.at[6 * n + i], (x, y, 1 - c)) for i in range(n)]

    def start(*refs):
        for cp in copies("send", *refs) + own_copies(*refs):
            cp.start()

    def mid(*refs):
        for landed, cp in zip(copies("landing", *refs), copies("pass", *refs)):
            landed.wait_recv()
            cp.start()

    def finish(*refs):
        for cp in copies("arrival", *refs):
            cp.wait_recv()
        for cp in copies("send", *refs) + copies("pass", *refs):
            cp.wait_send()
        for cp in own_copies(*refs):
            cp.wait()

    shapes = [_sds((N_CHIPS,) + SHARES[a][1:3], SHARES[a][3]) for a in idx]
    return _Hook(own, shapes, 7 * n, start, finish, mid)


def _chip_hook(sums, idx):
    def copies(ins, outs, ssem, rsem):
        x, y, c, chips = _place()
        return [_rcopy(ins[i].at[2 * cx + cy], outs[i].at[j], ssem.at[3 * i + j], rsem.at[3 * i + j], (cx, cy, c))
                for i in range(len(idx)) for j, (cx, cy) in enumerate(chips)]

    def start(*refs):
        for cp in copies(*refs):
            cp.start()

    def finish(*refs):
        for cp in copies(*refs):
            cp.wait()

    shapes = [_sds((N_CHIPS - 1, SHARES[a][1] // 2, SHARES[a][2]), SHARES[a][3]) for a in idx]
    return _Hook(sums, shapes, 3 * len(idx), start, finish)


def _swap_hook(parts, idx):
    def copies(ins, outs, ssem, rsem):
        x, y, c, _ = _place()
        return [_rcopy(ins[i].at[:, _halves(SHARES[a][1], c)[1]], outs[i], ssem.at[i], rsem.at[i], (x, y, 1 - c))
                for i, a in enumerate(idx)]

    def start(*refs):
        for cp in copies(*refs):
            cp.start()

    def finish(*refs):
        for cp in copies(*refs):
            cp.wait()

    shapes = [_sds((N_CHIPS, SHARES[a][1] // 2, SHARES[a][2]), SHARES[a][3]) for a in idx]
    return _Hook(parts, shapes, len(idx), start, finish)


def _sum_pair(parts, recvs, c, idx, tag):
    steps, n = 2, len(idx)

    def body(c_ref, *refs):
        for i, a in enumerate(idx):
            refs[2 * n + i][...] = (refs[i][...].astype(F32) + refs[n + i][...].astype(F32)).astype(SHARES[a][3])

    own, got, out, views, shapes = [], [], [], [], []
    for p, a in zip(parts, idx):
        _, rows, cols, dt = SHARES[a]
        blk = rows // 2 // steps
        own.append(pl.BlockSpec((None, None, blk, cols), lambda q, i, c_ref: (q, c_ref[0], i, 0)))
        got.append(pl.BlockSpec((None, blk, cols), lambda q, i, c_ref: (q, i, 0)))
        out.append(pl.BlockSpec((None, blk, cols), lambda q, i, c_ref: (q, i, 0)))
        views.append(p.reshape(N_CHIPS, 2, rows // 2, cols))
        shapes.append(_sds((N_CHIPS, rows // 2, cols), dt))
    grid_spec = pltpu.PrefetchScalarGridSpec(num_scalar_prefetch=1, grid=(N_CHIPS, steps),
                                             in_specs=own + got, out_specs=out)
    return pl.pallas_call(body, name="sum_pair_" + tag, grid_spec=grid_spec, out_shape=shapes,
                          compiler_params=_params(("arbitrary", "arbitrary")))(c, *views, *recvs)


def _sum_chips(sums, recvs, qc, idx, tag):
    steps, n = 2, len(idx)

    def body(qc_ref, *refs):
        for i in range(n):
            acc = refs[i][...].astype(F32)
            for j in range(1, N_CHIPS):
                acc = acc + refs[j * n + i][...].astype(F32)
            refs[N_CHIPS * n + i][...] = acc

    own, got, out, shapes = [], [[], [], []], [], []
    for a in idx:
        _, rows, cols, _ = SHARES[a]
        blk = rows // 2 // steps
        own.append(pl.BlockSpec((None, blk, cols), lambda i, qc_ref: (qc_ref[0], i, 0)))
        for j in range(N_CHIPS - 1):
            got[j].append(pl.BlockSpec((None, blk, cols), lambda i, qc_ref, j=j: (j, i, 0)))
        out.append(pl.BlockSpec((None, blk, cols), lambda i, qc_ref: (qc_ref[1], i, 0)))
        shapes.append(_sds((2, rows // 2, cols), F32))
    grid_spec = pltpu.PrefetchScalarGridSpec(num_scalar_prefetch=1, grid=(steps,),
                                             in_specs=own + got[0] + got[1] + got[2], out_specs=out)
    return pl.pallas_call(body, name="sum_chips_" + tag, grid_spec=grid_spec, out_shape=shapes,
                          compiler_params=_params(("arbitrary",)))(qc, *sums, *recvs, *recvs, *recvs)


def _sibling_share(halves, idx, tag):
    n = len(idx)

    def body(*refs):
        outs, (ssem, rsem) = refs[n:2 * n], refs[2 * n:]
        x, y, c, _ = _place()
        copies = []
        for i in range(n):
            cp = _rcopy(outs[i].at[c], outs[i].at[c], ssem.at[i], rsem.at[i], (x, y, 1 - c))
            cp.start()
            copies.append(cp)
        for i in range(n):
            theirs = outs[i].at[1 - c]
            _rcopy(theirs, theirs, ssem.at[i], rsem.at[i], (x, y, 1 - c)).wait_recv()
        for cp in copies:
            cp.wait_send()

    return pl.pallas_call(
        body, name="sibling_share_" + tag, in_specs=[ANY] * n, out_specs=[ANY] * n,
        out_shape=[_sds((2, SHARES[a][1] // 2, SHARES[a][2]), F32) for a in idx],
        input_output_aliases={i: i for i in range(n)},
        scratch_shapes=[pltpu.SemaphoreType.DMA((n,)), pltpu.SemaphoreType.DMA((n,))],
    )(*halves)


class _Dist:
    def __init__(self, own):
        self.own = own
        self.core = lax.axis_index("c")
        self.chip = 2 * lax.axis_index("x") + lax.axis_index("y")
        self.reduced = {}

    def gather_hook(self, idx):
        return _gather_hook([self.own[a] for a in idx], idx)

    def weights(self, idx, gathered):
        out = {}
        for a, full in zip(idx, gathered):
            name = SHARES[a][0]
            out[name] = full if name in ("w_up", "tiny") else full.reshape(-1, D)
        return out

    def pair_sums(self, idx, parts, tag, swapped=None):
        if swapped is None:
            swapped = _alone(_swap_hook(parts, idx), "sibling_swap_" + tag)
        return _sum_pair(parts, swapped, self.core.reshape(1), idx, tag)

    def swap_hook(self, idx, parts):
        return _swap_hook(parts, idx)

    def chip_hook(self, idx, sums):
        return _chip_hook(sums, idx)

    def finish(self, idx, sums, recvs, tag):
        halves = _sum_chips(sums, recvs, jnp.stack([self.chip, self.core]), idx, tag)
        for a, full in zip(idx, _sibling_share(halves, idx, tag)):
            self.reduced[SHARES[a][0]] = full.reshape(SHARES[a][1:3])


N_DEV = 8
SMALL_ROWS = 40


def _small_allreduce(sm):
    def body(s_ref, o_ref, buf, ssem, rsem):
        x, y, c, _ = _place()
        me = 4 * x + 2 * y + c
        buf[me] = s_ref[...]
        copies = []
        for d in range(1, N_DEV):
            dx, dy, dc = d >> 2, (d >> 1) & 1, d & 1
            to = (x ^ dx, y ^ dy, c ^ dc)
            cp = _rcopy(s_ref, buf.at[me], ssem.at[d - 1], rsem.at[d - 1], to)
            cp.start()
            copies.append(cp)
        for d in range(1, N_DEV):
            src = me ^ d
            _rcopy(s_ref, buf.at[src], ssem.at[d - 1], rsem.at[d - 1], (x, y, c)).wait_recv()
        for cp in copies:
            cp.wait_send()
        acc = buf[0]
        for k in range(1, N_DEV):
            acc = acc + buf[k]
        o_ref[...] = acc

    vm = pl.BlockSpec(memory_space=pltpu.VMEM)
    return pl.pallas_call(
        body, name="small_allreduce", in_specs=[vm], out_specs=vm,
        out_shape=_sds((SMALL_ROWS, D), F32),
        scratch_shapes=[pltpu.VMEM((N_DEV, SMALL_ROWS, D), F32),
                        pltpu.SemaphoreType.DMA((N_DEV - 1,)), pltpu.SemaphoreType.DMA((N_DEV - 1,))],
    )(sm)


SMALL_PLAN = [("norm_pre_mix", D), ("norm_post_mix", D), ("b_in", IN_W), ("attn_sinks", NH), ("conv_dw_b", D),
              ("conv_ln_g", D), ("conv_ln_b", D), ("b_conv_proj", D), ("norm_pre_ffn", D), ("norm_post_ffn", D),
              ("ffn_dw_b", 2 * FFN), ("loss", D), ("meta_tokens", NMETA * D)]


def _pack_small(parts):
    rows = [_to_planes(parts[name].reshape(-1), -(-n // D)) for name, n in SMALL_PLAN]
    used = sum(r.shape[0] for r in rows)
    return jnp.concatenate(rows + [jnp.zeros((SMALL_ROWS - used, D), F32)], 0)


def _unpack_small(packed):
    out, r0 = {}, 0
    for name, n in SMALL_PLAN:
        rows = -(-n // D)
        out[name] = packed[r0:r0 + rows].reshape(-1)[:n].reshape(1, n)
        r0 += rows
    return out


def _adamw_update(w_ref, g_ref, m_ref, v_ref, d_ref, nm_ref, nv_ref):
    g = g_ref[...]
    m = B1 * m_ref[...] + (1.0 - B1) * g
    v = B2 * v_ref[...] + (1.0 - B2) * (g * g)
    nm_ref[...] = m
    nv_ref[...] = v
    m_hat = m / (1.0 - B1 ** STEP)
    v_hat = v / (1.0 - B2 ** STEP)
    d_ref[...] = -LR * (m_hat / (jnp.sqrt(v_hat) + ADAM_EPS) + WD * w_ref[...])


def _adamw_vectors(ws, gs, ms, vs):
    n = len(ws)

    def body(*refs):
        for j in range(n):
            _adamw_update(*[refs[k * n + j] for k in range(7)])

    vm = pl.BlockSpec(memory_space=pltpu.VMEM)
    outs = pl.pallas_call(body, name="adamw_vectors", in_specs=[vm] * (4 * n), out_specs=[vm] * (3 * n),
                          out_shape=[_sds(w.shape, F32) for w in ws] * 3)(*ws, *gs, *ms, *vs)
    return outs[:n], outs[n:2 * n], outs[2 * n:]


def _adamw(w, g, m, v, name):
    rows, cols = w.shape
    tr = _tile(rows, 256, 8) if rows % 8 == 0 else rows

    def body(*refs):
        _adamw_update(*refs)

    spec = pl.BlockSpec((tr, cols), lambda i: (i, 0))
    return pl.pallas_call(
        body, name=name, grid=(rows // tr,), in_specs=[spec] * 4, out_specs=[spec] * 3,
        out_shape=[_sds((rows, cols), F32)] * 3, compiler_params=_params(("arbitrary",)),
    )(w, g, m, v)


NAMES = ["meta_tokens", "norm_pre_mix", "norm_post_mix", "w_in", "b_in", "attn_sinks", "w_attn_proj", "conv_dw_w",
         "conv_dw_b", "conv_ln_g", "conv_ln_b", "w_conv_proj", "b_conv_proj", "w_out", "norm_pre_ffn", "norm_post_ffn",
         "w_up", "ffn_dw_w", "ffn_dw_b", "w_down"]
MATMUL = ("w_in", "w_attn_proj", "w_conv_proj", "w_out", "w_up", "w_down")


def _two_d(a):
    return a.reshape(a.shape[-2:])


def kernel(x, meta_tokens, norm_pre_mix, norm_post_mix, w_in, b_in, attn_sinks, w_attn_proj, conv_dw_w, conv_dw_b, conv_ln_g, conv_ln_b, w_conv_proj, b_conv_proj, w_out, norm_pre_ffn, norm_post_ffn, w_up, ffn_dw_w, ffn_dw_b, w_down, loss_target, m_meta_tokens, m_norm_pre_mix, m_norm_post_mix, m_w_in, m_b_in, m_attn_sinks, m_w_attn_proj, m_conv_dw_w, m_conv_dw_b, m_conv_ln_g, m_conv_ln_b, m_w_conv_proj, m_b_conv_proj, m_w_out, m_norm_pre_ffn, m_norm_post_ffn, m_w_up, m_ffn_dw_w, m_ffn_dw_b, m_w_down, v_meta_tokens, v_norm_pre_mix, v_norm_post_mix, v_w_in, v_b_in, v_attn_sinks, v_w_attn_proj, v_conv_dw_w, v_conv_dw_b, v_conv_ln_g, v_conv_ln_b, v_w_conv_proj, v_b_conv_proj, v_w_out, v_norm_pre_ffn, v_norm_post_ffn, v_w_up, v_ffn_dw_w, v_ffn_dw_b, v_w_down):
    args = locals()
    w = {n: args[n] for n in NAMES}
    m = {n: args["m_" + n] for n in NAMES}
    v = {n: args["v_" + n] for n in NAMES}
    tiny_names = [part[0] for part in TINY_PARTS]
    big = list(MATMUL) + tiny_names

    def shard_2d(a, name):
        return _two_d(a).T if name == "w_in" else _two_d(a)

    own = {n: shard_2d(w[n], n).astype(BF16) for n in MATMUL}
    own["tiny"] = _tiny_pack({n: _two_d(w[n]) for n in tiny_names})
    dist = _Dist([own[n] for n, _, _, _ in SHARES])
    W = {n: _two_d(w[n]) for n in SMALL}
    W.update(dist.weights(FIRST_SHARES, _alone(dist.gather_hook(FIRST_SHARES), "gather_first")))

    loss_cols, grad_x, grads = local_step(x[0], loss_target[0], W, dist)

    small = dict(grads)
    small["loss"] = loss_cols
    g_small = _unpack_small(_small_allreduce(_pack_small(small)))
    loss = jnp.sum(g_small["loss"])
    g_big = {n: dist.reduced[n] for n in MATMUL}
    g_big.update(_tiny_unpack(dist.reduced["tiny"]))
    g_big["meta_tokens"] = lax.dynamic_slice(g_small["meta_tokens"].reshape(NMETA, D), (0, dist.chip * DQ), (NMETA, DQ))

    g, delta, new_m, new_v = {}, {}, {}, {}
    for n in big:
        shape = w[n].shape
        back = (lambda a: a.T.reshape(shape)) if n == "w_in" else (lambda a: a.reshape(shape))
        outs = _adamw(shard_2d(w[n], n), g_big[n], shard_2d(m[n], n), shard_2d(v[n], n), "adamw_" + n)
        g[n], delta[n], new_m[n], new_v[n] = (back(a) for a in (g_big[n], *outs))
    ud, um, uv = _adamw_vectors(*[[_two_d(d[n]) for n in SMALL] for d in (w, g_small, m, v)])
    for j, n in enumerate(SMALL):
        g[n], delta[n], new_m[n], new_v[n] = g_small[n], ud[j], um[j], uv[j]

    return (loss, grad_x[None], *[g[n] for n in NAMES], *[delta[n] for n in NAMES],
            *[new_m[n] for n in NAMES], *[new_v[n] for n in NAMES])
```
